```python
import math
import jax, jax.numpy as jnp
from jax import lax
import numpy as np

D_MODEL = 1024
BATCH = 8
SEQ = 16384
DEPTH = 4

N_MIXERS = 2
ROPE_THETA = 500000.0
LN_EPS = 1e-5
RMS_EPS = 1e-6

MLA_HEADS = 8
MLA_NOPE = 128
MLA_ROPE = 64
MLA_V = 128
MLA_Q_RANK = 384
MLA_KV_RANK = 256
MLA_QBLOCK = 128

SWA_HEADS = 16
SWA_KV_HEADS = 4
SWA_HEAD_DIM = 64
SWA_WINDOW = 128
SWA_ROT = SWA_HEAD_DIM // 4

D_FF = ((8 * D_MODEL + 3 * 256 - 1) // (3 * 256)) * 256

DEEPNORM_ALPHA = (2 * DEPTH) ** 0.25
DEEPNORM_BETA = (8 * DEPTH) ** -0.25

N_MLA = (DEPTH + 1) // 2
N_SWA = DEPTH // 2

kernel_name = "hybrid_mla_swa_sink_deepnorm_adaln"


def layer_norm(x, g, b):
    xf = x.astype(jnp.float32)
    mu = jnp.mean(xf, -1, keepdims=True)
    var = jnp.mean(jnp.square(xf - mu), -1, keepdims=True)
    return ((xf - mu) * lax.rsqrt(var + LN_EPS) * g + b).astype(x.dtype)


def rms_norm(x, g):
    xf = x.astype(jnp.float32)
    return (xf * lax.rsqrt(jnp.mean(jnp.square(xf), -1, keepdims=True) + RMS_EPS) * g).astype(x.dtype)


def rope_cos_sin(positions, rot_dim):
    inv = ROPE_THETA ** (-jnp.arange(0, rot_dim, 2, dtype=jnp.float32) / rot_dim)
    ang = positions.astype(jnp.float32)[..., None] * inv
    return jnp.cos(ang), jnp.sin(ang)


def apply_rope(x, cos, sin):
    half = x.shape[-1] // 2
    x1, x2 = x[..., :half], x[..., half:]
    c = cos[:, :, None, :]
    s = sin[:, :, None, :]
    return jnp.concatenate([x1 * c - x2 * s, x2 * c + x1 * s], -1).astype(x.dtype)


def mla_mixer(h, cos, sin, w_in, q_norm, w_q_b, kv_norm, w_kv_b, w_o):
    B, S, _ = h.shape
    H = MLA_HEADS
    lat = h @ w_in
    q_lat = lat[..., :MLA_Q_RANK]
    kv_lat = lat[..., MLA_Q_RANK:MLA_Q_RANK + MLA_KV_RANK]
    k_rope = lat[..., MLA_Q_RANK + MLA_KV_RANK:]
    q = (rms_norm(q_lat, q_norm) @ w_q_b).reshape(B, S, H, MLA_NOPE + MLA_ROPE)
    q_nope = q[..., :MLA_NOPE]
    q_rope = apply_rope(q[..., MLA_NOPE:], cos, sin)
    k_rope = apply_rope(k_rope[:, :, None, :], cos, sin)[:, :, 0, :]
    kv = (rms_norm(kv_lat, kv_norm) @ w_kv_b).reshape(B, S, H, MLA_NOPE + MLA_V)
    k_nope = kv[..., :MLA_NOPE]
    v = kv[..., MLA_NOPE:]
    scale = (MLA_NOPE + MLA_ROPE) ** -0.5
    nb = S // MLA_QBLOCK
    qn_b = q_nope.reshape(B, nb, MLA_QBLOCK, H, MLA_NOPE).transpose(1, 0, 2, 3, 4)
    qr_b = q_rope.reshape(B, nb, MLA_QBLOCK, H, MLA_ROPE).transpose(1, 0, 2, 3, 4)
    starts = jnp.arange(nb, dtype=jnp.int32) * MLA_QBLOCK
    k_idx = jnp.arange(S, dtype=jnp.int32)

    def q_block(args):
        qn, qr, start = args
        s = (jnp.einsum('bqhd,bkhd->bhqk', qn, k_nope, preferred_element_type=jnp.float32)
             + jnp.einsum('bqhr,bkr->bhqk', qr, k_rope, preferred_element_type=jnp.float32)) * scale
        q_idx = start + jnp.arange(MLA_QBLOCK, dtype=jnp.int32)
        causal = k_idx[None, :] <= q_idx[:, None]
        s = jnp.where(causal[None, None], s, -jnp.inf)
        p = jax.nn.softmax(s, axis=-1).astype(v.dtype)
        return jnp.einsum('bhqk,bkhd->bqhd', p, v)

    o = lax.map(q_block, (qn_b, qr_b, starts))
    o = o.transpose(1, 0, 2, 3, 4).reshape(B, S, H * MLA_V)
    return o @ w_o


def partial_rope(x, cos, sin):
    return jnp.concatenate([apply_rope(x[..., :SWA_ROT], cos, sin), x[..., SWA_ROT:]], -1)


def swa_mixer(h, cos, sin, w_qkv, b_qkv, sinks, w_o, b_o):
    B, S, _ = h.shape
    HQ, HKV, HD, W = SWA_HEADS, SWA_KV_HEADS, SWA_HEAD_DIM, SWA_WINDOW
    G = HQ // HKV
    qkv = h @ w_qkv + b_qkv
    q = qkv[..., :HQ * HD].reshape(B, S, HQ, HD)
    k = qkv[..., HQ * HD:(HQ + HKV) * HD].reshape(B, S, HKV, HD)
    v = qkv[..., (HQ + HKV) * HD:].reshape(B, S, HKV, HD)
    q = partial_rope(q, cos, sin)
    k = partial_rope(k, cos, sin)
    nb = S // W
    qb = q.reshape(B, nb, W, HKV, G, HD)
    kb = k.reshape(B, nb, W, HKV, HD)
    vb = v.reshape(B, nb, W, HKV, HD)
    kpad = jnp.zeros_like(kb[:, :1])
    vpad = jnp.zeros_like(vb[:, :1])
    k2 = jnp.concatenate([jnp.concatenate([kpad, kb[:, :-1]], 1), kb], axis=2)
    v2 = jnp.concatenate([jnp.concatenate([vpad, vb[:, :-1]], 1), vb], axis=2)
    s = jnp.einsum('bnqhgd,bnkhd->bnhgqk', qb, k2, preferred_element_type=jnp.float32) * (HD ** -0.5)
    q_pos = jnp.arange(W, dtype=jnp.int32)[:, None] + W
    k_pos = jnp.arange(2 * W, dtype=jnp.int32)[None, :]
    rel = q_pos - k_pos
    band = (rel >= 0) & (rel < W)
    has_prev = (jnp.arange(nb)[:, None, None] > 0) | (k_pos[None] >= W)
    mask = band[None] & has_prev
    s = jnp.where(mask[None, :, None, None], s, -jnp.inf)
    sink = jnp.broadcast_to(sinks.astype(jnp.float32).reshape(HKV, G)[None, None, :, :, None, None],
                            s.shape[:-1] + (1,))
    p = jax.nn.softmax(jnp.concatenate([s, sink], -1), axis=-1)[..., :-1]
    o = jnp.einsum('bnhgqk,bnkhd->bnqhgd', p.astype(v.dtype), v2).reshape(B, S, HQ * HD)
    return o @ w_o + b_o


def swiglu(h, w_gate, w_up, w_down):
    return (jax.nn.silu(h @ w_gate) * (h @ w_up)) @ w_down


def _fwd_setup_inputs(seed: int = 0) -> dict:
    key = jax.random.key(seed)
    ks = iter(jax.random.split(key, 32))
    D, F = D_MODEL, D_FF
    nrm = lambda shape, std: jax.random.normal(next(ks), shape, jnp.float32) * std
    offs = jax.random.randint(next(ks), (BATCH, 1), 0, 4096, dtype=jnp.int32)
    positions = offs + jnp.arange(SEQ, dtype=jnp.int32)[None, :]
    mla_in_w = MLA_Q_RANK + MLA_KV_RANK + MLA_ROPE
    swa_qkv_w = (SWA_HEADS + 2 * SWA_KV_HEADS) * SWA_HEAD_DIM
    return {
        "x": nrm((BATCH, SEQ, D), 1.0),
        "c": nrm((BATCH, D), 1.0),
        "positions": positions,
        "ada_w": nrm((DEPTH, D, 6 * D), 0.5 * D ** -0.5),
        "ada_b": nrm((DEPTH, 6 * D), 0.02),
        "ln_mix_g": 1.0 + nrm((DEPTH, D), 0.02),
        "ln_mix_b": nrm((DEPTH, D), 0.02),
        "ln_ffn_g": 1.0 + nrm((DEPTH, D), 0.02),
        "ln_ffn_b": nrm((DEPTH, D), 0.02),
        "ffn_w_gate": nrm((DEPTH, D, F), D ** -0.5),
        "ffn_w_up": nrm((DEPTH, D, F), D ** -0.5),
        "ffn_w_down": nrm((DEPTH, F, D), F ** -0.5 * DEEPNORM_BETA),
        "mla_w_in": nrm((N_MLA, D, mla_in_w), D ** -0.5),
        "mla_q_norm": 1.0 + nrm((N_MLA, MLA_Q_RANK), 0.02),
        "mla_w_q_b": nrm((N_MLA, MLA_Q_RANK, MLA_HEADS * (MLA_NOPE + MLA_ROPE)), MLA_Q_RANK ** -0.5),
        "mla_kv_norm": 1.0 + nrm((N_MLA, MLA_KV_RANK), 0.02),
        "mla_w_kv_b": nrm((N_MLA, MLA_KV_RANK, MLA_HEADS * (MLA_NOPE + MLA_V)), MLA_KV_RANK ** -0.5),
        "mla_w_o": nrm((N_MLA, MLA_HEADS * MLA_V, D), (MLA_HEADS * MLA_V) ** -0.5 * DEEPNORM_BETA),
        "swa_w_qkv": nrm((N_SWA, D, swa_qkv_w), D ** -0.5),
        "swa_b_qkv": nrm((N_SWA, swa_qkv_w), 0.02),
        "swa_sinks": nrm((N_SWA, SWA_HEADS), 1.0),
        "swa_w_o": nrm((N_SWA, SWA_HEADS * SWA_HEAD_DIM, D), (SWA_HEADS * SWA_HEAD_DIM) ** -0.5 * DEEPNORM_BETA),
        "swa_b_o": nrm((N_SWA, D), 0.02),
    }


def _fwd_reference(x, c, positions, ada_w, ada_b, ln_mix_g, ln_mix_b, ln_ffn_g, ln_ffn_b,
              ffn_w_gate, ffn_w_up, ffn_w_down, mla_w_in, mla_q_norm, mla_w_q_b, mla_kv_norm,
              mla_w_kv_b, mla_w_o, swa_w_qkv, swa_b_qkv, swa_sinks, swa_w_o, swa_b_o):
    cos_a, sin_a = rope_cos_sin(positions, MLA_ROPE)
    cos_b, sin_b = rope_cos_sin(positions, SWA_ROT)
    cond = jax.nn.silu(c)
    for i in range(DEPTH):
        mod = (cond @ ada_w[i] + ada_b[i])[:, None, :]
        sh_m, sc_m, g_m, sh_f, sc_f, g_f = jnp.split(mod, 6, axis=-1)
        h = x * (1.0 + sc_m) + sh_m
        j = i // N_MIXERS
        if i % N_MIXERS == 0:
            y = mla_mixer(h, cos_a, sin_a, mla_w_in[j], mla_q_norm[j], mla_w_q_b[j],
                          mla_kv_norm[j], mla_w_kv_b[j], mla_w_o[j])
        else:
            y = swa_mixer(h, cos_b, sin_b, swa_w_qkv[j], swa_b_qkv[j], swa_sinks[j],
                          swa_w_o[j], swa_b_o[j])
        x = layer_norm(DEEPNORM_ALPHA * x + g_m * y, ln_mix_g[i], ln_mix_b[i])
        h = x * (1.0 + sc_f) + sh_f
        y = swiglu(h, ffn_w_gate[i], ffn_w_up[i], ffn_w_down[i])
        x = layer_norm(DEEPNORM_ALPHA * x + g_f * y, ln_ffn_g[i], ln_ffn_b[i])
    return x


import jax as _jax
import jax.numpy as _jnp

TWIN_FORMAT = 'train_step'
FWD_PARAMS = ['x', 'c', 'positions', 'ada_w', 'ada_b', 'ln_mix_g', 'ln_mix_b', 'ln_ffn_g', 'ln_ffn_b', 'ffn_w_gate', 'ffn_w_up', 'ffn_w_down', 'mla_w_in', 'mla_q_norm', 'mla_w_q_b', 'mla_kv_norm', 'mla_w_kv_b', 'mla_w_o', 'swa_w_qkv', 'swa_b_qkv', 'swa_sinks', 'swa_w_o', 'swa_b_o']
TWIN_WEIGHTS = ['ada_w', 'ada_b', 'ln_mix_g', 'ln_mix_b', 'ln_ffn_g', 'ln_ffn_b', 'ffn_w_gate', 'ffn_w_up', 'ffn_w_down', 'mla_w_in', 'mla_q_norm', 'mla_w_q_b', 'mla_kv_norm', 'mla_w_kv_b', 'mla_w_o', 'swa_w_qkv', 'swa_b_qkv', 'swa_sinks', 'swa_w_o', 'swa_b_o']
TWIN_DIFF_INPUT = 'x'
TWIN_INPUTS = ['x', 'c', 'positions', 'ada_w', 'ada_b', 'ln_mix_g', 'ln_mix_b', 'ln_ffn_g', 'ln_ffn_b', 'ffn_w_gate', 'ffn_w_up', 'ffn_w_down', 'mla_w_in', 'mla_q_norm', 'mla_w_q_b', 'mla_kv_norm', 'mla_w_kv_b', 'mla_w_o', 'swa_w_qkv', 'swa_b_qkv', 'swa_sinks', 'swa_w_o', 'swa_b_o', 'loss_target', 'm_ada_w', 'm_ada_b', 'm_ln_mix_g', 'm_ln_mix_b', 'm_ln_ffn_g', 'm_ln_ffn_b', 'm_ffn_w_gate', 'm_ffn_w_up', 'm_ffn_w_down', 'm_mla_w_in', 'm_mla_q_norm', 'm_mla_w_q_b', 'm_mla_kv_norm', 'm_mla_w_kv_b', 'm_mla_w_o', 'm_swa_w_qkv', 'm_swa_b_qkv', 'm_swa_sinks', 'm_swa_w_o', 'm_swa_b_o', 'v_ada_w', 'v_ada_b', 'v_ln_mix_g', 'v_ln_mix_b', 'v_ln_ffn_g', 'v_ln_ffn_b', 'v_ffn_w_gate', 'v_ffn_w_up', 'v_ffn_w_down', 'v_mla_w_in', 'v_mla_q_norm', 'v_mla_w_q_b', 'v_mla_kv_norm', 'v_mla_w_kv_b', 'v_mla_w_o', 'v_swa_w_qkv', 'v_swa_b_qkv', 'v_swa_sinks', 'v_swa_w_o', 'v_swa_b_o']
TWIN_OUTPUTS = ['loss', 'grad_x', 'grad_ada_w', 'grad_ada_b', 'grad_ln_mix_g', 'grad_ln_mix_b', 'grad_ln_ffn_g', 'grad_ln_ffn_b', 'grad_ffn_w_gate', 'grad_ffn_w_up', 'grad_ffn_w_down', 'grad_mla_w_in', 'grad_mla_q_norm', 'grad_mla_w_q_b', 'grad_mla_kv_norm', 'grad_mla_w_kv_b', 'grad_mla_w_o', 'grad_swa_w_qkv', 'grad_swa_b_qkv', 'grad_swa_sinks', 'grad_swa_w_o', 'grad_swa_b_o', 'delta_ada_w', 'delta_ada_b', 'delta_ln_mix_g', 'delta_ln_mix_b', 'delta_ln_ffn_g', 'delta_ln_ffn_b', 'delta_ffn_w_gate', 'delta_ffn_w_up', 'delta_ffn_w_down', 'delta_mla_w_in', 'delta_mla_q_norm', 'delta_mla_w_q_b', 'delta_mla_kv_norm', 'delta_mla_w_kv_b', 'delta_mla_w_o', 'delta_swa_w_qkv', 'delta_swa_b_qkv', 'delta_swa_sinks', 'delta_swa_w_o', 'delta_swa_b_o', 'new_m_ada_w', 'new_m_ada_b', 'new_m_ln_mix_g', 'new_m_ln_mix_b', 'new_m_ln_ffn_g', 'new_m_ln_ffn_b', 'new_m_ffn_w_gate', 'new_m_ffn_w_up', 'new_m_ffn_w_down', 'new_m_mla_w_in', 'new_m_mla_q_norm', 'new_m_mla_w_q_b', 'new_m_mla_kv_norm', 'new_m_mla_w_kv_b', 'new_m_mla_w_o', 'new_m_swa_w_qkv', 'new_m_swa_b_qkv', 'new_m_swa_sinks', 'new_m_swa_w_o', 'new_m_swa_b_o', 'new_v_ada_w', 'new_v_ada_b', 'new_v_ln_mix_g', 'new_v_ln_mix_b', 'new_v_ln_ffn_g', 'new_v_ln_ffn_b', 'new_v_ffn_w_gate', 'new_v_ffn_w_up', 'new_v_ffn_w_down', 'new_v_mla_w_in', 'new_v_mla_q_norm', 'new_v_mla_w_q_b', 'new_v_mla_kv_norm', 'new_v_mla_w_kv_b', 'new_v_mla_w_o', 'new_v_swa_w_qkv', 'new_v_swa_b_qkv', 'new_v_swa_sinks', 'new_v_swa_w_o', 'new_v_swa_b_o']
TWIN_LEAF_KINDS = {'loss': 'loss', 'grad_x': 'grad_x', 'grad_ada_w': 'grad_w', 'grad_ada_b': 'grad_w', 'grad_ln_mix_g': 'grad_w', 'grad_ln_mix_b': 'grad_w', 'grad_ln_ffn_g': 'grad_w', 'grad_ln_ffn_b': 'grad_w', 'grad_ffn_w_gate': 'grad_w', 'grad_ffn_w_up': 'grad_w', 'grad_ffn_w_down': 'grad_w', 'grad_mla_w_in': 'grad_w', 'grad_mla_q_norm': 'grad_w', 'grad_mla_w_q_b': 'grad_w', 'grad_mla_kv_norm': 'grad_w', 'grad_mla_w_kv_b': 'grad_w', 'grad_mla_w_o': 'grad_w', 'grad_swa_w_qkv': 'grad_w', 'grad_swa_b_qkv': 'grad_w', 'grad_swa_sinks': 'grad_w', 'grad_swa_w_o': 'grad_w', 'grad_swa_b_o': 'grad_w', 'delta_ada_w': 'delta_w', 'delta_ada_b': 'delta_w', 'delta_ln_mix_g': 'delta_w', 'delta_ln_mix_b': 'delta_w', 'delta_ln_ffn_g': 'delta_w', 'delta_ln_ffn_b': 'delta_w', 'delta_ffn_w_gate': 'delta_w', 'delta_ffn_w_up': 'delta_w', 'delta_ffn_w_down': 'delta_w', 'delta_mla_w_in': 'delta_w', 'delta_mla_q_norm': 'delta_w', 'delta_mla_w_q_b': 'delta_w', 'delta_mla_kv_norm': 'delta_w', 'delta_mla_w_kv_b': 'delta_w', 'delta_mla_w_o': 'delta_w', 'delta_swa_w_qkv': 'delta_w', 'delta_swa_b_qkv': 'delta_w', 'delta_swa_sinks': 'delta_w', 'delta_swa_w_o': 'delta_w', 'delta_swa_b_o': 'delta_w', 'new_m_ada_w': 'new_m', 'new_m_ada_b': 'new_m', 'new_m_ln_mix_g': 'new_m', 'new_m_ln_mix_b': 'new_m', 'new_m_ln_ffn_g': 'new_m', 'new_m_ln_ffn_b': 'new_m', 'new_m_ffn_w_gate': 'new_m', 'new_m_ffn_w_up': 'new_m', 'new_m_ffn_w_down': 'new_m', 'new_m_mla_w_in': 'new_m', 'new_m_mla_q_norm': 'new_m', 'new_m_mla_w_q_b': 'new_m', 'new_m_mla_kv_norm': 'new_m', 'new_m_mla_w_kv_b': 'new_m', 'new_m_mla_w_o': 'new_m', 'new_m_swa_w_qkv': 'new_m', 'new_m_swa_b_qkv': 'new_m', 'new_m_swa_sinks': 'new_m', 'new_m_swa_w_o': 'new_m', 'new_m_swa_b_o': 'new_m', 'new_v_ada_w': 'new_v', 'new_v_ada_b': 'new_v', 'new_v_ln_mix_g': 'new_v', 'new_v_ln_mix_b': 'new_v', 'new_v_ln_ffn_g': 'new_v', 'new_v_ln_ffn_b': 'new_v', 'new_v_ffn_w_gate': 'new_v', 'new_v_ffn_w_up': 'new_v', 'new_v_ffn_w_down': 'new_v', 'new_v_mla_w_in': 'new_v', 'new_v_mla_q_norm': 'new_v', 'new_v_mla_w_q_b': 'new_v', 'new_v_mla_kv_norm': 'new_v', 'new_v_mla_w_kv_b': 'new_v', 'new_v_mla_w_o': 'new_v', 'new_v_swa_w_qkv': 'new_v', 'new_v_swa_b_qkv': 'new_v', 'new_v_swa_sinks': 'new_v', 'new_v_swa_w_o': 'new_v', 'new_v_swa_b_o': 'new_v'}


def _forward(args):
    return _fwd_reference(*[args[k] for k in FWD_PARAMS])


def _output_shape():
    def fwd():
        inp = _fwd_setup_inputs(0)
        return _fwd_reference(*[inp[k] for k in FWD_PARAMS])
    out = _jax.eval_shape(fwd)
    return out.shape, out.dtype

N_MICROBATCH = 1
ADAM_LR = 0.001
ADAM_B1 = 0.9
ADAM_B2 = 0.999
ADAM_EPS = 1e-08
ADAM_WD = 0.01
ADAM_STEP = 10
PER_EXAMPLE_BATCH_AXIS = {'x': 0, 'c': 0, 'positions': 0, 'loss_target': 0}
SHARED_INPUTS = []
_WEIGHT_DTYPES = {'ada_w': _jnp.float32, 'ada_b': _jnp.float32, 'ln_mix_g': _jnp.float32, 'ln_mix_b': _jnp.float32, 'ln_ffn_g': _jnp.float32, 'ln_ffn_b': _jnp.float32, 'ffn_w_gate': _jnp.float32, 'ffn_w_up': _jnp.float32, 'ffn_w_down': _jnp.float32, 'mla_w_in': _jnp.float32, 'mla_q_norm': _jnp.float32, 'mla_w_q_b': _jnp.float32, 'mla_kv_norm': _jnp.float32, 'mla_w_kv_b': _jnp.float32, 'mla_w_o': _jnp.float32, 'swa_w_qkv': _jnp.float32, 'swa_b_qkv': _jnp.float32, 'swa_sinks': _jnp.float32, 'swa_w_o': _jnp.float32, 'swa_b_o': _jnp.float32}
MOMENT_SCALE = {'ada_w': 2.208780e-02, 'ada_b': 4.025992e-02, 'ln_mix_g': 3.965982e+00, 'ln_mix_b': 1.882180e+00, 'ln_ffn_g': 6.444938e+01, 'ln_ffn_b': 3.631201e+00, 'ffn_w_gate': 1.178453e-02, 'ffn_w_up': 1.143943e-02, 'ffn_w_down': 4.525232e-02, 'mla_w_in': 1.370677e-02, 'mla_q_norm': 7.594490e-03, 'mla_w_q_b': 3.730403e-03, 'mla_kv_norm': 2.077817e-02, 'mla_w_kv_b': 7.203214e-03, 'mla_w_o': 2.293118e-02, 'swa_w_qkv': 1.086955e-02, 'swa_b_qkv': 4.138626e-02, 'swa_sinks': 6.100980e-03, 'swa_w_o': 2.893101e-02, 'swa_b_o': 1.249616e-01}


def _to_microbatches(a, axis):
    t = _jnp.moveaxis(a, axis, 0)
    t = t.reshape((N_MICROBATCH, t.shape[0] // N_MICROBATCH) + t.shape[1:])
    return _jnp.moveaxis(t, 1, axis + 1)


def setup_inputs(seed: int = 0) -> dict:
    inp = _fwd_setup_inputs(seed)
    key = _jax.random.fold_in(_jax.random.key(seed), 7919)
    shape, _ = _output_shape()
    out = dict(inp)
    out["loss_target"] = _jax.random.normal(_jax.random.fold_in(key, 0), shape, _jnp.float32)
    for i, name in enumerate(TWIN_WEIGHTS):
        w = inp[name].astype(_jnp.float32)
        if MOMENT_SCALE is None:
            s = _jnp.sqrt(_jnp.mean(_jnp.square(w)) + 1e-30)
        else:
            s = MOMENT_SCALE[name]
        km, kv = _jax.random.split(_jax.random.fold_in(key, i + 1))
        out[name] = w
        out["m_" + name] = s * _jax.random.normal(km, w.shape, _jnp.float32)
        out["v_" + name] = (s * s) * _jax.random.uniform(kv, w.shape, _jnp.float32, 0.5, 1.5)
    if N_MICROBATCH > 1:
        for name, axis in PER_EXAMPLE_BATCH_AXIS.items():
            out[name] = _to_microbatches(out[name], axis)
    return {'x': out['x'], 'c': out['c'], 'positions': out['positions'], 'ada_w': out['ada_w'], 'ada_b': out['ada_b'], 'ln_mix_g': out['ln_mix_g'], 'ln_mix_b': out['ln_mix_b'], 'ln_ffn_g': out['ln_ffn_g'], 'ln_ffn_b': out['ln_ffn_b'], 'ffn_w_gate': out['ffn_w_gate'], 'ffn_w_up': out['ffn_w_up'], 'ffn_w_down': out['ffn_w_down'], 'mla_w_in': out['mla_w_in'], 'mla_q_norm': out['mla_q_norm'], 'mla_w_q_b': out['mla_w_q_b'], 'mla_kv_norm': out['mla_kv_norm'], 'mla_w_kv_b': out['mla_w_kv_b'], 'mla_w_o': out['mla_w_o'], 'swa_w_qkv': out['swa_w_qkv'], 'swa_b_qkv': out['swa_b_qkv'], 'swa_sinks': out['swa_sinks'], 'swa_w_o': out['swa_w_o'], 'swa_b_o': out['swa_b_o'], 'loss_target': out['loss_target'], 'm_ada_w': out['m_ada_w'], 'm_ada_b': out['m_ada_b'], 'm_ln_mix_g': out['m_ln_mix_g'], 'm_ln_mix_b': out['m_ln_mix_b'], 'm_ln_ffn_g': out['m_ln_ffn_g'], 'm_ln_ffn_b': out['m_ln_ffn_b'], 'm_ffn_w_gate': out['m_ffn_w_gate'], 'm_ffn_w_up': out['m_ffn_w_up'], 'm_ffn_w_down': out['m_ffn_w_down'], 'm_mla_w_in': out['m_mla_w_in'], 'm_mla_q_norm': out['m_mla_q_norm'], 'm_mla_w_q_b': out['m_mla_w_q_b'], 'm_mla_kv_norm': out['m_mla_kv_norm'], 'm_mla_w_kv_b': out['m_mla_w_kv_b'], 'm_mla_w_o': out['m_mla_w_o'], 'm_swa_w_qkv': out['m_swa_w_qkv'], 'm_swa_b_qkv': out['m_swa_b_qkv'], 'm_swa_sinks': out['m_swa_sinks'], 'm_swa_w_o': out['m_swa_w_o'], 'm_swa_b_o': out['m_swa_b_o'], 'v_ada_w': out['v_ada_w'], 'v_ada_b': out['v_ada_b'], 'v_ln_mix_g': out['v_ln_mix_g'], 'v_ln_mix_b': out['v_ln_mix_b'], 'v_ln_ffn_g': out['v_ln_ffn_g'], 'v_ln_ffn_b': out['v_ln_ffn_b'], 'v_ffn_w_gate': out['v_ffn_w_gate'], 'v_ffn_w_up': out['v_ffn_w_up'], 'v_ffn_w_down': out['v_ffn_w_down'], 'v_mla_w_in': out['v_mla_w_in'], 'v_mla_q_norm': out['v_mla_q_norm'], 'v_mla_w_q_b': out['v_mla_w_q_b'], 'v_mla_kv_norm': out['v_mla_kv_norm'], 'v_mla_w_kv_b': out['v_mla_w_kv_b'], 'v_mla_w_o': out['v_mla_w_o'], 'v_swa_w_qkv': out['v_swa_w_qkv'], 'v_swa_b_qkv': out['v_swa_b_qkv'], 'v_swa_sinks': out['v_swa_sinks'], 'v_swa_w_o': out['v_swa_w_o'], 'v_swa_b_o': out['v_swa_b_o']}


def _loss(weights, diff, rest, loss_target):
    with _jax.named_scope("forward"):
        args = {**rest, TWIN_DIFF_INPUT: diff, **{k: w.astype(_WEIGHT_DTYPES[k]) for k, w in weights.items()}}
        y = _forward(args)
    with _jax.named_scope("loss_head"):
        err = _jnp.square(y.astype(_jnp.float32) - loss_target)
        return 0.5 * _jnp.sum(_jnp.mean(err, axis=-1)) if err.ndim else 0.5 * err


def _adamw(w, g, m, v):
    m = ADAM_B1 * m + (1.0 - ADAM_B1) * g
    v = ADAM_B2 * v + (1.0 - ADAM_B2) * _jnp.square(g)
    m_hat = m / (1.0 - ADAM_B1 ** ADAM_STEP)
    v_hat = v / (1.0 - ADAM_B2 ** ADAM_STEP)
    delta = -ADAM_LR * (m_hat / (_jnp.sqrt(v_hat) + ADAM_EPS) + ADAM_WD * w)
    return delta, m, v


def reference(x, c, positions, ada_w, ada_b, ln_mix_g, ln_mix_b, ln_ffn_g, ln_ffn_b, ffn_w_gate, ffn_w_up, ffn_w_down, mla_w_in, mla_q_norm, mla_w_q_b, mla_kv_norm, mla_w_kv_b, mla_w_o, swa_w_qkv, swa_b_qkv, swa_sinks, swa_w_o, swa_b_o, loss_target, m_ada_w, m_ada_b, m_ln_mix_g, m_ln_mix_b, m_ln_ffn_g, m_ln_ffn_b, m_ffn_w_gate, m_ffn_w_up, m_ffn_w_down, m_mla_w_in, m_mla_q_norm, m_mla_w_q_b, m_mla_kv_norm, m_mla_w_kv_b, m_mla_w_o, m_swa_w_qkv, m_swa_b_qkv, m_swa_sinks, m_swa_w_o, m_swa_b_o, v_ada_w, v_ada_b, v_ln_mix_g, v_ln_mix_b, v_ln_ffn_g, v_ln_ffn_b, v_ffn_w_gate, v_ffn_w_up, v_ffn_w_down, v_mla_w_in, v_mla_q_norm, v_mla_w_q_b, v_mla_kv_norm, v_mla_w_kv_b, v_mla_w_o, v_swa_w_qkv, v_swa_b_qkv, v_swa_sinks, v_swa_w_o, v_swa_b_o):
    given = dict(x=x, c=c, positions=positions, ada_w=ada_w, ada_b=ada_b, ln_mix_g=ln_mix_g, ln_mix_b=ln_mix_b, ln_ffn_g=ln_ffn_g, ln_ffn_b=ln_ffn_b, ffn_w_gate=ffn_w_gate, ffn_w_up=ffn_w_up, ffn_w_down=ffn_w_down, mla_w_in=mla_w_in, mla_q_norm=mla_q_norm, mla_w_q_b=mla_w_q_b, mla_kv_norm=mla_kv_norm, mla_w_kv_b=mla_w_kv_b, mla_w_o=mla_w_o, swa_w_qkv=swa_w_qkv, swa_b_qkv=swa_b_qkv, swa_sinks=swa_sinks, swa_w_o=swa_w_o, swa_b_o=swa_b_o, loss_target=loss_target, m_ada_w=m_ada_w, m_ada_b=m_ada_b, m_ln_mix_g=m_ln_mix_g, m_ln_mix_b=m_ln_mix_b, m_ln_ffn_g=m_ln_ffn_g, m_ln_ffn_b=m_ln_ffn_b, m_ffn_w_gate=m_ffn_w_gate, m_ffn_w_up=m_ffn_w_up, m_ffn_w_down=m_ffn_w_down, m_mla_w_in=m_mla_w_in, m_mla_q_norm=m_mla_q_norm, m_mla_w_q_b=m_mla_w_q_b, m_mla_kv_norm=m_mla_kv_norm, m_mla_w_kv_b=m_mla_w_kv_b, m_mla_w_o=m_mla_w_o, m_swa_w_qkv=m_swa_w_qkv, m_swa_b_qkv=m_swa_b_qkv, m_swa_sinks=m_swa_sinks, m_swa_w_o=m_swa_w_o, m_swa_b_o=m_swa_b_o, v_ada_w=v_ada_w, v_ada_b=v_ada_b, v_ln_mix_g=v_ln_mix_g, v_ln_mix_b=v_ln_mix_b, v_ln_ffn_g=v_ln_ffn_g, v_ln_ffn_b=v_ln_ffn_b, v_ffn_w_gate=v_ffn_w_gate, v_ffn_w_up=v_ffn_w_up, v_ffn_w_down=v_ffn_w_down, v_mla_w_in=v_mla_w_in, v_mla_q_norm=v_mla_q_norm, v_mla_w_q_b=v_mla_w_q_b, v_mla_kv_norm=v_mla_kv_norm, v_mla_w_kv_b=v_mla_w_kv_b, v_mla_w_o=v_mla_w_o, v_swa_w_qkv=v_swa_w_qkv, v_swa_b_qkv=v_swa_b_qkv, v_swa_sinks=v_swa_sinks, v_swa_w_o=v_swa_w_o, v_swa_b_o=v_swa_b_o)
    weights = {n: given[n] for n in TWIN_WEIGHTS}
    shared = {n: given[n] for n in SHARED_INPUTS}
    per_example = {n: given[n] for n in ['x', 'c', 'positions']}
    grad_fn = _jax.value_and_grad(_loss, argnums=(0, 1))

    def one_microbatch(ex, loss_target):
        ex = dict(ex)
        diff = ex.pop(TWIN_DIFF_INPUT)
        return grad_fn(weights, diff, {**shared, **ex}, loss_target)

    if N_MICROBATCH == 1:
        loss, (grad_w, grad_x) = one_microbatch(per_example, given["loss_target"])
    else:
        def body(carry, xs):
            loss_sum, grad_sum = carry
            l_k, (gw_k, gx_k) = one_microbatch(xs[0], xs[1])
            with _jax.named_scope("update"):
                return (loss_sum + l_k, _jax.tree.map(_jnp.add, grad_sum, gw_k)), gx_k

        init = (_jnp.zeros((), _jnp.float32), _jax.tree.map(_jnp.zeros_like, weights))
        (loss, grad_w), grad_x = _jax.lax.scan(body, init, (per_example, given["loss_target"]))
    with _jax.named_scope("update"):
        delta_w, new_m, new_v = {}, {}, {}
        for n in TWIN_WEIGHTS:
            delta_w[n], new_m[n], new_v[n] = _adamw(weights[n], grad_w[n], given["m_" + n], given["v_" + n])
    return (loss, grad_x, *[grad_w[n] for n in TWIN_WEIGHTS], *[delta_w[n] for n in TWIN_WEIGHTS],
            *[new_m[n] for n in TWIN_WEIGHTS], *[new_v[n] for n in TWIN_WEIGHTS])
```

```python
import functools
import math

import jax
import jax.numpy as jnp
from jax import lax
from jax.experimental import pallas as pl
from jax.experimental.pallas import tpu as pltpu

F32 = jnp.float32
BF = jnp.bfloat16

NDEV = 8
D = 1024
DEPTH = 4
F = 2816
ALPHA = (2 * DEPTH) ** 0.25
LN_EPS = 1e-5
RMS_EPS = 1e-6
ROPE_THETA = 500000.0

MLA_H = 8
MLA_QR = 384
MLA_KVR = 256
MLA_ROPE = 64
MLA_LAT = 768
MLA_HD = 256
MLA_SCALE = (128 + 64) ** -0.5

SWA_HQ = 16
SWA_HKV = 4
SWA_W = 128
SWA_SCALE = 64 ** -0.5
SWA_QKV = (SWA_HQ + 2 * SWA_HKV) * 128
SWA_O = SWA_HQ * 128

LANE = 128
VMEM_LIMIT = 56 * 2 ** 20

ADAM_LR, ADAM_B1, ADAM_B2, ADAM_EPS, ADAM_WD, ADAM_STEP = 0.001, 0.9, 0.999, 1e-8, 0.01, 10


def _params(n_axes):
    return pltpu.CompilerParams(dimension_semantics=("arbitrary",) * n_axes, vmem_limit_bytes=VMEM_LIMIT)


def _dot(a, b):
    return jnp.dot(a, b, preferred_element_type=F32)


def _dot_nt(a, b):
    return lax.dot_general(a, b, (((1,), (1,)), ((), ())), preferred_element_type=F32)


def _dot_tn(a, b):
    return lax.dot_general(a, b, (((0,), (0,)), ((), ())), preferred_element_type=F32)


def _full(shape):
    return pl.BlockSpec(shape, lambda *_: (0,) * len(shape))


def _sigmoid(x):
    return 1.0 / (1.0 + jnp.exp(-x))


def _rope128(x, ct, s1, s2, half):
    return x * ct + pltpu.roll(x, LANE - half, 1) * s1 + pltpu.roll(x, half, 1) * s2


def _eye(n):
    return lax.broadcasted_iota(jnp.int32, (n, n), 0) == lax.broadcasted_iota(jnp.int32, (n, n), 1)


def _col_to_row(col):
    n = col.shape[0]
    return jnp.sum(jnp.where(_eye(n), col, 0.0), axis=0, keepdims=True)


def _row_to_col(row):
    n = row.shape[1]
    return jnp.sum(jnp.where(_eye(n), row, 0.0), axis=1, keepdims=True)


def _modulate(x, modl_ref, sc_row, sh_row):
    return x * (1.0 + modl_ref[sc_row:sc_row + 1, :]) + modl_ref[sh_row:sh_row + 1, :]


def _exchange(src, gather, name):
    blk = tuple(src.shape) if gather else tuple(src.shape[1:])

    def body(src_ref, out_ref, send_sems, recv_sems, local_sem):
        x, y, c = lax.axis_index("x"), lax.axis_index("y"), lax.axis_index("c")
        me = 4 * x + 2 * y + c

        def piece(dev):
            return src_ref if gather else src_ref.at[dev]

        mine = pltpu.make_async_copy(piece(me), out_ref.at[me], local_sem)
        mine.start()
        sends, recvs = [], []
        for k in range(1, NDEV):
            px = 1 - x if k & 4 else x
            py = 1 - y if k & 2 else y
            pc = 1 - c if k & 1 else c
            peer = 4 * px + 2 * py + pc
            common = dict(send_sem=send_sems.at[k - 1], recv_sem=recv_sems.at[k - 1],
                          device_id=(px, py, pc), device_id_type=pl.DeviceIdType.MESH)
            snd = pltpu.make_async_remote_copy(src_ref=piece(peer), dst_ref=out_ref.at[me], **common)
            snd.start()
            sends.append(snd)
            recvs.append(pltpu.make_async_remote_copy(src_ref=piece(peer), dst_ref=out_ref.at[peer], **common))
        for r in recvs:
            r.wait_recv()
        for s in sends:
            s.wait_send()
        mine.wait()

    return pl.pallas_call(
        body, name=name,
        out_shape=jax.ShapeDtypeStruct((NDEV,) + blk, src.dtype),
        in_specs=[pl.BlockSpec(memory_space=pltpu.HBM)],
        out_specs=pl.BlockSpec(memory_space=pltpu.HBM),
        scratch_shapes=[pltpu.SemaphoreType.DMA((NDEV - 1,)), pltpu.SemaphoreType.DMA((NDEV - 1,)),
                        pltpu.SemaphoreType.DMA],
    )(src)


def _mod_all(c_all, ada_w, ada_b_loc):
    ncol = ada_w.shape[2]

    def body(c_ref, w_ref, b_ref, o_ref):
        cv = c_ref[...]
        cond = cv * _sigmoid(cv)
        o_ref[0] = _dot(cond.astype(BF), w_ref[0].astype(BF)) + b_ref[0]

    return pl.pallas_call(
        body, name="mod_all", grid=(DEPTH,),
        out_shape=jax.ShapeDtypeStruct((DEPTH, NDEV, ncol), F32),
        in_specs=[_full((NDEV, D)), pl.BlockSpec((1, D, ncol), lambda i: (i, 0, 0)),
                  pl.BlockSpec((1, 1, ncol), lambda i: (i, 0, 0))],
        out_specs=pl.BlockSpec((1, NDEV, ncol), lambda i: (i, 0, 0)),
        compiler_params=_params(1),
    )(c_all, ada_w, ada_b_loc)


def _ada_w_grad(c_all_t, dmod_loc):
    ncol = dmod_loc.shape[2]

    def body(ct_ref, dm_ref, o_ref):
        cv = ct_ref[...]
        cond = cv * _sigmoid(cv)
        acc = cond[:, 0:1] * dm_ref[0, 0:1, :]
        for b in range(1, NDEV):
            acc = acc + cond[:, b:b + 1] * dm_ref[0, b:b + 1, :]
        o_ref[0] = acc

    return pl.pallas_call(
        body, name="ada_w_grad", grid=(DEPTH,),
        out_shape=jax.ShapeDtypeStruct((DEPTH, D, ncol), F32),
        in_specs=[_full((D, NDEV)), pl.BlockSpec((1, NDEV, ncol), lambda i: (i, 0, 0))],
        out_specs=pl.BlockSpec((1, D, ncol), lambda i: (i, 0, 0)),
        compiler_params=_params(1),
    )(c_all_t, dmod_loc)


def _mla_proj_fwd(x, modl, w_in, q_norm, w_q, kv_norm, w_kv, tabs, tm):
    T = x.shape[0]
    ct_a, s1_a, s2_a = tabs

    def body(x_ref, modl_ref, win_ref, qn_ref, wq_ref, kvn_ref, wkv_ref, ct_ref, s1_ref, s2_ref,
             q_ref, k_ref, v_ref):
        h = _modulate(x_ref[...], modl_ref, 1, 0).astype(BF)
        lat = _dot(h, win_ref[...])
        ql, kvl, kr = lat[:, :MLA_QR], lat[:, MLA_QR:MLA_QR + MLA_KVR], lat[:, MLA_QR + MLA_KVR:]
        qn = (ql * lax.rsqrt(jnp.mean(ql * ql, axis=1, keepdims=True) + RMS_EPS) * qn_ref[...]).astype(BF)
        kvn = (kvl * lax.rsqrt(jnp.mean(kvl * kvl, axis=1, keepdims=True) + RMS_EPS) * kvn_ref[...]).astype(BF)
        ct, s1, s2 = ct_ref[...], s1_ref[...], s2_ref[...]
        kr = _rope128(kr, ct, s1, s2, MLA_ROPE // 2).astype(BF)
        for hd in range(MLA_H):
            cols = slice(hd * MLA_HD, (hd + 1) * MLA_HD)
            qh = _dot(qn, wq_ref[:, cols])
            q_ref[hd, :, 0:LANE] = qh[:, :LANE].astype(BF)
            q_ref[hd, :, LANE:MLA_HD] = _rope128(qh[:, LANE:], ct, s1, s2, MLA_ROPE // 2).astype(BF)
            kvh = _dot(kvn, wkv_ref[:, cols])
            k_ref[hd, :, 0:LANE] = kvh[:, :LANE].astype(BF)
            k_ref[hd, :, LANE:MLA_HD] = kr
            v_ref[hd] = kvh[:, LANE:].astype(BF)

    row = lambda i: (i, 0)
    head = lambda i: (0, i, 0)
    return pl.pallas_call(
        body, name="mla_proj_fwd", grid=(T // tm,),
        out_shape=(jax.ShapeDtypeStruct((MLA_H, T, MLA_HD), BF), jax.ShapeDtypeStruct((MLA_H, T, MLA_HD), BF),
                   jax.ShapeDtypeStruct((MLA_H, T, LANE), BF)),
        in_specs=[pl.BlockSpec((tm, D), row), _full((8, D)), _full((D, MLA_LAT)), _full((1, MLA_QR)),
                  _full((MLA_QR, MLA_H * MLA_HD)), _full((1, MLA_KVR)), _full((MLA_KVR, MLA_H * MLA_HD)),
                  pl.BlockSpec((tm, LANE), row), pl.BlockSpec((tm, LANE), row), pl.BlockSpec((tm, LANE), row)],
        out_specs=(pl.BlockSpec((MLA_H, tm, MLA_HD), head), pl.BlockSpec((MLA_H, tm, MLA_HD), head),
                   pl.BlockSpec((MLA_H, tm, LANE), head)),
        compiler_params=_params(1),
    )(x, modl, w_in, q_norm, w_q, kv_norm, w_kv, ct_a, s1_a, s2_a)


def _causal_mask(n, transposed):
    row = lax.broadcasted_iota(jnp.int32, (n, n), 0)
    col = lax.broadcasted_iota(jnp.int32, (n, n), 1)
    return (row <= col) if transposed else (col <= row)


def _flash_fwd(q, k, v, tb):
    H, T, _ = q.shape

    def body(q_ref, k_ref, v_ref, o_ref, lse_ref, m_s, l_s, acc_s):
        i = pl.program_id(1)
        qb = q_ref[0]
        m_s[...] = jnp.full(m_s.shape, -jnp.inf, F32)
        l_s[...] = jnp.zeros(l_s.shape, F32)
        acc_s[...] = jnp.zeros(acc_s.shape, F32)

        def block(j, masked):
            rows = pl.ds(pl.multiple_of(j * tb, tb), tb)
            s = _dot_nt(qb, k_ref[0, rows, :]) * MLA_SCALE
            if masked:
                s = jnp.where(_causal_mask(tb, False), s, -jnp.inf)
            m_prev = m_s[...]
            m_new = jnp.maximum(m_prev, jnp.max(s, axis=1, keepdims=True))
            p = jnp.exp(s - m_new)
            corr = jnp.exp(m_prev - m_new)
            l_s[...] = corr * l_s[...] + jnp.sum(p, axis=1, keepdims=True)
            acc_s[...] = corr * acc_s[...] + _dot(p.astype(BF), v_ref[0, rows, :])
            m_s[...] = m_new

        def step(j, carry):
            block(j, False)
            return carry

        lax.fori_loop(0, i, step, 0)
        block(i, True)
        l = l_s[...]
        o_ref[...] = (acc_s[...] / l).astype(BF)
        lse_ref[0] = _col_to_row(m_s[...] + jnp.log(l))

    return pl.pallas_call(
        body, name="flash_fwd", grid=(H, T // tb),
        out_shape=(jax.ShapeDtypeStruct((T, H * LANE), BF), jax.ShapeDtypeStruct((H, 1, T), F32)),
        in_specs=[pl.BlockSpec((1, tb, MLA_HD), lambda h, i: (h, i, 0)),
                  pl.BlockSpec((1, T, MLA_HD), lambda h, i: (h, 0, 0)),
                  pl.BlockSpec((1, T, LANE), lambda h, i: (h, 0, 0))],
        out_specs=(pl.BlockSpec((tb, LANE), lambda h, i: (i, h)),
                   pl.BlockSpec((1, 1, tb), lambda h, i: (h, 0, i))),
        scratch_shapes=[pltpu.VMEM((tb, 1), F32), pltpu.VMEM((tb, 1), F32), pltpu.VMEM((tb, LANE), F32)],
        compiler_params=_params(2),
    )(q, k, v)


def _outproj_ln_fwd(a, w, bias, x, modl, g_row, ln_g, ln_b, tm, name):
    T, K = a.shape

    def body(a_ref, w_ref, b_ref, x_ref, modl_ref, g_ref, bb_ref, y_ref, u_ref, xn_ref):
        y = _dot(a_ref[...], w_ref[...]) + b_ref[...]
        u = ALPHA * x_ref[...] + modl_ref[g_row:g_row + 1, :] * y
        mu = jnp.mean(u, axis=1, keepdims=True)
        uc = u - mu
        var = jnp.mean(uc * uc, axis=1, keepdims=True)
        y_ref[...] = y.astype(BF)
        u_ref[...] = u
        xn_ref[...] = uc * lax.rsqrt(var + LN_EPS) * g_ref[...] + bb_ref[...]

    row = lambda i: (i, 0)
    return pl.pallas_call(
        body, name=name, grid=(T // tm,),
        out_shape=(jax.ShapeDtypeStruct((T, D), BF), jax.ShapeDtypeStruct((T, D), F32),
                   jax.ShapeDtypeStruct((T, D), F32)),
        in_specs=[pl.BlockSpec((tm, K), row), _full((K, D)), _full((1, D)), pl.BlockSpec((tm, D), row),
                  _full((8, D)), _full((1, D)), _full((1, D))],
        out_specs=(pl.BlockSpec((tm, D), row),) * 3,
        compiler_params=_params(1),
    )(a, w, bias, x, modl, ln_g, ln_b)


def _ffn_up_fwd(x, modl, wg, wu, tm, tn):
    T = x.shape[0]

    def body(x_ref, modl_ref, wg_ref, wu_ref, g_ref, u_ref, a_ref):
        h = _modulate(x_ref[...], modl_ref, 4, 3).astype(BF)
        g = _dot(h, wg_ref[...])
        u = _dot(h, wu_ref[...])
        g_ref[...] = g.astype(BF)
        u_ref[...] = u.astype(BF)
        a_ref[...] = (g * _sigmoid(g) * u).astype(BF)

    tile = pl.BlockSpec((tm, tn), lambda n, i: (i, n))
    wcol = pl.BlockSpec((D, tn), lambda n, i: (0, n))
    return pl.pallas_call(
        body, name="ffn_up_fwd", grid=(F // tn, T // tm),
        out_shape=(jax.ShapeDtypeStruct((T, F), BF),) * 3,
        in_specs=[pl.BlockSpec((tm, D), lambda n, i: (i, 0)), _full((8, D)), wcol, wcol],
        out_specs=(tile, tile, tile),
        compiler_params=_params(2),
    )(x, modl, wg, wu)


def _swa_proj_fwd(x, modl, w, b, tabs, tm):
    T = x.shape[0]
    ct_b, s1_b, s2_b = tabs
    n_rope = SWA_HQ + SWA_HKV

    def body(x_ref, modl_ref, w_ref, b_ref, ct_ref, s1_ref, s2_ref, o_ref):
        h = _modulate(x_ref[...], modl_ref, 1, 0).astype(BF)
        ct, s1, s2 = ct_ref[...], s1_ref[...], s2_ref[...]
        for grp in range(SWA_QKV // LANE):
            cols = slice(grp * LANE, (grp + 1) * LANE)
            z = _dot(h, w_ref[:, cols]) + b_ref[:, cols]
            if grp < n_rope:
                z = _rope128(z, ct, s1, s2, 8)
            o_ref[:, cols] = z.astype(BF)

    row = lambda i: (i, 0)
    return pl.pallas_call(
        body, name="swa_proj_fwd", grid=(T // tm,),
        out_shape=jax.ShapeDtypeStruct((T, SWA_QKV), BF),
        in_specs=[pl.BlockSpec((tm, D), row), _full((8, D)), _full((D, SWA_QKV)), _full((1, SWA_QKV)),
                  pl.BlockSpec((tm, LANE), row), pl.BlockSpec((tm, LANE), row), pl.BlockSpec((tm, LANE), row)],
        out_specs=pl.BlockSpec((tm, SWA_QKV), row),
        compiler_params=_params(1),
    )(x, modl, w, b, ct_b, s1_b, s2_b)


def _swa_mask(first_has_prev):
    r = lax.broadcasted_iota(jnp.int32, (4 * SWA_W, 2 * SWA_W), 0) % SWA_W
    c = lax.broadcasted_iota(jnp.int32, (4 * SWA_W, 2 * SWA_W), 1)
    band = (c > r) & (c <= r + SWA_W)
    if first_has_prev is not None:
        band = band & ((c >= SWA_W) | first_has_prev)
    return band


def _swa_specs(T, tb):
    nsub = tb // SWA_W
    q_spec = pl.BlockSpec((tb, SWA_O), lambda i: (i, 0))
    kvc_spec = pl.BlockSpec((tb, 2 * SWA_HKV * LANE), lambda i: (i, 2))
    kvp_spec = pl.BlockSpec((SWA_W, 2 * SWA_HKV * LANE), lambda i: (jnp.maximum(i * nsub - 1, 0), 2))
    return q_spec, kvc_spec, kvp_spec


def _swa_softmax(q_ref, kall, sink_ref, g, b, i):
    q4 = jnp.concatenate([q_ref[b * SWA_W:(b + 1) * SWA_W, (4 * g + hh) * LANE:(4 * g + hh + 1) * LANE]
                          for hh in range(4)], axis=0)
    k2 = kall[b * SWA_W:(b + 2) * SWA_W]
    s = _dot_nt(q4, k2) * SWA_SCALE
    s = jnp.where(_swa_mask((i > 0) if b == 0 else None), s, -jnp.inf)
    sink = jnp.concatenate([jnp.full((SWA_W, 1), sink_ref[4 * g + hh], F32) for hh in range(4)], axis=0)
    m = jnp.maximum(jnp.max(s, axis=1, keepdims=True), sink)
    e = jnp.exp(s - m)
    es = jnp.exp(sink - m)
    linv = 1.0 / (jnp.sum(e, axis=1, keepdims=True) + es)
    return q4, k2, e * linv, es * linv


def _swa_attn_fwd(qkv, sinks, tb):
    T = qkv.shape[0]
    nsub = tb // SWA_W
    kw = SWA_HKV * LANE

    def body(q_ref, kvc_ref, kvp_ref, sink_ref, o_ref):
        i = pl.program_id(0)
        for g in range(SWA_HKV):
            kall = jnp.concatenate([kvp_ref[:, g * LANE:(g + 1) * LANE], kvc_ref[:, g * LANE:(g + 1) * LANE]], axis=0)
            vall = jnp.concatenate([kvp_ref[:, kw + g * LANE:kw + (g + 1) * LANE],
                                    kvc_ref[:, kw + g * LANE:kw + (g + 1) * LANE]], axis=0)
            for b in range(nsub):
                _, _, p, _ = _swa_softmax(q_ref, kall, sink_ref, g, b, i)
                o4 = _dot(p.astype(BF), vall[b * SWA_W:(b + 2) * SWA_W])
                for hh in range(4):
                    o_ref[b * SWA_W:(b + 1) * SWA_W, (4 * g + hh) * LANE:(4 * g + hh + 1) * LANE] = (
                        o4[hh * SWA_W:(hh + 1) * SWA_W].astype(BF))

    q_spec, kvc_spec, kvp_spec = _swa_specs(T, tb)
    return pl.pallas_call(
        body, name="swa_attn_fwd", grid=(T // tb,),
        out_shape=jax.ShapeDtypeStruct((T, SWA_O), BF),
        in_specs=[q_spec, kvc_spec, kvp_spec, pl.BlockSpec(memory_space=pltpu.SMEM)],
        out_specs=pl.BlockSpec((tb, SWA_O), lambda i: (i, 0)),
        compiler_params=_params(1),
    )(qkv, qkv, qkv, sinks)


def _loss_grad(x, tgt, tm):
    T = x.shape[0]

    def body(x_ref, t_ref, dx_ref, l_ref):
        @pl.when(pl.program_id(0) == 0)
        def _():
            l_ref[...] = jnp.zeros(l_ref.shape, F32)
        diff = x_ref[...] - t_ref[...]
        dx_ref[...] = diff * (1.0 / D)
        l_ref[0:1, :] += jnp.sum(diff * diff, axis=0, keepdims=True) * (0.5 / D)

    row = lambda i: (i, 0)
    return pl.pallas_call(
        body, name="loss_grad", grid=(T // tm,),
        out_shape=(jax.ShapeDtypeStruct((T, D), F32), jax.ShapeDtypeStruct((8, D), F32)),
        in_specs=[pl.BlockSpec((tm, D), row), pl.BlockSpec((tm, D), row)],
        out_specs=(pl.BlockSpec((tm, D), row), _full((8, D))),
        compiler_params=_params(1),
    )(x, tgt)


def _outproj_ln_bwd(dxn, u, y, w, modl, g_row, ln_g, tm, name):
    T = dxn.shape[0]
    K = w.shape[0]

    def body(dxn_ref, u_ref, y_ref, w_ref, modl_ref, g_ref, dres_ref, dy_ref, da_ref, sm_ref):
        @pl.when(pl.program_id(0) == 0)
        def _():
            sm_ref[...] = jnp.zeros(sm_ref.shape, F32)
        uu = u_ref[...]
        mu = jnp.mean(uu, axis=1, keepdims=True)
        uc = uu - mu
        rstd = lax.rsqrt(jnp.mean(uc * uc, axis=1, keepdims=True) + LN_EPS)
        xhat = uc * rstd
        dxo = dxn_ref[...]
        dyh = dxo * g_ref[...]
        du = rstd * (dyh - jnp.mean(dyh, axis=1, keepdims=True)
                     - xhat * jnp.mean(dyh * xhat, axis=1, keepdims=True))
        dy = modl_ref[g_row:g_row + 1, :] * du
        dyb = dy.astype(BF)
        dres_ref[...] = ALPHA * du
        dy_ref[...] = dyb
        da_ref[...] = _dot_nt(dyb, w_ref[...]).astype(BF)
        sm_ref[0:1, :] += jnp.sum(dxo * xhat, axis=0, keepdims=True)
        sm_ref[1:2, :] += jnp.sum(dxo, axis=0, keepdims=True)
        sm_ref[2:3, :] += jnp.sum(du * y_ref[...].astype(F32), axis=0, keepdims=True)
        sm_ref[3:4, :] += jnp.sum(dy, axis=0, keepdims=True)

    row = lambda i: (i, 0)
    return pl.pallas_call(
        body, name=name, grid=(T // tm,),
        out_shape=(jax.ShapeDtypeStruct((T, D), F32), jax.ShapeDtypeStruct((T, D), BF),
                   jax.ShapeDtypeStruct((T, K), BF), jax.ShapeDtypeStruct((8, D), F32)),
        in_specs=[pl.BlockSpec((tm, D), row), pl.BlockSpec((tm, D), row), pl.BlockSpec((tm, D), row),
                  _full((K, D)), _full((8, D)), _full((1, D))],
        out_specs=(pl.BlockSpec((tm, D), row), pl.BlockSpec((tm, D), row), pl.BlockSpec((tm, K), row),
                   _full((8, D))),
        compiler_params=_params(1),
    )(dxn, u, y, w, modl, ln_g)


def _ffn_mid_bwd(da, g, u, x, modl, dres, wg, wu, tm, tn):
    T = x.shape[0]
    nn = F // tn

    def body(da_ref, g_ref, u_ref, x_ref, modl_ref, dres_ref, wg_ref, wu_ref, dg_ref, du_ref, dx_ref, sm_ref):
        i, n = pl.program_id(0), pl.program_id(1)

        @pl.when((i == 0) & (n == 0))
        def _():
            sm_ref[...] = jnp.zeros(sm_ref.shape, F32)

        gg = g_ref[...].astype(F32)
        sg = _sigmoid(gg)
        dav = da_ref[...].astype(F32)
        dgp = (dav * u_ref[...].astype(F32) * sg * (1.0 + gg * (1.0 - sg))).astype(BF)
        dup = (dav * gg * sg).astype(BF)
        dg_ref[...] = dgp
        du_ref[...] = dup
        dh = _dot_nt(dgp, wg_ref[...]) + _dot_nt(dup, wu_ref[...])

        @pl.when(n == 0)
        def _():
            dx_ref[...] = dh

        @pl.when(n > 0)
        def _():
            dx_ref[...] += dh

        @pl.when(n == nn - 1)
        def _():
            dht = dx_ref[...]
            sm_ref[0:1, :] += jnp.sum(dht * x_ref[...], axis=0, keepdims=True)
            sm_ref[1:2, :] += jnp.sum(dht, axis=0, keepdims=True)
            dx_ref[...] = dres_ref[...] + dht * (1.0 + modl_ref[4:5, :])

    tile = pl.BlockSpec((tm, tn), lambda i, n: (i, n))
    rowd = pl.BlockSpec((tm, D), lambda i, n: (i, 0))
    wcol = pl.BlockSpec((D, tn), lambda i, n: (0, n))
    return pl.pallas_call(
        body, name="ffn_mid_bwd", grid=(T // tm, nn),
        out_shape=(jax.ShapeDtypeStruct((T, F), BF), jax.ShapeDtypeStruct((T, F), BF),
                   jax.ShapeDtypeStruct((T, D), F32), jax.ShapeDtypeStruct((8, D), F32)),
        in_specs=[tile, tile, tile, rowd, _full((8, D)), rowd, wcol, wcol],
        out_specs=(tile, tile, rowd, _full((8, D))),
        compiler_params=_params(2),
    )(da, g, u, x, modl, dres, wg, wu)


def _wgrad(a, b, tm, tk, tn, name, modl=None, rows=None):
    T, K = a.shape
    N = b.shape[1]

    def body(*refs):
        if modl is None:
            a_ref, b_ref, o_ref = refs
            av = a_ref[...]
        else:
            a_ref, modl_ref, b_ref, o_ref = refs
            av = _modulate(a_ref[...], modl_ref, rows[0], rows[1]).astype(BF)

        @pl.when(pl.program_id(2) == 0)
        def _():
            o_ref[...] = jnp.zeros(o_ref.shape, F32)
        o_ref[...] += _dot_tn(av, b_ref[...])

    in_specs = [pl.BlockSpec((tm, tk), lambda k, n, t: (t, k))]
    args = [a]
    if modl is not None:
        in_specs.append(_full((8, D)))
        args.append(modl)
    in_specs.append(pl.BlockSpec((tm, tn), lambda k, n, t: (t, n)))
    args.append(b)
    return pl.pallas_call(
        body, name=name, grid=(K // tk, N // tn, T // tm),
        out_shape=jax.ShapeDtypeStruct((K, N), F32),
        in_specs=in_specs,
        out_specs=pl.BlockSpec((tk, tn), lambda k, n, t: (k, n)),
        compiler_params=_params(3),
    )(*args)


def _flash_dq(q, k, v, o, do, lse, tb):
    H, T, _ = q.shape

    def body(q_ref, k_ref, v_ref, o_ref, do_ref, lse_ref, dq_ref, delta_ref, dq_s):
        i = pl.program_id(1)
        qb = q_ref[0]
        dob = do_ref[...]
        delta = jnp.sum(o_ref[...].astype(F32) * dob.astype(F32), axis=1, keepdims=True)
        lse = _row_to_col(lse_ref[0])
        dq_s[...] = jnp.zeros(dq_s.shape, F32)

        def block(j, masked):
            rows = pl.ds(pl.multiple_of(j * tb, tb), tb)
            kb = k_ref[0, rows, :]
            s = _dot_nt(qb, kb) * MLA_SCALE
            if masked:
                s = jnp.where(_causal_mask(tb, False), s, -jnp.inf)
            p = jnp.exp(s - lse)
            dp = _dot_nt(dob, v_ref[0, rows, :])
            ds = (p * (dp - delta) * MLA_SCALE).astype(BF)
            dq_s[...] += _dot(ds, kb)

        def step(j, carry):
            block(j, False)
            return carry

        lax.fori_loop(0, i, step, 0)
        block(i, True)
        dq_ref[0] = dq_s[...].astype(BF)
        delta_ref[0] = _col_to_row(delta)

    return pl.pallas_call(
        body, name="flash_dq", grid=(H, T // tb),
        out_shape=(jax.ShapeDtypeStruct((H, T, MLA_HD), BF), jax.ShapeDtypeStruct((H, 1, T), F32)),
        in_specs=[pl.BlockSpec((1, tb, MLA_HD), lambda h, i: (h, i, 0)),
                  pl.BlockSpec((1, T, MLA_HD), lambda h, i: (h, 0, 0)),
                  pl.BlockSpec((1, T, LANE), lambda h, i: (h, 0, 0)),
                  pl.BlockSpec((tb, LANE), lambda h, i: (i, h)),
                  pl.BlockSpec((tb, LANE), lambda h, i: (i, h)),
                  pl.BlockSpec((1, 1, tb), lambda h, i: (h, 0, i))],
        out_specs=(pl.BlockSpec((1, tb, MLA_HD), lambda h, i: (h, i, 0)),
                   pl.BlockSpec((1, 1, tb), lambda h, i: (h, 0, i))),
        scratch_shapes=[pltpu.VMEM((tb, MLA_HD), F32)],
        compiler_params=_params(2),
    )(q, k, v, o, do, lse)


def _flash_dkv(q, k, v, do, lse, delta, tb):
    H, T, _ = q.shape
    nq = T // tb

    def body(k_ref, v_ref, q_ref, do_ref, lse_ref, delta_ref, dk_ref, dv_ref, dk_s, dv_s):
        j = pl.program_id(1)
        kb = k_ref[0]
        vb = v_ref[0]
        dk_s[...] = jnp.zeros(dk_s.shape, F32)
        dv_s[...] = jnp.zeros(dv_s.shape, F32)

        def block(i, masked):
            start = pl.multiple_of(i * tb, tb)
            qb = q_ref[0, pl.ds(start, tb), :]
            dob = do_ref[pl.ds(start, tb), :]
            st = _dot_nt(kb, qb) * MLA_SCALE
            if masked:
                st = jnp.where(_causal_mask(tb, True), st, -jnp.inf)
            pt = jnp.exp(st - lse_ref[0, :, pl.ds(start, tb)])
            dpt = _dot_nt(vb, dob)
            dst = (pt * (dpt - delta_ref[0, :, pl.ds(start, tb)]) * MLA_SCALE).astype(BF)
            dv_s[...] += _dot(pt.astype(BF), dob)
            dk_s[...] += _dot(dst, qb)

        def step(i, carry):
            block(i, False)
            return carry

        block(j, True)
        lax.fori_loop(j + 1, nq, step, 0)
        dk_ref[0] = dk_s[...].astype(BF)
        dv_ref[0] = dv_s[...].astype(BF)

    return pl.pallas_call(
        body, name="flash_dkv", grid=(H, nq),
        out_shape=(jax.ShapeDtypeStruct((H, T, MLA_HD), BF), jax.ShapeDtypeStruct((H, T, LANE), BF)),
        in_specs=[pl.BlockSpec((1, tb, MLA_HD), lambda h, j: (h, j, 0)),
                  pl.BlockSpec((1, tb, LANE), lambda h, j: (h, j, 0)),
                  pl.BlockSpec((1, T, MLA_HD), lambda h, j: (h, 0, 0)),
                  pl.BlockSpec((T, LANE), lambda h, j: (0, h)),
                  pl.BlockSpec((1, 1, T), lambda h, j: (h, 0, 0)),
                  pl.BlockSpec((1, 1, T), lambda h, j: (h, 0, 0))],
        out_specs=(pl.BlockSpec((1, tb, MLA_HD), lambda h, j: (h, j, 0)),
                   pl.BlockSpec((1, tb, LANE), lambda h, j: (h, j, 0))),
        scratch_shapes=[pltpu.VMEM((tb, MLA_HD), F32), pltpu.VMEM((tb, LANE), F32)],
        compiler_params=_params(2),
    )(k, v, q, do, lse, delta)


def _mla_proj_bwd(dq, dk, dv, x, modl, dres, w_in, q_norm, w_q, kv_norm, w_kv, tabs_neg, tm):
    T = x.shape[0]
    ct_a, s1_n, s2_n = tabs_neg

    def body(dq_ref, dk_ref, dv_ref, x_ref, modl_ref, dres_ref, win_ref, qn_ref, wq_ref, kvn_ref, wkv_ref,
             ct_ref, s1_ref, s2_ref, dx_ref, dwin_ref, dwq_ref, dwkv_ref, sm_ref, dqn_ref, dkvn_ref):
        @pl.when(pl.program_id(0) == 0)
        def _():
            for r in (dwin_ref, dwq_ref, dwkv_ref, sm_ref, dqn_ref, dkvn_ref):
                r[...] = jnp.zeros(r.shape, F32)

        xv = x_ref[...]
        h = _modulate(xv, modl_ref, 1, 0).astype(BF)
        lat = _dot(h, win_ref[...])
        ql, kvl = lat[:, :MLA_QR], lat[:, MLA_QR:MLA_QR + MLA_KVR]
        qhat = ql * lax.rsqrt(jnp.mean(ql * ql, axis=1, keepdims=True) + RMS_EPS)
        kvhat = kvl * lax.rsqrt(jnp.mean(kvl * kvl, axis=1, keepdims=True) + RMS_EPS)
        rq = lax.rsqrt(jnp.mean(ql * ql, axis=1, keepdims=True) + RMS_EPS)
        rkv = lax.rsqrt(jnp.mean(kvl * kvl, axis=1, keepdims=True) + RMS_EPS)
        qn = (qhat * qn_ref[...]).astype(BF)
        kvn = (kvhat * kvn_ref[...]).astype(BF)
        ct, s1, s2 = ct_ref[...], s1_ref[...], s2_ref[...]

        dqn = jnp.zeros((tm, MLA_QR), F32)
        dkvn = jnp.zeros((tm, MLA_KVR), F32)
        dkr = jnp.zeros((tm, LANE), F32)
        for hd in range(MLA_H):
            cols = slice(hd * MLA_HD, (hd + 1) * MLA_HD)
            dqh = dq_ref[hd]
            dqr = _rope128(dqh[:, LANE:].astype(F32), ct, s1, s2, MLA_ROPE // 2).astype(BF)
            dqh = jnp.concatenate([dqh[:, :LANE], dqr], axis=1)
            dqn = dqn + _dot_nt(dqh, wq_ref[:, cols])
            dwq_ref[:, cols] += _dot_tn(qn, dqh)
            dkh = dk_ref[hd]
            dkr = dkr + dkh[:, LANE:].astype(F32)
            dkvh = jnp.concatenate([dkh[:, :LANE], dv_ref[hd]], axis=1)
            dkvn = dkvn + _dot_nt(dkvh, wkv_ref[:, cols])
            dwkv_ref[:, cols] += _dot_tn(kvn, dkvh)
        dkr = _rope128(dkr, ct, s1, s2, MLA_ROPE // 2)

        dqn_ref[...] += jnp.sum(dqn * qhat, axis=0, keepdims=True)
        dkvn_ref[...] += jnp.sum(dkvn * kvhat, axis=0, keepdims=True)
        dqh_ = dqn * qn_ref[...]
        dkvh_ = dkvn * kvn_ref[...]
        dql = rq * (dqh_ - qhat * jnp.mean(dqh_ * qhat, axis=1, keepdims=True))
        dkvl = rkv * (dkvh_ - kvhat * jnp.mean(dkvh_ * kvhat, axis=1, keepdims=True))
        dlat = jnp.concatenate([dql, dkvl, dkr], axis=1).astype(BF)
        dwin_ref[...] += _dot_tn(h, dlat)
        dh = _dot_nt(dlat, win_ref[...])
        sm_ref[0:1, :] += jnp.sum(dh * xv, axis=0, keepdims=True)
        sm_ref[1:2, :] += jnp.sum(dh, axis=0, keepdims=True)
        dx_ref[...] = dres_ref[...] + dh * (1.0 + modl_ref[1:2, :])

    row = lambda i: (i, 0)
    head = lambda i: (0, i, 0)
    nq = MLA_H * MLA_HD
    return pl.pallas_call(
        body, name="mla_proj_bwd", grid=(T // tm,),
        out_shape=(jax.ShapeDtypeStruct((T, D), F32), jax.ShapeDtypeStruct((D, MLA_LAT), F32),
                   jax.ShapeDtypeStruct((MLA_QR, nq), F32), jax.ShapeDtypeStruct((MLA_KVR, nq), F32),
                   jax.ShapeDtypeStruct((8, D), F32), jax.ShapeDtypeStruct((1, MLA_QR), F32),
                   jax.ShapeDtypeStruct((1, MLA_KVR), F32)),
        in_specs=[pl.BlockSpec((MLA_H, tm, MLA_HD), head), pl.BlockSpec((MLA_H, tm, MLA_HD), head),
                  pl.BlockSpec((MLA_H, tm, LANE), head), pl.BlockSpec((tm, D), row), _full((8, D)),
                  pl.BlockSpec((tm, D), row), _full((D, MLA_LAT)), _full((1, MLA_QR)), _full((MLA_QR, nq)),
                  _full((1, MLA_KVR)), _full((MLA_KVR, nq)),
                  pl.BlockSpec((tm, LANE), row), pl.BlockSpec((tm, LANE), row), pl.BlockSpec((tm, LANE), row)],
        out_specs=(pl.BlockSpec((tm, D), row), _full((D, MLA_LAT)), _full((MLA_QR, nq)), _full((MLA_KVR, nq)),
                   _full((8, D)), _full((1, MLA_QR)), _full((1, MLA_KVR))),
        compiler_params=_params(1),
    )(dq, dk, dv, x, modl, dres, w_in, q_norm, w_q, kv_norm, w_kv, ct_a, s1_n, s2_n)


def _swa_attn_bwd(qkv, sinks, do, tb):
    T = qkv.shape[0]
    nsub = tb // SWA_W
    kw = SWA_HKV * LANE
    nstep = T // tb

    def body(q_ref, kvc_ref, kvp_ref, sink_ref, do_ref, dq_ref, dkvc_ref, dkvp_ref, dsink_ref, dk_s, dv_s):
        i = pl.program_id(0)

        @pl.when(i == 0)
        def _():
            dsink_ref[...] = jnp.zeros(dsink_ref.shape, F32)

        dk_s[...] = jnp.zeros(dk_s.shape, F32)
        dv_s[...] = jnp.zeros(dv_s.shape, F32)
        lane = lax.broadcasted_iota(jnp.int32, (1, LANE), 1)
        for g in range(SWA_HKV):
            gl = slice(g * LANE, (g + 1) * LANE)
            kall = jnp.concatenate([kvp_ref[:, gl], kvc_ref[:, gl]], axis=0)
            vall = jnp.concatenate([kvp_ref[:, kw + g * LANE:kw + (g + 1) * LANE],
                                    kvc_ref[:, kw + g * LANE:kw + (g + 1) * LANE]], axis=0)
            for b in range(nsub):
                q4, k2, p, psink = _swa_softmax(q_ref, kall, sink_ref, g, b, i)
                v2 = vall[b * SWA_W:(b + 2) * SWA_W]
                do4 = jnp.concatenate([do_ref[b * SWA_W:(b + 1) * SWA_W, (4 * g + hh) * LANE:(4 * g + hh + 1) * LANE]
                                       for hh in range(4)], axis=0)
                dp = _dot_nt(do4, v2)
                delta = jnp.sum(dp * p, axis=1, keepdims=True)
                ds = (p * (dp - delta) * SWA_SCALE).astype(BF)
                dq4 = _dot(ds, k2)
                rows2 = slice(b * SWA_W, (b + 2) * SWA_W)
                dk_s[rows2, gl] += _dot_tn(ds, q4)
                dv_s[rows2, gl] += _dot_tn(p.astype(BF), do4)
                dsk = psink * delta
                for hh in range(4):
                    hq = 4 * g + hh
                    dq_ref[b * SWA_W:(b + 1) * SWA_W, hq * LANE:(hq + 1) * LANE] = (
                        dq4[hh * SWA_W:(hh + 1) * SWA_W].astype(BF))
                    tot = jnp.sum(dsk[hh * SWA_W:(hh + 1) * SWA_W], axis=0, keepdims=True)
                    dsink_ref[0:1, :] -= jnp.where(lane == hq, tot, 0.0)
        dkvp_ref[0, :, 0:kw] = dk_s[0:SWA_W, :]
        dkvp_ref[0, :, kw:2 * kw] = dv_s[0:SWA_W, :]
        dkvc_ref[:, 0:kw] = dk_s[SWA_W:, :]
        dkvc_ref[:, kw:2 * kw] = dv_s[SWA_W:, :]

    q_spec, kvc_spec, kvp_spec = _swa_specs(T, tb)
    return pl.pallas_call(
        body, name="swa_attn_bwd", grid=(nstep,),
        out_shape=(jax.ShapeDtypeStruct((T, SWA_O), BF), jax.ShapeDtypeStruct((T, 2 * kw), F32),
                   jax.ShapeDtypeStruct((nstep, SWA_W, 2 * kw), F32), jax.ShapeDtypeStruct((8, LANE), F32)),
        in_specs=[q_spec, kvc_spec, kvp_spec, pl.BlockSpec(memory_space=pltpu.SMEM),
                  pl.BlockSpec((tb, SWA_O), lambda i: (i, 0))],
        out_specs=(pl.BlockSpec((tb, SWA_O), lambda i: (i, 0)), pl.BlockSpec((tb, 2 * kw), lambda i: (i, 0)),
                   pl.BlockSpec((1, SWA_W, 2 * kw), lambda i: (i, 0, 0)), _full((8, LANE))),
        scratch_shapes=[pltpu.VMEM((tb + SWA_W, kw), F32), pltpu.VMEM((tb + SWA_W, kw), F32)],
        compiler_params=_params(1),
    )(qkv, qkv, qkv, sinks, do)


def _swa_proj_bwd(dq, dkvc, dkvp, x, modl, dres, w, tabs_neg, tm):
    T = x.shape[0]
    nstep = T // tm
    kw = SWA_HKV * LANE
    ct_b, s1_n, s2_n = tabs_neg

    def body(dq_ref, dkvc_ref, dkvp_ref, x_ref, modl_ref, dres_ref, w_ref, ct_ref, s1_ref, s2_ref,
             dx_ref, dz_ref, sm_ref, db_ref):
        i = pl.program_id(0)

        @pl.when(i == 0)
        def _():
            sm_ref[...] = jnp.zeros(sm_ref.shape, F32)
            db_ref[...] = jnp.zeros(db_ref.shape, F32)

        ct, s1, s2 = ct_ref[...], s1_ref[...], s2_ref[...]
        has_next = i + 1 < nstep
        for grp in range(SWA_QKV // LANE):
            cols = slice(grp * LANE, (grp + 1) * LANE)
            if grp < SWA_HQ:
                z = dq_ref[:, cols].astype(F32)
            else:
                kc = slice((grp - SWA_HQ) * LANE, (grp - SWA_HQ + 1) * LANE)
                cur = dkvc_ref[:, kc]
                tail = cur[tm - SWA_W:] + jnp.where(has_next, dkvp_ref[0, :, kc], 0.0)
                z = jnp.concatenate([cur[:tm - SWA_W], tail], axis=0)
            if grp < SWA_HQ + SWA_HKV:
                z = _rope128(z, ct, s1, s2, 8)
            db_ref[0:1, cols] += jnp.sum(z, axis=0, keepdims=True)
            dz_ref[:, cols] = z.astype(BF)
        dh = _dot_nt(dz_ref[...], w_ref[...])
        sm_ref[0:1, :] += jnp.sum(dh * x_ref[...], axis=0, keepdims=True)
        sm_ref[1:2, :] += jnp.sum(dh, axis=0, keepdims=True)
        dx_ref[...] = dres_ref[...] + dh * (1.0 + modl_ref[1:2, :])

    row = lambda i: (i, 0)
    return pl.pallas_call(
        body, name="swa_proj_bwd", grid=(nstep,),
        out_shape=(jax.ShapeDtypeStruct((T, D), F32), jax.ShapeDtypeStruct((T, SWA_QKV), BF),
                   jax.ShapeDtypeStruct((8, D), F32), jax.ShapeDtypeStruct((8, SWA_QKV), F32)),
        in_specs=[pl.BlockSpec((tm, SWA_O), row), pl.BlockSpec((tm, 2 * kw), row),
                  pl.BlockSpec((1, SWA_W, 2 * kw), lambda i: (jnp.minimum(i + 1, nstep - 1), 0, 0)),
                  pl.BlockSpec((tm, D), row), _full((8, D)), pl.BlockSpec((tm, D), row), _full((D, SWA_QKV)),
                  pl.BlockSpec((tm, LANE), row), pl.BlockSpec((tm, LANE), row), pl.BlockSpec((tm, LANE), row)],
        out_specs=(pl.BlockSpec((tm, D), row), pl.BlockSpec((tm, SWA_QKV), row), _full((8, D)),
                   _full((8, SWA_QKV))),
        compiler_params=_params(1),
    )(dq, dkvc, dkvp, x, modl, dres, w, ct_b, s1_n, s2_n)


def _adamw(gparts, w, m, v, name):
    P, R, C = gparts.shape
    tr = R
    for cand in (512, 256, 128):
        if R % cand == 0 and R > cand:
            tr = cand
            break
    c1 = 1.0 / (1.0 - ADAM_B1 ** ADAM_STEP)
    c2 = 1.0 / (1.0 - ADAM_B2 ** ADAM_STEP)

    def body(gp_ref, w_ref, m_ref, v_ref, g_ref, d_ref, nm_ref, nv_ref):
        g = gp_ref[0].astype(F32)
        for p in range(1, P):
            g = g + gp_ref[p].astype(F32)
        nm = ADAM_B1 * m_ref[...] + (1.0 - ADAM_B1) * g
        nv = ADAM_B2 * v_ref[...] + (1.0 - ADAM_B2) * (g * g)
        g_ref[...] = g
        nm_ref[...] = nm
        nv_ref[...] = nv
        d_ref[...] = -ADAM_LR * ((nm * c1) / (jnp.sqrt(nv * c2) + ADAM_EPS) + ADAM_WD * w_ref[...])

    blk = pl.BlockSpec((tr, C), lambda i: (i, 0))
    return pl.pallas_call(
        body, name=name, grid=(R // tr,),
        out_shape=(jax.ShapeDtypeStruct((R, C), F32),) * 4,
        in_specs=[pl.BlockSpec((P, tr, C), lambda i: (0, i, 0)), blk, blk, blk],
        out_specs=(blk,) * 4,
        compiler_params=_params(1),
    )(gparts, w, m, v)


PACK_W = 1024

BIG = {
    "ffn_w_gate": ((DEPTH, D, F // NDEV), 2),
    "ffn_w_up": ((DEPTH, D, F // NDEV), 2),
    "ffn_w_down": ((DEPTH, F // NDEV, D), 1),
    "mla_w_in": ((2, D // NDEV, 704), 1),
    "mla_w_q_b": ((2, MLA_QR, 1536 // NDEV), 2),
    "mla_w_kv_b": ((2, MLA_KVR, 2048 // NDEV), 2),
    "mla_w_o": ((2, D // NDEV, D), 1),
    "swa_w_qkv": ((2, D, 1536 // NDEV), 2),
    "swa_w_o": ((2, D // NDEV, D), 1),
}


def _pack_rows(n):
    return -(-n // (16 * PACK_W)) * 16


def _pack_local(blocks):
    parts = []
    for name, (shape, _) in BIG.items():
        n = math.prod(shape)
        flat = blocks[name].astype(BF).reshape(-1)
        parts.append(jnp.pad(flat, (0, _pack_rows(n) * PACK_W - n)).reshape(-1, PACK_W))
    return jnp.concatenate(parts, axis=0)


def _pack_full(full):
    parts = []
    for name, (shape, axis) in BIG.items():
        a = full[name]
        split = a.shape[:axis] + (NDEV, a.shape[axis] // NDEV) + a.shape[axis + 1:]
        a = jnp.moveaxis(a.reshape(split), axis, 0).astype(BF).reshape(NDEV, -1)
        n = math.prod(shape)
        parts.append(jnp.pad(a, ((0, 0), (0, _pack_rows(n) * PACK_W - n))).reshape(NDEV, -1, PACK_W))
    return jnp.concatenate(parts, axis=1)


def _unpack_blocks(packed):
    out, r0 = {}, 0
    for name, (shape, _) in BIG.items():
        n = math.prod(shape)
        rows = _pack_rows(n)
        out[name] = packed[:, r0:r0 + rows].reshape(NDEV, -1)[:, :n].reshape((NDEV,) + shape)
        r0 += rows
    return out


def _unpack_full(packed):
    out = {}
    for name, blk in _unpack_blocks(packed).items():
        axis = BIG[name][1]
        a = jnp.moveaxis(blk, 0, axis)
        out[name] = a.reshape(a.shape[:axis] + (a.shape[axis] * a.shape[axis + 1],) + a.shape[axis + 2:])
    return out


def _pad_heads(a, axis, nheads, width, to):
    shp = a.shape[:axis] + (nheads, width) + a.shape[axis + 1:]
    a = a.reshape(shp)
    pad = [(0, 0)] * a.ndim
    pad[axis + 1] = (0, to - width)
    a = jnp.pad(a, pad)
    return a.reshape(a.shape[:axis] + (nheads * to,) + a.shape[axis + 2:])


def _unpad_heads(a, axis, nheads, width, to):
    shp = a.shape[:axis] + (nheads, to) + a.shape[axis + 1:]
    a = lax.slice_in_dim(a.reshape(shp), 0, width, axis=axis + 1)
    return a.reshape(a.shape[:axis] + (nheads * width,) + a.shape[axis + 2:])


def _swa_pad_cols(a):
    return _pad_heads(a, a.ndim - 1, SWA_HQ + 2 * SWA_HKV, 64, LANE)


def _rope_tables(positions, half):
    rot = 2 * half
    inv = ROPE_THETA ** (-jnp.arange(0, rot, 2, dtype=F32) / rot)
    ang = positions.astype(F32)[:, None] * inv
    cos, sin = jnp.cos(ang), jnp.sin(ang)
    T = positions.shape[0]
    ones = jnp.ones((T, LANE - rot), F32)
    zeros = jnp.zeros((T, LANE - rot), F32)
    zh = jnp.zeros((T, half), F32)
    ct = jnp.concatenate([cos, cos, ones], axis=1)
    s1 = jnp.concatenate([-sin, zh, zeros], axis=1)
    s2 = jnp.concatenate([zh, sin, zeros], axis=1)
    return (ct, s1, s2), (ct, -s1, -s2)


def _small_pack(vecs, rows):
    flat = jnp.concatenate([v.astype(F32).reshape(-1) for v in vecs])
    return jnp.pad(flat, (0, rows * LANE - flat.shape[0])).reshape(rows, LANE)


def _small_unpack(buf, shapes):
    flat = buf.reshape(NDEV, -1)
    out, o = [], 0
    for shp in shapes:
        n = math.prod(shp)
        out.append(flat[:, o:o + n].reshape((NDEV,) + shp))
        o += n
    return out


def _step(inp):
    x = inp["x"][0]
    tgt = inp["loss_target"][0]
    T = x.shape[0]
    tm = min(512, T)
    tmf = min(256, T)
    tb = min(512, T)
    tnf = F // 2
    me = 4 * lax.axis_index("x") + 2 * lax.axis_index("y") + lax.axis_index("c")

    small_in = _small_pack([inp["c"], inp["swa_b_qkv"], inp["swa_b_o"]], 16)
    c_all, bqkv_blk, bo_blk = _small_unpack(_exchange(small_in, True, "gather_small"),
                                            [(D,), (2, 1536 // NDEV), (2, D // NDEV)])
    swa_b_qkv = jnp.moveaxis(bqkv_blk, 0, 1).reshape(2, 1536)
    swa_b_o = jnp.moveaxis(bo_blk, 0, 1).reshape(2, D)

    wfull = _unpack_full(_exchange(_pack_local(inp), True, "gather_weights"))

    ncol = 6 * D // NDEV
    ada_b_loc = lax.dynamic_slice_in_dim(inp["ada_b"], me * ncol, ncol, axis=1)[:, None, :]
    mod_all = _mod_all(c_all, inp["ada_w"], ada_b_loc)
    mod_src = jnp.moveaxis(mod_all, 1, 0).reshape(NDEV, DEPTH * ncol // LANE, LANE)
    mod_got = _exchange(mod_src, False, "scatter_mod").reshape(NDEV, DEPTH, ncol)
    mod = jnp.moveaxis(mod_got, 0, 1).reshape(DEPTH, 6, D)
    modl = jnp.pad(mod, ((0, 0), (0, 2), (0, 0)))

    w_in = jnp.pad(wfull["mla_w_in"], ((0, 0), (0, 0), (0, MLA_LAT - 704)))
    w_q = _pad_heads(wfull["mla_w_q_b"], 2, MLA_H, 192, MLA_HD)
    w_kv = wfull["mla_w_kv_b"]
    w_o_mla = wfull["mla_w_o"]
    w_qkv = _swa_pad_cols(wfull["swa_w_qkv"])
    b_qkv = _swa_pad_cols(swa_b_qkv)
    w_o_swa = _pad_heads(wfull["swa_w_o"], 1, SWA_HQ, 64, LANE)
    zero_bias = jnp.zeros((1, D), F32)

    pos = inp["positions"][0]
    tabs_a, tabs_a_neg = _rope_tables(pos, MLA_ROPE // 2)
    tabs_b, tabs_b_neg = _rope_tables(pos, 8)

    saved = []
    xs = x
    for i in range(DEPTH):
        j = i // 2
        st = {"x0": xs}
        if i % 2 == 0:
            q, k, v = _mla_proj_fwd(xs, modl[i], w_in[j], inp["mla_q_norm"][j][None], w_q[j],
                                    inp["mla_kv_norm"][j][None], w_kv[j], tabs_a, tm)
            o, lse = _flash_fwd(q, k, v, tb)
            st.update(q=q, k=k, v=v, o=o, lse=lse)
            w_o, b_o = w_o_mla[j], zero_bias
        else:
            qkv = _swa_proj_fwd(xs, modl[i], w_qkv[j], b_qkv[j][None], tabs_b, tm)
            o = _swa_attn_fwd(qkv, inp["swa_sinks"][j], tb)
            st.update(qkv=qkv, o=o)
            w_o, b_o = w_o_swa[j], swa_b_o[j][None]
        y, u, xs = _outproj_ln_fwd(o, w_o, b_o, xs, modl[i], 2, inp["ln_mix_g"][i][None],
                                   inp["ln_mix_b"][i][None], tm, f"mix_out_fwd_{i % 2}")
        st.update(y_m=y, u_m=u, x1=xs, w_o=w_o)
        g, up, a = _ffn_up_fwd(xs, modl[i], wfull["ffn_w_gate"][i], wfull["ffn_w_up"][i], tmf, tnf)
        y, u, xs = _outproj_ln_fwd(a, wfull["ffn_w_down"][i], zero_bias, xs, modl[i], 5,
                                   inp["ln_ffn_g"][i][None], inp["ln_ffn_b"][i][None], tmf, "ffn_out_fwd")
        st.update(g=g, up=up, a=a, y_f=y, u_f=u)
        saved.append(st)

    dx, loss_rows = _loss_grad(xs, tgt, tm)
    loss = lax.psum(jnp.sum(loss_rows[0]), ("x", "y", "c"))

    gfull = {n: [None] * (DEPTH if n.startswith("ffn") else 2) for n in BIG}
    dmod = [None] * DEPTH
    g_ln = {n: [None] * DEPTH for n in ("ln_mix_g", "ln_mix_b", "ln_ffn_g", "ln_ffn_b")}
    g_qn, g_kvn, g_sink, g_bqkv, g_bo = [None] * 2, [None] * 2, [None] * 2, [None] * 2, [None] * 2
    for i in reversed(range(DEPTH)):
        j = i // 2
        st = saved[i]
        dres, dy, da, sm = _outproj_ln_bwd(dx, st["u_f"], st["y_f"], wfull["ffn_w_down"][i], modl[i], 5,
                                           inp["ln_ffn_g"][i][None], tmf, "ffn_out_bwd")
        g_ln["ln_ffn_g"][i], g_ln["ln_ffn_b"][i], dg_f = sm[0], sm[1], sm[2]
        gfull["ffn_w_down"][i] = _wgrad(st["a"], dy, tm, F // 2, D, "wgrad_down")
        dgp, dup, dx, sm = _ffn_mid_bwd(da, st["g"], st["up"], st["x1"], modl[i], dres,
                                        wfull["ffn_w_gate"][i], wfull["ffn_w_up"][i], tmf, tnf)
        dsc_f, dsh_f = sm[0], sm[1]
        gfull["ffn_w_gate"][i] = _wgrad(st["x1"], dgp, tm, D, tnf, "wgrad_gate", modl[i], (4, 3))
        gfull["ffn_w_up"][i] = _wgrad(st["x1"], dup, tm, D, tnf, "wgrad_up", modl[i], (4, 3))

        dres, dy, do, sm = _outproj_ln_bwd(dx, st["u_m"], st["y_m"], st["w_o"], modl[i], 2,
                                           inp["ln_mix_g"][i][None], tm, f"mix_out_bwd_{i % 2}")
        g_ln["ln_mix_g"][i], g_ln["ln_mix_b"][i], dg_m = sm[0], sm[1], sm[2]
        if i % 2 == 0:
            gfull["mla_w_o"][j] = _wgrad(st["o"], dy, tm, D, D, "wgrad_mla_o")
            dq, delta = _flash_dq(st["q"], st["k"], st["v"], st["o"], do, st["lse"], tb)
            dk, dv = _flash_dkv(st["q"], st["k"], st["v"], do, st["lse"], delta, tb)
            dx, dwin, dwq, dwkv, sm, dqn, dkvn = _mla_proj_bwd(
                dq, dk, dv, st["x0"], modl[i], dres, w_in[j], inp["mla_q_norm"][j][None], w_q[j],
                inp["mla_kv_norm"][j][None], w_kv[j], tabs_a_neg, tm)
            gfull["mla_w_in"][j] = dwin[:, :704]
            gfull["mla_w_q_b"][j] = _unpad_heads(dwq, 1, MLA_H, 192, MLA_HD)
            gfull["mla_w_kv_b"][j] = dwkv
            g_qn[j], g_kvn[j] = dqn[0], dkvn[0]
        else:
            g_bo[j] = sm[3]
            dwo = _wgrad(st["o"], dy, tm, SWA_O // 2, D, "wgrad_swa_o")
            gfull["swa_w_o"][j] = _unpad_heads(dwo, 0, SWA_HQ, 64, LANE)
            dq, dkvc, dkvp, dsink = _swa_attn_bwd(st["qkv"], inp["swa_sinks"][j], do, tb)
            g_sink[j] = dsink[0, :SWA_HQ]
            dx, dz, sm, db = _swa_proj_bwd(dq, dkvc, dkvp, st["x0"], modl[i], dres, w_qkv[j], tabs_b_neg, tm)
            dwqkv = _wgrad(st["x0"], dz, tm, D, SWA_QKV // 2, "wgrad_swa_qkv", modl[i], (1, 0))
            gfull["swa_w_qkv"][j] = _unpad_heads(dwqkv, 1, SWA_HQ + 2 * SWA_HKV, 64, LANE)
            g_bqkv[j] = _unpad_heads(db[0], 0, SWA_HQ + 2 * SWA_HKV, 64, LANE)
        dmod[i] = jnp.stack([sm[1], sm[0], dg_m, dsh_f, dsc_f, dg_f])
    grad_x = dx[None]

    small_shapes = [(DEPTH, 6 * D), (DEPTH, D), (DEPTH, D), (DEPTH, D), (DEPTH, D), (2, MLA_QR), (2, MLA_KVR),
                    (2, SWA_HQ), (2, 1536), (2, D)]
    small_vals = [jnp.stack(dmod).reshape(DEPTH, 6 * D), jnp.stack(g_ln["ln_mix_g"]), jnp.stack(g_ln["ln_mix_b"]),
                  jnp.stack(g_ln["ln_ffn_g"]), jnp.stack(g_ln["ln_ffn_b"]), jnp.stack(g_qn), jnp.stack(g_kvn),
                  jnp.stack(g_sink), jnp.stack(g_bqkv), jnp.stack(g_bo)]
    nsmall = sum(math.prod(s) for s in small_shapes)
    small_rows = -(-nsmall // (8 * LANE)) * 8
    (dmod_all, p_lmg, p_lmb, p_lfg, p_lfb, p_qn, p_kvn, p_sink, p_bqkv, p_bo) = _small_unpack(
        _exchange(_small_pack(small_vals, small_rows), True, "gather_small_grads"), small_shapes)

    gpacked = _exchange(_pack_full({n: jnp.stack(v) for n, v in gfull.items()}), False, "scatter_grads")
    gparts = _unpack_blocks(gpacked)

    res = {}

    def update(name, parts):
        w = inp[name]
        shp = w.shape
        r2 = (math.prod(shp[:-1]), shp[-1])
        outs = _adamw(parts.reshape((parts.shape[0],) + r2), w.reshape(r2), inp["m_" + name].reshape(r2),
                      inp["v_" + name].reshape(r2), "adamw_" + name)
        res[name] = tuple(o.reshape(shp) for o in outs)

    dmod_loc = lax.dynamic_slice_in_dim(dmod_all, me * ncol, ncol, axis=2)
    g_ada_w = _ada_w_grad(c_all.T, jnp.moveaxis(dmod_loc, 0, 1))
    update("ada_w", g_ada_w[None])
    update("ada_b", dmod_all)
    update("ln_mix_g", p_lmg)
    update("ln_mix_b", p_lmb)
    update("ln_ffn_g", p_lfg)
    update("ln_ffn_b", p_lfb)
    for name in BIG:
        update(name, gparts[name])
    update("mla_q_norm", p_qn)
    update("mla_kv_norm", p_kvn)
    update("swa_sinks", p_sink)
    nb = 1536 // NDEV
    update("swa_b_qkv", lax.dynamic_slice_in_dim(p_bqkv, me * nb, nb, axis=2))
    update("swa_b_o", lax.dynamic_slice_in_dim(p_bo, me * (D // NDEV), D // NDEV, axis=2))
    return loss, grad_x, res


WEIGHTS = ["ada_w", "ada_b", "ln_mix_g", "ln_mix_b", "ln_ffn_g", "ln_ffn_b", "ffn_w_gate", "ffn_w_up",
           "ffn_w_down", "mla_w_in", "mla_q_norm", "mla_w_q_b", "mla_kv_norm", "mla_w_kv_b", "mla_w_o",
           "swa_w_qkv", "swa_b_qkv", "swa_sinks", "swa_w_o", "swa_b_o"]
INPUTS = (["x", "c", "positions"] + WEIGHTS + ["loss_target"] + ["m_" + n for n in WEIGHTS]
          + ["v_" + n for n in WEIGHTS])


def kernel(x, c, positions, ada_w, ada_b, ln_mix_g, ln_mix_b, ln_ffn_g, ln_ffn_b, ffn_w_gate, ffn_w_up, ffn_w_down, mla_w_in, mla_q_norm, mla_w_q_b, mla_kv_norm, mla_w_kv_b, mla_w_o, swa_w_qkv, swa_b_qkv, swa_sinks, swa_w_o, swa_b_o, loss_target, m_ada_w, m_ada_b, m_ln_mix_g, m_ln_mix_b, m_ln_ffn_g, m_ln_ffn_b, m_ffn_w_gate, m_ffn_w_up, m_ffn_w_down, m_mla_w_in, m_mla_q_norm, m_mla_w_q_b, m_mla_kv_norm, m_mla_w_kv_b, m_mla_w_o, m_swa_w_qkv, m_swa_b_qkv, m_swa_sinks, m_swa_w_o, m_swa_b_o, v_ada_w, v_ada_b, v_ln_mix_g, v_ln_mix_b, v_ln_ffn_g, v_ln_ffn_b, v_ffn_w_gate, v_ffn_w_up, v_ffn_w_down, v_mla_w_in, v_mla_q_norm, v_mla_w_q_b, v_mla_kv_norm, v_mla_w_kv_b, v_mla_w_o, v_swa_w_qkv, v_swa_b_qkv, v_swa_sinks, v_swa_w_o, v_swa_b_o):
    args = (x, c, positions, ada_w, ada_b, ln_mix_g, ln_mix_b, ln_ffn_g, ln_ffn_b, ffn_w_gate, ffn_w_up, ffn_w_down, mla_w_in, mla_q_norm, mla_w_q_b, mla_kv_norm, mla_w_kv_b, mla_w_o, swa_w_qkv, swa_b_qkv, swa_sinks, swa_w_o, swa_b_o, loss_target, m_ada_w, m_ada_b, m_ln_mix_g, m_ln_mix_b, m_ln_ffn_g, m_ln_ffn_b, m_ffn_w_gate, m_ffn_w_up, m_ffn_w_down, m_mla_w_in, m_mla_q_norm, m_mla_w_q_b, m_mla_kv_norm, m_mla_w_kv_b, m_mla_w_o, m_swa_w_qkv, m_swa_b_qkv, m_swa_sinks, m_swa_w_o, m_swa_b_o, v_ada_w, v_ada_b, v_ln_mix_g, v_ln_mix_b, v_ln_ffn_g, v_ln_ffn_b, v_ffn_w_gate, v_ffn_w_up, v_ffn_w_down, v_mla_w_in, v_mla_q_norm, v_mla_w_q_b, v_mla_kv_norm, v_mla_w_kv_b, v_mla_w_o, v_swa_w_qkv, v_swa_b_qkv, v_swa_sinks, v_swa_w_o, v_swa_b_o)
    assert len(args) == len(INPUTS)
    loss, grad_x, res = _step(dict(zip(INPUTS, args)))
    return (loss, grad_x, *[res[n][0] for n in WEIGHTS], *[res[n][1] for n in WEIGHTS],
            *[res[n][2] for n in WEIGHTS], *[res[n][3] for n in WEIGHTS])
```

```python
import functools
import math

import jax
import jax.numpy as jnp
from jax import lax
from jax.experimental import pallas as pl
from jax.experimental.pallas import tpu as pltpu

F32 = jnp.float32
BF = jnp.bfloat16

NDEV = 8
D = 1024
DEPTH = 4
F = 2816
ALPHA = (2 * DEPTH) ** 0.25
LN_EPS = 1e-5
RMS_EPS = 1e-6
ROPE_THETA = 500000.0

MLA_H = 8
MLA_QR = 384
MLA_KVR = 256
MLA_ROPE = 64
MLA_LAT = 768
MLA_HD = 256
MLA_SCALE = (128 + 64) ** -0.5

SWA_HQ = 16
SWA_HKV = 4
SWA_W = 128
SWA_SCALE = 64 ** -0.5
SWA_QKV = (SWA_HQ + 2 * SWA_HKV) * 128
SWA_O = SWA_HQ * 128

LANE = 128
VMEM_LIMIT = 56 * 2 ** 20

ADAM_LR, ADAM_B1, ADAM_B2, ADAM_EPS, ADAM_WD, ADAM_STEP = 0.001, 0.9, 0.999, 1e-8, 0.01, 10


def _params(n_axes):
    return pltpu.CompilerParams(dimension_semantics=("arbitrary",) * n_axes, vmem_limit_bytes=VMEM_LIMIT)


def _dot(a, b):
    return jnp.dot(a, b, preferred_element_type=F32)


def _dot_nt(a, b):
    return lax.dot_general(a, b, (((1,), (1,)), ((), ())), preferred_element_type=F32)


def _dot_tn(a, b):
    return lax.dot_general(a, b, (((0,), (0,)), ((), ())), preferred_element_type=F32)


def _full(shape):
    return pl.BlockSpec(shape, lambda *_: (0,) * len(shape))


def _sigmoid(x):
    return 1.0 / (1.0 + jnp.exp(-x))


def _rope128(x, ct, s1, s2, half):
    return x * ct + pltpu.roll(x, LANE - half, 1) * s1 + pltpu.roll(x, half, 1) * s2


def _eye(n):
    return lax.broadcasted_iota(jnp.int32, (n, n), 0) == lax.broadcasted_iota(jnp.int32, (n, n), 1)


def _col_to_row(col):
    n = col.shape[0]
    return jnp.sum(jnp.where(_eye(n), col, 0.0), axis=0, keepdims=True)


def _row_to_col(row):
    n = row.shape[1]
    return jnp.sum(jnp.where(_eye(n), row, 0.0), axis=1, keepdims=True)


def _modulate(x, modl_ref, sc_row, sh_row):
    return x * (1.0 + modl_ref[sc_row:sc_row + 1, :]) + modl_ref[sh_row:sh_row + 1, :]


def _exchange(src, gather, name):
    blk = tuple(src.shape) if gather else tuple(src.shape[1:])

    def body(src_ref, out_ref, send_sems, recv_sems, local_sem):
        x, y, c = lax.axis_index("x"), lax.axis_index("y"), lax.axis_index("c")
        me = 4 * x + 2 * y + c

        def piece(dev):
            return src_ref if gather else src_ref.at[dev]

        mine = pltpu.make_async_copy(piece(me), out_ref.at[me], local_sem)
        mine.start()
        sends, recvs = [], []
        for k in range(1, NDEV):
            px = 1 - x if k & 4 else x
            py = 1 - y if k & 2 else y
            pc = 1 - c if k & 1 else c
            peer = 4 * px + 2 * py + pc
            common = dict(send_sem=send_sems.at[k - 1], recv_sem=recv_sems.at[k - 1],
                          device_id=(px, py, pc), device_id_type=pl.DeviceIdType.MESH)
            snd = pltpu.make_async_remote_copy(src_ref=piece(peer), dst_ref=out_ref.at[me], **common)
            snd.start()
            sends.append(snd)
            recvs.append(pltpu.make_async_remote_copy(src_ref=piece(peer), dst_ref=out_ref.at[peer], **common))
        for r in recvs:
            r.wait_recv()
        for s in sends:
            s.wait_send()
        mine.wait()

    return pl.pallas_call(
        body, name=name,
        out_shape=jax.ShapeDtypeStruct((NDEV,) + blk, src.dtype),
        in_specs=[pl.BlockSpec(memory_space=pltpu.HBM)],
        out_specs=pl.BlockSpec(memory_space=pltpu.HBM),
        scratch_shapes=[pltpu.SemaphoreType.DMA((NDEV - 1,)), pltpu.SemaphoreType.DMA((NDEV - 1,)),
                        pltpu.SemaphoreType.DMA],
    )(src)


def _mod_all(c_all, ada_w, ada_b_loc):
    ncol = ada_w.shape[2]

    def body(c_ref, w_ref, b_ref, o_ref):
        cv = c_ref[...]
        cond = cv * _sigmoid(cv)
        o_ref[0] = _dot(cond.astype(BF), w_ref[0].astype(BF)) + b_ref[0]

    return pl.pallas_call(
        body, name="mod_all", grid=(DEPTH,),
        out_shape=jax.ShapeDtypeStruct((DEPTH, NDEV, ncol), F32),
        in_specs=[_full((NDEV, D)), pl.BlockSpec((1, D, ncol), lambda i: (i, 0, 0)),
                  pl.BlockSpec((1, 1, ncol), lambda i: (i, 0, 0))],
        out_specs=pl.BlockSpec((1, NDEV, ncol), lambda i: (i, 0, 0)),
        compiler_params=_params(1),
    )(c_all, ada_w, ada_b_loc)


def _ada_w_grad(c_all_t, dmod_loc):
    ncol = dmod_loc.shape[2]

    def body(ct_ref, dm_ref, o_ref):
        cv = ct_ref[...]
        cond = cv * _sigmoid(cv)
        acc = cond[:, 0:1] * dm_ref[0, 0:1, :]
        for b in range(1, NDEV):
            acc = acc + cond[:, b:b + 1] * dm_ref[0, b:b + 1, :]
        o_ref[0] = acc

    return pl.pallas_call(
        body, name="ada_w_grad", grid=(DEPTH,),
        out_shape=jax.ShapeDtypeStruct((DEPTH, D, ncol), F32),
        in_specs=[_full((D, NDEV)), pl.BlockSpec((1, NDEV, ncol), lambda i: (i, 0, 0))],
        out_specs=pl.BlockSpec((1, D, ncol), lambda i: (i, 0, 0)),
        compiler_params=_params(1),
    )(c_all_t, dmod_loc)


def _mla_proj_fwd(x, modl, w_in, q_norm, w_q, kv_norm, w_kv, tabs, tm):
    T = x.shape[0]
    ct_a, s1_a, s2_a = tabs

    def body(x_ref, modl_ref, win_ref, qn_ref, wq_ref, kvn_ref, wkv_ref, ct_ref, s1_ref, s2_ref,
             q_ref, k_ref, v_ref):
        h = _modulate(x_ref[...], modl_ref, 1, 0).astype(BF)
        lat = _dot(h, win_ref[...])
        ql, kvl, kr = lat[:, :MLA_QR], lat[:, MLA_QR:MLA_QR + MLA_KVR], lat[:, MLA_QR + MLA_KVR:]
        qn = (ql * lax.rsqrt(jnp.mean(ql * ql, axis=1, keepdims=True) + RMS_EPS) * qn_ref[...]).astype(BF)
        kvn = (kvl * lax.rsqrt(jnp.mean(kvl * kvl, axis=1, keepdims=True) + RMS_EPS) * kvn_ref[...]).astype(BF)
        ct, s1, s2 = ct_ref[...], s1_ref[...], s2_ref[...]
        kr = _rope128(kr, ct, s1, s2, MLA_ROPE // 2).astype(BF)
        for hd in range(MLA_H):
            cols = slice(hd * MLA_HD, (hd + 1) * MLA_HD)
            qh = _dot(qn, wq_ref[:, cols])
            q_ref[hd, :, 0:LANE] = qh[:, :LANE].astype(BF)
            q_ref[hd, :, LANE:MLA_HD] = _rope128(qh[:, LANE:], ct, s1, s2, MLA_ROPE // 2).astype(BF)
            kvh = _dot(kvn, wkv_ref[:, cols])
            k_ref[hd, :, 0:LANE] = kvh[:, :LANE].astype(BF)
            k_ref[hd, :, LANE:MLA_HD] = kr
            v_ref[hd, :, 0:LANE] = kvh[:, LANE:].astype(BF)
            v_ref[hd, :, LANE:2 * LANE] = jnp.ones((tm, LANE), BF)

    row = lambda i: (i, 0)
    head = lambda i: (0, i, 0)
    return pl.pallas_call(
        body, name="mla_proj_fwd", grid=(T // tm,),
        out_shape=(jax.ShapeDtypeStruct((MLA_H, T, MLA_HD), BF), jax.ShapeDtypeStruct((MLA_H, T, MLA_HD), BF),
                   jax.ShapeDtypeStruct((MLA_H, T, 2 * LANE), BF)),
        in_specs=[pl.BlockSpec((tm, D), row), _full((8, D)), _full((D, MLA_LAT)), _full((1, MLA_QR)),
                  _full((MLA_QR, MLA_H * MLA_HD)), _full((1, MLA_KVR)), _full((MLA_KVR, MLA_H * MLA_HD)),
                  pl.BlockSpec((tm, LANE), row), pl.BlockSpec((tm, LANE), row), pl.BlockSpec((tm, LANE), row)],
        out_specs=(pl.BlockSpec((MLA_H, tm, MLA_HD), head), pl.BlockSpec((MLA_H, tm, MLA_HD), head),
                   pl.BlockSpec((MLA_H, tm, 2 * LANE), head)),
        compiler_params=_params(1),
    )(x, modl, w_in, q_norm, w_q, kv_norm, w_kv, ct_a, s1_a, s2_a)


def _causal_mask(n, transposed):
    row = lax.broadcasted_iota(jnp.int32, (n, n), 0)
    col = lax.broadcasted_iota(jnp.int32, (n, n), 1)
    return (row <= col) if transposed else (col <= row)


def _flash_fwd(q, k, v, tb):
    H, T, _ = q.shape
    rh = min(256, tb)
    nch = tb // LANE
    c_exp = MLA_SCALE * math.log2(math.e)

    def body(q_ref, k_ref, v_ref, o_ref, lse_ref, m_s, acc_s):
        i = pl.program_id(1)
        m_s[...] = jnp.full(m_s.shape, -jnp.inf, F32)
        acc_s[...] = jnp.zeros(acc_s.shape, F32)

        def block(j, masked):
            rows = pl.ds(pl.multiple_of(j * tb, tb), tb)
            kb = k_ref[0, rows, :]
            vb = v_ref[0, rows, :]
            for hf in range(tb // rh):
                r = slice(hf * rh, (hf + 1) * rh)
                s = _dot_nt(q_ref[0, r, :], kb)
                if masked:
                    s = jnp.where(_causal_mask(tb, False)[r], s, -jnp.inf)
                sc = [s[:, c * LANE:(c + 1) * LANE] for c in range(nch)]
                mx = sc[0]
                for c in range(1, nch):
                    mx = jnp.maximum(mx, sc[c])
                m_prev = m_s[r, :]
                m_new = jnp.maximum(m_prev, jnp.max(mx, axis=1, keepdims=True))
                p = jnp.concatenate([jnp.exp2((sc[c] - m_new) * c_exp) for c in range(nch)], axis=1)
                corr = jnp.exp2((m_prev - m_new) * c_exp)
                acc_s[r, :] = jnp.concatenate([corr, corr], axis=1) * acc_s[r, :] + _dot(p.astype(BF), vb)
                m_s[r, :] = m_new

        def step(j, carry):
            block(j, False)
            return carry

        lax.fori_loop(0, i, step, 0)
        block(i, True)
        l = acc_s[:, LANE:]
        o_ref[...] = (acc_s[:, :LANE] / l).astype(BF)
        lse_ref[0] = _col_to_row((m_s[...] * MLA_SCALE + jnp.log(l))[:, 0:1])

    return pl.pallas_call(
        body, name="flash_fwd", grid=(H, T // tb),
        out_shape=(jax.ShapeDtypeStruct((T, H * LANE), BF), jax.ShapeDtypeStruct((H, 1, T), F32)),
        in_specs=[pl.BlockSpec((1, tb, MLA_HD), lambda h, i: (h, i, 0)),
                  pl.BlockSpec((1, T, MLA_HD), lambda h, i: (h, 0, 0)),
                  pl.BlockSpec((1, T, 2 * LANE), lambda h, i: (h, 0, 0))],
        out_specs=(pl.BlockSpec((tb, LANE), lambda h, i: (i, h)),
                   pl.BlockSpec((1, 1, tb), lambda h, i: (h, 0, i))),
        scratch_shapes=[pltpu.VMEM((tb, LANE), F32), pltpu.VMEM((tb, 2 * LANE), F32)],
        compiler_params=_params(2),
    )(q, k, v)


def _outproj_ln_fwd(a, w, bias, x, modl, g_row, ln_g, ln_b, tm, name):
    T, K = a.shape

    def body(a_ref, w_ref, b_ref, x_ref, modl_ref, g_ref, bb_ref, y_ref, u_ref, xn_ref):
        y = _dot(a_ref[...], w_ref[...]) + b_ref[...]
        u = ALPHA * x_ref[...] + modl_ref[g_row:g_row + 1, :] * y
        mu = jnp.mean(u, axis=1, keepdims=True)
        uc = u - mu
        var = jnp.mean(uc * uc, axis=1, keepdims=True)
        y_ref[...] = y.astype(BF)
        u_ref[...] = u
        xn_ref[...] = uc * lax.rsqrt(var + LN_EPS) * g_ref[...] + bb_ref[...]

    row = lambda i: (i, 0)
    return pl.pallas_call(
        body, name=name, grid=(T // tm,),
        out_shape=(jax.ShapeDtypeStruct((T, D), BF), jax.ShapeDtypeStruct((T, D), F32),
                   jax.ShapeDtypeStruct((T, D), F32)),
        in_specs=[pl.BlockSpec((tm, K), row), _full((K, D)), _full((1, D)), pl.BlockSpec((tm, D), row),
                  _full((8, D)), _full((1, D)), _full((1, D))],
        out_specs=(pl.BlockSpec((tm, D), row),) * 3,
        compiler_params=_params(1),
    )(a, w, bias, x, modl, ln_g, ln_b)


def _ffn_up_fwd(x, modl, wg, wu, tm, tn):
    T = x.shape[0]

    def body(x_ref, modl_ref, wg_ref, wu_ref, g_ref, u_ref, a_ref):
        h = _modulate(x_ref[...], modl_ref, 4, 3).astype(BF)
        g = _dot(h, wg_ref[...])
        u = _dot(h, wu_ref[...])
        g_ref[...] = g.astype(BF)
        u_ref[...] = u.astype(BF)
        a_ref[...] = (g * _sigmoid(g) * u).astype(BF)

    tile = pl.BlockSpec((tm, tn), lambda n, i: (i, n))
    wcol = pl.BlockSpec((D, tn), lambda n, i: (0, n))
    return pl.pallas_call(
        body, name="ffn_up_fwd", grid=(F // tn, T // tm),
        out_shape=(jax.ShapeDtypeStruct((T, F), BF),) * 3,
        in_specs=[pl.BlockSpec((tm, D), lambda n, i: (i, 0)), _full((8, D)), wcol, wcol],
        out_specs=(tile, tile, tile),
        compiler_params=_params(2),
    )(x, modl, wg, wu)


def _swa_proj_fwd(x, modl, w, b, tabs, tm):
    T = x.shape[0]
    ct_b, s1_b, s2_b = tabs
    n_rope = SWA_HQ + SWA_HKV

    def body(x_ref, modl_ref, w_ref, b_ref, ct_ref, s1_ref, s2_ref, o_ref):
        h = _modulate(x_ref[...], modl_ref, 1, 0).astype(BF)
        ct, s1, s2 = ct_ref[...], s1_ref[...], s2_ref[...]
        for grp in range(SWA_QKV // LANE):
            cols = slice(grp * LANE, (grp + 1) * LANE)
            z = _dot(h, w_ref[:, cols]) + b_ref[:, cols]
            if grp < n_rope:
                z = _rope128(z, ct, s1, s2, 8)
            o_ref[:, cols] = z.astype(BF)

    row = lambda i: (i, 0)
    return pl.pallas_call(
        body, name="swa_proj_fwd", grid=(T // tm,),
        out_shape=jax.ShapeDtypeStruct((T, SWA_QKV), BF),
        in_specs=[pl.BlockSpec((tm, D), row), _full((8, D)), _full((D, SWA_QKV)), _full((1, SWA_QKV)),
                  pl.BlockSpec((tm, LANE), row), pl.BlockSpec((tm, LANE), row), pl.BlockSpec((tm, LANE), row)],
        out_specs=pl.BlockSpec((tm, SWA_QKV), row),
        compiler_params=_params(1),
    )(x, modl, w, b, ct_b, s1_b, s2_b)


def _swa_mask(first_has_prev):
    r = lax.broadcasted_iota(jnp.int32, (4 * SWA_W, 2 * SWA_W), 0) % SWA_W
    c = lax.broadcasted_iota(jnp.int32, (4 * SWA_W, 2 * SWA_W), 1)
    band = (c > r) & (c <= r + SWA_W)
    if first_has_prev is not None:
        band = band & ((c >= SWA_W) | first_has_prev)
    return band


def _swa_specs(T, tb):
    nsub = tb // SWA_W
    q_spec = pl.BlockSpec((tb, SWA_O), lambda i: (i, 0))
    kvc_spec = pl.BlockSpec((tb, 2 * SWA_HKV * LANE), lambda i: (i, 2))
    kvp_spec = pl.BlockSpec((SWA_W, 2 * SWA_HKV * LANE), lambda i: (jnp.maximum(i * nsub - 1, 0), 2))
    return q_spec, kvc_spec, kvp_spec


def _swa_softmax(q_ref, kall, sink_ref, g, b, i):
    q4 = jnp.concatenate([q_ref[b * SWA_W:(b + 1) * SWA_W, (4 * g + hh) * LANE:(4 * g + hh + 1) * LANE]
                          for hh in range(4)], axis=0)
    k2 = kall[b * SWA_W:(b + 2) * SWA_W]
    s = _dot_nt(q4, k2) * SWA_SCALE
    s = jnp.where(_swa_mask((i > 0) if b == 0 else None), s, -jnp.inf)
    sink = jnp.concatenate([jnp.full((SWA_W, 1), sink_ref[4 * g + hh], F32) for hh in range(4)], axis=0)
    m = jnp.maximum(jnp.max(s, axis=1, keepdims=True), sink)
    e = jnp.exp(s - m)
    es = jnp.exp(sink - m)
    linv = 1.0 / (jnp.sum(e, axis=1, keepdims=True) + es)
    return q4, k2, e * linv, es * linv


def _swa_attn_fwd(qkv, sinks, tb):
    T = qkv.shape[0]
    nsub = tb // SWA_W
    kw = SWA_HKV * LANE

    def body(q_ref, kvc_ref, kvp_ref, sink_ref, o_ref):
        i = pl.program_id(0)
        for g in range(SWA_HKV):
            kall = jnp.concatenate([kvp_ref[:, g * LANE:(g + 1) * LANE], kvc_ref[:, g * LANE:(g + 1) * LANE]], axis=0)
            vall = jnp.concatenate([kvp_ref[:, kw + g * LANE:kw + (g + 1) * LANE],
                                    kvc_ref[:, kw + g * LANE:kw + (g + 1) * LANE]], axis=0)
            for b in range(nsub):
                _, _, p, _ = _swa_softmax(q_ref, kall, sink_ref, g, b, i)
                o4 = _dot(p.astype(BF), vall[b * SWA_W:(b + 2) * SWA_W])
                for hh in range(4):
                    o_ref[b * SWA_W:(b + 1) * SWA_W, (4 * g + hh) * LANE:(4 * g + hh + 1) * LANE] = (
                        o4[hh * SWA_W:(hh + 1) * SWA_W].astype(BF))

    q_spec, kvc_spec, kvp_spec = _swa_specs(T, tb)
    return pl.pallas_call(
        body, name="swa_attn_fwd", grid=(T // tb,),
        out_shape=jax.ShapeDtypeStruct((T, SWA_O), BF),
        in_specs=[q_spec, kvc_spec, kvp_spec, pl.BlockSpec(memory_space=pltpu.SMEM)],
        out_specs=pl.BlockSpec((tb, SWA_O), lambda i: (i, 0)),
        compiler_params=_params(1),
    )(qkv, qkv, qkv, sinks)


def _loss_grad(x, tgt, tm):
    T = x.shape[0]

    def body(x_ref, t_ref, dx_ref, l_ref):
        @pl.when(pl.program_id(0) == 0)
        def _():
            l_ref[...] = jnp.zeros(l_ref.shape, F32)
        diff = x_ref[...] - t_ref[...]
        dx_ref[...] = diff * (1.0 / D)
        l_ref[0:1, :] += jnp.sum(diff * diff, axis=0, keepdims=True) * (0.5 / D)

    row = lambda i: (i, 0)
    return pl.pallas_call(
        body, name="loss_grad", grid=(T // tm,),
        out_shape=(jax.ShapeDtypeStruct((T, D), F32), jax.ShapeDtypeStruct((8, D), F32)),
        in_specs=[pl.BlockSpec((tm, D), row), pl.BlockSpec((tm, D), row)],
        out_specs=(pl.BlockSpec((tm, D), row), _full((8, D))),
        compiler_params=_params(1),
    )(x, tgt)


def _outproj_ln_bwd(dxn, u, y, w, modl, g_row, ln_g, tm, name):
    T = dxn.shape[0]
    K = w.shape[0]

    def body(dxn_ref, u_ref, y_ref, w_ref, modl_ref, g_ref, dres_ref, dy_ref, da_ref, sm_ref):
        @pl.when(pl.program_id(0) == 0)
        def _():
            sm_ref[...] = jnp.zeros(sm_ref.shape, F32)
        uu = u_ref[...]
        mu = jnp.mean(uu, axis=1, keepdims=True)
        uc = uu - mu
        rstd = lax.rsqrt(jnp.mean(uc * uc, axis=1, keepdims=True) + LN_EPS)
        xhat = uc * rstd
        dxo = dxn_ref[...]
        dyh = dxo * g_ref[...]
        du = rstd * (dyh - jnp.mean(dyh, axis=1, keepdims=True)
                     - xhat * jnp.mean(dyh * xhat, axis=1, keepdims=True))
        dy = modl_ref[g_row:g_row + 1, :] * du
        dyb = dy.astype(BF)
        dres_ref[...] = ALPHA * du
        dy_ref[...] = dyb
        da_ref[...] = _dot_nt(dyb, w_ref[...]).astype(BF)
        sm_ref[0:1, :] += jnp.sum(dxo * xhat, axis=0, keepdims=True)
        sm_ref[1:2, :] += jnp.sum(dxo, axis=0, keepdims=True)
        sm_ref[2:3, :] += jnp.sum(du * y_ref[...].astype(F32), axis=0, keepdims=True)
        sm_ref[3:4, :] += jnp.sum(dy, axis=0, keepdims=True)

    row = lambda i: (i, 0)
    return pl.pallas_call(
        body, name=name, grid=(T // tm,),
        out_shape=(jax.ShapeDtypeStruct((T, D), F32), jax.ShapeDtypeStruct((T, D), BF),
                   jax.ShapeDtypeStruct((T, K), BF), jax.ShapeDtypeStruct((8, D), F32)),
        in_specs=[pl.BlockSpec((tm, D), row), pl.BlockSpec((tm, D), row), pl.BlockSpec((tm, D), row),
                  _full((K, D)), _full((8, D)), _full((1, D))],
        out_specs=(pl.BlockSpec((tm, D), row), pl.BlockSpec((tm, D), row), pl.BlockSpec((tm, K), row),
                   _full((8, D))),
        compiler_params=_params(1),
    )(dxn, u, y, w, modl, ln_g)


def _ffn_mid_bwd(da, g, u, x, modl, dres, wg, wu, tm, tn):
    T = x.shape[0]
    nn = F // tn

    def body(da_ref, g_ref, u_ref, x_ref, modl_ref, dres_ref, wg_ref, wu_ref, dg_ref, du_ref, dx_ref, sm_ref):
        i, n = pl.program_id(0), pl.program_id(1)

        @pl.when((i == 0) & (n == 0))
        def _():
            sm_ref[...] = jnp.zeros(sm_ref.shape, F32)

        gg = g_ref[...].astype(F32)
        sg = _sigmoid(gg)
        dav = da_ref[...].astype(F32)
        dgp = (dav * u_ref[...].astype(F32) * sg * (1.0 + gg * (1.0 - sg))).astype(BF)
        dup = (dav * gg * sg).astype(BF)
        dg_ref[...] = dgp
        du_ref[...] = dup
        dh = _dot_nt(dgp, wg_ref[...]) + _dot_nt(dup, wu_ref[...])

        @pl.when(n == 0)
        def _():
            dx_ref[...] = dh

        @pl.when(n > 0)
        def _():
            dx_ref[...] += dh

        @pl.when(n == nn - 1)
        def _():
            dht = dx_ref[...]
            sm_ref[0:1, :] += jnp.sum(dht * x_ref[...], axis=0, keepdims=True)
            sm_ref[1:2, :] += jnp.sum(dht, axis=0, keepdims=True)
            dx_ref[...] = dres_ref[...] + dht * (1.0 + modl_ref[4:5, :])

    tile = pl.BlockSpec((tm, tn), lambda i, n: (i, n))
    rowd = pl.BlockSpec((tm, D), lambda i, n: (i, 0))
    wcol = pl.BlockSpec((D, tn), lambda i, n: (0, n))
    return pl.pallas_call(
        body, name="ffn_mid_bwd", grid=(T // tm, nn),
        out_shape=(jax.ShapeDtypeStruct((T, F), BF), jax.ShapeDtypeStruct((T, F), BF),
                   jax.ShapeDtypeStruct((T, D), F32), jax.ShapeDtypeStruct((8, D), F32)),
        in_specs=[tile, tile, tile, rowd, _full((8, D)), rowd, wcol, wcol],
        out_specs=(tile, tile, rowd, _full((8, D))),
        compiler_params=_params(2),
    )(da, g, u, x, modl, dres, wg, wu)


def _wgrad(a, b, tm, tk, tn, name, modl=None, rows=None):
    T, K = a.shape
    N = b.shape[1]

    def body(*refs):
        if modl is None:
            a_ref, b_ref, o_ref = refs
            av = a_ref[...]
        else:
            a_ref, modl_ref, b_ref, o_ref = refs
            av = _modulate(a_ref[...], modl_ref, rows[0], rows[1]).astype(BF)

        @pl.when(pl.program_id(2) == 0)
        def _():
            o_ref[...] = jnp.zeros(o_ref.shape, F32)
        o_ref[...] += _dot_tn(av, b_ref[...])

    in_specs = [pl.BlockSpec((tm, tk), lambda k, n, t: (t, k))]
    args = [a]
    if modl is not None:
        in_specs.append(_full((8, D)))
        args.append(modl)
    in_specs.append(pl.BlockSpec((tm, tn), lambda k, n, t: (t, n)))
    args.append(b)
    return pl.pallas_call(
        body, name=name, grid=(K // tk, N // tn, T // tm),
        out_shape=jax.ShapeDtypeStruct((K, N), F32),
        in_specs=in_specs,
        out_specs=pl.BlockSpec((tk, tn), lambda k, n, t: (k, n)),
        compiler_params=_params(3),
    )(*args)


def _flash_dq(q, k, v, o, do, lse, tb):
    H, T, _ = q.shape

    rh = min(256, tb)
    nch = tb // LANE
    c_exp = MLA_SCALE * math.log2(math.e)

    def body(q_ref, k_ref, v_ref, o_ref, do_ref, lse_ref, dq_ref, delta_ref, dq_s, l2_s, dl_s):
        i = pl.program_id(1)
        delta = jnp.sum(o_ref[...].astype(F32) * do_ref[...].astype(F32), axis=1, keepdims=True)
        dl_s[...] = jnp.broadcast_to(delta, dl_s.shape)
        l2_s[...] = jnp.broadcast_to(_row_to_col(lse_ref[0]) * math.log2(math.e), l2_s.shape)
        dq_s[...] = jnp.zeros(dq_s.shape, F32)

        def block(j, masked):
            rows = pl.ds(pl.multiple_of(j * tb, tb), tb)
            kb = k_ref[0, rows, :]
            vb = v_ref[0, rows, :]
            for hf in range(tb // rh):
                r = slice(hf * rh, (hf + 1) * rh)
                s = _dot_nt(q_ref[0, r, :], kb)
                if masked:
                    s = jnp.where(_causal_mask(tb, False)[r], s, -jnp.inf)
                dp = _dot_nt(do_ref[r, :], vb)
                l2, dl = l2_s[r, :], dl_s[r, :]
                ds = jnp.concatenate(
                    [jnp.exp2(s[:, c * LANE:(c + 1) * LANE] * c_exp - l2) * (dp[:, c * LANE:(c + 1) * LANE] - dl)
                     for c in range(nch)], axis=1)
                dq_s[r, :] += _dot(ds.astype(BF), kb)

        def step(j, carry):
            block(j, False)
            return carry

        lax.fori_loop(0, i, step, 0)
        block(i, True)
        dq_ref[0] = (dq_s[...] * MLA_SCALE).astype(BF)
        delta_ref[0] = _col_to_row(delta)

    return pl.pallas_call(
        body, name="flash_dq", grid=(H, T // tb),
        out_shape=(jax.ShapeDtypeStruct((H, T, MLA_HD), BF), jax.ShapeDtypeStruct((H, 1, T), F32)),
        in_specs=[pl.BlockSpec((1, tb, MLA_HD), lambda h, i: (h, i, 0)),
                  pl.BlockSpec((1, T, MLA_HD), lambda h, i: (h, 0, 0)),
                  pl.BlockSpec((1, T, LANE), lambda h, i: (h, 0, 0)),
                  pl.BlockSpec((tb, LANE), lambda h, i: (i, h)),
                  pl.BlockSpec((tb, LANE), lambda h, i: (i, h)),
                  pl.BlockSpec((1, 1, tb), lambda h, i: (h, 0, i))],
        out_specs=(pl.BlockSpec((1, tb, MLA_HD), lambda h, i: (h, i, 0)),
                   pl.BlockSpec((1, 1, tb), lambda h, i: (h, 0, i))),
        scratch_shapes=[pltpu.VMEM((tb, MLA_HD), F32), pltpu.VMEM((tb, LANE), F32), pltpu.VMEM((tb, LANE), F32)],
        compiler_params=_params(2),
    )(q, k, v, o, do, lse)


def _flash_dkv(q, k, v, do, lse, delta, tb):
    H, T, _ = q.shape
    nq = T // tb

    rh = min(256, tb)
    c_exp = MLA_SCALE * math.log2(math.e)

    def body(k_ref, v_ref, q_ref, do_ref, lse_ref, delta_ref, dk_ref, dv_ref, dk_s, dv_s):
        j = pl.program_id(1)
        dk_s[...] = jnp.zeros(dk_s.shape, F32)
        dv_s[...] = jnp.zeros(dv_s.shape, F32)

        def block(i, masked):
            start = pl.multiple_of(i * tb, tb)
            qb = q_ref[0, pl.ds(start, tb), :]
            dob = do_ref[pl.ds(start, tb), :]
            l2 = lse_ref[0, :, pl.ds(start, tb)] * math.log2(math.e)
            dl = delta_ref[0, :, pl.ds(start, tb)]
            for hf in range(tb // rh):
                r = slice(hf * rh, (hf + 1) * rh)
                st = _dot_nt(k_ref[0, r, :], qb)
                if masked:
                    st = jnp.where(_causal_mask(tb, True)[r], st, -jnp.inf)
                pt = jnp.exp2(st * c_exp - l2)
                dpt = _dot_nt(v_ref[0, r, :], dob)
                dst = (pt * (dpt - dl)).astype(BF)
                dv_s[r, :] += _dot(pt.astype(BF), dob)
                dk_s[r, :] += _dot(dst, qb)

        def step(i, carry):
            block(i, False)
            return carry

        block(j, True)
        lax.fori_loop(j + 1, nq, step, 0)
        dk_ref[0] = (dk_s[...] * MLA_SCALE).astype(BF)
        dv_ref[0] = dv_s[...].astype(BF)

    return pl.pallas_call(
        body, name="flash_dkv", grid=(H, nq),
        out_shape=(jax.ShapeDtypeStruct((H, T, MLA_HD), BF), jax.ShapeDtypeStruct((H, T, LANE), BF)),
        in_specs=[pl.BlockSpec((1, tb, MLA_HD), lambda h, j: (h, j, 0)),
                  pl.BlockSpec((1, tb, LANE), lambda h, j: (h, j, 0)),
                  pl.BlockSpec((1, T, MLA_HD), lambda h, j: (h, 0, 0)),
                  pl.BlockSpec((T, LANE), lambda h, j: (0, h)),
                  pl.BlockSpec((1, 1, T), lambda h, j: (h, 0, 0)),
                  pl.BlockSpec((1, 1, T), lambda h, j: (h, 0, 0))],
        out_specs=(pl.BlockSpec((1, tb, MLA_HD), lambda h, j: (h, j, 0)),
                   pl.BlockSpec((1, tb, LANE), lambda h, j: (h, j, 0))),
        scratch_shapes=[pltpu.VMEM((tb, MLA_HD), F32), pltpu.VMEM((tb, LANE), F32)],
        compiler_params=_params(2),
    )(k, v, q, do, lse, delta)


def _mla_proj_bwd(dq, dk, dv, x, modl, dres, w_in, q_norm, w_q, kv_norm, w_kv, tabs_neg, tm):
    T = x.shape[0]
    ct_a, s1_n, s2_n = tabs_neg

    def body(dq_ref, dk_ref, dv_ref, x_ref, modl_ref, dres_ref, win_ref, qn_ref, wq_ref, kvn_ref, wkv_ref,
             ct_ref, s1_ref, s2_ref, dx_ref, dwin_ref, dwq_ref, dwkv_ref, sm_ref, dqn_ref, dkvn_ref):
        @pl.when(pl.program_id(0) == 0)
        def _():
            for r in (dwin_ref, dwq_ref, dwkv_ref, sm_ref, dqn_ref, dkvn_ref):
                r[...] = jnp.zeros(r.shape, F32)

        xv = x_ref[...]
        h = _modulate(xv, modl_ref, 1, 0).astype(BF)
        lat = _dot(h, win_ref[...])
        ql, kvl = lat[:, :MLA_QR], lat[:, MLA_QR:MLA_QR + MLA_KVR]
        qhat = ql * lax.rsqrt(jnp.mean(ql * ql, axis=1, keepdims=True) + RMS_EPS)
        kvhat = kvl * lax.rsqrt(jnp.mean(kvl * kvl, axis=1, keepdims=True) + RMS_EPS)
        rq = lax.rsqrt(jnp.mean(ql * ql, axis=1, keepdims=True) + RMS_EPS)
        rkv = lax.rsqrt(jnp.mean(kvl * kvl, axis=1, keepdims=True) + RMS_EPS)
        qn = (qhat * qn_ref[...]).astype(BF)
        kvn = (kvhat * kvn_ref[...]).astype(BF)
        ct, s1, s2 = ct_ref[...], s1_ref[...], s2_ref[...]

        dqn = jnp.zeros((tm, MLA_QR), F32)
        dkvn = jnp.zeros((tm, MLA_KVR), F32)
        dkr = jnp.zeros((tm, LANE), F32)
        for hd in range(MLA_H):
            cols = slice(hd * MLA_HD, (hd + 1) * MLA_HD)
            dqh = dq_ref[hd]
            dqr = _rope128(dqh[:, LANE:].astype(F32), ct, s1, s2, MLA_ROPE // 2).astype(BF)
            dqh = jnp.concatenate([dqh[:, :LANE], dqr], axis=1)
            dqn = dqn + _dot_nt(dqh, wq_ref[:, cols])
            dwq_ref[:, cols] += _dot_tn(qn, dqh)
            dkh = dk_ref[hd]
            dkr = dkr + dkh[:, LANE:].astype(F32)
            dkvh = jnp.concatenate([dkh[:, :LANE], dv_ref[hd]], axis=1)
            dkvn = dkvn + _dot_nt(dkvh, wkv_ref[:, cols])
            dwkv_ref[:, cols] += _dot_tn(kvn, dkvh)
        dkr = _rope128(dkr, ct, s1, s2, MLA_ROPE // 2)

        dqn_ref[...] += jnp.sum(dqn * qhat, axis=0, keepdims=True)
        dkvn_ref[...] += jnp.sum(dkvn * kvhat, axis=0, keepdims=True)
        dqh_ = dqn * qn_ref[...]
        dkvh_ = dkvn * kvn_ref[...]
        dql = rq * (dqh_ - qhat * jnp.mean(dqh_ * qhat, axis=1, keepdims=True))
        dkvl = rkv * (dkvh_ - kvhat * jnp.mean(dkvh_ * kvhat, axis=1, keepdims=True))
        dlat = jnp.concatenate([dql, dkvl, dkr], axis=1).astype(BF)
        dwin_ref[...] += _dot_tn(h, dlat)
        dh = _dot_nt(dlat, win_ref[...])
        sm_ref[0:1, :] += jnp.sum(dh * xv, axis=0, keepdims=True)
        sm_ref[1:2, :] += jnp.sum(dh, axis=0, keepdims=True)
        dx_ref[...] = dres_ref[...] + dh * (1.0 + modl_ref[1:2, :])

    row = lambda i: (i, 0)
    head = lambda i: (0, i, 0)
    nq = MLA_H * MLA_HD
    return pl.pallas_call(
        body, name="mla_proj_bwd", grid=(T // tm,),
        out_shape=(jax.ShapeDtypeStruct((T, D), F32), jax.ShapeDtypeStruct((D, MLA_LAT), F32),
                   jax.ShapeDtypeStruct((MLA_QR, nq), F32), jax.ShapeDtypeStruct((MLA_KVR, nq), F32),
                   jax.ShapeDtypeStruct((8, D), F32), jax.ShapeDtypeStruct((1, MLA_QR), F32),
                   jax.ShapeDtypeStruct((1, MLA_KVR), F32)),
        in_specs=[pl.BlockSpec((MLA_H, tm, MLA_HD), head), pl.BlockSpec((MLA_H, tm, MLA_HD), head),
                  pl.BlockSpec((MLA_H, tm, LANE), head), pl.BlockSpec((tm, D), row), _full((8, D)),
                  pl.BlockSpec((tm, D), row), _full((D, MLA_LAT)), _full((1, MLA_QR)), _full((MLA_QR, nq)),
                  _full((1, MLA_KVR)), _full((MLA_KVR, nq)),
                  pl.BlockSpec((tm, LANE), row), pl.BlockSpec((tm, LANE), row), pl.BlockSpec((tm, LANE), row)],
        out_specs=(pl.BlockSpec((tm, D), row), _full((D, MLA_LAT)), _full((MLA_QR, nq)), _full((MLA_KVR, nq)),
                   _full((8, D)), _full((1, MLA_QR)), _full((1, MLA_KVR))),
        compiler_params=_params(1),
    )(dq, dk, dv, x, modl, dres, w_in, q_norm, w_q, kv_norm, w_kv, ct_a, s1_n, s2_n)


def _swa_attn_bwd(qkv, sinks, do, tb):
    T = qkv.shape[0]
    nsub = tb // SWA_W
    kw = SWA_HKV * LANE
    nstep = T // tb

    def body(q_ref, kvc_ref, kvp_ref, sink_ref, do_ref, dq_ref, dkvc_ref, dkvp_ref, dsink_ref, dk_s, dv_s):
        i = pl.program_id(0)

        @pl.when(i == 0)
        def _():
            dsink_ref[...] = jnp.zeros(dsink_ref.shape, F32)

        dk_s[...] = jnp.zeros(dk_s.shape, F32)
        dv_s[...] = jnp.zeros(dv_s.shape, F32)
        lane = lax.broadcasted_iota(jnp.int32, (1, LANE), 1)
        for g in range(SWA_HKV):
            gl = slice(g * LANE, (g + 1) * LANE)
            kall = jnp.concatenate([kvp_ref[:, gl], kvc_ref[:, gl]], axis=0)
            vall = jnp.concatenate([kvp_ref[:, kw + g * LANE:kw + (g + 1) * LANE],
                                    kvc_ref[:, kw + g * LANE:kw + (g + 1) * LANE]], axis=0)
            for b in range(nsub):
                q4, k2, p, psink = _swa_softmax(q_ref, kall, sink_ref, g, b, i)
                v2 = vall[b * SWA_W:(b + 2) * SWA_W]
                do4 = jnp.concatenate([do_ref[b * SWA_W:(b + 1) * SWA_W, (4 * g + hh) * LANE:(4 * g + hh + 1) * LANE]
                                       for hh in range(4)], axis=0)
                dp = _dot_nt(do4, v2)
                delta = jnp.sum(dp * p, axis=1, keepdims=True)
                ds = (p * (dp - delta) * SWA_SCALE).astype(BF)
                dq4 = _dot(ds, k2)
                rows2 = slice(b * SWA_W, (b + 2) * SWA_W)
                dk_s[rows2, gl] += _dot_tn(ds, q4)
                dv_s[rows2, gl] += _dot_tn(p.astype(BF), do4)
                dsk = psink * delta
                for hh in range(4):
                    hq = 4 * g + hh
                    dq_ref[b * SWA_W:(b + 1) * SWA_W, hq * LANE:(hq + 1) * LANE] = (
                        dq4[hh * SWA_W:(hh + 1) * SWA_W].astype(BF))
                    tot = jnp.sum(dsk[hh * SWA_W:(hh + 1) * SWA_W], axis=0, keepdims=True)
                    dsink_ref[0:1, :] -= jnp.where(lane == hq, tot, 0.0)
        dkvp_ref[0, :, 0:kw] = dk_s[0:SWA_W, :]
        dkvp_ref[0, :, kw:2 * kw] = dv_s[0:SWA_W, :]
        dkvc_ref[:, 0:kw] = dk_s[SWA_W:, :]
        dkvc_ref[:, kw:2 * kw] = dv_s[SWA_W:, :]

    q_spec, kvc_spec, kvp_spec = _swa_specs(T, tb)
    return pl.pallas_call(
        body, name="swa_attn_bwd", grid=(nstep,),
        out_shape=(jax.ShapeDtypeStruct((T, SWA_O), BF), jax.ShapeDtypeStruct((T, 2 * kw), F32),
                   jax.ShapeDtypeStruct((nstep, SWA_W, 2 * kw), F32), jax.ShapeDtypeStruct((8, LANE), F32)),
        in_specs=[q_spec, kvc_spec, kvp_spec, pl.BlockSpec(memory_space=pltpu.SMEM),
                  pl.BlockSpec((tb, SWA_O), lambda i: (i, 0))],
        out_specs=(pl.BlockSpec((tb, SWA_O), lambda i: (i, 0)), pl.BlockSpec((tb, 2 * kw), lambda i: (i, 0)),
                   pl.BlockSpec((1, SWA_W, 2 * kw), lambda i: (i, 0, 0)), _full((8, LANE))),
        scratch_shapes=[pltpu.VMEM((tb + SWA_W, kw), F32), pltpu.VMEM((tb + SWA_W, kw), F32)],
        compiler_params=_params(1),
    )(qkv, qkv, qkv, sinks, do)


def _swa_proj_bwd(dq, dkvc, dkvp, x, modl, dres, w, tabs_neg, tm):
    T = x.shape[0]
    nstep = T // tm
    kw = SWA_HKV * LANE
    ct_b, s1_n, s2_n = tabs_neg

    def body(dq_ref, dkvc_ref, dkvp_ref, x_ref, modl_ref, dres_ref, w_ref, ct_ref, s1_ref, s2_ref,
             dx_ref, dz_ref, sm_ref, db_ref):
        i = pl.program_id(0)

        @pl.when(i == 0)
        def _():
            sm_ref[...] = jnp.zeros(sm_ref.shape, F32)
            db_ref[...] = jnp.zeros(db_ref.shape, F32)

        ct, s1, s2 = ct_ref[...], s1_ref[...], s2_ref[...]
        has_next = i + 1 < nstep
        for grp in range(SWA_QKV // LANE):
            cols = slice(grp * LANE, (grp + 1) * LANE)
            if grp < SWA_HQ:
                z = dq_ref[:, cols].astype(F32)
            else:
                kc = slice((grp - SWA_HQ) * LANE, (grp - SWA_HQ + 1) * LANE)
                cur = dkvc_ref[:, kc]
                tail = cur[tm - SWA_W:] + jnp.where(has_next, dkvp_ref[0, :, kc], 0.0)
                z = jnp.concatenate([cur[:tm - SWA_W], tail], axis=0)
            if grp < SWA_HQ + SWA_HKV:
                z = _rope128(z, ct, s1, s2, 8)
            db_ref[0:1, cols] += jnp.sum(z, axis=0, keepdims=True)
            dz_ref[:, cols] = z.astype(BF)
        dh = _dot_nt(dz_ref[...], w_ref[...])
        sm_ref[0:1, :] += jnp.sum(dh * x_ref[...], axis=0, keepdims=True)
        sm_ref[1:2, :] += jnp.sum(dh, axis=0, keepdims=True)
        dx_ref[...] = dres_ref[...] + dh * (1.0 + modl_ref[1:2, :])

    row = lambda i: (i, 0)
    return pl.pallas_call(
        body, name="swa_proj_bwd", grid=(nstep,),
        out_shape=(jax.ShapeDtypeStruct((T, D), F32), jax.ShapeDtypeStruct((T, SWA_QKV), BF),
                   jax.ShapeDtypeStruct((8, D), F32), jax.ShapeDtypeStruct((8, SWA_QKV), F32)),
        in_specs=[pl.BlockSpec((tm, SWA_O), row), pl.BlockSpec((tm, 2 * kw), row),
                  pl.BlockSpec((1, SWA_W, 2 * kw), lambda i: (jnp.minimum(i + 1, nstep - 1), 0, 0)),
                  pl.BlockSpec((tm, D), row), _full((8, D)), pl.BlockSpec((tm, D), row), _full((D, SWA_QKV)),
                  pl.BlockSpec((tm, LANE), row), pl.BlockSpec((tm, LANE), row), pl.BlockSpec((tm, LANE), row)],
        out_specs=(pl.BlockSpec((tm, D), row), pl.BlockSpec((tm, SWA_QKV), row), _full((8, D)),
                   _full((8, SWA_QKV))),
        compiler_params=_params(1),
    )(dq, dkvc, dkvp, x, modl, dres, w, ct_b, s1_n, s2_n)


def _adamw(gparts, w, m, v, name):
    P, R, C = gparts.shape
    tr = R
    for cand in (512, 256, 128):
        if R % cand == 0 and R > cand:
            tr = cand
            break
    c1 = 1.0 / (1.0 - ADAM_B1 ** ADAM_STEP)
    c2 = 1.0 / (1.0 - ADAM_B2 ** ADAM_STEP)

    def body(gp_ref, w_ref, m_ref, v_ref, g_ref, d_ref, nm_ref, nv_ref):
        g = gp_ref[0].astype(F32)
        for p in range(1, P):
            g = g + gp_ref[p].astype(F32)
        nm = ADAM_B1 * m_ref[...] + (1.0 - ADAM_B1) * g
        nv = ADAM_B2 * v_ref[...] + (1.0 - ADAM_B2) * (g * g)
        g_ref[...] = g
        nm_ref[...] = nm
        nv_ref[...] = nv
        d_ref[...] = -ADAM_LR * ((nm * c1) / (jnp.sqrt(nv * c2) + ADAM_EPS) + ADAM_WD * w_ref[...])

    blk = pl.BlockSpec((tr, C), lambda i: (i, 0))
    return pl.pallas_call(
        body, name=name, grid=(R // tr,),
        out_shape=(jax.ShapeDtypeStruct((R, C), F32),) * 4,
        in_specs=[pl.BlockSpec((P, tr, C), lambda i: (0, i, 0)), blk, blk, blk],
        out_specs=(blk,) * 4,
        compiler_params=_params(1),
    )(gparts, w, m, v)


PACK_W = 1024

BIG = {
    "ffn_w_gate": ((DEPTH, D, F // NDEV), 2),
    "ffn_w_up": ((DEPTH, D, F // NDEV), 2),
    "ffn_w_down": ((DEPTH, F // NDEV, D), 1),
    "mla_w_in": ((2, D // NDEV, 704), 1),
    "mla_w_q_b": ((2, MLA_QR, 1536 // NDEV), 2),
    "mla_w_kv_b": ((2, MLA_KVR, 2048 // NDEV), 2),
    "mla_w_o": ((2, D // NDEV, D), 1),
    "swa_w_qkv": ((2, D, 1536 // NDEV), 2),
    "swa_w_o": ((2, D // NDEV, D), 1),
}


def _pack_rows(n):
    return -(-n // (16 * PACK_W)) * 16


def _pack_local(blocks):
    parts = []
    for name, (shape, _) in BIG.items():
        n = math.prod(shape)
        flat = blocks[name].astype(BF).reshape(-1)
        parts.append(jnp.pad(flat, (0, _pack_rows(n) * PACK_W - n)).reshape(-1, PACK_W))
    return jnp.concatenate(parts, axis=0)


def _pack_full(full):
    parts = []
    for name, (shape, axis) in BIG.items():
        a = full[name]
        split = a.shape[:axis] + (NDEV, a.shape[axis] // NDEV) + a.shape[axis + 1:]
        a = jnp.moveaxis(a.reshape(split), axis, 0).astype(BF).reshape(NDEV, -1)
        n = math.prod(shape)
        parts.append(jnp.pad(a, ((0, 0), (0, _pack_rows(n) * PACK_W - n))).reshape(NDEV, -1, PACK_W))
    return jnp.concatenate(parts, axis=1)


def _unpack_blocks(packed):
    out, r0 = {}, 0
    for name, (shape, _) in BIG.items():
        n = math.prod(shape)
        rows = _pack_rows(n)
        out[name] = packed[:, r0:r0 + rows].reshape(NDEV, -1)[:, :n].reshape((NDEV,) + shape)
        r0 += rows
    return out


def _unpack_full(packed):
    out = {}
    for name, blk in _unpack_blocks(packed).items():
        axis = BIG[name][1]
        a = jnp.moveaxis(blk, 0, axis)
        out[name] = a.reshape(a.shape[:axis] + (a.shape[axis] * a.shape[axis + 1],) + a.shape[axis + 2:])
    return out


def _pad_heads(a, axis, nheads, width, to):
    shp = a.shape[:axis] + (nheads, width) + a.shape[axis + 1:]
    a = a.reshape(shp)
    pad = [(0, 0)] * a.ndim
    pad[axis + 1] = (0, to - width)
    a = jnp.pad(a, pad)
    return a.reshape(a.shape[:axis] + (nheads * to,) + a.shape[axis + 2:])


def _unpad_heads(a, axis, nheads, width, to):
    shp = a.shape[:axis] + (nheads, to) + a.shape[axis + 1:]
    a = lax.slice_in_dim(a.reshape(shp), 0, width, axis=axis + 1)
    return a.reshape(a.shape[:axis] + (nheads * width,) + a.shape[axis + 2:])


def _swa_pad_cols(a):
    return _pad_heads(a, a.ndim - 1, SWA_HQ + 2 * SWA_HKV, 64, LANE)


def _rope_tables(positions, half):
    rot = 2 * half
    inv = ROPE_THETA ** (-jnp.arange(0, rot, 2, dtype=F32) / rot)
    ang = positions.astype(F32)[:, None] * inv
    cos, sin = jnp.cos(ang), jnp.sin(ang)
    T = positions.shape[0]
    ones = jnp.ones((T, LANE - rot), F32)
    zeros = jnp.zeros((T, LANE - rot), F32)
    zh = jnp.zeros((T, half), F32)
    ct = jnp.concatenate([cos, cos, ones], axis=1)
    s1 = jnp.concatenate([-sin, zh, zeros], axis=1)
    s2 = jnp.concatenate([zh, sin, zeros], axis=1)
    return (ct, s1, s2), (ct, -s1, -s2)


def _small_pack(vecs, rows):
    flat = jnp.concatenate([v.astype(F32).reshape(-1) for v in vecs])
    return jnp.pad(flat, (0, rows * LANE - flat.shape[0])).reshape(rows, LANE)


def _small_unpack(buf, shapes):
    flat = buf.reshape(NDEV, -1)
    out, o = [], 0
    for shp in shapes:
        n = math.prod(shp)
        out.append(flat[:, o:o + n].reshape((NDEV,) + shp))
        o += n
    return out


def _step(inp):
    x = inp["x"][0]
    tgt = inp["loss_target"][0]
    T = x.shape[0]
    tm = min(512, T)
    tmf = min(256, T)
    tb = min(512, T)
    tbf = min(1024, T)
    tnf = F // 2
    me = 4 * lax.axis_index("x") + 2 * lax.axis_index("y") + lax.axis_index("c")

    small_in = _small_pack([inp["c"], inp["swa_b_qkv"], inp["swa_b_o"]], 16)
    c_all, bqkv_blk, bo_blk = _small_unpack(_exchange(small_in, True, "gather_small"),
                                            [(D,), (2, 1536 // NDEV), (2, D // NDEV)])
    swa_b_qkv = jnp.moveaxis(bqkv_blk, 0, 1).reshape(2, 1536)
    swa_b_o = jnp.moveaxis(bo_blk, 0, 1).reshape(2, D)

    wfull = _unpack_full(_exchange(_pack_local(inp), True, "gather_weights"))

    ncol = 6 * D // NDEV
    ada_b_loc = lax.dynamic_slice_in_dim(inp["ada_b"], me * ncol, ncol, axis=1)[:, None, :]
    mod_all = _mod_all(c_all, inp["ada_w"], ada_b_loc)
    mod_src = jnp.moveaxis(mod_all, 1, 0).reshape(NDEV, DEPTH * ncol // LANE, LANE)
    mod_got = _exchange(mod_src, False, "scatter_mod").reshape(NDEV, DEPTH, ncol)
    mod = jnp.moveaxis(mod_got, 0, 1).reshape(DEPTH, 6, D)
    modl = jnp.pad(mod, ((0, 0), (0, 2), (0, 0)))

    w_in = jnp.pad(wfull["mla_w_in"], ((0, 0), (0, 0), (0, MLA_LAT - 704)))
    w_q = _pad_heads(wfull["mla_w_q_b"], 2, MLA_H, 192, MLA_HD)
    w_kv = wfull["mla_w_kv_b"]
    w_o_mla = wfull["mla_w_o"]
    w_qkv = _swa_pad_cols(wfull["swa_w_qkv"])
    b_qkv = _swa_pad_cols(swa_b_qkv)
    w_o_swa = _pad_heads(wfull["swa_w_o"], 1, SWA_HQ, 64, LANE)
    zero_bias = jnp.zeros((1, D), F32)

    pos = inp["positions"][0]
    tabs_a, tabs_a_neg = _rope_tables(pos, MLA_ROPE // 2)
    tabs_b, tabs_b_neg = _rope_tables(pos, 8)

    saved = []
    xs = x
    for i in range(DEPTH):
        j = i // 2
        st = {"x0": xs}
        if i % 2 == 0:
            q, k, v = _mla_proj_fwd(xs, modl[i], w_in[j], inp["mla_q_norm"][j][None], w_q[j],
                                    inp["mla_kv_norm"][j][None], w_kv[j], tabs_a, tm)
            o, lse = _flash_fwd(q, k, v, tbf)
            st.update(q=q, k=k, v=v, o=o, lse=lse)
            w_o, b_o = w_o_mla[j], zero_bias
        else:
            qkv = _swa_proj_fwd(xs, modl[i], w_qkv[j], b_qkv[j][None], tabs_b, tm)
            o = _swa_attn_fwd(qkv, inp["swa_sinks"][j], tb)
            st.update(qkv=qkv, o=o)
            w_o, b_o = w_o_swa[j], swa_b_o[j][None]
        y, u, xs = _outproj_ln_fwd(o, w_o, b_o, xs, modl[i], 2, inp["ln_mix_g"][i][None],
                                   inp["ln_mix_b"][i][None], tm, f"mix_out_fwd_{i % 2}")
        st.update(y_m=y, u_m=u, x1=xs, w_o=w_o)
        g, up, a = _ffn_up_fwd(xs, modl[i], wfull["ffn_w_gate"][i], wfull["ffn_w_up"][i], tmf, tnf)
        y, u, xs = _outproj_ln_fwd(a, wfull["ffn_w_down"][i], zero_bias, xs, modl[i], 5,
                                   inp["ln_ffn_g"][i][None], inp["ln_ffn_b"][i][None], tmf, "ffn_out_fwd")
        st.update(g=g, up=up, a=a, y_f=y, u_f=u)
        saved.append(st)

    dx, loss_rows = _loss_grad(xs, tgt, tm)
    loss = lax.psum(jnp.sum(loss_rows[0]), ("x", "y", "c"))

    gfull = {n: [None] * (DEPTH if n.startswith("ffn") else 2) for n in BIG}
    dmod = [None] * DEPTH
    g_ln = {n: [None] * DEPTH for n in ("ln_mix_g", "ln_mix_b", "ln_ffn_g", "ln_ffn_b")}
    g_qn, g_kvn, g_sink, g_bqkv, g_bo = [None] * 2, [None] * 2, [None] * 2, [None] * 2, [None] * 2
    for i in reversed(range(DEPTH)):
        j = i // 2
        st = saved[i]
        dres, dy, da, sm = _outproj_ln_bwd(dx, st["u_f"], st["y_f"], wfull["ffn_w_down"][i], modl[i], 5,
                                           inp["ln_ffn_g"][i][None], tmf, "ffn_out_bwd")
        g_ln["ln_ffn_g"][i], g_ln["ln_ffn_b"][i], dg_f = sm[0], sm[1], sm[2]
        gfull["ffn_w_down"][i] = _wgrad(st["a"], dy, tm, F // 2, D, "wgrad_down")
        dgp, dup, dx, sm = _ffn_mid_bwd(da, st["g"], st["up"], st["x1"], modl[i], dres,
                                        wfull["ffn_w_gate"][i], wfull["ffn_w_up"][i], tmf, tnf)
        dsc_f, dsh_f = sm[0], sm[1]
        gfull["ffn_w_gate"][i] = _wgrad(st["x1"], dgp, tm, D, tnf, "wgrad_gate", modl[i], (4, 3))
        gfull["ffn_w_up"][i] = _wgrad(st["x1"], dup, tm, D, tnf, "wgrad_up", modl[i], (4, 3))

        dres, dy, do, sm = _outproj_ln_bwd(dx, st["u_m"], st["y_m"], st["w_o"], modl[i], 2,
                                           inp["ln_mix_g"][i][None], tm, f"mix_out_bwd_{i % 2}")
        g_ln["ln_mix_g"][i], g_ln["ln_mix_b"][i], dg_m = sm[0], sm[1], sm[2]
        if i % 2 == 0:
            gfull["mla_w_o"][j] = _wgrad(st["o"], dy, tm, D, D, "wgrad_mla_o")
            dq, delta = _flash_dq(st["q"], st["k"], st["v"], st["o"], do, st["lse"], tbf)
            dk, dv = _flash_dkv(st["q"], st["k"], st["v"], do, st["lse"], delta, tbf)
            dx, dwin, dwq, dwkv, sm, dqn, dkvn = _mla_proj_bwd(
                dq, dk, dv, st["x0"], modl[i], dres, w_in[j], inp["mla_q_norm"][j][None], w_q[j],
                inp["mla_kv_norm"][j][None], w_kv[j], tabs_a_neg, tm)
            gfull["mla_w_in"][j] = dwin[:, :704]
            gfull["mla_w_q_b"][j] = _unpad_heads(dwq, 1, MLA_H, 192, MLA_HD)
            gfull["mla_w_kv_b"][j] = dwkv
            g_qn[j], g_kvn[j] = dqn[0], dkvn[0]
        else:
            g_bo[j] = sm[3]
            dwo = _wgrad(st["o"], dy, tm, SWA_O // 2, D, "wgrad_swa_o")
            gfull["swa_w_o"][j] = _unpad_heads(dwo, 0, SWA_HQ, 64, LANE)
            dq, dkvc, dkvp, dsink = _swa_attn_bwd(st["qkv"], inp["swa_sinks"][j], do, tb)
            g_sink[j] = dsink[0, :SWA_HQ]
            dx, dz, sm, db = _swa_proj_bwd(dq, dkvc, dkvp, st["x0"], modl[i], dres, w_qkv[j], tabs_b_neg, tm)
            dwqkv = _wgrad(st["x0"], dz, tm, D, SWA_QKV // 2, "wgrad_swa_qkv", modl[i], (1, 0))
            gfull["swa_w_qkv"][j] = _unpad_heads(dwqkv, 1, SWA_HQ + 2 * SWA_HKV, 64, LANE)
            g_bqkv[j] = _unpad_heads(db[0], 0, SWA_HQ + 2 * SWA_HKV, 64, LANE)
        dmod[i] = jnp.stack([sm[1], sm[0], dg_m, dsh_f, dsc_f, dg_f])
    grad_x = dx[None]

    small_shapes = [(DEPTH, 6 * D), (DEPTH, D), (DEPTH, D), (DEPTH, D), (DEPTH, D), (2, MLA_QR), (2, MLA_KVR),
                    (2, SWA_HQ), (2, 1536), (2, D)]
    small_vals = [jnp.stack(dmod).reshape(DEPTH, 6 * D), jnp.stack(g_ln["ln_mix_g"]), jnp.stack(g_ln["ln_mix_b"]),
                  jnp.stack(g_ln["ln_ffn_g"]), jnp.stack(g_ln["ln_ffn_b"]), jnp.stack(g_qn), jnp.stack(g_kvn),
                  jnp.stack(g_sink), jnp.stack(g_bqkv), jnp.stack(g_bo)]
    nsmall = sum(math.prod(s) for s in small_shapes)
    small_rows = -(-nsmall // (8 * LANE)) * 8
    (dmod_all, p_lmg, p_lmb, p_lfg, p_lfb, p_qn, p_kvn, p_sink, p_bqkv, p_bo) = _small_unpack(
        _exchange(_small_pack(small_vals, small_rows), True, "gather_small_grads"), small_shapes)

    gpacked = _exchange(_pack_full({n: jnp.stack(v) for n, v in gfull.items()}), False, "scatter_grads")
    gparts = _unpack_blocks(gpacked)

    res = {}

    def update(name, parts):
        w = inp[name]
        shp = w.shape
        r2 = (math.prod(shp[:-1]), shp[-1])
        outs = _adamw(parts.reshape((parts.shape[0],) + r2), w.reshape(r2), inp["m_" + name].reshape(r2),
                      inp["v_" + name].reshape(r2), "adamw_" + name)
        res[name] = tuple(o.reshape(shp) for o in outs)

    dmod_loc = lax.dynamic_slice_in_dim(dmod_all, me * ncol, ncol, axis=2)
    g_ada_w = _ada_w_grad(c_all.T, jnp.moveaxis(dmod_loc, 0, 1))
    update("ada_w", g_ada_w[None])
    update("ada_b", dmod_all)
    update("ln_mix_g", p_lmg)
    update("ln_mix_b", p_lmb)
    update("ln_ffn_g", p_lfg)
    update("ln_ffn_b", p_lfb)
    for name in BIG:
        update(name, gparts[name])
    update("mla_q_norm", p_qn)
    update("mla_kv_norm", p_kvn)
    update("swa_sinks", p_sink)
    nb = 1536 // NDEV
    update("swa_b_qkv", lax.dynamic_slice_in_dim(p_bqkv, me * nb, nb, axis=2))
    update("swa_b_o", lax.dynamic_slice_in_dim(p_bo, me * (D // NDEV), D // NDEV, axis=2))
    return loss, grad_x, res


WEIGHTS = ["ada_w", "ada_b", "ln_mix_g", "ln_mix_b", "ln_ffn_g", "ln_ffn_b", "ffn_w_gate", "ffn_w_up",
           "ffn_w_down", "mla_w_in", "mla_q_norm", "mla_w_q_b", "mla_kv_norm", "mla_w_kv_b", "mla_w_o",
           "swa_w_qkv", "swa_b_qkv", "swa_sinks", "swa_w_o", "swa_b_o"]
INPUTS = (["x", "c", "positions"] + WEIGHTS + ["loss_target"] + ["m_" + n for n in WEIGHTS]
          + ["v_" + n for n in WEIGHTS])


def kernel(x, c, positions, ada_w, ada_b, ln_mix_g, ln_mix_b, ln_ffn_g, ln_ffn_b, ffn_w_gate, ffn_w_up, ffn_w_down, mla_w_in, mla_q_norm, mla_w_q_b, mla_kv_norm, mla_w_kv_b, mla_w_o, swa_w_qkv, swa_b_qkv, swa_sinks, swa_w_o, swa_b_o, loss_target, m_ada_w, m_ada_b, m_ln_mix_g, m_ln_mix_b, m_ln_ffn_g, m_ln_ffn_b, m_ffn_w_gate, m_ffn_w_up, m_ffn_w_down, m_mla_w_in, m_mla_q_norm, m_mla_w_q_b, m_mla_kv_norm, m_mla_w_kv_b, m_mla_w_o, m_swa_w_qkv, m_swa_b_qkv, m_swa_sinks, m_swa_w_o, m_swa_b_o, v_ada_w, v_ada_b, v_ln_mix_g, v_ln_mix_b, v_ln_ffn_g, v_ln_ffn_b, v_ffn_w_gate, v_ffn_w_up, v_ffn_w_down, v_mla_w_in, v_mla_q_norm, v_mla_w_q_b, v_mla_kv_norm, v_mla_w_kv_b, v_mla_w_o, v_swa_w_qkv, v_swa_b_qkv, v_swa_sinks, v_swa_w_o, v_swa_b_o):
    args = (x, c, positions, ada_w, ada_b, ln_mix_g, ln_mix_b, ln_ffn_g, ln_ffn_b, ffn_w_gate, ffn_w_up, ffn_w_down, mla_w_in, mla_q_norm, mla_w_q_b, mla_kv_norm, mla_w_kv_b, mla_w_o, swa_w_qkv, swa_b_qkv, swa_sinks, swa_w_o, swa_b_o, loss_target, m_ada_w, m_ada_b, m_ln_mix_g, m_ln_mix_b, m_ln_ffn_g, m_ln_ffn_b, m_ffn_w_gate, m_ffn_w_up, m_ffn_w_down, m_mla_w_in, m_mla_q_norm, m_mla_w_q_b, m_mla_kv_norm, m_mla_w_kv_b, m_mla_w_o, m_swa_w_qkv, m_swa_b_qkv, m_swa_sinks, m_swa_w_o, m_swa_b_o, v_ada_w, v_ada_b, v_ln_mix_g, v_ln_mix_b, v_ln_ffn_g, v_ln_ffn_b, v_ffn_w_gate, v_ffn_w_up, v_ffn_w_down, v_mla_w_in, v_mla_q_norm, v_mla_w_q_b, v_mla_kv_norm, v_mla_w_kv_b, v_mla_w_o, v_swa_w_qkv, v_swa_b_qkv, v_swa_sinks, v_swa_w_o, v_swa_b_o)
    assert len(args) == len(INPUTS)
    loss, grad_x, res = _step(dict(zip(INPUTS, args)))
    return (loss, grad_x, *[res[n][0] for n in WEIGHTS], *[res[n][1] for n in WEIGHTS],
            *[res[n][2] for n in WEIGHTS], *[res[n][3] for n in WEIGHTS])
```

```python
import functools
import math

import jax
import jax.numpy as jnp
from jax import lax
from jax.experimental import pallas as pl
from jax.experimental.pallas import tpu as pltpu

F32 = jnp.float32
BF = jnp.bfloat16

NDEV = 8
D = 1024
DEPTH = 4
F = 2816
ALPHA = (2 * DEPTH) ** 0.25
LN_EPS = 1e-5
RMS_EPS = 1e-6
ROPE_THETA = 500000.0

MLA_H = 8
MLA_QR = 384
MLA_KVR = 256
MLA_ROPE = 64
MLA_LAT = 768
MLA_HD = 256
MLA_SCALE = (128 + 64) ** -0.5

SWA_HQ = 16
SWA_HKV = 4
SWA_W = 128
SWA_SCALE = 64 ** -0.5
SWA_QKV = (SWA_HQ + 2 * SWA_HKV) * 128
SWA_O = SWA_HQ * 128

LANE = 128
VMEM_LIMIT = 56 * 2 ** 20

ADAM_LR, ADAM_B1, ADAM_B2, ADAM_EPS, ADAM_WD, ADAM_STEP = 0.001, 0.9, 0.999, 1e-8, 0.01, 10


def _params(n_axes):
    return pltpu.CompilerParams(dimension_semantics=("arbitrary",) * n_axes, vmem_limit_bytes=VMEM_LIMIT)


def _dot(a, b):
    return jnp.dot(a, b, preferred_element_type=F32)


def _dot_nt(a, b):
    return lax.dot_general(a, b, (((1,), (1,)), ((), ())), preferred_element_type=F32)


def _dot_tn(a, b):
    return lax.dot_general(a, b, (((0,), (0,)), ((), ())), preferred_element_type=F32)


def _full(shape):
    return pl.BlockSpec(shape, lambda *_: (0,) * len(shape))


def _sigmoid(x):
    return 1.0 / (1.0 + jnp.exp(-x))


def _rope128(x, ct, s1, s2, half):
    return x * ct + pltpu.roll(x, LANE - half, 1) * s1 + pltpu.roll(x, half, 1) * s2


def _eye(n):
    return lax.broadcasted_iota(jnp.int32, (n, n), 0) == lax.broadcasted_iota(jnp.int32, (n, n), 1)


def _col_to_row(col):
    n = col.shape[0]
    return jnp.sum(jnp.where(_eye(n), col, 0.0), axis=0, keepdims=True)


def _row_to_col(row):
    n = row.shape[1]
    return jnp.sum(jnp.where(_eye(n), row, 0.0), axis=1, keepdims=True)


def _modulate(x, modl_ref, sc_row, sh_row):
    return x * (1.0 + modl_ref[sc_row:sc_row + 1, :]) + modl_ref[sh_row:sh_row + 1, :]


def _exchange(src, gather, name):
    blk = tuple(src.shape) if gather else tuple(src.shape[1:])

    def body(src_ref, out_ref, send_sems, recv_sems, local_sem):
        x, y, c = lax.axis_index("x"), lax.axis_index("y"), lax.axis_index("c")
        me = 4 * x + 2 * y + c

        def piece(dev):
            return src_ref if gather else src_ref.at[dev]

        mine = pltpu.make_async_copy(piece(me), out_ref.at[me], local_sem)
        mine.start()
        sends, recvs = [], []
        for k in range(1, NDEV):
            px = 1 - x if k & 4 else x
            py = 1 - y if k & 2 else y
            pc = 1 - c if k & 1 else c
            peer = 4 * px + 2 * py + pc
            common = dict(send_sem=send_sems.at[k - 1], recv_sem=recv_sems.at[k - 1],
                          device_id=(px, py, pc), device_id_type=pl.DeviceIdType.MESH)
            snd = pltpu.make_async_remote_copy(src_ref=piece(peer), dst_ref=out_ref.at[me], **common)
            snd.start()
            sends.append(snd)
            recvs.append(pltpu.make_async_remote_copy(src_ref=piece(peer), dst_ref=out_ref.at[peer], **common))
        for r in recvs:
            r.wait_recv()
        for s in sends:
            s.wait_send()
        mine.wait()

    return pl.pallas_call(
        body, name=name,
        out_shape=jax.ShapeDtypeStruct((NDEV,) + blk, src.dtype),
        in_specs=[pl.BlockSpec(memory_space=pltpu.HBM)],
        out_specs=pl.BlockSpec(memory_space=pltpu.HBM),
        scratch_shapes=[pltpu.SemaphoreType.DMA((NDEV - 1,)), pltpu.SemaphoreType.DMA((NDEV - 1,)),
                        pltpu.SemaphoreType.DMA],
    )(src)


def _mod_all(c_all, ada_w, ada_b_loc):
    ncol = ada_w.shape[2]

    def body(c_ref, w_ref, b_ref, o_ref):
        cv = c_ref[...]
        cond = cv * _sigmoid(cv)
        o_ref[0] = _dot(cond.astype(BF), w_ref[0].astype(BF)) + b_ref[0]

    return pl.pallas_call(
        body, name="mod_all", grid=(DEPTH,),
        out_shape=jax.ShapeDtypeStruct((DEPTH, NDEV, ncol), F32),
        in_specs=[_full((NDEV, D)), pl.BlockSpec((1, D, ncol), lambda i: (i, 0, 0)),
                  pl.BlockSpec((1, 1, ncol), lambda i: (i, 0, 0))],
        out_specs=pl.BlockSpec((1, NDEV, ncol), lambda i: (i, 0, 0)),
        compiler_params=_params(1),
    )(c_all, ada_w, ada_b_loc)


def _ada_w_grad(c_all_t, dmod_loc):
    ncol = dmod_loc.shape[2]

    def body(ct_ref, dm_ref, o_ref):
        cv = ct_ref[...]
        cond = cv * _sigmoid(cv)
        acc = cond[:, 0:1] * dm_ref[0, 0:1, :]
        for b in range(1, NDEV):
            acc = acc + cond[:, b:b + 1] * dm_ref[0, b:b + 1, :]
        o_ref[0] = acc

    return pl.pallas_call(
        body, name="ada_w_grad", grid=(DEPTH,),
        out_shape=jax.ShapeDtypeStruct((DEPTH, D, ncol), F32),
        in_specs=[_full((D, NDEV)), pl.BlockSpec((1, NDEV, ncol), lambda i: (i, 0, 0))],
        out_specs=pl.BlockSpec((1, D, ncol), lambda i: (i, 0, 0)),
        compiler_params=_params(1),
    )(c_all_t, dmod_loc)


def _mla_proj_fwd(x, modl, w_in, q_norm, w_q, kv_norm, w_kv, tabs, tm):
    T = x.shape[0]
    ct_a, s1_a, s2_a = tabs

    def body(x_ref, modl_ref, win_ref, qn_ref, wq_ref, kvn_ref, wkv_ref, ct_ref, s1_ref, s2_ref,
             q_ref, k_ref, v_ref):
        h = _modulate(x_ref[...], modl_ref, 1, 0).astype(BF)
        lat = _dot(h, win_ref[...])
        ql, kvl, kr = lat[:, :MLA_QR], lat[:, MLA_QR:MLA_QR + MLA_KVR], lat[:, MLA_QR + MLA_KVR:]
        qn = (ql * lax.rsqrt(jnp.mean(ql * ql, axis=1, keepdims=True) + RMS_EPS) * qn_ref[...]).astype(BF)
        kvn = (kvl * lax.rsqrt(jnp.mean(kvl * kvl, axis=1, keepdims=True) + RMS_EPS) * kvn_ref[...]).astype(BF)
        ct, s1, s2 = ct_ref[...], s1_ref[...], s2_ref[...]
        kr = _rope128(kr, ct, s1, s2, MLA_ROPE // 2).astype(BF)
        for hd in range(MLA_H):
            cols = slice(hd * MLA_HD, (hd + 1) * MLA_HD)
            qh = _dot(qn, wq_ref[:, cols])
            q_ref[hd, :, 0:LANE] = qh[:, :LANE].astype(BF)
            q_ref[hd, :, LANE:MLA_HD] = _rope128(qh[:, LANE:], ct, s1, s2, MLA_ROPE // 2).astype(BF)
            kvh = _dot(kvn, wkv_ref[:, cols])
            k_ref[hd, :, 0:LANE] = kvh[:, :LANE].astype(BF)
            k_ref[hd, :, LANE:MLA_HD] = kr
            v_ref[hd, :, 0:LANE] = kvh[:, LANE:].astype(BF)
            v_ref[hd, :, LANE:2 * LANE] = jnp.ones((tm, LANE), BF)

    row = lambda i: (i, 0)
    head = lambda i: (0, i, 0)
    return pl.pallas_call(
        body, name="mla_proj_fwd", grid=(T // tm,),
        out_shape=(jax.ShapeDtypeStruct((MLA_H, T, MLA_HD), BF), jax.ShapeDtypeStruct((MLA_H, T, MLA_HD), BF),
                   jax.ShapeDtypeStruct((MLA_H, T, 2 * LANE), BF)),
        in_specs=[pl.BlockSpec((tm, D), row), _full((8, D)), _full((D, MLA_LAT)), _full((1, MLA_QR)),
                  _full((MLA_QR, MLA_H * MLA_HD)), _full((1, MLA_KVR)), _full((MLA_KVR, MLA_H * MLA_HD)),
                  pl.BlockSpec((tm, LANE), row), pl.BlockSpec((tm, LANE), row), pl.BlockSpec((tm, LANE), row)],
        out_specs=(pl.BlockSpec((MLA_H, tm, MLA_HD), head), pl.BlockSpec((MLA_H, tm, MLA_HD), head),
                   pl.BlockSpec((MLA_H, tm, 2 * LANE), head)),
        compiler_params=_params(1),
    )(x, modl, w_in, q_norm, w_q, kv_norm, w_kv, ct_a, s1_a, s2_a)


def _causal_mask(n, transposed):
    row = lax.broadcasted_iota(jnp.int32, (n, n), 0)
    col = lax.broadcasted_iota(jnp.int32, (n, n), 1)
    return (row <= col) if transposed else (col <= row)


def _flash_fwd(q, k, v, tb):
    H, T, _ = q.shape
    rh = min(256, tb)
    nch = tb // LANE
    c_exp = MLA_SCALE * math.log2(math.e)

    def body(q_ref, k_ref, v_ref, o_ref, lse_ref, m_s, acc_s):
        i = pl.program_id(1)
        m_s[...] = jnp.full(m_s.shape, -jnp.inf, F32)
        acc_s[...] = jnp.zeros(acc_s.shape, F32)

        def block(j, masked):
            rows = pl.ds(pl.multiple_of(j * tb, tb), tb)
            kb = k_ref[0, rows, :]
            vb = v_ref[0, rows, :]
            for hf in range(tb // rh):
                r = slice(hf * rh, (hf + 1) * rh)
                s = _dot_nt(q_ref[0, r, :], kb)
                if masked:
                    s = jnp.where(_causal_mask(tb, False)[r], s, -jnp.inf)
                sc = [s[:, c * LANE:(c + 1) * LANE] for c in range(nch)]
                mx = sc[0]
                for c in range(1, nch):
                    mx = jnp.maximum(mx, sc[c])
                m_prev = m_s[r, :]
                m_new = jnp.maximum(m_prev, jnp.max(mx, axis=1, keepdims=True))
                p = jnp.concatenate([jnp.exp2((sc[c] - m_new) * c_exp) for c in range(nch)], axis=1)
                corr = jnp.exp2((m_prev - m_new) * c_exp)
                acc_s[r, :] = jnp.concatenate([corr, corr], axis=1) * acc_s[r, :] + _dot(p.astype(BF), vb)
                m_s[r, :] = m_new

        def step(j, carry):
            block(j, False)
            return carry

        lax.fori_loop(0, i, step, 0)
        block(i, True)
        l = acc_s[:, LANE:]
        o_ref[...] = (acc_s[:, :LANE] / l).astype(BF)
        lse_ref[0] = _col_to_row((m_s[...] * MLA_SCALE + jnp.log(l))[:, 0:1])

    return pl.pallas_call(
        body, name="flash_fwd", grid=(H, T // tb),
        out_shape=(jax.ShapeDtypeStruct((T, H * LANE), BF), jax.ShapeDtypeStruct((H, 1, T), F32)),
        in_specs=[pl.BlockSpec((1, tb, MLA_HD), lambda h, i: (h, i, 0)),
                  pl.BlockSpec((1, T, MLA_HD), lambda h, i: (h, 0, 0)),
                  pl.BlockSpec((1, T, 2 * LANE), lambda h, i: (h, 0, 0))],
        out_specs=(pl.BlockSpec((tb, LANE), lambda h, i: (i, h)),
                   pl.BlockSpec((1, 1, tb), lambda h, i: (h, 0, i))),
        scratch_shapes=[pltpu.VMEM((tb, LANE), F32), pltpu.VMEM((tb, 2 * LANE), F32)],
        compiler_params=_params(2),
    )(q, k, v)


def _outproj_ln_fwd(a, w, bias, x, modl, g_row, ln_g, ln_b, tm, name):
    T, K = a.shape

    def body(a_ref, w_ref, b_ref, x_ref, modl_ref, g_ref, bb_ref, y_ref, u_ref, xn_ref):
        y = _dot(a_ref[...], w_ref[...]) + b_ref[...]
        u = ALPHA * x_ref[...] + modl_ref[g_row:g_row + 1, :] * y
        mu = jnp.mean(u, axis=1, keepdims=True)
        uc = u - mu
        var = jnp.mean(uc * uc, axis=1, keepdims=True)
        y_ref[...] = y.astype(BF)
        u_ref[...] = u
        xn_ref[...] = uc * lax.rsqrt(var + LN_EPS) * g_ref[...] + bb_ref[...]

    row = lambda i: (i, 0)
    return pl.pallas_call(
        body, name=name, grid=(T // tm,),
        out_shape=(jax.ShapeDtypeStruct((T, D), BF), jax.ShapeDtypeStruct((T, D), F32),
                   jax.ShapeDtypeStruct((T, D), F32)),
        in_specs=[pl.BlockSpec((tm, K), row), _full((K, D)), _full((1, D)), pl.BlockSpec((tm, D), row),
                  _full((8, D)), _full((1, D)), _full((1, D))],
        out_specs=(pl.BlockSpec((tm, D), row),) * 3,
        compiler_params=_params(1),
    )(a, w, bias, x, modl, ln_g, ln_b)


def _ffn_up_fwd(x, modl, wg, wu, tm, tn):
    T = x.shape[0]

    def body(x_ref, modl_ref, wg_ref, wu_ref, g_ref, u_ref, a_ref):
        h = _modulate(x_ref[...], modl_ref, 4, 3).astype(BF)
        g = _dot(h, wg_ref[...])
        u = _dot(h, wu_ref[...])
        g_ref[...] = g.astype(BF)
        u_ref[...] = u.astype(BF)
        a_ref[...] = (g * _sigmoid(g) * u).astype(BF)

    tile = pl.BlockSpec((tm, tn), lambda n, i: (i, n))
    wcol = pl.BlockSpec((D, tn), lambda n, i: (0, n))
    return pl.pallas_call(
        body, name="ffn_up_fwd", grid=(F // tn, T // tm),
        out_shape=(jax.ShapeDtypeStruct((T, F), BF),) * 3,
        in_specs=[pl.BlockSpec((tm, D), lambda n, i: (i, 0)), _full((8, D)), wcol, wcol],
        out_specs=(tile, tile, tile),
        compiler_params=_params(2),
    )(x, modl, wg, wu)


def _swa_proj_fwd(x, modl, w, b, tabs, tm):
    T = x.shape[0]
    ct_b, s1_b, s2_b = tabs
    n_rope = SWA_HQ + SWA_HKV

    def body(x_ref, modl_ref, w_ref, b_ref, ct_ref, s1_ref, s2_ref, o_ref):
        h = _modulate(x_ref[...], modl_ref, 1, 0).astype(BF)
        ct, s1, s2 = ct_ref[...], s1_ref[...], s2_ref[...]
        for grp in range(SWA_QKV // LANE):
            cols = slice(grp * LANE, (grp + 1) * LANE)
            z = _dot(h, w_ref[:, cols]) + b_ref[:, cols]
            if grp < n_rope:
                z = _rope128(z, ct, s1, s2, 8)
            o_ref[:, cols] = z.astype(BF)

    row = lambda i: (i, 0)
    return pl.pallas_call(
        body, name="swa_proj_fwd", grid=(T // tm,),
        out_shape=jax.ShapeDtypeStruct((T, SWA_QKV), BF),
        in_specs=[pl.BlockSpec((tm, D), row), _full((8, D)), _full((D, SWA_QKV)), _full((1, SWA_QKV)),
                  pl.BlockSpec((tm, LANE), row), pl.BlockSpec((tm, LANE), row), pl.BlockSpec((tm, LANE), row)],
        out_specs=pl.BlockSpec((tm, SWA_QKV), row),
        compiler_params=_params(1),
    )(x, modl, w, b, ct_b, s1_b, s2_b)


def _swa_mask(first_has_prev):
    r = lax.broadcasted_iota(jnp.int32, (4 * SWA_W, 2 * SWA_W), 0) % SWA_W
    c = lax.broadcasted_iota(jnp.int32, (4 * SWA_W, 2 * SWA_W), 1)
    band = (c > r) & (c <= r + SWA_W)
    if first_has_prev is not None:
        band = band & ((c >= SWA_W) | first_has_prev)
    return band


def _swa_specs(T, tb):
    nsub = tb // SWA_W
    q_spec = pl.BlockSpec((tb, SWA_O), lambda i: (i, 0))
    kvc_spec = pl.BlockSpec((tb, 2 * SWA_HKV * LANE), lambda i: (i, 2))
    kvp_spec = pl.BlockSpec((SWA_W, 2 * SWA_HKV * LANE), lambda i: (jnp.maximum(i * nsub - 1, 0), 2))
    return q_spec, kvc_spec, kvp_spec


def _swa_softmax(q_ref, kall, sink_ref, g, b, i):
    q4 = jnp.concatenate([q_ref[b * SWA_W:(b + 1) * SWA_W, (4 * g + hh) * LANE:(4 * g + hh + 1) * LANE]
                          for hh in range(4)], axis=0)
    k2 = kall[b * SWA_W:(b + 2) * SWA_W]
    s = _dot_nt(q4, k2) * SWA_SCALE
    s = jnp.where(_swa_mask((i > 0) if b == 0 else None), s, -jnp.inf)
    sink = jnp.concatenate([jnp.full((SWA_W, 1), sink_ref[4 * g + hh], F32) for hh in range(4)], axis=0)
    m = jnp.maximum(jnp.max(s, axis=1, keepdims=True), sink)
    e = jnp.exp(s - m)
    es = jnp.exp(sink - m)
    linv = 1.0 / (jnp.sum(e, axis=1, keepdims=True) + es)
    return q4, k2, e * linv, es * linv


def _swa_attn_fwd(qkv, sinks, tb):
    T = qkv.shape[0]
    nsub = tb // SWA_W
    kw = SWA_HKV * LANE

    def body(q_ref, kvc_ref, kvp_ref, sink_ref, o_ref):
        i = pl.program_id(0)
        for g in range(SWA_HKV):
            kall = jnp.concatenate([kvp_ref[:, g * LANE:(g + 1) * LANE], kvc_ref[:, g * LANE:(g + 1) * LANE]], axis=0)
            vall = jnp.concatenate([kvp_ref[:, kw + g * LANE:kw + (g + 1) * LANE],
                                    kvc_ref[:, kw + g * LANE:kw + (g + 1) * LANE]], axis=0)
            for b in range(nsub):
                _, _, p, _ = _swa_softmax(q_ref, kall, sink_ref, g, b, i)
                o4 = _dot(p.astype(BF), vall[b * SWA_W:(b + 2) * SWA_W])
                for hh in range(4):
                    o_ref[b * SWA_W:(b + 1) * SWA_W, (4 * g + hh) * LANE:(4 * g + hh + 1) * LANE] = (
                        o4[hh * SWA_W:(hh + 1) * SWA_W].astype(BF))

    q_spec, kvc_spec, kvp_spec = _swa_specs(T, tb)
    return pl.pallas_call(
        body, name="swa_attn_fwd", grid=(T // tb,),
        out_shape=jax.ShapeDtypeStruct((T, SWA_O), BF),
        in_specs=[q_spec, kvc_spec, kvp_spec, pl.BlockSpec(memory_space=pltpu.SMEM)],
        out_specs=pl.BlockSpec((tb, SWA_O), lambda i: (i, 0)),
        compiler_params=_params(1),
    )(qkv, qkv, qkv, sinks)


def _loss_grad(x, tgt, tm):
    T = x.shape[0]

    def body(x_ref, t_ref, dx_ref, l_ref):
        @pl.when(pl.program_id(0) == 0)
        def _():
            l_ref[...] = jnp.zeros(l_ref.shape, F32)
        diff = x_ref[...] - t_ref[...]
        dx_ref[...] = diff * (1.0 / D)
        l_ref[0:1, :] += jnp.sum(diff * diff, axis=0, keepdims=True) * (0.5 / D)

    row = lambda i: (i, 0)
    return pl.pallas_call(
        body, name="loss_grad", grid=(T // tm,),
        out_shape=(jax.ShapeDtypeStruct((T, D), F32), jax.ShapeDtypeStruct((8, D), F32)),
        in_specs=[pl.BlockSpec((tm, D), row), pl.BlockSpec((tm, D), row)],
        out_specs=(pl.BlockSpec((tm, D), row), _full((8, D))),
        compiler_params=_params(1),
    )(x, tgt)


def _outproj_ln_bwd(dxn, u, y, w, modl, g_row, ln_g, tm, name):
    T = dxn.shape[0]
    K = w.shape[0]

    def body(dxn_ref, u_ref, y_ref, w_ref, modl_ref, g_ref, dres_ref, dy_ref, da_ref, sm_ref):
        @pl.when(pl.program_id(0) == 0)
        def _():
            sm_ref[...] = jnp.zeros(sm_ref.shape, F32)
        uu = u_ref[...]
        mu = jnp.mean(uu, axis=1, keepdims=True)
        uc = uu - mu
        rstd = lax.rsqrt(jnp.mean(uc * uc, axis=1, keepdims=True) + LN_EPS)
        xhat = uc * rstd
        dxo = dxn_ref[...]
        dyh = dxo * g_ref[...]
        du = rstd * (dyh - jnp.mean(dyh, axis=1, keepdims=True)
                     - xhat * jnp.mean(dyh * xhat, axis=1, keepdims=True))
        dy = modl_ref[g_row:g_row + 1, :] * du
        dyb = dy.astype(BF)
        dres_ref[...] = ALPHA * du
        dy_ref[...] = dyb
        da_ref[...] = _dot_nt(dyb, w_ref[...]).astype(BF)
        sm_ref[0:1, :] += jnp.sum(dxo * xhat, axis=0, keepdims=True)
        sm_ref[1:2, :] += jnp.sum(dxo, axis=0, keepdims=True)
        sm_ref[2:3, :] += jnp.sum(du * y_ref[...].astype(F32), axis=0, keepdims=True)
        sm_ref[3:4, :] += jnp.sum(dy, axis=0, keepdims=True)

    row = lambda i: (i, 0)
    return pl.pallas_call(
        body, name=name, grid=(T // tm,),
        out_shape=(jax.ShapeDtypeStruct((T, D), F32), jax.ShapeDtypeStruct((T, D), BF),
                   jax.ShapeDtypeStruct((T, K), BF), jax.ShapeDtypeStruct((8, D), F32)),
        in_specs=[pl.BlockSpec((tm, D), row), pl.BlockSpec((tm, D), row), pl.BlockSpec((tm, D), row),
                  _full((K, D)), _full((8, D)), _full((1, D))],
        out_specs=(pl.BlockSpec((tm, D), row), pl.BlockSpec((tm, D), row), pl.BlockSpec((tm, K), row),
                   _full((8, D))),
        compiler_params=_params(1),
    )(dxn, u, y, w, modl, ln_g)


def _ffn_mid_bwd(da, g, u, x, modl, dres, wg, wu, tm, tn):
    T = x.shape[0]
    nn = F // tn

    def body(da_ref, g_ref, u_ref, x_ref, modl_ref, dres_ref, wg_ref, wu_ref, dg_ref, du_ref, dx_ref, sm_ref):
        i, n = pl.program_id(0), pl.program_id(1)

        @pl.when((i == 0) & (n == 0))
        def _():
            sm_ref[...] = jnp.zeros(sm_ref.shape, F32)

        gg = g_ref[...].astype(F32)
        sg = _sigmoid(gg)
        dav = da_ref[...].astype(F32)
        dgp = (dav * u_ref[...].astype(F32) * sg * (1.0 + gg * (1.0 - sg))).astype(BF)
        dup = (dav * gg * sg).astype(BF)
        dg_ref[...] = dgp
        du_ref[...] = dup
        dh = _dot_nt(dgp, wg_ref[...]) + _dot_nt(dup, wu_ref[...])

        @pl.when(n == 0)
        def _():
            dx_ref[...] = dh

        @pl.when(n > 0)
        def _():
            dx_ref[...] += dh

        @pl.when(n == nn - 1)
        def _():
            dht = dx_ref[...]
            sm_ref[0:1, :] += jnp.sum(dht * x_ref[...], axis=0, keepdims=True)
            sm_ref[1:2, :] += jnp.sum(dht, axis=0, keepdims=True)
            dx_ref[...] = dres_ref[...] + dht * (1.0 + modl_ref[4:5, :])

    tile = pl.BlockSpec((tm, tn), lambda i, n: (i, n))
    rowd = pl.BlockSpec((tm, D), lambda i, n: (i, 0))
    wcol = pl.BlockSpec((D, tn), lambda i, n: (0, n))
    return pl.pallas_call(
        body, name="ffn_mid_bwd", grid=(T // tm, nn),
        out_shape=(jax.ShapeDtypeStruct((T, F), BF), jax.ShapeDtypeStruct((T, F), BF),
                   jax.ShapeDtypeStruct((T, D), F32), jax.ShapeDtypeStruct((8, D), F32)),
        in_specs=[tile, tile, tile, rowd, _full((8, D)), rowd, wcol, wcol],
        out_specs=(tile, tile, rowd, _full((8, D))),
        compiler_params=_params(2),
    )(da, g, u, x, modl, dres, wg, wu)


def _wgrad(a, b, tm, tk, tn, name, modl=None, rows=None):
    T, K = a.shape
    N = b.shape[1]

    def body(*refs):
        if modl is None:
            a_ref, b_ref, o_ref = refs
            av = a_ref[...]
        else:
            a_ref, modl_ref, b_ref, o_ref = refs
            av = _modulate(a_ref[...], modl_ref, rows[0], rows[1]).astype(BF)

        @pl.when(pl.program_id(2) == 0)
        def _():
            o_ref[...] = jnp.zeros(o_ref.shape, F32)
        o_ref[...] += _dot_tn(av, b_ref[...])

    in_specs = [pl.BlockSpec((tm, tk), lambda k, n, t: (t, k))]
    args = [a]
    if modl is not None:
        in_specs.append(_full((8, D)))
        args.append(modl)
    in_specs.append(pl.BlockSpec((tm, tn), lambda k, n, t: (t, n)))
    args.append(b)
    return pl.pallas_call(
        body, name=name, grid=(K // tk, N // tn, T // tm),
        out_shape=jax.ShapeDtypeStruct((K, N), F32),
        in_specs=in_specs,
        out_specs=pl.BlockSpec((tk, tn), lambda k, n, t: (k, n)),
        compiler_params=_params(3),
    )(*args)


def _flash_dq(q, k, v, o, do, lse, tb):
    H, T, _ = q.shape

    rh = min(256, tb)
    nch = tb // LANE
    c_exp = MLA_SCALE * math.log2(math.e)

    def body(q_ref, k_ref, v_ref, o_ref, do_ref, lse_ref, dq_ref, delta_ref, dq_s, l2_s, dl_s):
        i = pl.program_id(1)
        delta = jnp.sum(o_ref[...].astype(F32) * do_ref[...].astype(F32), axis=1, keepdims=True)
        dl_s[...] = jnp.broadcast_to(delta, dl_s.shape)
        l2_s[...] = jnp.broadcast_to(_row_to_col(lse_ref[0]) * math.log2(math.e), l2_s.shape)
        dq_s[...] = jnp.zeros(dq_s.shape, F32)

        def block(j, masked):
            rows = pl.ds(pl.multiple_of(j * tb, tb), tb)
            kb = k_ref[0, rows, :]
            vb = v_ref[0, rows, :]
            for hf in range(tb // rh):
                r = slice(hf * rh, (hf + 1) * rh)
                s = _dot_nt(q_ref[0, r, :], kb)
                if masked:
                    s = jnp.where(_causal_mask(tb, False)[r], s, -jnp.inf)
                dp = _dot_nt(do_ref[r, :], vb)
                l2, dl = l2_s[r, :], dl_s[r, :]
                ds = jnp.concatenate(
                    [jnp.exp2(s[:, c * LANE:(c + 1) * LANE] * c_exp - l2) * (dp[:, c * LANE:(c + 1) * LANE] - dl)
                     for c in range(nch)], axis=1)
                dq_s[r, :] += _dot(ds.astype(BF), kb)

        def step(j, carry):
            block(j, False)
            return carry

        lax.fori_loop(0, i, step, 0)
        block(i, True)
        dq_ref[0] = (dq_s[...] * MLA_SCALE).astype(BF)
        delta_ref[0] = _col_to_row(delta)

    return pl.pallas_call(
        body, name="flash_dq", grid=(H, T // tb),
        out_shape=(jax.ShapeDtypeStruct((H, T, MLA_HD), BF), jax.ShapeDtypeStruct((H, 1, T), F32)),
        in_specs=[pl.BlockSpec((1, tb, MLA_HD), lambda h, i: (h, i, 0)),
                  pl.BlockSpec((1, T, MLA_HD), lambda h, i: (h, 0, 0)),
                  pl.BlockSpec((1, T, LANE), lambda h, i: (h, 0, 0)),
                  pl.BlockSpec((tb, LANE), lambda h, i: (i, h)),
                  pl.BlockSpec((tb, LANE), lambda h, i: (i, h)),
                  pl.BlockSpec((1, 1, tb), lambda h, i: (h, 0, i))],
        out_specs=(pl.BlockSpec((1, tb, MLA_HD), lambda h, i: (h, i, 0)),
                   pl.BlockSpec((1, 1, tb), lambda h, i: (h, 0, i))),
        scratch_shapes=[pltpu.VMEM((tb, MLA_HD), F32), pltpu.VMEM((tb, LANE), F32), pltpu.VMEM((tb, LANE), F32)],
        compiler_params=_params(2),
    )(q, k, v, o, do, lse)


def _flash_dkv(q, k, v, do, lse, delta, tb):
    H, T, _ = q.shape
    nq = T // tb

    rh = min(256, tb)
    c_exp = MLA_SCALE * math.log2(math.e)

    def body(k_ref, v_ref, q_ref, do_ref, lse_ref, delta_ref, dk_ref, dv_ref, dk_s, dv_s):
        j = pl.program_id(1)
        dk_s[...] = jnp.zeros(dk_s.shape, F32)
        dv_s[...] = jnp.zeros(dv_s.shape, F32)

        def block(i, masked):
            start = pl.multiple_of(i * tb, tb)
            qb = q_ref[0, pl.ds(start, tb), :]
            dob = do_ref[pl.ds(start, tb), :]
            l2 = lse_ref[0, :, pl.ds(start, tb)] * math.log2(math.e)
            dl = delta_ref[0, :, pl.ds(start, tb)]
            for hf in range(tb // rh):
                r = slice(hf * rh, (hf + 1) * rh)
                st = _dot_nt(k_ref[0, r, :], qb)
                if masked:
                    st = jnp.where(_causal_mask(tb, True)[r], st, -jnp.inf)
                pt = jnp.exp2(st * c_exp - l2)
                dpt = _dot_nt(v_ref[0, r, :], dob)
                dst = (pt * (dpt - dl)).astype(BF)
                dv_s[r, :] += _dot(pt.astype(BF), dob)
                dk_s[r, :] += _dot(dst, qb)

        def step(i, carry):
            block(i, False)
            return carry

        block(j, True)
        lax.fori_loop(j + 1, nq, step, 0)
        dk_ref[0] = (dk_s[...] * MLA_SCALE).astype(BF)
        dv_ref[0] = dv_s[...].astype(BF)

    return pl.pallas_call(
        body, name="flash_dkv", grid=(H, nq),
        out_shape=(jax.ShapeDtypeStruct((H, T, MLA_HD), BF), jax.ShapeDtypeStruct((H, T, LANE), BF)),
        in_specs=[pl.BlockSpec((1, tb, MLA_HD), lambda h, j: (h, j, 0)),
                  pl.BlockSpec((1, tb, LANE), lambda h, j: (h, j, 0)),
                  pl.BlockSpec((1, T, MLA_HD), lambda h, j: (h, 0, 0)),
                  pl.BlockSpec((T, LANE), lambda h, j: (0, h)),
                  pl.BlockSpec((1, 1, T), lambda h, j: (h, 0, 0)),
                  pl.BlockSpec((1, 1, T), lambda h, j: (h, 0, 0))],
        out_specs=(pl.BlockSpec((1, tb, MLA_HD), lambda h, j: (h, j, 0)),
                   pl.BlockSpec((1, tb, LANE), lambda h, j: (h, j, 0))),
        scratch_shapes=[pltpu.VMEM((tb, MLA_HD), F32), pltpu.VMEM((tb, LANE), F32)],
        compiler_params=_params(2),
    )(k, v, q, do, lse, delta)


def _flash_delta(o, do, tb):
    T = o.shape[0]
    H = o.shape[1] // LANE

    def body(o_ref, do_ref, delta_ref):
        delta = jnp.sum(o_ref[...].astype(F32) * do_ref[...].astype(F32), axis=1, keepdims=True)
        delta_ref[0] = _col_to_row(delta)

    blk = pl.BlockSpec((tb, LANE), lambda h, i: (i, h))
    return pl.pallas_call(
        body, name="flash_delta", grid=(H, T // tb),
        out_shape=jax.ShapeDtypeStruct((H, 1, T), F32),
        in_specs=[blk, blk],
        out_specs=pl.BlockSpec((1, 1, tb), lambda h, i: (h, 0, i)),
        compiler_params=_params(2),
    )(o, do)


def _flash_bwd(q, k, v, do, lse, delta, tb):
    H, T, _ = q.shape
    nq = T // tb
    rh = min(256, tb)
    c_exp = MLA_SCALE * math.log2(math.e)

    def body(k_ref, v_ref, q_ref, do_ref, lse_ref, delta_ref, dq_ref, dk_ref, dv_ref, dk_s, dv_s):
        j = pl.program_id(1)

        @pl.when(j == 0)
        def _():
            def zero(cix, carry):
                dq_ref[0, pl.ds(pl.multiple_of(cix * tb, tb), tb), :] = jnp.zeros((tb, MLA_HD), F32)
                return carry
            lax.fori_loop(0, nq, zero, 0)

        dk_s[...] = jnp.zeros(dk_s.shape, F32)
        dv_s[...] = jnp.zeros(dv_s.shape, F32)

        def block(i, masked):
            start = pl.multiple_of(i * tb, tb)
            qb = q_ref[0, pl.ds(start, tb), :]
            dob = do_ref[pl.ds(start, tb), :]
            l2 = lse_ref[0, :, pl.ds(start, tb)] * math.log2(math.e)
            dl = delta_ref[0, :, pl.ds(start, tb)]
            dq_part = None
            for hf in range(tb // rh):
                r = slice(hf * rh, (hf + 1) * rh)
                kc = k_ref[0, r, :]
                st = _dot_nt(kc, qb)
                if masked:
                    st = jnp.where(_causal_mask(tb, True)[r], st, -jnp.inf)
                pt = jnp.exp2(st * c_exp - l2)
                dpt = _dot_nt(v_ref[0, r, :], dob)
                dst = (pt * (dpt - dl)).astype(BF)
                dv_s[r, :] += _dot(pt.astype(BF), dob)
                dk_s[r, :] += _dot(dst, qb)
                part = _dot_tn(dst, kc)
                dq_part = part if dq_part is None else dq_part + part
            dq_ref[0, pl.ds(start, tb), :] += dq_part * MLA_SCALE

        def step(i, carry):
            block(i, False)
            return carry

        block(j, True)
        lax.fori_loop(j + 1, nq, step, 0)
        dk_ref[0] = (dk_s[...] * MLA_SCALE).astype(BF)
        dv_ref[0] = dv_s[...].astype(BF)

    once = pl.Buffered(1)
    return pl.pallas_call(
        body, name="flash_bwd", grid=(H, nq),
        out_shape=(jax.ShapeDtypeStruct((H, T, MLA_HD), F32), jax.ShapeDtypeStruct((H, T, MLA_HD), BF),
                   jax.ShapeDtypeStruct((H, T, LANE), BF)),
        in_specs=[pl.BlockSpec((1, tb, MLA_HD), lambda h, j: (h, j, 0)),
                  pl.BlockSpec((1, tb, LANE), lambda h, j: (h, j, 0)),
                  pl.BlockSpec((1, T, MLA_HD), lambda h, j: (h, 0, 0), pipeline_mode=once),
                  pl.BlockSpec((T, LANE), lambda h, j: (0, h), pipeline_mode=once),
                  pl.BlockSpec((1, 1, T), lambda h, j: (h, 0, 0)),
                  pl.BlockSpec((1, 1, T), lambda h, j: (h, 0, 0))],
        out_specs=(pl.BlockSpec((1, T, MLA_HD), lambda h, j: (h, 0, 0), pipeline_mode=once),
                   pl.BlockSpec((1, tb, MLA_HD), lambda h, j: (h, j, 0)),
                   pl.BlockSpec((1, tb, LANE), lambda h, j: (h, j, 0))),
        scratch_shapes=[pltpu.VMEM((tb, MLA_HD), F32), pltpu.VMEM((tb, LANE), F32)],
        compiler_params=_params(2),
    )(k, v, q, do, lse, delta)


def _mla_proj_bwd(dq, dk, dv, x, modl, dres, w_in, q_norm, w_q, kv_norm, w_kv, tabs_neg, tm):
    T = x.shape[0]
    ct_a, s1_n, s2_n = tabs_neg

    def body(dq_ref, dk_ref, dv_ref, x_ref, modl_ref, dres_ref, win_ref, qn_ref, wq_ref, kvn_ref, wkv_ref,
             ct_ref, s1_ref, s2_ref, dx_ref, dwin_ref, dwq_ref, dwkv_ref, sm_ref, dqn_ref, dkvn_ref):
        @pl.when(pl.program_id(0) == 0)
        def _():
            for r in (dwin_ref, dwq_ref, dwkv_ref, sm_ref, dqn_ref, dkvn_ref):
                r[...] = jnp.zeros(r.shape, F32)

        xv = x_ref[...]
        h = _modulate(xv, modl_ref, 1, 0).astype(BF)
        lat = _dot(h, win_ref[...])
        ql, kvl = lat[:, :MLA_QR], lat[:, MLA_QR:MLA_QR + MLA_KVR]
        qhat = ql * lax.rsqrt(jnp.mean(ql * ql, axis=1, keepdims=True) + RMS_EPS)
        kvhat = kvl * lax.rsqrt(jnp.mean(kvl * kvl, axis=1, keepdims=True) + RMS_EPS)
        rq = lax.rsqrt(jnp.mean(ql * ql, axis=1, keepdims=True) + RMS_EPS)
        rkv = lax.rsqrt(jnp.mean(kvl * kvl, axis=1, keepdims=True) + RMS_EPS)
        qn = (qhat * qn_ref[...]).astype(BF)
        kvn = (kvhat * kvn_ref[...]).astype(BF)
        ct, s1, s2 = ct_ref[...], s1_ref[...], s2_ref[...]

        dqn = jnp.zeros((tm, MLA_QR), F32)
        dkvn = jnp.zeros((tm, MLA_KVR), F32)
        dkr = jnp.zeros((tm, LANE), F32)
        for hd in range(MLA_H):
            cols = slice(hd * MLA_HD, (hd + 1) * MLA_HD)
            dqh = dq_ref[hd]
            dqr = _rope128(dqh[:, LANE:], ct, s1, s2, MLA_ROPE // 2).astype(BF)
            dqh = jnp.concatenate([dqh[:, :LANE].astype(BF), dqr], axis=1)
            dqn = dqn + _dot_nt(dqh, wq_ref[:, cols])
            dwq_ref[:, cols] += _dot_tn(qn, dqh)
            dkh = dk_ref[hd]
            dkr = dkr + dkh[:, LANE:].astype(F32)
            dkvh = jnp.concatenate([dkh[:, :LANE], dv_ref[hd]], axis=1)
            dkvn = dkvn + _dot_nt(dkvh, wkv_ref[:, cols])
            dwkv_ref[:, cols] += _dot_tn(kvn, dkvh)
        dkr = _rope128(dkr, ct, s1, s2, MLA_ROPE // 2)

        dqn_ref[...] += jnp.sum(dqn * qhat, axis=0, keepdims=True)
        dkvn_ref[...] += jnp.sum(dkvn * kvhat, axis=0, keepdims=True)
        dqh_ = dqn * qn_ref[...]
        dkvh_ = dkvn * kvn_ref[...]
        dql = rq * (dqh_ - qhat * jnp.mean(dqh_ * qhat, axis=1, keepdims=True))
        dkvl = rkv * (dkvh_ - kvhat * jnp.mean(dkvh_ * kvhat, axis=1, keepdims=True))
        dlat = jnp.concatenate([dql, dkvl, dkr], axis=1).astype(BF)
        dwin_ref[...] += _dot_tn(h, dlat)
        dh = _dot_nt(dlat, win_ref[...])
        sm_ref[0:1, :] += jnp.sum(dh * xv, axis=0, keepdims=True)
        sm_ref[1:2, :] += jnp.sum(dh, axis=0, keepdims=True)
        dx_ref[...] = dres_ref[...] + dh * (1.0 + modl_ref[1:2, :])

    row = lambda i: (i, 0)
    head = lambda i: (0, i, 0)
    nq = MLA_H * MLA_HD
    return pl.pallas_call(
        body, name="mla_proj_bwd", grid=(T // tm,),
        out_shape=(jax.ShapeDtypeStruct((T, D), F32), jax.ShapeDtypeStruct((D, MLA_LAT), F32),
                   jax.ShapeDtypeStruct((MLA_QR, nq), F32), jax.ShapeDtypeStruct((MLA_KVR, nq), F32),
                   jax.ShapeDtypeStruct((8, D), F32), jax.ShapeDtypeStruct((1, MLA_QR), F32),
                   jax.ShapeDtypeStruct((1, MLA_KVR), F32)),
        in_specs=[pl.BlockSpec((MLA_H, tm, MLA_HD), head), pl.BlockSpec((MLA_H, tm, MLA_HD), head),
                  pl.BlockSpec((MLA_H, tm, LANE), head), pl.BlockSpec((tm, D), row), _full((8, D)),
                  pl.BlockSpec((tm, D), row), _full((D, MLA_LAT)), _full((1, MLA_QR)), _full((MLA_QR, nq)),
                  _full((1, MLA_KVR)), _full((MLA_KVR, nq)),
                  pl.BlockSpec((tm, LANE), row), pl.BlockSpec((tm, LANE), row), pl.BlockSpec((tm, LANE), row)],
        out_specs=(pl.BlockSpec((tm, D), row), _full((D, MLA_LAT)), _full((MLA_QR, nq)), _full((MLA_KVR, nq)),
                   _full((8, D)), _full((1, MLA_QR)), _full((1, MLA_KVR))),
        compiler_params=_params(1),
    )(dq, dk, dv, x, modl, dres, w_in, q_norm, w_q, kv_norm, w_kv, ct_a, s1_n, s2_n)


def _swa_attn_bwd(qkv, sinks, do, tb):
    T = qkv.shape[0]
    nsub = tb // SWA_W
    kw = SWA_HKV * LANE
    nstep = T // tb

    def body(q_ref, kvc_ref, kvp_ref, sink_ref, do_ref, dq_ref, dkvc_ref, dkvp_ref, dsink_ref, dk_s, dv_s):
        i = pl.program_id(0)

        @pl.when(i == 0)
        def _():
            dsink_ref[...] = jnp.zeros(dsink_ref.shape, F32)

        dk_s[...] = jnp.zeros(dk_s.shape, F32)
        dv_s[...] = jnp.zeros(dv_s.shape, F32)
        lane = lax.broadcasted_iota(jnp.int32, (1, LANE), 1)
        for g in range(SWA_HKV):
            gl = slice(g * LANE, (g + 1) * LANE)
            kall = jnp.concatenate([kvp_ref[:, gl], kvc_ref[:, gl]], axis=0)
            vall = jnp.concatenate([kvp_ref[:, kw + g * LANE:kw + (g + 1) * LANE],
                                    kvc_ref[:, kw + g * LANE:kw + (g + 1) * LANE]], axis=0)
            for b in range(nsub):
                q4, k2, p, psink = _swa_softmax(q_ref, kall, sink_ref, g, b, i)
                v2 = vall[b * SWA_W:(b + 2) * SWA_W]
                do4 = jnp.concatenate([do_ref[b * SWA_W:(b + 1) * SWA_W, (4 * g + hh) * LANE:(4 * g + hh + 1) * LANE]
                                       for hh in range(4)], axis=0)
                dp = _dot_nt(do4, v2)
                delta = jnp.sum(dp * p, axis=1, keepdims=True)
                ds = (p * (dp - delta) * SWA_SCALE).astype(BF)
                dq4 = _dot(ds, k2)
                rows2 = slice(b * SWA_W, (b + 2) * SWA_W)
                dk_s[rows2, gl] += _dot_tn(ds, q4)
                dv_s[rows2, gl] += _dot_tn(p.astype(BF), do4)
                dsk = psink * delta
                for hh in range(4):
                    hq = 4 * g + hh
                    dq_ref[b * SWA_W:(b + 1) * SWA_W, hq * LANE:(hq + 1) * LANE] = (
                        dq4[hh * SWA_W:(hh + 1) * SWA_W].astype(BF))
                    tot = jnp.sum(dsk[hh * SWA_W:(hh + 1) * SWA_W], axis=0, keepdims=True)
                    dsink_ref[0:1, :] -= jnp.where(lane == hq, tot, 0.0)
        dkvp_ref[0, :, 0:kw] = dk_s[0:SWA_W, :]
        dkvp_ref[0, :, kw:2 * kw] = dv_s[0:SWA_W, :]
        dkvc_ref[:, 0:kw] = dk_s[SWA_W:, :]
        dkvc_ref[:, kw:2 * kw] = dv_s[SWA_W:, :]

    q_spec, kvc_spec, kvp_spec = _swa_specs(T, tb)
    return pl.pallas_call(
        body, name="swa_attn_bwd", grid=(nstep,),
        out_shape=(jax.ShapeDtypeStruct((T, SWA_O), BF), jax.ShapeDtypeStruct((T, 2 * kw), F32),
                   jax.ShapeDtypeStruct((nstep, SWA_W, 2 * kw), F32), jax.ShapeDtypeStruct((8, LANE), F32)),
        in_specs=[q_spec, kvc_spec, kvp_spec, pl.BlockSpec(memory_space=pltpu.SMEM),
                  pl.BlockSpec((tb, SWA_O), lambda i: (i, 0))],
        out_specs=(pl.BlockSpec((tb, SWA_O), lambda i: (i, 0)), pl.BlockSpec((tb, 2 * kw), lambda i: (i, 0)),
                   pl.BlockSpec((1, SWA_W, 2 * kw), lambda i: (i, 0, 0)), _full((8, LANE))),
        scratch_shapes=[pltpu.VMEM((tb + SWA_W, kw), F32), pltpu.VMEM((tb + SWA_W, kw), F32)],
        compiler_params=_params(1),
    )(qkv, qkv, qkv, sinks, do)


def _swa_proj_bwd(dq, dkvc, dkvp, x, modl, dres, w, tabs_neg, tm):
    T = x.shape[0]
    nstep = T // tm
    kw = SWA_HKV * LANE
    ct_b, s1_n, s2_n = tabs_neg

    def body(dq_ref, dkvc_ref, dkvp_ref, x_ref, modl_ref, dres_ref, w_ref, ct_ref, s1_ref, s2_ref,
             dx_ref, dz_ref, sm_ref, db_ref):
        i = pl.program_id(0)

        @pl.when(i == 0)
        def _():
            sm_ref[...] = jnp.zeros(sm_ref.shape, F32)
            db_ref[...] = jnp.zeros(db_ref.shape, F32)

        ct, s1, s2 = ct_ref[...], s1_ref[...], s2_ref[...]
        has_next = i + 1 < nstep
        for grp in range(SWA_QKV // LANE):
            cols = slice(grp * LANE, (grp + 1) * LANE)
            if grp < SWA_HQ:
                z = dq_ref[:, cols].astype(F32)
            else:
                kc = slice((grp - SWA_HQ) * LANE, (grp - SWA_HQ + 1) * LANE)
                cur = dkvc_ref[:, kc]
                tail = cur[tm - SWA_W:] + jnp.where(has_next, dkvp_ref[0, :, kc], 0.0)
                z = jnp.concatenate([cur[:tm - SWA_W], tail], axis=0)
            if grp < SWA_HQ + SWA_HKV:
                z = _rope128(z, ct, s1, s2, 8)
            db_ref[0:1, cols] += jnp.sum(z, axis=0, keepdims=True)
            dz_ref[:, cols] = z.astype(BF)
        dh = _dot_nt(dz_ref[...], w_ref[...])
        sm_ref[0:1, :] += jnp.sum(dh * x_ref[...], axis=0, keepdims=True)
        sm_ref[1:2, :] += jnp.sum(dh, axis=0, keepdims=True)
        dx_ref[...] = dres_ref[...] + dh * (1.0 + modl_ref[1:2, :])

    row = lambda i: (i, 0)
    return pl.pallas_call(
        body, name="swa_proj_bwd", grid=(nstep,),
        out_shape=(jax.ShapeDtypeStruct((T, D), F32), jax.ShapeDtypeStruct((T, SWA_QKV), BF),
                   jax.ShapeDtypeStruct((8, D), F32), jax.ShapeDtypeStruct((8, SWA_QKV), F32)),
        in_specs=[pl.BlockSpec((tm, SWA_O), row), pl.BlockSpec((tm, 2 * kw), row),
                  pl.BlockSpec((1, SWA_W, 2 * kw), lambda i: (jnp.minimum(i + 1, nstep - 1), 0, 0)),
                  pl.BlockSpec((tm, D), row), _full((8, D)), pl.BlockSpec((tm, D), row), _full((D, SWA_QKV)),
                  pl.BlockSpec((tm, LANE), row), pl.BlockSpec((tm, LANE), row), pl.BlockSpec((tm, LANE), row)],
        out_specs=(pl.BlockSpec((tm, D), row), pl.BlockSpec((tm, SWA_QKV), row), _full((8, D)),
                   _full((8, SWA_QKV))),
        compiler_params=_params(1),
    )(dq, dkvc, dkvp, x, modl, dres, w, ct_b, s1_n, s2_n)


def _adamw(gparts, w, m, v, name):
    P, R, C = gparts.shape
    tr = R
    for cand in (512, 256, 128):
        if R % cand == 0 and R > cand:
            tr = cand
            break
    c1 = 1.0 / (1.0 - ADAM_B1 ** ADAM_STEP)
    c2 = 1.0 / (1.0 - ADAM_B2 ** ADAM_STEP)

    def body(gp_ref, w_ref, m_ref, v_ref, g_ref, d_ref, nm_ref, nv_ref):
        g = gp_ref[0].astype(F32)
        for p in range(1, P):
            g = g + gp_ref[p].astype(F32)
        nm = ADAM_B1 * m_ref[...] + (1.0 - ADAM_B1) * g
        nv = ADAM_B2 * v_ref[...] + (1.0 - ADAM_B2) * (g * g)
        g_ref[...] = g
        nm_ref[...] = nm
        nv_ref[...] = nv
        d_ref[...] = -ADAM_LR * ((nm * c1) / (jnp.sqrt(nv * c2) + ADAM_EPS) + ADAM_WD * w_ref[...])

    blk = pl.BlockSpec((tr, C), lambda i: (i, 0))
    return pl.pallas_call(
        body, name=name, grid=(R // tr,),
        out_shape=(jax.ShapeDtypeStruct((R, C), F32),) * 4,
        in_specs=[pl.BlockSpec((P, tr, C), lambda i: (0, i, 0)), blk, blk, blk],
        out_specs=(blk,) * 4,
        compiler_params=_params(1),
    )(gparts, w, m, v)


PACK_W = 1024

BIG = {
    "ffn_w_gate": ((DEPTH, D, F // NDEV), 2),
    "ffn_w_up": ((DEPTH, D, F // NDEV), 2),
    "ffn_w_down": ((DEPTH, F // NDEV, D), 1),
    "mla_w_in": ((2, D // NDEV, 704), 1),
    "mla_w_q_b": ((2, MLA_QR, 1536 // NDEV), 2),
    "mla_w_kv_b": ((2, MLA_KVR, 2048 // NDEV), 2),
    "mla_w_o": ((2, D // NDEV, D), 1),
    "swa_w_qkv": ((2, D, 1536 // NDEV), 2),
    "swa_w_o": ((2, D // NDEV, D), 1),
}


def _pack_rows(n):
    return -(-n // (16 * PACK_W)) * 16


def _pack_local(blocks):
    parts = []
    for name, (shape, _) in BIG.items():
        n = math.prod(shape)
        flat = blocks[name].astype(BF).reshape(-1)
        parts.append(jnp.pad(flat, (0, _pack_rows(n) * PACK_W - n)).reshape(-1, PACK_W))
    return jnp.concatenate(parts, axis=0)


def _pack_full(full):
    parts = []
    for name, (shape, axis) in BIG.items():
        a = full[name]
        split = a.shape[:axis] + (NDEV, a.shape[axis] // NDEV) + a.shape[axis + 1:]
        a = jnp.moveaxis(a.reshape(split), axis, 0).astype(BF).reshape(NDEV, -1)
        n = math.prod(shape)
        parts.append(jnp.pad(a, ((0, 0), (0, _pack_rows(n) * PACK_W - n))).reshape(NDEV, -1, PACK_W))
    return jnp.concatenate(parts, axis=1)


def _unpack_blocks(packed):
    out, r0 = {}, 0
    for name, (shape, _) in BIG.items():
        n = math.prod(shape)
        rows = _pack_rows(n)
        out[name] = packed[:, r0:r0 + rows].reshape(NDEV, -1)[:, :n].reshape((NDEV,) + shape)
        r0 += rows
    return out


def _unpack_full(packed):
    out = {}
    for name, blk in _unpack_blocks(packed).items():
        axis = BIG[name][1]
        a = jnp.moveaxis(blk, 0, axis)
        out[name] = a.reshape(a.shape[:axis] + (a.shape[axis] * a.shape[axis + 1],) + a.shape[axis + 2:])
    return out


def _pad_heads(a, axis, nheads, width, to):
    shp = a.shape[:axis] + (nheads, width) + a.shape[axis + 1:]
    a = a.reshape(shp)
    pad = [(0, 0)] * a.ndim
    pad[axis + 1] = (0, to - width)
    a = jnp.pad(a, pad)
    return a.reshape(a.shape[:axis] + (nheads * to,) + a.shape[axis + 2:])


def _unpad_heads(a, axis, nheads, width, to):
    shp = a.shape[:axis] + (nheads, to) + a.shape[axis + 1:]
    a = lax.slice_in_dim(a.reshape(shp), 0, width, axis=axis + 1)
    return a.reshape(a.shape[:axis] + (nheads * width,) + a.shape[axis + 2:])


def _swa_pad_cols(a):
    return _pad_heads(a, a.ndim - 1, SWA_HQ + 2 * SWA_HKV, 64, LANE)


def _rope_tables(positions, half):
    rot = 2 * half
    inv = ROPE_THETA ** (-jnp.arange(0, rot, 2, dtype=F32) / rot)
    ang = positions.astype(F32)[:, None] * inv
    cos, sin = jnp.cos(ang), jnp.sin(ang)
    T = positions.shape[0]
    ones = jnp.ones((T, LANE - rot), F32)
    zeros = jnp.zeros((T, LANE - rot), F32)
    zh = jnp.zeros((T, half), F32)
    ct = jnp.concatenate([cos, cos, ones], axis=1)
    s1 = jnp.concatenate([-sin, zh, zeros], axis=1)
    s2 = jnp.concatenate([zh, sin, zeros], axis=1)
    return (ct, s1, s2), (ct, -s1, -s2)


def _small_pack(vecs, rows):
    flat = jnp.concatenate([v.astype(F32).reshape(-1) for v in vecs])
    return jnp.pad(flat, (0, rows * LANE - flat.shape[0])).reshape(rows, LANE)


def _small_unpack(buf, shapes):
    flat = buf.reshape(NDEV, -1)
    out, o = [], 0
    for shp in shapes:
        n = math.prod(shp)
        out.append(flat[:, o:o + n].reshape((NDEV,) + shp))
        o += n
    return out


def _step(inp):
    x = inp["x"][0]
    tgt = inp["loss_target"][0]
    T = x.shape[0]
    tm = min(512, T)
    tmf = min(256, T)
    tb = min(512, T)
    tbf = min(1024, T)
    tnf = F // 2
    me = 4 * lax.axis_index("x") + 2 * lax.axis_index("y") + lax.axis_index("c")

    small_in = _small_pack([inp["c"], inp["swa_b_qkv"], inp["swa_b_o"]], 16)
    c_all, bqkv_blk, bo_blk = _small_unpack(_exchange(small_in, True, "gather_small"),
                                            [(D,), (2, 1536 // NDEV), (2, D // NDEV)])
    swa_b_qkv = jnp.moveaxis(bqkv_blk, 0, 1).reshape(2, 1536)
    swa_b_o = jnp.moveaxis(bo_blk, 0, 1).reshape(2, D)

    wfull = _unpack_full(_exchange(_pack_local(inp), True, "gather_weights"))

    ncol = 6 * D // NDEV
    ada_b_loc = lax.dynamic_slice_in_dim(inp["ada_b"], me * ncol, ncol, axis=1)[:, None, :]
    mod_all = _mod_all(c_all, inp["ada_w"], ada_b_loc)
    mod_src = jnp.moveaxis(mod_all, 1, 0).reshape(NDEV, DEPTH * ncol // LANE, LANE)
    mod_got = _exchange(mod_src, False, "scatter_mod").reshape(NDEV, DEPTH, ncol)
    mod = jnp.moveaxis(mod_got, 0, 1).reshape(DEPTH, 6, D)
    modl = jnp.pad(mod, ((0, 0), (0, 2), (0, 0)))

    w_in = jnp.pad(wfull["mla_w_in"], ((0, 0), (0, 0), (0, MLA_LAT - 704)))
    w_q = _pad_heads(wfull["mla_w_q_b"], 2, MLA_H, 192, MLA_HD)
    w_kv = wfull["mla_w_kv_b"]
    w_o_mla = wfull["mla_w_o"]
    w_qkv = _swa_pad_cols(wfull["swa_w_qkv"])
    b_qkv = _swa_pad_cols(swa_b_qkv)
    w_o_swa = _pad_heads(wfull["swa_w_o"], 1, SWA_HQ, 64, LANE)
    zero_bias = jnp.zeros((1, D), F32)

    pos = inp["positions"][0]
    tabs_a, tabs_a_neg = _rope_tables(pos, MLA_ROPE // 2)
    tabs_b, tabs_b_neg = _rope_tables(pos, 8)

    saved = []
    xs = x
    for i in range(DEPTH):
        j = i // 2
        st = {"x0": xs}
        if i % 2 == 0:
            q, k, v = _mla_proj_fwd(xs, modl[i], w_in[j], inp["mla_q_norm"][j][None], w_q[j],
                                    inp["mla_kv_norm"][j][None], w_kv[j], tabs_a, tm)
            o, lse = _flash_fwd(q, k, v, tbf)
            st.update(q=q, k=k, v=v, o=o, lse=lse)
            w_o, b_o = w_o_mla[j], zero_bias
        else:
            qkv = _swa_proj_fwd(xs, modl[i], w_qkv[j], b_qkv[j][None], tabs_b, tm)
            o = _swa_attn_fwd(qkv, inp["swa_sinks"][j], tb)
            st.update(qkv=qkv, o=o)
            w_o, b_o = w_o_swa[j], swa_b_o[j][None]
        y, u, xs = _outproj_ln_fwd(o, w_o, b_o, xs, modl[i], 2, inp["ln_mix_g"][i][None],
                                   inp["ln_mix_b"][i][None], tm, f"mix_out_fwd_{i % 2}")
        st.update(y_m=y, u_m=u, x1=xs, w_o=w_o)
        g, up, a = _ffn_up_fwd(xs, modl[i], wfull["ffn_w_gate"][i], wfull["ffn_w_up"][i], tmf, tnf)
        y, u, xs = _outproj_ln_fwd(a, wfull["ffn_w_down"][i], zero_bias, xs, modl[i], 5,
                                   inp["ln_ffn_g"][i][None], inp["ln_ffn_b"][i][None], tmf, "ffn_out_fwd")
        st.update(g=g, up=up, a=a, y_f=y, u_f=u)
        saved.append(st)

    dx, loss_rows = _loss_grad(xs, tgt, tm)
    loss = lax.psum(jnp.sum(loss_rows[0]), ("x", "y", "c"))

    gfull = {n: [None] * (DEPTH if n.startswith("ffn") else 2) for n in BIG}
    dmod = [None] * DEPTH
    g_ln = {n: [None] * DEPTH for n in ("ln_mix_g", "ln_mix_b", "ln_ffn_g", "ln_ffn_b")}
    g_qn, g_kvn, g_sink, g_bqkv, g_bo = [None] * 2, [None] * 2, [None] * 2, [None] * 2, [None] * 2
    for i in reversed(range(DEPTH)):
        j = i // 2
        st = saved[i]
        dres, dy, da, sm = _outproj_ln_bwd(dx, st["u_f"], st["y_f"], wfull["ffn_w_down"][i], modl[i], 5,
                                           inp["ln_ffn_g"][i][None], tmf, "ffn_out_bwd")
        g_ln["ln_ffn_g"][i], g_ln["ln_ffn_b"][i], dg_f = sm[0], sm[1], sm[2]
        gfull["ffn_w_down"][i] = _wgrad(st["a"], dy, tm, F // 2, D, "wgrad_down")
        dgp, dup, dx, sm = _ffn_mid_bwd(da, st["g"], st["up"], st["x1"], modl[i], dres,
                                        wfull["ffn_w_gate"][i], wfull["ffn_w_up"][i], tmf, tnf)
        dsc_f, dsh_f = sm[0], sm[1]
        gfull["ffn_w_gate"][i] = _wgrad(st["x1"], dgp, tm, D, tnf, "wgrad_gate", modl[i], (4, 3))
        gfull["ffn_w_up"][i] = _wgrad(st["x1"], dup, tm, D, tnf, "wgrad_up", modl[i], (4, 3))

        dres, dy, do, sm = _outproj_ln_bwd(dx, st["u_m"], st["y_m"], st["w_o"], modl[i], 2,
                                           inp["ln_mix_g"][i][None], tm, f"mix_out_bwd_{i % 2}")
        g_ln["ln_mix_g"][i], g_ln["ln_mix_b"][i], dg_m = sm[0], sm[1], sm[2]
        if i % 2 == 0:
            gfull["mla_w_o"][j] = _wgrad(st["o"], dy, tm, D, D, "wgrad_mla_o")
            delta = _flash_delta(st["o"], do, tb)
            dq, dk, dv = _flash_bwd(st["q"], st["k"], st["v"], do, st["lse"], delta, tbf)
            dx, dwin, dwq, dwkv, sm, dqn, dkvn = _mla_proj_bwd(
                dq, dk, dv, st["x0"], modl[i], dres, w_in[j], inp["mla_q_norm"][j][None], w_q[j],
                inp["mla_kv_norm"][j][None], w_kv[j], tabs_a_neg, tm)
            gfull["mla_w_in"][j] = dwin[:, :704]
            gfull["mla_w_q_b"][j] = _unpad_heads(dwq, 1, MLA_H, 192, MLA_HD)
            gfull["mla_w_kv_b"][j] = dwkv
            g_qn[j], g_kvn[j] = dqn[0], dkvn[0]
        else:
            g_bo[j] = sm[3]
            dwo = _wgrad(st["o"], dy, tm, SWA_O // 2, D, "wgrad_swa_o")
            gfull["swa_w_o"][j] = _unpad_heads(dwo, 0, SWA_HQ, 64, LANE)
            dq, dkvc, dkvp, dsink = _swa_attn_bwd(st["qkv"], inp["swa_sinks"][j], do, tb)
            g_sink[j] = dsink[0, :SWA_HQ]
            dx, dz, sm, db = _swa_proj_bwd(dq, dkvc, dkvp, st["x0"], modl[i], dres, w_qkv[j], tabs_b_neg, tm)
            dwqkv = _wgrad(st["x0"], dz, tm, D, SWA_QKV // 2, "wgrad_swa_qkv", modl[i], (1, 0))
            gfull["swa_w_qkv"][j] = _unpad_heads(dwqkv, 1, SWA_HQ + 2 * SWA_HKV, 64, LANE)
            g_bqkv[j] = _unpad_heads(db[0], 0, SWA_HQ + 2 * SWA_HKV, 64, LANE)
        dmod[i] = jnp.stack([sm[1], sm[0], dg_m, dsh_f, dsc_f, dg_f])
    grad_x = dx[None]

    small_shapes = [(DEPTH, 6 * D), (DEPTH, D), (DEPTH, D), (DEPTH, D), (DEPTH, D), (2, MLA_QR), (2, MLA_KVR),
                    (2, SWA_HQ), (2, 1536), (2, D)]
    small_vals = [jnp.stack(dmod).reshape(DEPTH, 6 * D), jnp.stack(g_ln["ln_mix_g"]), jnp.stack(g_ln["ln_mix_b"]),
                  jnp.stack(g_ln["ln_ffn_g"]), jnp.stack(g_ln["ln_ffn_b"]), jnp.stack(g_qn), jnp.stack(g_kvn),
                  jnp.stack(g_sink), jnp.stack(g_bqkv), jnp.stack(g_bo)]
    nsmall = sum(math.prod(s) for s in small_shapes)
    small_rows = -(-nsmall // (8 * LANE)) * 8
    (dmod_all, p_lmg, p_lmb, p_lfg, p_lfb, p_qn, p_kvn, p_sink, p_bqkv, p_bo) = _small_unpack(
        _exchange(_small_pack(small_vals, small_rows), True, "gather_small_grads"), small_shapes)

    gpacked = _exchange(_pack_full({n: jnp.stack(v) for n, v in gfull.items()}), False, "scatter_grads")
    gparts = _unpack_blocks(gpacked)

    res = {}

    def update(name, parts):
        w = inp[name]
        shp = w.shape
        r2 = (math.prod(shp[:-1]), shp[-1])
        outs = _adamw(parts.reshape((parts.shape[0],) + r2), w.reshape(r2), inp["m_" + name].reshape(r2),
                      inp["v_" + name].reshape(r2), "adamw_" + name)
        res[name] = tuple(o.reshape(shp) for o in outs)

    dmod_loc = lax.dynamic_slice_in_dim(dmod_all, me * ncol, ncol, axis=2)
    g_ada_w = _ada_w_grad(c_all.T, jnp.moveaxis(dmod_loc, 0, 1))
    update("ada_w", g_ada_w[None])
    update("ada_b", dmod_all)
    update("ln_mix_g", p_lmg)
    update("ln_mix_b", p_lmb)
    update("ln_ffn_g", p_lfg)
    update("ln_ffn_b", p_lfb)
    for name in BIG:
        update(name, gparts[name])
    update("mla_q_norm", p_qn)
    update("mla_kv_norm", p_kvn)
    update("swa_sinks", p_sink)
    nb = 1536 // NDEV
    update("swa_b_qkv", lax.dynamic_slice_in_dim(p_bqkv, me * nb, nb, axis=2))
    update("swa_b_o", lax.dynamic_slice_in_dim(p_bo, me * (D // NDEV), D // NDEV, axis=2))
    return loss, grad_x, res


WEIGHTS = ["ada_w", "ada_b", "ln_mix_g", "ln_mix_b", "ln_ffn_g", "ln_ffn_b", "ffn_w_gate", "ffn_w_up",
           "ffn_w_down", "mla_w_in", "mla_q_norm", "mla_w_q_b", "mla_kv_norm", "mla_w_kv_b", "mla_w_o",
           "swa_w_qkv", "swa_b_qkv", "swa_sinks", "swa_w_o", "swa_b_o"]
INPUTS = (["x", "c", "positions"] + WEIGHTS + ["loss_target"] + ["m_" + n for n in WEIGHTS]
          + ["v_" + n for n in WEIGHTS])


def kernel(x, c, positions, ada_w, ada_b, ln_mix_g, ln_mix_b, ln_ffn_g, ln_ffn_b, ffn_w_gate, ffn_w_up, ffn_w_down, mla_w_in, mla_q_norm, mla_w_q_b, mla_kv_norm, mla_w_kv_b, mla_w_o, swa_w_qkv, swa_b_qkv, swa_sinks, swa_w_o, swa_b_o, loss_target, m_ada_w, m_ada_b, m_ln_mix_g, m_ln_mix_b, m_ln_ffn_g, m_ln_ffn_b, m_ffn_w_gate, m_ffn_w_up, m_ffn_w_down, m_mla_w_in, m_mla_q_norm, m_mla_w_q_b, m_mla_kv_norm, m_mla_w_kv_b, m_mla_w_o, m_swa_w_qkv, m_swa_b_qkv, m_swa_sinks, m_swa_w_o, m_swa_b_o, v_ada_w, v_ada_b, v_ln_mix_g, v_ln_mix_b, v_ln_ffn_g, v_ln_ffn_b, v_ffn_w_gate, v_ffn_w_up, v_ffn_w_down, v_mla_w_in, v_mla_q_norm, v_mla_w_q_b, v_mla_kv_norm, v_mla_w_kv_b, v_mla_w_o, v_swa_w_qkv, v_swa_b_qkv, v_swa_sinks, v_swa_w_o, v_swa_b_o):
    args = (x, c, positions, ada_w, ada_b, ln_mix_g, ln_mix_b, ln_ffn_g, ln_ffn_b, ffn_w_gate, ffn_w_up, ffn_w_down, mla_w_in, mla_q_norm, mla_w_q_b, mla_kv_norm, mla_w_kv_b, mla_w_o, swa_w_qkv, swa_b_qkv, swa_sinks, swa_w_o, swa_b_o, loss_target, m_ada_w, m_ada_b, m_ln_mix_g, m_ln_mix_b, m_ln_ffn_g, m_ln_ffn_b, m_ffn_w_gate, m_ffn_w_up, m_ffn_w_down, m_mla_w_in, m_mla_q_norm, m_mla_w_q_b, m_mla_kv_norm, m_mla_w_kv_b, m_mla_w_o, m_swa_w_qkv, m_swa_b_qkv, m_swa_sinks, m_swa_w_o, m_swa_b_o, v_ada_w, v_ada_b, v_ln_mix_g, v_ln_mix_b, v_ln_ffn_g, v_ln_ffn_b, v_ffn_w_gate, v_ffn_w_up, v_ffn_w_down, v_mla_w_in, v_mla_q_norm, v_mla_w_q_b, v_mla_kv_norm, v_mla_w_kv_b, v_mla_w_o, v_swa_w_qkv, v_swa_b_qkv, v_swa_sinks, v_swa_w_o, v_swa_b_o)
    assert len(args) == len(INPUTS)
    loss, grad_x, res = _step(dict(zip(INPUTS, args)))
    return (loss, grad_x, *[res[n][0] for n in WEIGHTS], *[res[n][1] for n in WEIGHTS],
            *[res[n][2] for n in WEIGHTS], *[res[n][3] for n in WEIGHTS])
```

```python
import functools
import math

import jax
import jax.numpy as jnp
from jax import lax
from jax.experimental import pallas as pl
from jax.experimental.pallas import tpu as pltpu

F32 = jnp.float32
BF = jnp.bfloat16

NDEV = 8
D = 1024
DEPTH = 4
F = 2816
ALPHA = (2 * DEPTH) ** 0.25
LN_EPS = 1e-5
RMS_EPS = 1e-6
ROPE_THETA = 500000.0

MLA_H = 8
MLA_QR = 384
MLA_KVR = 256
MLA_ROPE = 64
MLA_LAT = 768
MLA_HD = 256
MLA_SCALE = (128 + 64) ** -0.5

SWA_HQ = 16
SWA_HKV = 4
SWA_W = 128
SWA_SCALE = 64 ** -0.5
SWA_QKV = (SWA_HQ + 2 * SWA_HKV) * 128
SWA_O = SWA_HQ * 128

LANE = 128
VMEM_LIMIT = 56 * 2 ** 20

ADAM_LR, ADAM_B1, ADAM_B2, ADAM_EPS, ADAM_WD, ADAM_STEP = 0.001, 0.9, 0.999, 1e-8, 0.01, 10


def _params(n_axes):
    return pltpu.CompilerParams(dimension_semantics=("arbitrary",) * n_axes, vmem_limit_bytes=VMEM_LIMIT)


def _dot(a, b):
    return jnp.dot(a, b, preferred_element_type=F32)


def _dot_nt(a, b):
    return lax.dot_general(a, b, (((1,), (1,)), ((), ())), preferred_element_type=F32)


def _dot_tn(a, b):
    return lax.dot_general(a, b, (((0,), (0,)), ((), ())), preferred_element_type=F32)


def _full(shape):
    return pl.BlockSpec(shape, lambda *_: (0,) * len(shape))


def _sigmoid(x):
    return 1.0 / (1.0 + jnp.exp(-x))


def _rope128(x, ct, s1, s2, half):
    return x * ct + pltpu.roll(x, LANE - half, 1) * s1 + pltpu.roll(x, half, 1) * s2


def _eye(n):
    return lax.broadcasted_iota(jnp.int32, (n, n), 0) == lax.broadcasted_iota(jnp.int32, (n, n), 1)


def _col_to_row(col):
    n = col.shape[0]
    return jnp.sum(jnp.where(_eye(n), col, 0.0), axis=0, keepdims=True)


def _row_to_col(row):
    n = row.shape[1]
    return jnp.sum(jnp.where(_eye(n), row, 0.0), axis=1, keepdims=True)


def _modulate(x, modl_ref, sc_row, sh_row):
    return x * (1.0 + modl_ref[sc_row:sc_row + 1, :]) + modl_ref[sh_row:sh_row + 1, :]


def _exchange(src, gather, name):
    blk = tuple(src.shape) if gather else tuple(src.shape[1:])

    def body(src_ref, out_ref, send_sems, recv_sems, local_sem):
        x, y, c = lax.axis_index("x"), lax.axis_index("y"), lax.axis_index("c")
        me = 4 * x + 2 * y + c

        def piece(dev):
            return src_ref if gather else src_ref.at[dev]

        mine = pltpu.make_async_copy(piece(me), out_ref.at[me], local_sem)
        mine.start()
        sends, recvs = [], []
        for k in range(1, NDEV):
            px = 1 - x if k & 4 else x
            py = 1 - y if k & 2 else y
            pc = 1 - c if k & 1 else c
            peer = 4 * px + 2 * py + pc
            common = dict(send_sem=send_sems.at[k - 1], recv_sem=recv_sems.at[k - 1],
                          device_id=(px, py, pc), device_id_type=pl.DeviceIdType.MESH)
            snd = pltpu.make_async_remote_copy(src_ref=piece(peer), dst_ref=out_ref.at[me], **common)
            snd.start()
            sends.append(snd)
            recvs.append(pltpu.make_async_remote_copy(src_ref=piece(peer), dst_ref=out_ref.at[peer], **common))
        for r in recvs:
            r.wait_recv()
        for s in sends:
            s.wait_send()
        mine.wait()

    return pl.pallas_call(
        body, name=name,
        out_shape=jax.ShapeDtypeStruct((NDEV,) + blk, src.dtype),
        in_specs=[pl.BlockSpec(memory_space=pltpu.HBM)],
        out_specs=pl.BlockSpec(memory_space=pltpu.HBM),
        scratch_shapes=[pltpu.SemaphoreType.DMA((NDEV - 1,)), pltpu.SemaphoreType.DMA((NDEV - 1,)),
                        pltpu.SemaphoreType.DMA],
    )(src)


def _mod_all(c_all, ada_w, ada_b_loc):
    ncol = ada_w.shape[2]

    def body(c_ref, w_ref, b_ref, o_ref):
        cv = c_ref[...]
        cond = cv * _sigmoid(cv)
        o_ref[0] = _dot(cond.astype(BF), w_ref[0].astype(BF)) + b_ref[0]

    return pl.pallas_call(
        body, name="mod_all", grid=(DEPTH,),
        out_shape=jax.ShapeDtypeStruct((DEPTH, NDEV, ncol), F32),
        in_specs=[_full((NDEV, D)), pl.BlockSpec((1, D, ncol), lambda i: (i, 0, 0)),
                  pl.BlockSpec((1, 1, ncol), lambda i: (i, 0, 0))],
        out_specs=pl.BlockSpec((1, NDEV, ncol), lambda i: (i, 0, 0)),
        compiler_params=_params(1),
    )(c_all, ada_w, ada_b_loc)


def _ada_w_grad(c_all_t, dmod_loc):
    ncol = dmod_loc.shape[2]

    def body(ct_ref, dm_ref, o_ref):
        cv = ct_ref[...]
        cond = cv * _sigmoid(cv)
        acc = cond[:, 0:1] * dm_ref[0, 0:1, :]
        for b in range(1, NDEV):
            acc = acc + cond[:, b:b + 1] * dm_ref[0, b:b + 1, :]
        o_ref[0] = acc

    return pl.pallas_call(
        body, name="ada_w_grad", grid=(DEPTH,),
        out_shape=jax.ShapeDtypeStruct((DEPTH, D, ncol), F32),
        in_specs=[_full((D, NDEV)), pl.BlockSpec((1, NDEV, ncol), lambda i: (i, 0, 0))],
        out_specs=pl.BlockSpec((1, D, ncol), lambda i: (i, 0, 0)),
        compiler_params=_params(1),
    )(c_all_t, dmod_loc)


def _mla_proj_fwd(x, modl, w_in, q_norm, w_q, kv_norm, w_kv, tabs, tm):
    T = x.shape[0]
    ct_a, s1_a, s2_a = tabs

    def body(x_ref, modl_ref, win_ref, qn_ref, wq_ref, kvn_ref, wkv_ref, ct_ref, s1_ref, s2_ref,
             q_ref, k_ref, v_ref):
        h = _modulate(x_ref[...], modl_ref, 1, 0).astype(BF)
        lat = _dot(h, win_ref[...])
        ql, kvl, kr = lat[:, :MLA_QR], lat[:, MLA_QR:MLA_QR + MLA_KVR], lat[:, MLA_QR + MLA_KVR:]
        qn = (ql * lax.rsqrt(jnp.mean(ql * ql, axis=1, keepdims=True) + RMS_EPS) * qn_ref[...]).astype(BF)
        kvn = (kvl * lax.rsqrt(jnp.mean(kvl * kvl, axis=1, keepdims=True) + RMS_EPS) * kvn_ref[...]).astype(BF)
        ct, s1, s2 = ct_ref[...], s1_ref[...], s2_ref[...]
        kr = _rope128(kr, ct, s1, s2, MLA_ROPE // 2).astype(BF)
        for hd in range(MLA_H):
            cols = slice(hd * MLA_HD, (hd + 1) * MLA_HD)
            qh = _dot(qn, wq_ref[:, cols])
            q_ref[hd, :, 0:LANE] = qh[:, :LANE].astype(BF)
            q_ref[hd, :, LANE:MLA_HD] = _rope128(qh[:, LANE:], ct, s1, s2, MLA_ROPE // 2).astype(BF)
            kvh = _dot(kvn, wkv_ref[:, cols])
            k_ref[hd, :, 0:LANE] = kvh[:, :LANE].astype(BF)
            k_ref[hd, :, LANE:MLA_HD] = kr
            v_ref[hd, :, 0:LANE] = kvh[:, LANE:].astype(BF)
            v_ref[hd, :, LANE:2 * LANE] = jnp.ones((tm, LANE), BF)

    row = lambda i: (i, 0)
    head = lambda i: (0, i, 0)
    return pl.pallas_call(
        body, name="mla_proj_fwd", grid=(T // tm,),
        out_shape=(jax.ShapeDtypeStruct((MLA_H, T, MLA_HD), BF), jax.ShapeDtypeStruct((MLA_H, T, MLA_HD), BF),
                   jax.ShapeDtypeStruct((MLA_H, T, 2 * LANE), BF)),
        in_specs=[pl.BlockSpec((tm, D), row), _full((8, D)), _full((D, MLA_LAT)), _full((1, MLA_QR)),
                  _full((MLA_QR, MLA_H * MLA_HD)), _full((1, MLA_KVR)), _full((MLA_KVR, MLA_H * MLA_HD)),
                  pl.BlockSpec((tm, LANE), row), pl.BlockSpec((tm, LANE), row), pl.BlockSpec((tm, LANE), row)],
        out_specs=(pl.BlockSpec((MLA_H, tm, MLA_HD), head), pl.BlockSpec((MLA_H, tm, MLA_HD), head),
                   pl.BlockSpec((MLA_H, tm, 2 * LANE), head)),
        compiler_params=_params(1),
    )(x, modl, w_in, q_norm, w_q, kv_norm, w_kv, ct_a, s1_a, s2_a)


def _causal_mask(row0, nrows, ncols, transposed):
    row = lax.broadcasted_iota(jnp.int32, (nrows, ncols), 0) + row0
    col = lax.broadcasted_iota(jnp.int32, (nrows, ncols), 1)
    return (row <= col) if transposed else (col <= row)


def _flash_fwd(q, k, v, tb):
    H, T, _ = q.shape
    rh = min(256, tb)
    nch = tb // LANE
    c_exp = MLA_SCALE * math.log2(math.e)

    def body(q_ref, k_ref, v_ref, o_ref, lse_ref, m_s, acc_s):
        i = pl.program_id(1)
        m_s[...] = jnp.full(m_s.shape, -jnp.inf, F32)
        acc_s[...] = jnp.zeros(acc_s.shape, F32)

        def block(j, masked):
            rows = pl.ds(pl.multiple_of(j * tb, tb), tb)
            kb = k_ref[0, rows, :]
            vb = v_ref[0, rows, :]
            for hf in range(tb // rh):
                r = slice(hf * rh, (hf + 1) * rh)
                s = _dot_nt(q_ref[0, r, :], kb)
                if masked:
                    s = jnp.where(_causal_mask(hf * rh, rh, tb, False), s, -jnp.inf)
                sc = [s[:, c * LANE:(c + 1) * LANE] for c in range(nch)]
                mx = sc[0]
                for c in range(1, nch):
                    mx = jnp.maximum(mx, sc[c])
                m_prev = m_s[r, :]
                m_new = jnp.maximum(m_prev, jnp.max(mx, axis=1, keepdims=True))
                p = jnp.concatenate([jnp.exp2((sc[c] - m_new) * c_exp) for c in range(nch)], axis=1)
                corr = jnp.exp2((m_prev - m_new) * c_exp)
                acc_s[r, :] = jnp.concatenate([corr, corr], axis=1) * acc_s[r, :] + _dot(p.astype(BF), vb)
                m_s[r, :] = m_new

        def step(j, carry):
            block(j, False)
            return carry

        lax.fori_loop(0, i, step, 0)
        block(i, True)
        l = acc_s[:, LANE:]
        o_ref[...] = (acc_s[:, :LANE] / l).astype(BF)
        lse = (m_s[...] * MLA_SCALE + jnp.log(l))[:, 0:1]
        for c0 in range(0, tb, rh):
            lse_ref[0, :, c0:c0 + rh] = _col_to_row(lse[c0:c0 + rh])

    return pl.pallas_call(
        body, name="flash_fwd", grid=(H, T // tb),
        out_shape=(jax.ShapeDtypeStruct((T, H * LANE), BF), jax.ShapeDtypeStruct((H, 1, T), F32)),
        in_specs=[pl.BlockSpec((1, tb, MLA_HD), lambda h, i: (h, i, 0)),
                  pl.BlockSpec((1, T, MLA_HD), lambda h, i: (h, 0, 0)),
                  pl.BlockSpec((1, T, 2 * LANE), lambda h, i: (h, 0, 0))],
        out_specs=(pl.BlockSpec((tb, LANE), lambda h, i: (i, h)),
                   pl.BlockSpec((1, 1, tb), lambda h, i: (h, 0, i))),
        scratch_shapes=[pltpu.VMEM((tb, LANE), F32), pltpu.VMEM((tb, 2 * LANE), F32)],
        compiler_params=_params(2),
    )(q, k, v)


def _outproj_ln_fwd(a, w, bias, x, modl, g_row, ln_g, ln_b, tm, name):
    T, K = a.shape

    def body(a_ref, w_ref, b_ref, x_ref, modl_ref, g_ref, bb_ref, y_ref, u_ref, xn_ref):
        y = _dot(a_ref[...], w_ref[...]) + b_ref[...]
        u = ALPHA * x_ref[...] + modl_ref[g_row:g_row + 1, :] * y
        mu = jnp.mean(u, axis=1, keepdims=True)
        uc = u - mu
        var = jnp.mean(uc * uc, axis=1, keepdims=True)
        y_ref[...] = y.astype(BF)
        u_ref[...] = u
        xn_ref[...] = uc * lax.rsqrt(var + LN_EPS) * g_ref[...] + bb_ref[...]

    row = lambda i: (i, 0)
    return pl.pallas_call(
        body, name=name, grid=(T // tm,),
        out_shape=(jax.ShapeDtypeStruct((T, D), BF), jax.ShapeDtypeStruct((T, D), F32),
                   jax.ShapeDtypeStruct((T, D), F32)),
        in_specs=[pl.BlockSpec((tm, K), row), _full((K, D)), _full((1, D)), pl.BlockSpec((tm, D), row),
                  _full((8, D)), _full((1, D)), _full((1, D))],
        out_specs=(pl.BlockSpec((tm, D), row),) * 3,
        compiler_params=_params(1),
    )(a, w, bias, x, modl, ln_g, ln_b)


def _ffn_up_fwd(x, modl, wg, wu, tm, tn):
    T = x.shape[0]

    def body(x_ref, modl_ref, wg_ref, wu_ref, g_ref, u_ref, a_ref):
        h = _modulate(x_ref[...], modl_ref, 4, 3).astype(BF)
        g = _dot(h, wg_ref[...])
        u = _dot(h, wu_ref[...])
        g_ref[...] = g.astype(BF)
        u_ref[...] = u.astype(BF)
        a_ref[...] = (g * _sigmoid(g) * u).astype(BF)

    tile = pl.BlockSpec((tm, tn), lambda n, i: (i, n))
    wcol = pl.BlockSpec((D, tn), lambda n, i: (0, n))
    return pl.pallas_call(
        body, name="ffn_up_fwd", grid=(F // tn, T // tm),
        out_shape=(jax.ShapeDtypeStruct((T, F), BF),) * 3,
        in_specs=[pl.BlockSpec((tm, D), lambda n, i: (i, 0)), _full((8, D)), wcol, wcol],
        out_specs=(tile, tile, tile),
        compiler_params=_params(2),
    )(x, modl, wg, wu)


def _swa_proj_fwd(x, modl, w, b, tabs, tm):
    T = x.shape[0]
    ct_b, s1_b, s2_b = tabs
    n_rope = SWA_HQ + SWA_HKV

    def body(x_ref, modl_ref, w_ref, b_ref, ct_ref, s1_ref, s2_ref, o_ref):
        h = _modulate(x_ref[...], modl_ref, 1, 0).astype(BF)
        ct, s1, s2 = ct_ref[...], s1_ref[...], s2_ref[...]
        for grp in range(SWA_QKV // LANE):
            cols = slice(grp * LANE, (grp + 1) * LANE)
            z = _dot(h, w_ref[:, cols]) + b_ref[:, cols]
            if grp < n_rope:
                z = _rope128(z, ct, s1, s2, 8)
            o_ref[:, cols] = z.astype(BF)

    row = lambda i: (i, 0)
    return pl.pallas_call(
        body, name="swa_proj_fwd", grid=(T // tm,),
        out_shape=jax.ShapeDtypeStruct((T, SWA_QKV), BF),
        in_specs=[pl.BlockSpec((tm, D), row), _full((8, D)), _full((D, SWA_QKV)), _full((1, SWA_QKV)),
                  pl.BlockSpec((tm, LANE), row), pl.BlockSpec((tm, LANE), row), pl.BlockSpec((tm, LANE), row)],
        out_specs=pl.BlockSpec((tm, SWA_QKV), row),
        compiler_params=_params(1),
    )(x, modl, w, b, ct_b, s1_b, s2_b)


SWA_NB = 1
SWA_R = SWA_NB * 4 * SWA_W
SWA_NK = (SWA_NB + 1) * SWA_W


def _swa_bias(first):
    row = lax.broadcasted_iota(jnp.int32, (SWA_R, SWA_NK), 0)
    col = lax.broadcasted_iota(jnp.int32, (SWA_R, SWA_NK), 1)
    bl = row // (4 * SWA_W)
    r = row % SWA_W
    cp = col - bl * SWA_W
    band = (cp > r) & (cp <= r + SWA_W)
    if first:
        band = band & ((col >= SWA_W) | (bl > 0))
    return jnp.where(band, 0.0, -jnp.inf).astype(F32)


def _swa_fill_bias(bias_s):
    @pl.when(pl.program_id(0) == 0)
    def _():
        bias_s[0] = _swa_bias(False)
        bias_s[1] = _swa_bias(True)


def _swa_specs(T, tb):
    nsub = tb // SWA_W
    q_spec = pl.BlockSpec((tb, SWA_O), lambda i: (i, 0))
    kvc_spec = pl.BlockSpec((tb, 2 * SWA_HKV * LANE), lambda i: (i, 2))
    kvp_spec = pl.BlockSpec((SWA_W, 2 * SWA_HKV * LANE), lambda i: (jnp.maximum(i * nsub - 1, 0), 2))
    return q_spec, kvc_spec, kvp_spec


def _swa_rows(ref, g, c):
    return jnp.concatenate(
        [ref[(c * SWA_NB + bl) * SWA_W:(c * SWA_NB + bl + 1) * SWA_W, (4 * g + hh) * LANE:(4 * g + hh + 1) * LANE]
         for bl in range(SWA_NB) for hh in range(4)], axis=0)


def _swa_chain(q_ref, kall, vall, sink_ref, bias_s, g, c):
    i = pl.program_id(0)
    lane = lax.broadcasted_iota(jnp.int32, (1, LANE), 1)
    qc = _swa_rows(q_ref, g, c)
    keys = slice(c * SWA_NB * SWA_W, c * SWA_NB * SWA_W + SWA_NK)
    k3 = kall[keys]
    v3 = jnp.where(lane < 64, vall[keys], jnp.ones((), BF))
    bias = bias_s[jnp.where(i == 0, 1, 0)] if c == 0 else bias_s[0]
    s = _dot_nt(qc, k3)
    c_exp = SWA_SCALE * math.log2(math.e)
    sb = [s[:, ch * LANE:(ch + 1) * LANE] * c_exp + bias[:, ch * LANE:(ch + 1) * LANE] for ch in range(SWA_NK // LANE)]
    mx = sb[0]
    for x in sb[1:]:
        mx = jnp.maximum(mx, x)
    sink2 = jnp.concatenate([jnp.full((SWA_W, 1), sink_ref[4 * g + hh] * math.log2(math.e), F32)
                             for _ in range(SWA_NB) for hh in range(4)], axis=0)
    m = jnp.maximum(jnp.max(mx, axis=1, keepdims=True), sink2)
    m_rep = jnp.broadcast_to(m, (SWA_R, LANE))
    eb = jnp.concatenate([jnp.exp2(x - m_rep) for x in sb], axis=1).astype(BF)
    es = jnp.exp2(sink2 - m)
    acc = _dot(eb, v3)
    linv = 1.0 / (acc[:, 64:65] + es)
    num = jnp.where(lane < 64, acc, 0.0)
    return qc, k3, v3, eb, num, linv, es


def _swa_attn_fwd(qkv, sinks, tb):
    T = qkv.shape[0]
    kw = SWA_HKV * LANE

    def body(q_ref, kvc_ref, kvp_ref, sink_ref, o_ref, bias_s):
        _swa_fill_bias(bias_s)
        for g in range(SWA_HKV):
            gl = slice(g * LANE, (g + 1) * LANE)
            gv = slice(kw + g * LANE, kw + (g + 1) * LANE)
            kall = jnp.concatenate([kvp_ref[:, gl], kvc_ref[:, gl]], axis=0)
            vall = jnp.concatenate([kvp_ref[:, gv], kvc_ref[:, gv]], axis=0)
            for c in range(tb // (SWA_NB * SWA_W)):
                _, _, _, _, num, linv, _ = _swa_chain(q_ref, kall, vall, sink_ref, bias_s, g, c)
                o = (num * linv).astype(BF)
                for bl in range(SWA_NB):
                    for hh in range(4):
                        piece = (bl * 4 + hh) * SWA_W
                        rows = slice((c * SWA_NB + bl) * SWA_W, (c * SWA_NB + bl + 1) * SWA_W)
                        o_ref[rows, (4 * g + hh) * LANE:(4 * g + hh + 1) * LANE] = o[piece:piece + SWA_W]

    q_spec, kvc_spec, kvp_spec = _swa_specs(T, tb)
    return pl.pallas_call(
        body, name="swa_attn_fwd", grid=(T // tb,),
        out_shape=jax.ShapeDtypeStruct((T, SWA_O), BF),
        in_specs=[q_spec, kvc_spec, kvp_spec, pl.BlockSpec(memory_space=pltpu.SMEM)],
        out_specs=pl.BlockSpec((tb, SWA_O), lambda i: (i, 0)),
        scratch_shapes=[pltpu.VMEM((2, SWA_R, SWA_NK), F32)],
        compiler_params=_params(1),
    )(qkv, qkv, qkv, sinks)


def _loss_grad(x, tgt, tm):
    T = x.shape[0]

    def body(x_ref, t_ref, dx_ref, l_ref):
        @pl.when(pl.program_id(0) == 0)
        def _():
            l_ref[...] = jnp.zeros(l_ref.shape, F32)
        diff = x_ref[...] - t_ref[...]
        dx_ref[...] = diff * (1.0 / D)
        l_ref[0:1, :] += jnp.sum(diff * diff, axis=0, keepdims=True) * (0.5 / D)

    row = lambda i: (i, 0)
    return pl.pallas_call(
        body, name="loss_grad", grid=(T // tm,),
        out_shape=(jax.ShapeDtypeStruct((T, D), F32), jax.ShapeDtypeStruct((8, D), F32)),
        in_specs=[pl.BlockSpec((tm, D), row), pl.BlockSpec((tm, D), row)],
        out_specs=(pl.BlockSpec((tm, D), row), _full((8, D))),
        compiler_params=_params(1),
    )(x, tgt)


def _outproj_ln_bwd(dxn, u, y, w, modl, g_row, ln_g, tm, name):
    T = dxn.shape[0]
    K = w.shape[0]

    def body(dxn_ref, u_ref, y_ref, w_ref, modl_ref, g_ref, dres_ref, dy_ref, da_ref, sm_ref):
        @pl.when(pl.program_id(0) == 0)
        def _():
            sm_ref[...] = jnp.zeros(sm_ref.shape, F32)
        uu = u_ref[...]
        mu = jnp.mean(uu, axis=1, keepdims=True)
        uc = uu - mu
        rstd = lax.rsqrt(jnp.mean(uc * uc, axis=1, keepdims=True) + LN_EPS)
        xhat = uc * rstd
        dxo = dxn_ref[...]
        dyh = dxo * g_ref[...]
        du = rstd * (dyh - jnp.mean(dyh, axis=1, keepdims=True)
                     - xhat * jnp.mean(dyh * xhat, axis=1, keepdims=True))
        dy = modl_ref[g_row:g_row + 1, :] * du
        dyb = dy.astype(BF)
        dres_ref[...] = ALPHA * du
        dy_ref[...] = dyb
        da_ref[...] = _dot_nt(dyb, w_ref[...]).astype(BF)
        sm_ref[0:1, :] += jnp.sum(dxo * xhat, axis=0, keepdims=True)
        sm_ref[1:2, :] += jnp.sum(dxo, axis=0, keepdims=True)
        sm_ref[2:3, :] += jnp.sum(du * y_ref[...].astype(F32), axis=0, keepdims=True)
        sm_ref[3:4, :] += jnp.sum(dy, axis=0, keepdims=True)

    row = lambda i: (i, 0)
    return pl.pallas_call(
        body, name=name, grid=(T // tm,),
        out_shape=(jax.ShapeDtypeStruct((T, D), F32), jax.ShapeDtypeStruct((T, D), BF),
                   jax.ShapeDtypeStruct((T, K), BF), jax.ShapeDtypeStruct((8, D), F32)),
        in_specs=[pl.BlockSpec((tm, D), row), pl.BlockSpec((tm, D), row), pl.BlockSpec((tm, D), row),
                  _full((K, D)), _full((8, D)), _full((1, D))],
        out_specs=(pl.BlockSpec((tm, D), row), pl.BlockSpec((tm, D), row), pl.BlockSpec((tm, K), row),
                   _full((8, D))),
        compiler_params=_params(1),
    )(dxn, u, y, w, modl, ln_g)


def _ffn_mid_bwd(da, g, u, x, modl, dres, wg, wu, tm, tn):
    T = x.shape[0]
    nn = F // tn

    def body(da_ref, g_ref, u_ref, x_ref, modl_ref, dres_ref, wg_ref, wu_ref, dg_ref, du_ref, dx_ref, sm_ref):
        i, n = pl.program_id(0), pl.program_id(1)

        @pl.when((i == 0) & (n == 0))
        def _():
            sm_ref[...] = jnp.zeros(sm_ref.shape, F32)

        gg = g_ref[...].astype(F32)
        sg = _sigmoid(gg)
        dav = da_ref[...].astype(F32)
        dgp = (dav * u_ref[...].astype(F32) * sg * (1.0 + gg * (1.0 - sg))).astype(BF)
        dup = (dav * gg * sg).astype(BF)
        dg_ref[...] = dgp
        du_ref[...] = dup
        dh = _dot_nt(dgp, wg_ref[...]) + _dot_nt(dup, wu_ref[...])

        @pl.when(n == 0)
        def _():
            dx_ref[...] = dh

        @pl.when(n > 0)
        def _():
            dx_ref[...] += dh

        @pl.when(n == nn - 1)
        def _():
            dht = dx_ref[...]
            sm_ref[0:1, :] += jnp.sum(dht * x_ref[...], axis=0, keepdims=True)
            sm_ref[1:2, :] += jnp.sum(dht, axis=0, keepdims=True)
            dx_ref[...] = dres_ref[...] + dht * (1.0 + modl_ref[4:5, :])

    tile = pl.BlockSpec((tm, tn), lambda i, n: (i, n))
    rowd = pl.BlockSpec((tm, D), lambda i, n: (i, 0))
    wcol = pl.BlockSpec((D, tn), lambda i, n: (0, n))
    return pl.pallas_call(
        body, name="ffn_mid_bwd", grid=(T // tm, nn),
        out_shape=(jax.ShapeDtypeStruct((T, F), BF), jax.ShapeDtypeStruct((T, F), BF),
                   jax.ShapeDtypeStruct((T, D), F32), jax.ShapeDtypeStruct((8, D), F32)),
        in_specs=[tile, tile, tile, rowd, _full((8, D)), rowd, wcol, wcol],
        out_specs=(tile, tile, rowd, _full((8, D))),
        compiler_params=_params(2),
    )(da, g, u, x, modl, dres, wg, wu)


def _wgrad(a, b, tm, tk, tn, name, modl=None, rows=None):
    T, K = a.shape
    N = b.shape[1]

    def body(*refs):
        if modl is None:
            a_ref, b_ref, o_ref = refs
            av = a_ref[...]
        else:
            a_ref, modl_ref, b_ref, o_ref = refs
            av = _modulate(a_ref[...], modl_ref, rows[0], rows[1]).astype(BF)

        @pl.when(pl.program_id(2) == 0)
        def _():
            o_ref[...] = jnp.zeros(o_ref.shape, F32)
        o_ref[...] += _dot_tn(av, b_ref[...])

    in_specs = [pl.BlockSpec((tm, tk), lambda k, n, t: (t, k))]
    args = [a]
    if modl is not None:
        in_specs.append(_full((8, D)))
        args.append(modl)
    in_specs.append(pl.BlockSpec((tm, tn), lambda k, n, t: (t, n)))
    args.append(b)
    return pl.pallas_call(
        body, name=name, grid=(K // tk, N // tn, T // tm),
        out_shape=jax.ShapeDtypeStruct((K, N), F32),
        in_specs=in_specs,
        out_specs=pl.BlockSpec((tk, tn), lambda k, n, t: (k, n)),
        compiler_params=_params(3),
    )(*args)


def _flash_delta(o, do, tb):
    T = o.shape[0]
    H = o.shape[1] // LANE

    def body(o_ref, do_ref, delta_ref):
        delta = jnp.sum(o_ref[...].astype(F32) * do_ref[...].astype(F32), axis=1, keepdims=True)
        delta_ref[0] = _col_to_row(delta)

    blk = pl.BlockSpec((tb, LANE), lambda h, i: (i, h))
    return pl.pallas_call(
        body, name="flash_delta", grid=(H, T // tb),
        out_shape=jax.ShapeDtypeStruct((H, 1, T), F32),
        in_specs=[blk, blk],
        out_specs=pl.BlockSpec((1, 1, tb), lambda h, i: (h, 0, i)),
        compiler_params=_params(2),
    )(o, do)


def _flash_bwd(q, k, v, do, lse, delta, tb):
    H, T, _ = q.shape
    nq = T // tb
    rh = min(256, tb)
    c_exp = MLA_SCALE * math.log2(math.e)

    def body(k_ref, v_ref, q_ref, do_ref, lse_ref, delta_ref, dq_ref, dk_ref, dv_ref, dk_s, dv_s):
        j = pl.program_id(1)

        @pl.when(j == 0)
        def _():
            def zero(cix, carry):
                dq_ref[0, pl.ds(pl.multiple_of(cix * tb, tb), tb), :] = jnp.zeros((tb, MLA_HD), F32)
                return carry
            lax.fori_loop(0, nq, zero, 0)

        dk_s[...] = jnp.zeros(dk_s.shape, F32)
        dv_s[...] = jnp.zeros(dv_s.shape, F32)

        def block(i, masked):
            start = pl.multiple_of(i * tb, tb)
            qb = q_ref[0, pl.ds(start, tb), :]
            dob = do_ref[pl.ds(start, tb), :]
            l2 = lse_ref[0, :, pl.ds(start, tb)] * math.log2(math.e)
            dl = delta_ref[0, :, pl.ds(start, tb)]
            dq_part = None
            for hf in range(tb // rh):
                r = slice(hf * rh, (hf + 1) * rh)
                kc = k_ref[0, r, :]
                st = _dot_nt(kc, qb)
                if masked:
                    st = jnp.where(_causal_mask(hf * rh, rh, tb, True), st, -jnp.inf)
                pt = jnp.exp2(st * c_exp - l2)
                dpt = _dot_nt(v_ref[0, r, :], dob)
                dst = (pt * (dpt - dl)).astype(BF)
                dv_s[r, :] += _dot(pt.astype(BF), dob)
                dk_s[r, :] += _dot(dst, qb)
                part = _dot_tn(dst, kc)
                dq_part = part if dq_part is None else dq_part + part
            dq_ref[0, pl.ds(start, tb), :] += dq_part * MLA_SCALE

        def step(i, carry):
            block(i, False)
            return carry

        block(j, True)
        lax.fori_loop(j + 1, nq, step, 0)
        dk_ref[0] = (dk_s[...] * MLA_SCALE).astype(BF)
        dv_ref[0] = dv_s[...].astype(BF)

    once = pl.Buffered(1)
    return pl.pallas_call(
        body, name="flash_bwd", grid=(H, nq),
        out_shape=(jax.ShapeDtypeStruct((H, T, MLA_HD), F32), jax.ShapeDtypeStruct((H, T, MLA_HD), BF),
                   jax.ShapeDtypeStruct((H, T, LANE), BF)),
        in_specs=[pl.BlockSpec((1, tb, MLA_HD), lambda h, j: (h, j, 0)),
                  pl.BlockSpec((1, tb, LANE), lambda h, j: (h, j, 0)),
                  pl.BlockSpec((1, T, MLA_HD), lambda h, j: (h, 0, 0), pipeline_mode=once),
                  pl.BlockSpec((T, LANE), lambda h, j: (0, h), pipeline_mode=once),
                  pl.BlockSpec((1, 1, T), lambda h, j: (h, 0, 0)),
                  pl.BlockSpec((1, 1, T), lambda h, j: (h, 0, 0))],
        out_specs=(pl.BlockSpec((1, T, MLA_HD), lambda h, j: (h, 0, 0), pipeline_mode=once),
                   pl.BlockSpec((1, tb, MLA_HD), lambda h, j: (h, j, 0)),
                   pl.BlockSpec((1, tb, LANE), lambda h, j: (h, j, 0))),
        scratch_shapes=[pltpu.VMEM((tb, MLA_HD), F32), pltpu.VMEM((tb, LANE), F32)],
        compiler_params=_params(2),
    )(k, v, q, do, lse, delta)


def _mla_proj_bwd(dq, dk, dv, x, modl, dres, w_in, q_norm, w_q, kv_norm, w_kv, tabs_neg, tm):
    T = x.shape[0]
    ct_a, s1_n, s2_n = tabs_neg

    def body(dq_ref, dk_ref, dv_ref, x_ref, modl_ref, dres_ref, win_ref, qn_ref, wq_ref, kvn_ref, wkv_ref,
             ct_ref, s1_ref, s2_ref, dx_ref, dwin_ref, dwq_ref, dwkv_ref, sm_ref, dqn_ref, dkvn_ref):
        @pl.when(pl.program_id(0) == 0)
        def _():
            for r in (dwin_ref, dwq_ref, dwkv_ref, sm_ref, dqn_ref, dkvn_ref):
                r[...] = jnp.zeros(r.shape, F32)

        xv = x_ref[...]
        h = _modulate(xv, modl_ref, 1, 0).astype(BF)
        lat = _dot(h, win_ref[...])
        ql, kvl = lat[:, :MLA_QR], lat[:, MLA_QR:MLA_QR + MLA_KVR]
        qhat = ql * lax.rsqrt(jnp.mean(ql * ql, axis=1, keepdims=True) + RMS_EPS)
        kvhat = kvl * lax.rsqrt(jnp.mean(kvl * kvl, axis=1, keepdims=True) + RMS_EPS)
        rq = lax.rsqrt(jnp.mean(ql * ql, axis=1, keepdims=True) + RMS_EPS)
        rkv = lax.rsqrt(jnp.mean(kvl * kvl, axis=1, keepdims=True) + RMS_EPS)
        qn = (qhat * qn_ref[...]).astype(BF)
        kvn = (kvhat * kvn_ref[...]).astype(BF)
        ct, s1, s2 = ct_ref[...], s1_ref[...], s2_ref[...]

        dqn = jnp.zeros((tm, MLA_QR), F32)
        dkvn = jnp.zeros((tm, MLA_KVR), F32)
        dkr = jnp.zeros((tm, LANE), F32)
        for hd in range(MLA_H):
            cols = slice(hd * MLA_HD, (hd + 1) * MLA_HD)
            dqh = dq_ref[hd]
            dqr = _rope128(dqh[:, LANE:], ct, s1, s2, MLA_ROPE // 2).astype(BF)
            dqh = jnp.concatenate([dqh[:, :LANE].astype(BF), dqr], axis=1)
            dqn = dqn + _dot_nt(dqh, wq_ref[:, cols])
            dwq_ref[:, cols] += _dot_tn(qn, dqh)
            dkh = dk_ref[hd]
            dkr = dkr + dkh[:, LANE:].astype(F32)
            dkvh = jnp.concatenate([dkh[:, :LANE], dv_ref[hd]], axis=1)
            dkvn = dkvn + _dot_nt(dkvh, wkv_ref[:, cols])
            dwkv_ref[:, cols] += _dot_tn(kvn, dkvh)
        dkr = _rope128(dkr, ct, s1, s2, MLA_ROPE // 2)

        dqn_ref[...] += jnp.sum(dqn * qhat, axis=0, keepdims=True)
        dkvn_ref[...] += jnp.sum(dkvn * kvhat, axis=0, keepdims=True)
        dqh_ = dqn * qn_ref[...]
        dkvh_ = dkvn * kvn_ref[...]
        dql = rq * (dqh_ - qhat * jnp.mean(dqh_ * qhat, axis=1, keepdims=True))
        dkvl = rkv * (dkvh_ - kvhat * jnp.mean(dkvh_ * kvhat, axis=1, keepdims=True))
        dlat = jnp.concatenate([dql, dkvl, dkr], axis=1).astype(BF)
        dwin_ref[...] += _dot_tn(h, dlat)
        dh = _dot_nt(dlat, win_ref[...])
        sm_ref[0:1, :] += jnp.sum(dh * xv, axis=0, keepdims=True)
        sm_ref[1:2, :] += jnp.sum(dh, axis=0, keepdims=True)
        dx_ref[...] = dres_ref[...] + dh * (1.0 + modl_ref[1:2, :])

    row = lambda i: (i, 0)
    head = lambda i: (0, i, 0)
    nq = MLA_H * MLA_HD
    return pl.pallas_call(
        body, name="mla_proj_bwd", grid=(T // tm,),
        out_shape=(jax.ShapeDtypeStruct((T, D), F32), jax.ShapeDtypeStruct((D, MLA_LAT), F32),
                   jax.ShapeDtypeStruct((MLA_QR, nq), F32), jax.ShapeDtypeStruct((MLA_KVR, nq), F32),
                   jax.ShapeDtypeStruct((8, D), F32), jax.ShapeDtypeStruct((1, MLA_QR), F32),
                   jax.ShapeDtypeStruct((1, MLA_KVR), F32)),
        in_specs=[pl.BlockSpec((MLA_H, tm, MLA_HD), head), pl.BlockSpec((MLA_H, tm, MLA_HD), head),
                  pl.BlockSpec((MLA_H, tm, LANE), head), pl.BlockSpec((tm, D), row), _full((8, D)),
                  pl.BlockSpec((tm, D), row), _full((D, MLA_LAT)), _full((1, MLA_QR)), _full((MLA_QR, nq)),
                  _full((1, MLA_KVR)), _full((MLA_KVR, nq)),
                  pl.BlockSpec((tm, LANE), row), pl.BlockSpec((tm, LANE), row), pl.BlockSpec((tm, LANE), row)],
        out_specs=(pl.BlockSpec((tm, D), row), _full((D, MLA_LAT)), _full((MLA_QR, nq)), _full((MLA_KVR, nq)),
                   _full((8, D)), _full((1, MLA_QR)), _full((1, MLA_KVR))),
        compiler_params=_params(1),
    )(dq, dk, dv, x, modl, dres, w_in, q_norm, w_q, kv_norm, w_kv, ct_a, s1_n, s2_n)


def _swa_attn_bwd(qkv, sinks, do, tb):
    T = qkv.shape[0]
    kw = SWA_HKV * LANE
    nstep = T // tb

    def body(q_ref, kvc_ref, kvp_ref, sink_ref, do_ref, dq_ref, dkvc_ref, dkvp_ref, dsink_ref, dk_s, dv_s, bias_s):
        i = pl.program_id(0)
        _swa_fill_bias(bias_s)

        @pl.when(i == 0)
        def _():
            dsink_ref[...] = jnp.zeros(dsink_ref.shape, F32)

        dk_s[...] = jnp.zeros(dk_s.shape, F32)
        dv_s[...] = jnp.zeros(dv_s.shape, F32)
        lane = lax.broadcasted_iota(jnp.int32, (1, LANE), 1)
        for g in range(SWA_HKV):
            gl = slice(g * LANE, (g + 1) * LANE)
            gv = slice(kw + g * LANE, kw + (g + 1) * LANE)
            kall = jnp.concatenate([kvp_ref[:, gl], kvc_ref[:, gl]], axis=0)
            vall = jnp.concatenate([kvp_ref[:, gv], kvc_ref[:, gv]], axis=0)
            for c in range(tb // (SWA_NB * SWA_W)):
                qc, k3, v3, eb, num, linv, es = _swa_chain(q_ref, kall, vall, sink_ref, bias_s, g, c)
                doc = _swa_rows(do_ref, g, c)
                delta = jnp.sum(doc.astype(F32) * (num * linv), axis=1, keepdims=True)
                delta_rep = jnp.broadcast_to(delta, (SWA_R, LANE))
                dp = _dot_nt(doc, v3)
                linv_rep = jnp.broadcast_to(linv, (SWA_R, LANE))
                pch = [eb[:, ch * LANE:(ch + 1) * LANE].astype(F32) * linv_rep for ch in range(SWA_NK // LANE)]
                ds = jnp.concatenate([pch[ch] * (dp[:, ch * LANE:(ch + 1) * LANE] - delta_rep)
                                      for ch in range(SWA_NK // LANE)], axis=1).astype(BF)
                pb = jnp.concatenate(pch, axis=1).astype(BF)
                dqc = (_dot(ds, k3) * SWA_SCALE).astype(BF)
                keys = slice(c * SWA_NB * SWA_W, c * SWA_NB * SWA_W + SWA_NK)
                dk_s[keys, gl] += _dot_tn(ds, qc) * SWA_SCALE
                dv_s[keys, gl] += _dot_tn(pb, doc)
                dsk = es * linv * delta
                for hh in range(4):
                    hq = 4 * g + hh
                    tot = jnp.zeros((1, 1), F32)
                    for bl in range(SWA_NB):
                        piece = (bl * 4 + hh) * SWA_W
                        rows = slice((c * SWA_NB + bl) * SWA_W, (c * SWA_NB + bl + 1) * SWA_W)
                        dq_ref[rows, hq * LANE:(hq + 1) * LANE] = dqc[piece:piece + SWA_W]
                        tot = tot + jnp.sum(dsk[piece:piece + SWA_W], axis=0, keepdims=True)
                    dsink_ref[0:1, :] -= jnp.where(lane == hq, tot, 0.0)
        dkvp_ref[0, :, 0:kw] = dk_s[0:SWA_W, :]
        dkvp_ref[0, :, kw:2 * kw] = dv_s[0:SWA_W, :]
        dkvc_ref[:, 0:kw] = dk_s[SWA_W:, :]
        dkvc_ref[:, kw:2 * kw] = dv_s[SWA_W:, :]

    q_spec, kvc_spec, kvp_spec = _swa_specs(T, tb)
    return pl.pallas_call(
        body, name="swa_attn_bwd", grid=(nstep,),
        out_shape=(jax.ShapeDtypeStruct((T, SWA_O), BF), jax.ShapeDtypeStruct((T, 2 * kw), F32),
                   jax.ShapeDtypeStruct((nstep, SWA_W, 2 * kw), F32), jax.ShapeDtypeStruct((8, LANE), F32)),
        in_specs=[q_spec, kvc_spec, kvp_spec, pl.BlockSpec(memory_space=pltpu.SMEM),
                  pl.BlockSpec((tb, SWA_O), lambda i: (i, 0))],
        out_specs=(pl.BlockSpec((tb, SWA_O), lambda i: (i, 0)), pl.BlockSpec((tb, 2 * kw), lambda i: (i, 0)),
                   pl.BlockSpec((1, SWA_W, 2 * kw), lambda i: (i, 0, 0)), _full((8, LANE))),
        scratch_shapes=[pltpu.VMEM((tb + SWA_W, kw), F32), pltpu.VMEM((tb + SWA_W, kw), F32),
                        pltpu.VMEM((2, SWA_R, SWA_NK), F32)],
        compiler_params=_params(1),
    )(qkv, qkv, qkv, sinks, do)


def _swa_proj_bwd(dq, dkvc, dkvp, x, modl, dres, w, tabs_neg, tm):
    T = x.shape[0]
    nstep = T // tm
    kw = SWA_HKV * LANE
    ct_b, s1_n, s2_n = tabs_neg

    def body(dq_ref, dkvc_ref, dkvp_ref, x_ref, modl_ref, dres_ref, w_ref, ct_ref, s1_ref, s2_ref,
             dx_ref, dz_ref, sm_ref, db_ref):
        i = pl.program_id(0)

        @pl.when(i == 0)
        def _():
            sm_ref[...] = jnp.zeros(sm_ref.shape, F32)
            db_ref[...] = jnp.zeros(db_ref.shape, F32)

        ct, s1, s2 = ct_ref[...], s1_ref[...], s2_ref[...]
        has_next = i + 1 < nstep
        for grp in range(SWA_QKV // LANE):
            cols = slice(grp * LANE, (grp + 1) * LANE)
            if grp < SWA_HQ:
                z = dq_ref[:, cols].astype(F32)
            else:
                kc = slice((grp - SWA_HQ) * LANE, (grp - SWA_HQ + 1) * LANE)
                cur = dkvc_ref[:, kc]
                tail = cur[tm - SWA_W:] + jnp.where(has_next, dkvp_ref[0, :, kc], 0.0)
                z = jnp.concatenate([cur[:tm - SWA_W], tail], axis=0)
            if grp < SWA_HQ + SWA_HKV:
                z = _rope128(z, ct, s1, s2, 8)
            db_ref[0:1, cols] += jnp.sum(z, axis=0, keepdims=True)
            dz_ref[:, cols] = z.astype(BF)
        dh = _dot_nt(dz_ref[...], w_ref[...])
        sm_ref[0:1, :] += jnp.sum(dh * x_ref[...], axis=0, keepdims=True)
        sm_ref[1:2, :] += jnp.sum(dh, axis=0, keepdims=True)
        dx_ref[...] = dres_ref[...] + dh * (1.0 + modl_ref[1:2, :])

    row = lambda i: (i, 0)
    return pl.pallas_call(
        body, name="swa_proj_bwd", grid=(nstep,),
        out_shape=(jax.ShapeDtypeStruct((T, D), F32), jax.ShapeDtypeStruct((T, SWA_QKV), BF),
                   jax.ShapeDtypeStruct((8, D), F32), jax.ShapeDtypeStruct((8, SWA_QKV), F32)),
        in_specs=[pl.BlockSpec((tm, SWA_O), row), pl.BlockSpec((tm, 2 * kw), row),
                  pl.BlockSpec((1, SWA_W, 2 * kw), lambda i: (jnp.minimum(i + 1, nstep - 1), 0, 0)),
                  pl.BlockSpec((tm, D), row), _full((8, D)), pl.BlockSpec((tm, D), row), _full((D, SWA_QKV)),
                  pl.BlockSpec((tm, LANE), row), pl.BlockSpec((tm, LANE), row), pl.BlockSpec((tm, LANE), row)],
        out_specs=(pl.BlockSpec((tm, D), row), pl.BlockSpec((tm, SWA_QKV), row), _full((8, D)),
                   _full((8, SWA_QKV))),
        compiler_params=_params(1),
    )(dq, dkvc, dkvp, x, modl, dres, w, ct_b, s1_n, s2_n)


def _adamw(gparts, w, m, v, name):
    P, R, C = gparts.shape
    tr = R
    for cand in (512, 256, 128):
        if R % cand == 0 and R > cand:
            tr = cand
            break
    c1 = 1.0 / (1.0 - ADAM_B1 ** ADAM_STEP)
    c2 = 1.0 / (1.0 - ADAM_B2 ** ADAM_STEP)

    def body(gp_ref, w_ref, m_ref, v_ref, g_ref, d_ref, nm_ref, nv_ref):
        g = gp_ref[0].astype(F32)
        for p in range(1, P):
            g = g + gp_ref[p].astype(F32)
        nm = ADAM_B1 * m_ref[...] + (1.0 - ADAM_B1) * g
        nv = ADAM_B2 * v_ref[...] + (1.0 - ADAM_B2) * (g * g)
        g_ref[...] = g
        nm_ref[...] = nm
        nv_ref[...] = nv
        d_ref[...] = -ADAM_LR * ((nm * c1) / (jnp.sqrt(nv * c2) + ADAM_EPS) + ADAM_WD * w_ref[...])

    blk = pl.BlockSpec((tr, C), lambda i: (i, 0))
    return pl.pallas_call(
        body, name=name, grid=(R // tr,),
        out_shape=(jax.ShapeDtypeStruct((R, C), F32),) * 4,
        in_specs=[pl.BlockSpec((P, tr, C), lambda i: (0, i, 0)), blk, blk, blk],
        out_specs=(blk,) * 4,
        compiler_params=_params(1),
    )(gparts, w, m, v)


PACK_W = 1024

BIG = {
    "ffn_w_gate": ((DEPTH, D, F // NDEV), 2),
    "ffn_w_up": ((DEPTH, D, F // NDEV), 2),
    "ffn_w_down": ((DEPTH, F // NDEV, D), 1),
    "mla_w_in": ((2, D // NDEV, 704), 1),
    "mla_w_q_b": ((2, MLA_QR, 1536 // NDEV), 2),
    "mla_w_kv_b": ((2, MLA_KVR, 2048 // NDEV), 2),
    "mla_w_o": ((2, D // NDEV, D), 1),
    "swa_w_qkv": ((2, D, 1536 // NDEV), 2),
    "swa_w_o": ((2, D // NDEV, D), 1),
}


def _pack_rows(n):
    return -(-n // (16 * PACK_W)) * 16


def _pack_local(blocks):
    parts = []
    for name, (shape, _) in BIG.items():
        n = math.prod(shape)
        flat = blocks[name].astype(BF).reshape(-1)
        parts.append(jnp.pad(flat, (0, _pack_rows(n) * PACK_W - n)).reshape(-1, PACK_W))
    return jnp.concatenate(parts, axis=0)


def _pack_full(full):
    parts = []
    for name, (shape, axis) in BIG.items():
        a = full[name]
        split = a.shape[:axis] + (NDEV, a.shape[axis] // NDEV) + a.shape[axis + 1:]
        a = jnp.moveaxis(a.reshape(split), axis, 0).astype(BF).reshape(NDEV, -1)
        n = math.prod(shape)
        parts.append(jnp.pad(a, ((0, 0), (0, _pack_rows(n) * PACK_W - n))).reshape(NDEV, -1, PACK_W))
    return jnp.concatenate(parts, axis=1)


def _unpack_blocks(packed):
    out, r0 = {}, 0
    for name, (shape, _) in BIG.items():
        n = math.prod(shape)
        rows = _pack_rows(n)
        out[name] = packed[:, r0:r0 + rows].reshape(NDEV, -1)[:, :n].reshape((NDEV,) + shape)
        r0 += rows
    return out


def _unpack_full(packed):
    out = {}
    for name, blk in _unpack_blocks(packed).items():
        axis = BIG[name][1]
        a = jnp.moveaxis(blk, 0, axis)
        out[name] = a.reshape(a.shape[:axis] + (a.shape[axis] * a.shape[axis + 1],) + a.shape[axis + 2:])
    return out


def _pad_heads(a, axis, nheads, width, to):
    shp = a.shape[:axis] + (nheads, width) + a.shape[axis + 1:]
    a = a.reshape(shp)
    pad = [(0, 0)] * a.ndim
    pad[axis + 1] = (0, to - width)
    a = jnp.pad(a, pad)
    return a.reshape(a.shape[:axis] + (nheads * to,) + a.shape[axis + 2:])


def _unpad_heads(a, axis, nheads, width, to):
    shp = a.shape[:axis] + (nheads, to) + a.shape[axis + 1:]
    a = lax.slice_in_dim(a.reshape(shp), 0, width, axis=axis + 1)
    return a.reshape(a.shape[:axis] + (nheads * width,) + a.shape[axis + 2:])


def _swa_pad_cols(a):
    return _pad_heads(a, a.ndim - 1, SWA_HQ + 2 * SWA_HKV, 64, LANE)


def _rope_tables(positions, half):
    rot = 2 * half
    inv = ROPE_THETA ** (-jnp.arange(0, rot, 2, dtype=F32) / rot)
    ang = positions.astype(F32)[:, None] * inv
    cos, sin = jnp.cos(ang), jnp.sin(ang)
    T = positions.shape[0]
    ones = jnp.ones((T, LANE - rot), F32)
    zeros = jnp.zeros((T, LANE - rot), F32)
    zh = jnp.zeros((T, half), F32)
    ct = jnp.concatenate([cos, cos, ones], axis=1)
    s1 = jnp.concatenate([-sin, zh, zeros], axis=1)
    s2 = jnp.concatenate([zh, sin, zeros], axis=1)
    return (ct, s1, s2), (ct, -s1, -s2)


def _small_pack(vecs, rows):
    flat = jnp.concatenate([v.astype(F32).reshape(-1) for v in vecs])
    return jnp.pad(flat, (0, rows * LANE - flat.shape[0])).reshape(rows, LANE)


def _small_unpack(buf, shapes):
    flat = buf.reshape(NDEV, -1)
    out, o = [], 0
    for shp in shapes:
        n = math.prod(shp)
        out.append(flat[:, o:o + n].reshape((NDEV,) + shp))
        o += n
    return out


def _step(inp):
    x = inp["x"][0]
    tgt = inp["loss_target"][0]
    T = x.shape[0]
    tm = min(512, T)
    tmf = min(256, T)
    tb = min(512, T)
    tbf = min(1024, T)
    tbq = min(2048, T)
    tnf = F // 2
    me = 4 * lax.axis_index("x") + 2 * lax.axis_index("y") + lax.axis_index("c")

    small_in = _small_pack([inp["c"], inp["swa_b_qkv"], inp["swa_b_o"]], 16)
    c_all, bqkv_blk, bo_blk = _small_unpack(_exchange(small_in, True, "gather_small"),
                                            [(D,), (2, 1536 // NDEV), (2, D // NDEV)])
    swa_b_qkv = jnp.moveaxis(bqkv_blk, 0, 1).reshape(2, 1536)
    swa_b_o = jnp.moveaxis(bo_blk, 0, 1).reshape(2, D)

    wfull = _unpack_full(_exchange(_pack_local(inp), True, "gather_weights"))

    ncol = 6 * D // NDEV
    ada_b_loc = lax.dynamic_slice_in_dim(inp["ada_b"], me * ncol, ncol, axis=1)[:, None, :]
    mod_all = _mod_all(c_all, inp["ada_w"], ada_b_loc)
    mod_src = jnp.moveaxis(mod_all, 1, 0).reshape(NDEV, DEPTH * ncol // LANE, LANE)
    mod_got = _exchange(mod_src, False, "scatter_mod").reshape(NDEV, DEPTH, ncol)
    mod = jnp.moveaxis(mod_got, 0, 1).reshape(DEPTH, 6, D)
    modl = jnp.pad(mod, ((0, 0), (0, 2), (0, 0)))

    w_in = jnp.pad(wfull["mla_w_in"], ((0, 0), (0, 0), (0, MLA_LAT - 704)))
    w_q = _pad_heads(wfull["mla_w_q_b"], 2, MLA_H, 192, MLA_HD)
    w_kv = wfull["mla_w_kv_b"]
    w_o_mla = wfull["mla_w_o"]
    w_qkv = _swa_pad_cols(wfull["swa_w_qkv"])
    b_qkv = _swa_pad_cols(swa_b_qkv)
    w_o_swa = _pad_heads(wfull["swa_w_o"], 1, SWA_HQ, 64, LANE)
    zero_bias = jnp.zeros((1, D), F32)

    pos = inp["positions"][0]
    tabs_a, tabs_a_neg = _rope_tables(pos, MLA_ROPE // 2)
    tabs_b, tabs_b_neg = _rope_tables(pos, 8)

    saved = []
    xs = x
    for i in range(DEPTH):
        j = i // 2
        st = {"x0": xs}
        if i % 2 == 0:
            q, k, v = _mla_proj_fwd(xs, modl[i], w_in[j], inp["mla_q_norm"][j][None], w_q[j],
                                    inp["mla_kv_norm"][j][None], w_kv[j], tabs_a, tm)
            o, lse = _flash_fwd(q, k, v, tbq)
            st.update(q=q, k=k, v=v, o=o, lse=lse)
            w_o, b_o = w_o_mla[j], zero_bias
        else:
            qkv = _swa_proj_fwd(xs, modl[i], w_qkv[j], b_qkv[j][None], tabs_b, tm)
            o = _swa_attn_fwd(qkv, inp["swa_sinks"][j], tb)
            st.update(qkv=qkv, o=o)
            w_o, b_o = w_o_swa[j], swa_b_o[j][None]
        y, u, xs = _outproj_ln_fwd(o, w_o, b_o, xs, modl[i], 2, inp["ln_mix_g"][i][None],
                                   inp["ln_mix_b"][i][None], tm, f"mix_out_fwd_{i % 2}")
        st.update(y_m=y, u_m=u, x1=xs, w_o=w_o)
        g, up, a = _ffn_up_fwd(xs, modl[i], wfull["ffn_w_gate"][i], wfull["ffn_w_up"][i], tmf, tnf)
        y, u, xs = _outproj_ln_fwd(a, wfull["ffn_w_down"][i], zero_bias, xs, modl[i], 5,
                                   inp["ln_ffn_g"][i][None], inp["ln_ffn_b"][i][None], tmf, "ffn_out_fwd")
        st.update(g=g, up=up, a=a, y_f=y, u_f=u)
        saved.append(st)

    dx, loss_rows = _loss_grad(xs, tgt, tm)
    loss = lax.psum(jnp.sum(loss_rows[0]), ("x", "y", "c"))

    gfull = {n: [None] * (DEPTH if n.startswith("ffn") else 2) for n in BIG}
    dmod = [None] * DEPTH
    g_ln = {n: [None] * DEPTH for n in ("ln_mix_g", "ln_mix_b", "ln_ffn_g", "ln_ffn_b")}
    g_qn, g_kvn, g_sink, g_bqkv, g_bo = [None] * 2, [None] * 2, [None] * 2, [None] * 2, [None] * 2
    for i in reversed(range(DEPTH)):
        j = i // 2
        st = saved[i]
        dres, dy, da, sm = _outproj_ln_bwd(dx, st["u_f"], st["y_f"], wfull["ffn_w_down"][i], modl[i], 5,
                                           inp["ln_ffn_g"][i][None], tmf, "ffn_out_bwd")
        g_ln["ln_ffn_g"][i], g_ln["ln_ffn_b"][i], dg_f = sm[0], sm[1], sm[2]
        gfull["ffn_w_down"][i] = _wgrad(st["a"], dy, tm, F // 2, D, "wgrad_down")
        dgp, dup, dx, sm = _ffn_mid_bwd(da, st["g"], st["up"], st["x1"], modl[i], dres,
                                        wfull["ffn_w_gate"][i], wfull["ffn_w_up"][i], tmf, tnf)
        dsc_f, dsh_f = sm[0], sm[1]
        gfull["ffn_w_gate"][i] = _wgrad(st["x1"], dgp, tm, D, tnf, "wgrad_gate", modl[i], (4, 3))
        gfull["ffn_w_up"][i] = _wgrad(st["x1"], dup, tm, D, tnf, "wgrad_up", modl[i], (4, 3))

        dres, dy, do, sm = _outproj_ln_bwd(dx, st["u_m"], st["y_m"], st["w_o"], modl[i], 2,
                                           inp["ln_mix_g"][i][None], tm, f"mix_out_bwd_{i % 2}")
        g_ln["ln_mix_g"][i], g_ln["ln_mix_b"][i], dg_m = sm[0], sm[1], sm[2]
        if i % 2 == 0:
            gfull["mla_w_o"][j] = _wgrad(st["o"], dy, tm, D, D, "wgrad_mla_o")
            delta = _flash_delta(st["o"], do, tb)
            dq, dk, dv = _flash_bwd(st["q"], st["k"], st["v"], do, st["lse"], delta, tbf)
            dx, dwin, dwq, dwkv, sm, dqn, dkvn = _mla_proj_bwd(
                dq, dk, dv, st["x0"], modl[i], dres, w_in[j], inp["mla_q_norm"][j][None], w_q[j],
                inp["mla_kv_norm"][j][None], w_kv[j], tabs_a_neg, tm)
            gfull["mla_w_in"][j] = dwin[:, :704]
            gfull["mla_w_q_b"][j] = _unpad_heads(dwq, 1, MLA_H, 192, MLA_HD)
            gfull["mla_w_kv_b"][j] = dwkv
            g_qn[j], g_kvn[j] = dqn[0], dkvn[0]
        else:
            g_bo[j] = sm[3]
            dwo = _wgrad(st["o"], dy, tm, SWA_O // 2, D, "wgrad_swa_o")
            gfull["swa_w_o"][j] = _unpad_heads(dwo, 0, SWA_HQ, 64, LANE)
            dq, dkvc, dkvp, dsink = _swa_attn_bwd(st["qkv"], inp["swa_sinks"][j], do, tb)
            g_sink[j] = dsink[0, :SWA_HQ]
            dx, dz, sm, db = _swa_proj_bwd(dq, dkvc, dkvp, st["x0"], modl[i], dres, w_qkv[j], tabs_b_neg, tm)
            dwqkv = _wgrad(st["x0"], dz, tm, D, SWA_QKV // 2, "wgrad_swa_qkv", modl[i], (1, 0))
            gfull["swa_w_qkv"][j] = _unpad_heads(dwqkv, 1, SWA_HQ + 2 * SWA_HKV, 64, LANE)
            g_bqkv[j] = _unpad_heads(db[0], 0, SWA_HQ + 2 * SWA_HKV, 64, LANE)
        dmod[i] = jnp.stack([sm[1], sm[0], dg_m, dsh_f, dsc_f, dg_f])
    grad_x = dx[None]

    small_shapes = [(DEPTH, 6 * D), (DEPTH, D), (DEPTH, D), (DEPTH, D), (DEPTH, D), (2, MLA_QR), (2, MLA_KVR),
                    (2, SWA_HQ), (2, 1536), (2, D)]
    small_vals = [jnp.stack(dmod).reshape(DEPTH, 6 * D), jnp.stack(g_ln["ln_mix_g"]), jnp.stack(g_ln["ln_mix_b"]),
                  jnp.stack(g_ln["ln_ffn_g"]), jnp.stack(g_ln["ln_ffn_b"]), jnp.stack(g_qn), jnp.stack(g_kvn),
                  jnp.stack(g_sink), jnp.stack(g_bqkv), jnp.stack(g_bo)]
    nsmall = sum(math.prod(s) for s in small_shapes)
    small_rows = -(-nsmall // (8 * LANE)) * 8
    (dmod_all, p_lmg, p_lmb, p_lfg, p_lfb, p_qn, p_kvn, p_sink, p_bqkv, p_bo) = _small_unpack(
        _exchange(_small_pack(small_vals, small_rows), True, "gather_small_grads"), small_shapes)

    gpacked = _exchange(_pack_full({n: jnp.stack(v) for n, v in gfull.items()}), False, "scatter_grads")
    gparts = _unpack_blocks(gpacked)

    res = {}

    def update(name, parts):
        w = inp[name]
        shp = w.shape
        r2 = (math.prod(shp[:-1]), shp[-1])
        outs = _adamw(parts.reshape((parts.shape[0],) + r2), w.reshape(r2), inp["m_" + name].reshape(r2),
                      inp["v_" + name].reshape(r2), "adamw_" + name)
        res[name] = tuple(o.reshape(shp) for o in outs)

    dmod_loc = lax.dynamic_slice_in_dim(dmod_all, me * ncol, ncol, axis=2)
    g_ada_w = _ada_w_grad(c_all.T, jnp.moveaxis(dmod_loc, 0, 1))
    update("ada_w", g_ada_w[None])
    update("ada_b", dmod_all)
    update("ln_mix_g", p_lmg)
    update("ln_mix_b", p_lmb)
    update("ln_ffn_g", p_lfg)
    update("ln_ffn_b", p_lfb)
    for name in BIG:
        update(name, gparts[name])
    update("mla_q_norm", p_qn)
    update("mla_kv_norm", p_kvn)
    update("swa_sinks", p_sink)
    nb = 1536 // NDEV
    update("swa_b_qkv", lax.dynamic_slice_in_dim(p_bqkv, me * nb, nb, axis=2))
    update("swa_b_o", lax.dynamic_slice_in_dim(p_bo, me * (D // NDEV), D // NDEV, axis=2))
    return loss, grad_x, res


WEIGHTS = ["ada_w", "ada_b", "ln_mix_g", "ln_mix_b", "ln_ffn_g", "ln_ffn_b", "ffn_w_gate", "ffn_w_up",
           "ffn_w_down", "mla_w_in", "mla_q_norm", "mla_w_q_b", "mla_kv_norm", "mla_w_kv_b", "mla_w_o",
           "swa_w_qkv", "swa_b_qkv", "swa_sinks", "swa_w_o", "swa_b_o"]
INPUTS = (["x", "c", "positions"] + WEIGHTS + ["loss_target"] + ["m_" + n for n in WEIGHTS]
          + ["v_" + n for n in WEIGHTS])


def kernel(x, c, positions, ada_w, ada_b, ln_mix_g, ln_mix_b, ln_ffn_g, ln_ffn_b, ffn_w_gate, ffn_w_up, ffn_w_down, mla_w_in, mla_q_norm, mla_w_q_b, mla_kv_norm, mla_w_kv_b, mla_w_o, swa_w_qkv, swa_b_qkv, swa_sinks, swa_w_o, swa_b_o, loss_target, m_ada_w, m_ada_b, m_ln_mix_g, m_ln_mix_b, m_ln_ffn_g, m_ln_ffn_b, m_ffn_w_gate, m_ffn_w_up, m_ffn_w_down, m_mla_w_in, m_mla_q_norm, m_mla_w_q_b, m_mla_kv_norm, m_mla_w_kv_b, m_mla_w_o, m_swa_w_qkv, m_swa_b_qkv, m_swa_sinks, m_swa_w_o, m_swa_b_o, v_ada_w, v_ada_b, v_ln_mix_g, v_ln_mix_b, v_ln_ffn_g, v_ln_ffn_b, v_ffn_w_gate, v_ffn_w_up, v_ffn_w_down, v_mla_w_in, v_mla_q_norm, v_mla_w_q_b, v_mla_kv_norm, v_mla_w_kv_b, v_mla_w_o, v_swa_w_qkv, v_swa_b_qkv, v_swa_sinks, v_swa_w_o, v_swa_b_o):
    args = (x, c, positions, ada_w, ada_b, ln_mix_g, ln_mix_b, ln_ffn_g, ln_ffn_b, ffn_w_gate, ffn_w_up, ffn_w_down, mla_w_in, mla_q_norm, mla_w_q_b, mla_kv_norm, mla_w_kv_b, mla_w_o, swa_w_qkv, swa_b_qkv, swa_sinks, swa_w_o, swa_b_o, loss_target, m_ada_w, m_ada_b, m_ln_mix_g, m_ln_mix_b, m_ln_ffn_g, m_ln_ffn_b, m_ffn_w_gate, m_ffn_w_up, m_ffn_w_down, m_mla_w_in, m_mla_q_norm, m_mla_w_q_b, m_mla_kv_norm, m_mla_w_kv_b, m_mla_w_o, m_swa_w_qkv, m_swa_b_qkv, m_swa_sinks, m_swa_w_o, m_swa_b_o, v_ada_w, v_ada_b, v_ln_mix_g, v_ln_mix_b, v_ln_ffn_g, v_ln_ffn_b, v_ffn_w_gate, v_ffn_w_up, v_ffn_w_down, v_mla_w_in, v_mla_q_norm, v_mla_w_q_b, v_mla_kv_norm, v_mla_w_kv_b, v_mla_w_o, v_swa_w_qkv, v_swa_b_qkv, v_swa_sinks, v_swa_w_o, v_swa_b_o)
    assert len(args) == len(INPUTS)
    loss, grad_x, res = _step(dict(zip(INPUTS, args)))
    return (loss, grad_x, *[res[n][0] for n in WEIGHTS], *[res[n][1] for n in WEIGHTS],
            *[res[n][2] for n in WEIGHTS], *[res[n][3] for n in WEIGHTS])
```

```python
import functools
import math

import jax
import jax.numpy as jnp
from jax import lax
from jax.experimental import pallas as pl
from jax.experimental.pallas import tpu as pltpu

F32 = jnp.float32
BF = jnp.bfloat16

NDEV = 8
D = 1024
DEPTH = 4
F = 2816
ALPHA = (2 * DEPTH) ** 0.25
LN_EPS = 1e-5
RMS_EPS = 1e-6
ROPE_THETA = 500000.0

MLA_H = 8
MLA_QR = 384
MLA_KVR = 256
MLA_ROPE = 64
MLA_LAT = 768
MLA_HD = 256
MLA_SCALE = (128 + 64) ** -0.5

SWA_HQ = 16
SWA_HKV = 4
SWA_W = 128
SWA_SCALE = 64 ** -0.5
SWA_QKV = (SWA_HQ + 2 * SWA_HKV) * 128
SWA_O = SWA_HQ * 128

LANE = 128
VMEM_LIMIT = 56 * 2 ** 20

ADAM_LR, ADAM_B1, ADAM_B2, ADAM_EPS, ADAM_WD, ADAM_STEP = 0.001, 0.9, 0.999, 1e-8, 0.01, 10


def _params(n_axes):
    return pltpu.CompilerParams(dimension_semantics=("arbitrary",) * n_axes, vmem_limit_bytes=VMEM_LIMIT)


def _dot(a, b):
    return jnp.dot(a, b, preferred_element_type=F32)


def _dot_nt(a, b):
    return lax.dot_general(a, b, (((1,), (1,)), ((), ())), preferred_element_type=F32)


def _dot_tn(a, b):
    return lax.dot_general(a, b, (((0,), (0,)), ((), ())), preferred_element_type=F32)


def _full(shape):
    return pl.BlockSpec(shape, lambda *_: (0,) * len(shape))


def _sigmoid(x):
    return 1.0 / (1.0 + jnp.exp(-x))


def _rope128(x, ct, s1, s2, half):
    return x * ct + pltpu.roll(x, LANE - half, 1) * s1 + pltpu.roll(x, half, 1) * s2


def _eye(n):
    return lax.broadcasted_iota(jnp.int32, (n, n), 0) == lax.broadcasted_iota(jnp.int32, (n, n), 1)


def _col_to_row(col):
    n = col.shape[0]
    return jnp.sum(jnp.where(_eye(n), col, 0.0), axis=0, keepdims=True)


def _row_to_col(row):
    n = row.shape[1]
    return jnp.sum(jnp.where(_eye(n), row, 0.0), axis=1, keepdims=True)


def _modulate(x, modl_ref, sc_row, sh_row):
    return x * (1.0 + modl_ref[sc_row:sc_row + 1, :]) + modl_ref[sh_row:sh_row + 1, :]


EXCHANGE_SEMS = [pltpu.SemaphoreType.DMA((NDEV - 1,)), pltpu.SemaphoreType.DMA((NDEV - 1,)), pltpu.SemaphoreType.DMA]


def _exchange_copies(src_ref, out_ref, send_sems, recv_sems, local_sem, gather):
    x, y, c = lax.axis_index("x"), lax.axis_index("y"), lax.axis_index("c")
    me = 4 * x + 2 * y + c

    def piece(dev):
        return src_ref if gather else src_ref.at[dev]

    mine = pltpu.make_async_copy(piece(me), out_ref.at[me], local_sem)
    sends, recvs = [], []
    for k in range(1, NDEV):
        px = 1 - x if k & 4 else x
        py = 1 - y if k & 2 else y
        pc = 1 - c if k & 1 else c
        peer = 4 * px + 2 * py + pc
        common = dict(send_sem=send_sems.at[k - 1], recv_sem=recv_sems.at[k - 1],
                      device_id=(px, py, pc), device_id_type=pl.DeviceIdType.MESH)
        sends.append(pltpu.make_async_remote_copy(src_ref=piece(peer), dst_ref=out_ref.at[me], **common))
        recvs.append(pltpu.make_async_remote_copy(src_ref=piece(peer), dst_ref=out_ref.at[peer], **common))
    return mine, sends, recvs


def _exchange_start(*refs, gather):
    mine, sends, _ = _exchange_copies(*refs, gather)
    mine.start()
    for s in sends:
        s.start()


def _exchange_wait(*refs, gather):
    mine, sends, recvs = _exchange_copies(*refs, gather)
    for r in recvs:
        r.wait_recv()
    for s in sends:
        s.wait_send()
    mine.wait()


def _exchange_out(src, gather):
    blk = tuple(src.shape) if gather else tuple(src.shape[1:])
    return jax.ShapeDtypeStruct((NDEV,) + blk, src.dtype)


def _exchange(src, gather, name):
    def body(*refs):
        _exchange_start(*refs, gather=gather)
        _exchange_wait(*refs, gather=gather)

    return pl.pallas_call(
        body, name=name,
        out_shape=_exchange_out(src, gather),
        in_specs=[pl.BlockSpec(memory_space=pltpu.HBM)],
        out_specs=pl.BlockSpec(memory_space=pltpu.HBM),
        scratch_shapes=EXCHANGE_SEMS,
    )(src)


def _mod_all(c_all, ada_w, ada_b_loc):
    ncol = ada_w.shape[2]

    def body(c_ref, w_ref, b_ref, o_ref):
        cv = c_ref[...]
        cond = cv * _sigmoid(cv)
        o_ref[0] = _dot(cond.astype(BF), w_ref[0].astype(BF)) + b_ref[0]

    return pl.pallas_call(
        body, name="mod_all", grid=(DEPTH,),
        out_shape=jax.ShapeDtypeStruct((DEPTH, NDEV, ncol), F32),
        in_specs=[_full((NDEV, D)), pl.BlockSpec((1, D, ncol), lambda i: (i, 0, 0)),
                  pl.BlockSpec((1, 1, ncol), lambda i: (i, 0, 0))],
        out_specs=pl.BlockSpec((1, NDEV, ncol), lambda i: (i, 0, 0)),
        compiler_params=_params(1),
    )(c_all, ada_w, ada_b_loc)


def _ada_w_grad(c_all_t, dmod_loc):
    ncol = dmod_loc.shape[2]

    def body(ct_ref, dm_ref, o_ref):
        cv = ct_ref[...]
        cond = cv * _sigmoid(cv)
        acc = cond[:, 0:1] * dm_ref[0, 0:1, :]
        for b in range(1, NDEV):
            acc = acc + cond[:, b:b + 1] * dm_ref[0, b:b + 1, :]
        o_ref[0] = acc

    return pl.pallas_call(
        body, name="ada_w_grad", grid=(DEPTH,),
        out_shape=jax.ShapeDtypeStruct((DEPTH, D, ncol), F32),
        in_specs=[_full((D, NDEV)), pl.BlockSpec((1, NDEV, ncol), lambda i: (i, 0, 0))],
        out_specs=pl.BlockSpec((1, D, ncol), lambda i: (i, 0, 0)),
        compiler_params=_params(1),
    )(c_all_t, dmod_loc)


def _mla_proj_fwd(x, modl, w_in, q_norm, w_q, kv_norm, w_kv, tabs, tm):
    T = x.shape[0]
    ct_a, s1_a, s2_a = tabs

    def body(x_ref, modl_ref, win_ref, qn_ref, wq_ref, kvn_ref, wkv_ref, ct_ref, s1_ref, s2_ref,
             q_ref, k_ref, v_ref):
        h = _modulate(x_ref[...], modl_ref, 1, 0).astype(BF)
        lat = _dot(h, win_ref[...])
        ql, kvl, kr = lat[:, :MLA_QR], lat[:, MLA_QR:MLA_QR + MLA_KVR], lat[:, MLA_QR + MLA_KVR:]
        qn = (ql * lax.rsqrt(jnp.mean(ql * ql, axis=1, keepdims=True) + RMS_EPS) * qn_ref[...]).astype(BF)
        kvn = (kvl * lax.rsqrt(jnp.mean(kvl * kvl, axis=1, keepdims=True) + RMS_EPS) * kvn_ref[...]).astype(BF)
        ct, s1, s2 = ct_ref[...], s1_ref[...], s2_ref[...]
        kr = _rope128(kr, ct, s1, s2, MLA_ROPE // 2).astype(BF)
        for hd in range(MLA_H):
            cols = slice(hd * MLA_HD, (hd + 1) * MLA_HD)
            qh = _dot(qn, wq_ref[:, cols])
            q_ref[hd, :, 0:LANE] = qh[:, :LANE].astype(BF)
            q_ref[hd, :, LANE:MLA_HD] = _rope128(qh[:, LANE:], ct, s1, s2, MLA_ROPE // 2).astype(BF)
            kvh = _dot(kvn, wkv_ref[:, cols])
            k_ref[hd, :, 0:LANE] = kvh[:, :LANE].astype(BF)
            k_ref[hd, :, LANE:MLA_HD] = kr
            v_ref[hd, :, 0:LANE] = kvh[:, LANE:].astype(BF)
            v_ref[hd, :, LANE:2 * LANE] = jnp.ones((tm, LANE), BF)

    row = lambda i: (i, 0)
    head = lambda i: (0, i, 0)
    return pl.pallas_call(
        body, name="mla_proj_fwd", grid=(T // tm,),
        out_shape=(jax.ShapeDtypeStruct((MLA_H, T, MLA_HD), BF), jax.ShapeDtypeStruct((MLA_H, T, MLA_HD), BF),
                   jax.ShapeDtypeStruct((MLA_H, T, 2 * LANE), BF)),
        in_specs=[pl.BlockSpec((tm, D), row), _full((8, D)), _full((D, MLA_LAT)), _full((1, MLA_QR)),
                  _full((MLA_QR, MLA_H * MLA_HD)), _full((1, MLA_KVR)), _full((MLA_KVR, MLA_H * MLA_HD)),
                  pl.BlockSpec((tm, LANE), row), pl.BlockSpec((tm, LANE), row), pl.BlockSpec((tm, LANE), row)],
        out_specs=(pl.BlockSpec((MLA_H, tm, MLA_HD), head), pl.BlockSpec((MLA_H, tm, MLA_HD), head),
                   pl.BlockSpec((MLA_H, tm, 2 * LANE), head)),
        compiler_params=_params(1),
    )(x, modl, w_in, q_norm, w_q, kv_norm, w_kv, ct_a, s1_a, s2_a)


def _causal_mask(row0, nrows, ncols, transposed):
    row = lax.broadcasted_iota(jnp.int32, (nrows, ncols), 0) + row0
    col = lax.broadcasted_iota(jnp.int32, (nrows, ncols), 1)
    return (row <= col) if transposed else (col <= row)


def _flash_fwd(q, k, v, tb, comm_src=None):
    H, T, _ = q.shape
    rh = min(256, tb)
    nch = tb // LANE
    c_exp = MLA_SCALE * math.log2(math.e)

    def body(*refs):
        if comm_src is None:
            q_ref, k_ref, v_ref, o_ref, lse_ref, m_s, acc_s = refs
        else:
            q_ref, k_ref, v_ref, src_ref, o_ref, lse_ref, got_ref, m_s, acc_s, *sems = refs
            comm = (src_ref, got_ref, *sems)
            first = (pl.program_id(0) == 0) & (pl.program_id(1) == 0)
            last = (pl.program_id(0) == H - 1) & (pl.program_id(1) == T // tb - 1)
            pl.when(first)(functools.partial(_exchange_start, *comm, gather=True))
        i = pl.program_id(1)
        m_s[...] = jnp.full(m_s.shape, -jnp.inf, F32)
        acc_s[...] = jnp.zeros(acc_s.shape, F32)

        def block(j, masked):
            rows = pl.ds(pl.multiple_of(j * tb, tb), tb)
            kb = k_ref[0, rows, :]
            vb = v_ref[0, rows, :]
            for hf in range(tb // rh):
                r = slice(hf * rh, (hf + 1) * rh)
                s = _dot_nt(q_ref[0, r, :], kb)
                if masked:
                    s = jnp.where(_causal_mask(hf * rh, rh, tb, False), s, -jnp.inf)
                sc = [s[:, c * LANE:(c + 1) * LANE] for c in range(nch)]
                mx = sc[0]
                for c in range(1, nch):
                    mx = jnp.maximum(mx, sc[c])
                m_prev = m_s[r, :]
                m_new = jnp.maximum(m_prev, jnp.max(mx, axis=1, keepdims=True))
                p = jnp.concatenate([jnp.exp2((sc[c] - m_new) * c_exp) for c in range(nch)], axis=1)
                corr = jnp.exp2((m_prev - m_new) * c_exp)
                acc_s[r, :] = jnp.concatenate([corr, corr], axis=1) * acc_s[r, :] + _dot(p.astype(BF), vb)
                m_s[r, :] = m_new

        def step(j, carry):
            block(j, False)
            return carry

        lax.fori_loop(0, i, step, 0)
        block(i, True)
        l = acc_s[:, LANE:]
        o_ref[...] = (acc_s[:, :LANE] / l).astype(BF)
        lse = (m_s[...] * MLA_SCALE + jnp.log(l))[:, 0:1]
        for c0 in range(0, tb, rh):
            lse_ref[0, :, c0:c0 + rh] = _col_to_row(lse[c0:c0 + rh])
        if comm_src is not None:
            pl.when(last)(functools.partial(_exchange_wait, *comm, gather=True))

    hbm = pl.BlockSpec(memory_space=pltpu.HBM)
    with_comm = comm_src is not None
    return pl.pallas_call(
        body, name="flash_fwd_gather" if with_comm else "flash_fwd", grid=(H, T // tb),
        out_shape=(jax.ShapeDtypeStruct((T, H * LANE), BF), jax.ShapeDtypeStruct((H, 1, T), F32))
        + ((_exchange_out(comm_src, True),) if with_comm else ()),
        in_specs=[pl.BlockSpec((1, tb, MLA_HD), lambda h, i: (h, i, 0)),
                  pl.BlockSpec((1, T, MLA_HD), lambda h, i: (h, 0, 0)),
                  pl.BlockSpec((1, T, 2 * LANE), lambda h, i: (h, 0, 0))] + ([hbm] if with_comm else []),
        out_specs=(pl.BlockSpec((tb, LANE), lambda h, i: (i, h)),
                   pl.BlockSpec((1, 1, tb), lambda h, i: (h, 0, i))) + ((hbm,) if with_comm else ()),
        scratch_shapes=[pltpu.VMEM((tb, LANE), F32), pltpu.VMEM((tb, 2 * LANE), F32)]
        + (EXCHANGE_SEMS if with_comm else []),
        compiler_params=_params(2),
    )(*((q, k, v, comm_src) if with_comm else (q, k, v)))


def _outproj_ln_fwd(a, w, bias, x, modl, g_row, ln_g, ln_b, tm, name):
    T, K = a.shape

    def body(a_ref, w_ref, b_ref, x_ref, modl_ref, g_ref, bb_ref, y_ref, u_ref, xn_ref):
        y = _dot(a_ref[...], w_ref[...]) + b_ref[...]
        u = ALPHA * x_ref[...] + modl_ref[g_row:g_row + 1, :] * y
        mu = jnp.mean(u, axis=1, keepdims=True)
        uc = u - mu
        var = jnp.mean(uc * uc, axis=1, keepdims=True)
        y_ref[...] = y.astype(BF)
        u_ref[...] = u
        xn_ref[...] = uc * lax.rsqrt(var + LN_EPS) * g_ref[...] + bb_ref[...]

    row = lambda i: (i, 0)
    return pl.pallas_call(
        body, name=name, grid=(T // tm,),
        out_shape=(jax.ShapeDtypeStruct((T, D), BF), jax.ShapeDtypeStruct((T, D), F32),
                   jax.ShapeDtypeStruct((T, D), F32)),
        in_specs=[pl.BlockSpec((tm, K), row), _full((K, D)), _full((1, D)), pl.BlockSpec((tm, D), row),
                  _full((8, D)), _full((1, D)), _full((1, D))],
        out_specs=(pl.BlockSpec((tm, D), row),) * 3,
        compiler_params=_params(1),
    )(a, w, bias, x, modl, ln_g, ln_b)


def _ffn_up_fwd(x, modl, wg, wu, tm, tn):
    T = x.shape[0]

    def body(x_ref, modl_ref, wg_ref, wu_ref, g_ref, u_ref, a_ref):
        h = _modulate(x_ref[...], modl_ref, 4, 3).astype(BF)
        g = _dot(h, wg_ref[...])
        u = _dot(h, wu_ref[...])
        g_ref[...] = g.astype(BF)
        u_ref[...] = u.astype(BF)
        a_ref[...] = (g * _sigmoid(g) * u).astype(BF)

    tile = pl.BlockSpec((tm, tn), lambda n, i: (i, n))
    wcol = pl.BlockSpec((D, tn), lambda n, i: (0, n))
    return pl.pallas_call(
        body, name="ffn_up_fwd", grid=(F // tn, T // tm),
        out_shape=(jax.ShapeDtypeStruct((T, F), BF),) * 3,
        in_specs=[pl.BlockSpec((tm, D), lambda n, i: (i, 0)), _full((8, D)), wcol, wcol],
        out_specs=(tile, tile, tile),
        compiler_params=_params(2),
    )(x, modl, wg, wu)


def _swa_proj_fwd(x, modl, w, b, tabs, tm):
    T = x.shape[0]
    ct_b, s1_b, s2_b = tabs
    n_rope = SWA_HQ + SWA_HKV

    def body(x_ref, modl_ref, w_ref, b_ref, ct_ref, s1_ref, s2_ref, o_ref):
        h = _modulate(x_ref[...], modl_ref, 1, 0).astype(BF)
        ct, s1, s2 = ct_ref[...], s1_ref[...], s2_ref[...]
        for grp in range(SWA_QKV // LANE):
            cols = slice(grp * LANE, (grp + 1) * LANE)
            z = _dot(h, w_ref[:, cols]) + b_ref[:, cols]
            if grp < n_rope:
                z = _rope128(z, ct, s1, s2, 8)
            o_ref[:, cols] = z.astype(BF)

    row = lambda i: (i, 0)
    return pl.pallas_call(
        body, name="swa_proj_fwd", grid=(T // tm,),
        out_shape=jax.ShapeDtypeStruct((T, SWA_QKV), BF),
        in_specs=[pl.BlockSpec((tm, D), row), _full((8, D)), _full((D, SWA_QKV)), _full((1, SWA_QKV)),
                  pl.BlockSpec((tm, LANE), row), pl.BlockSpec((tm, LANE), row), pl.BlockSpec((tm, LANE), row)],
        out_specs=pl.BlockSpec((tm, SWA_QKV), row),
        compiler_params=_params(1),
    )(x, modl, w, b, ct_b, s1_b, s2_b)


SWA_NB = 1
SWA_R = SWA_NB * 4 * SWA_W
SWA_NK = (SWA_NB + 1) * SWA_W


def _swa_bias(first):
    row = lax.broadcasted_iota(jnp.int32, (SWA_R, SWA_NK), 0)
    col = lax.broadcasted_iota(jnp.int32, (SWA_R, SWA_NK), 1)
    bl = row // (4 * SWA_W)
    r = row % SWA_W
    cp = col - bl * SWA_W
    band = (cp > r) & (cp <= r + SWA_W)
    if first:
        band = band & ((col >= SWA_W) | (bl > 0))
    return jnp.where(band, 0.0, -jnp.inf).astype(F32)


def _swa_fill_bias(bias_s):
    @pl.when(pl.program_id(0) == 0)
    def _():
        bias_s[0] = _swa_bias(False)
        bias_s[1] = _swa_bias(True)


def _swa_specs(T, tb):
    nsub = tb // SWA_W
    q_spec = pl.BlockSpec((tb, SWA_O), lambda i: (i, 0))
    kvc_spec = pl.BlockSpec((tb, 2 * SWA_HKV * LANE), lambda i: (i, 2))
    kvp_spec = pl.BlockSpec((SWA_W, 2 * SWA_HKV * LANE), lambda i: (jnp.maximum(i * nsub - 1, 0), 2))
    return q_spec, kvc_spec, kvp_spec


def _swa_rows(ref, g, c):
    return jnp.concatenate(
        [ref[(c * SWA_NB + bl) * SWA_W:(c * SWA_NB + bl + 1) * SWA_W, (4 * g + hh) * LANE:(4 * g + hh + 1) * LANE]
         for bl in range(SWA_NB) for hh in range(4)], axis=0)


def _swa_chain(q_ref, kall, vall, sink_ref, bias_s, g, c):
    i = pl.program_id(0)
    lane = lax.broadcasted_iota(jnp.int32, (1, LANE), 1)
    qc = _swa_rows(q_ref, g, c)
    keys = slice(c * SWA_NB * SWA_W, c * SWA_NB * SWA_W + SWA_NK)
    k3 = kall[keys]
    v3 = jnp.where(lane < 64, vall[keys], jnp.ones((), BF))
    bias = bias_s[jnp.where(i == 0, 1, 0)] if c == 0 else bias_s[0]
    s = _dot_nt(qc, k3)
    c_exp = SWA_SCALE * math.log2(math.e)
    sb = [s[:, ch * LANE:(ch + 1) * LANE] * c_exp + bias[:, ch * LANE:(ch + 1) * LANE] for ch in range(SWA_NK // LANE)]
    mx = sb[0]
    for x in sb[1:]:
        mx = jnp.maximum(mx, x)
    sink2 = jnp.concatenate([jnp.full((SWA_W, 1), sink_ref[4 * g + hh] * math.log2(math.e), F32)
                             for _ in range(SWA_NB) for hh in range(4)], axis=0)
    m = jnp.maximum(jnp.max(mx, axis=1, keepdims=True), sink2)
    m_rep = jnp.broadcast_to(m, (SWA_R, LANE))
    eb = jnp.concatenate([jnp.exp2(x - m_rep) for x in sb], axis=1).astype(BF)
    es = jnp.exp2(sink2 - m)
    acc = _dot(eb, v3)
    linv = 1.0 / (acc[:, 64:65] + es)
    num = jnp.where(lane < 64, acc, 0.0)
    return qc, k3, v3, eb, num, linv, es


def _swa_attn_fwd(qkv, sinks, tb):
    T = qkv.shape[0]
    kw = SWA_HKV * LANE

    def body(q_ref, kvc_ref, kvp_ref, sink_ref, o_ref, bias_s):
        _swa_fill_bias(bias_s)
        for g in range(SWA_HKV):
            gl = slice(g * LANE, (g + 1) * LANE)
            gv = slice(kw + g * LANE, kw + (g + 1) * LANE)
            kall = jnp.concatenate([kvp_ref[:, gl], kvc_ref[:, gl]], axis=0)
            vall = jnp.concatenate([kvp_ref[:, gv], kvc_ref[:, gv]], axis=0)
            for c in range(tb // (SWA_NB * SWA_W)):
                _, _, _, _, num, linv, _ = _swa_chain(q_ref, kall, vall, sink_ref, bias_s, g, c)
                o = (num * linv).astype(BF)
                for bl in range(SWA_NB):
                    for hh in range(4):
                        piece = (bl * 4 + hh) * SWA_W
                        rows = slice((c * SWA_NB + bl) * SWA_W, (c * SWA_NB + bl + 1) * SWA_W)
                        o_ref[rows, (4 * g + hh) * LANE:(4 * g + hh + 1) * LANE] = o[piece:piece + SWA_W]

    q_spec, kvc_spec, kvp_spec = _swa_specs(T, tb)
    return pl.pallas_call(
        body, name="swa_attn_fwd", grid=(T // tb,),
        out_shape=jax.ShapeDtypeStruct((T, SWA_O), BF),
        in_specs=[q_spec, kvc_spec, kvp_spec, pl.BlockSpec(memory_space=pltpu.SMEM)],
        out_specs=pl.BlockSpec((tb, SWA_O), lambda i: (i, 0)),
        scratch_shapes=[pltpu.VMEM((2, SWA_R, SWA_NK), F32)],
        compiler_params=_params(1),
    )(qkv, qkv, qkv, sinks)


def _loss_grad(x, tgt, tm):
    T = x.shape[0]

    def body(x_ref, t_ref, dx_ref, l_ref):
        @pl.when(pl.program_id(0) == 0)
        def _():
            l_ref[...] = jnp.zeros(l_ref.shape, F32)
        diff = x_ref[...] - t_ref[...]
        dx_ref[...] = diff * (1.0 / D)
        l_ref[0:1, :] += jnp.sum(diff * diff, axis=0, keepdims=True) * (0.5 / D)

    row = lambda i: (i, 0)
    return pl.pallas_call(
        body, name="loss_grad", grid=(T // tm,),
        out_shape=(jax.ShapeDtypeStruct((T, D), F32), jax.ShapeDtypeStruct((8, D), F32)),
        in_specs=[pl.BlockSpec((tm, D), row), pl.BlockSpec((tm, D), row)],
        out_specs=(pl.BlockSpec((tm, D), row), _full((8, D))),
        compiler_params=_params(1),
    )(x, tgt)


def _outproj_ln_bwd(dxn, u, y, w, modl, g_row, ln_g, tm, name):
    T = dxn.shape[0]
    K = w.shape[0]

    def body(dxn_ref, u_ref, y_ref, w_ref, modl_ref, g_ref, dres_ref, dy_ref, da_ref, sm_ref):
        @pl.when(pl.program_id(0) == 0)
        def _():
            sm_ref[...] = jnp.zeros(sm_ref.shape, F32)
        uu = u_ref[...]
        mu = jnp.mean(uu, axis=1, keepdims=True)
        uc = uu - mu
        rstd = lax.rsqrt(jnp.mean(uc * uc, axis=1, keepdims=True) + LN_EPS)
        xhat = uc * rstd
        dxo = dxn_ref[...]
        dyh = dxo * g_ref[...]
        du = rstd * (dyh - jnp.mean(dyh, axis=1, keepdims=True)
                     - xhat * jnp.mean(dyh * xhat, axis=1, keepdims=True))
        dy = modl_ref[g_row:g_row + 1, :] * du
        dyb = dy.astype(BF)
        dres_ref[...] = ALPHA * du
        dy_ref[...] = dyb
        da_ref[...] = _dot_nt(dyb, w_ref[...]).astype(BF)
        sm_ref[0:1, :] += jnp.sum(dxo * xhat, axis=0, keepdims=True)
        sm_ref[1:2, :] += jnp.sum(dxo, axis=0, keepdims=True)
        sm_ref[2:3, :] += jnp.sum(du * y_ref[...].astype(F32), axis=0, keepdims=True)
        sm_ref[3:4, :] += jnp.sum(dy, axis=0, keepdims=True)

    row = lambda i: (i, 0)
    return pl.pallas_call(
        body, name=name, grid=(T // tm,),
        out_shape=(jax.ShapeDtypeStruct((T, D), F32), jax.ShapeDtypeStruct((T, D), BF),
                   jax.ShapeDtypeStruct((T, K), BF), jax.ShapeDtypeStruct((8, D), F32)),
        in_specs=[pl.BlockSpec((tm, D), row), pl.BlockSpec((tm, D), row), pl.BlockSpec((tm, D), row),
                  _full((K, D)), _full((8, D)), _full((1, D))],
        out_specs=(pl.BlockSpec((tm, D), row), pl.BlockSpec((tm, D), row), pl.BlockSpec((tm, K), row),
                   _full((8, D))),
        compiler_params=_params(1),
    )(dxn, u, y, w, modl, ln_g)


def _ffn_mid_bwd(da, g, u, x, modl, dres, wg, wu, tm, tn):
    T = x.shape[0]
    nn = F // tn

    def body(da_ref, g_ref, u_ref, x_ref, modl_ref, dres_ref, wg_ref, wu_ref, dg_ref, du_ref, dx_ref, sm_ref):
        i, n = pl.program_id(0), pl.program_id(1)

        @pl.when((i == 0) & (n == 0))
        def _():
            sm_ref[...] = jnp.zeros(sm_ref.shape, F32)

        gg = g_ref[...].astype(F32)
        sg = _sigmoid(gg)
        dav = da_ref[...].astype(F32)
        dgp = (dav * u_ref[...].astype(F32) * sg * (1.0 + gg * (1.0 - sg))).astype(BF)
        dup = (dav * gg * sg).astype(BF)
        dg_ref[...] = dgp
        du_ref[...] = dup
        dh = _dot_nt(dgp, wg_ref[...]) + _dot_nt(dup, wu_ref[...])

        @pl.when(n == 0)
        def _():
            dx_ref[...] = dh

        @pl.when(n > 0)
        def _():
            dx_ref[...] += dh

        @pl.when(n == nn - 1)
        def _():
            dht = dx_ref[...]
            sm_ref[0:1, :] += jnp.sum(dht * x_ref[...], axis=0, keepdims=True)
            sm_ref[1:2, :] += jnp.sum(dht, axis=0, keepdims=True)
            dx_ref[...] = dres_ref[...] + dht * (1.0 + modl_ref[4:5, :])

    tile = pl.BlockSpec((tm, tn), lambda i, n: (i, n))
    rowd = pl.BlockSpec((tm, D), lambda i, n: (i, 0))
    wcol = pl.BlockSpec((D, tn), lambda i, n: (0, n))
    return pl.pallas_call(
        body, name="ffn_mid_bwd", grid=(T // tm, nn),
        out_shape=(jax.ShapeDtypeStruct((T, F), BF), jax.ShapeDtypeStruct((T, F), BF),
                   jax.ShapeDtypeStruct((T, D), F32), jax.ShapeDtypeStruct((8, D), F32)),
        in_specs=[tile, tile, tile, rowd, _full((8, D)), rowd, wcol, wcol],
        out_specs=(tile, tile, rowd, _full((8, D))),
        compiler_params=_params(2),
    )(da, g, u, x, modl, dres, wg, wu)


def _wgrad(a, b, tm, tk, tn, name, modl=None, rows=None):
    T, K = a.shape
    N = b.shape[1]

    def body(*refs):
        if modl is None:
            a_ref, b_ref, o_ref = refs
            av = a_ref[...]
        else:
            a_ref, modl_ref, b_ref, o_ref = refs
            av = _modulate(a_ref[...], modl_ref, rows[0], rows[1]).astype(BF)

        @pl.when(pl.program_id(2) == 0)
        def _():
            o_ref[...] = jnp.zeros(o_ref.shape, F32)
        o_ref[...] += _dot_tn(av, b_ref[...])

    in_specs = [pl.BlockSpec((tm, tk), lambda k, n, t: (t, k))]
    args = [a]
    if modl is not None:
        in_specs.append(_full((8, D)))
        args.append(modl)
    in_specs.append(pl.BlockSpec((tm, tn), lambda k, n, t: (t, n)))
    args.append(b)
    return pl.pallas_call(
        body, name=name, grid=(K // tk, N // tn, T // tm),
        out_shape=jax.ShapeDtypeStruct((K, N), F32),
        in_specs=in_specs,
        out_specs=pl.BlockSpec((tk, tn), lambda k, n, t: (k, n)),
        compiler_params=_params(3),
    )(*args)


def _flash_delta(o, do, tb):
    T = o.shape[0]
    H = o.shape[1] // LANE

    def body(o_ref, do_ref, delta_ref):
        delta = jnp.sum(o_ref[...].astype(F32) * do_ref[...].astype(F32), axis=1, keepdims=True)
        delta_ref[0] = _col_to_row(delta)

    blk = pl.BlockSpec((tb, LANE), lambda h, i: (i, h))
    return pl.pallas_call(
        body, name="flash_delta", grid=(H, T // tb),
        out_shape=jax.ShapeDtypeStruct((H, 1, T), F32),
        in_specs=[blk, blk],
        out_specs=pl.BlockSpec((1, 1, tb), lambda h, i: (h, 0, i)),
        compiler_params=_params(2),
    )(o, do)


def _flash_bwd(q, k, v, do, lse, delta, tb, comm_src=None):
    H, T, _ = q.shape
    nq = T // tb
    rh = min(256, tb)
    c_exp = MLA_SCALE * math.log2(math.e)

    def body(*refs):
        if comm_src is None:
            k_ref, v_ref, q_ref, do_ref, lse_ref, delta_ref, dq_ref, dk_ref, dv_ref, dk_s, dv_s = refs
        else:
            (k_ref, v_ref, q_ref, do_ref, lse_ref, delta_ref, src_ref, dq_ref, dk_ref, dv_ref, got_ref,
             dk_s, dv_s, *sems) = refs
            comm = (src_ref, got_ref, *sems)
            first = (pl.program_id(0) == 0) & (pl.program_id(1) == 0)
            last = (pl.program_id(0) == H - 1) & (pl.program_id(1) == nq - 1)
            pl.when(first)(functools.partial(_exchange_start, *comm, gather=False))
        j = pl.program_id(1)

        @pl.when(j == 0)
        def _():
            def zero(cix, carry):
                dq_ref[0, pl.ds(pl.multiple_of(cix * tb, tb), tb), :] = jnp.zeros((tb, MLA_HD), F32)
                return carry
            lax.fori_loop(0, nq, zero, 0)

        dk_s[...] = jnp.zeros(dk_s.shape, F32)
        dv_s[...] = jnp.zeros(dv_s.shape, F32)

        def block(i, masked):
            start = pl.multiple_of(i * tb, tb)
            qb = q_ref[0, pl.ds(start, tb), :]
            dob = do_ref[pl.ds(start, tb), :]
            l2 = lse_ref[0, :, pl.ds(start, tb)] * math.log2(math.e)
            dl = delta_ref[0, :, pl.ds(start, tb)]
            dq_part = None
            for hf in range(tb // rh):
                r = slice(hf * rh, (hf + 1) * rh)
                kc = k_ref[0, r, :]
                st = _dot_nt(kc, qb)
                if masked:
                    st = jnp.where(_causal_mask(hf * rh, rh, tb, True), st, -jnp.inf)
                pt = jnp.exp2(st * c_exp - l2)
                dpt = _dot_nt(v_ref[0, r, :], dob)
                dst = (pt * (dpt - dl)).astype(BF)
                dv_s[r, :] += _dot(pt.astype(BF), dob)
                dk_s[r, :] += _dot(dst, qb)
                part = _dot_tn(dst, kc)
                dq_part = part if dq_part is None else dq_part + part
            dq_ref[0, pl.ds(start, tb), :] += dq_part * MLA_SCALE

        def step(i, carry):
            block(i, False)
            return carry

        block(j, True)
        lax.fori_loop(j + 1, nq, step, 0)
        dk_ref[0] = (dk_s[...] * MLA_SCALE).astype(BF)
        dv_ref[0] = dv_s[...].astype(BF)
        if comm_src is not None:
            pl.when(last)(functools.partial(_exchange_wait, *comm, gather=False))

    once = pl.Buffered(1)
    hbm = pl.BlockSpec(memory_space=pltpu.HBM)
    with_comm = comm_src is not None
    return pl.pallas_call(
        body, name="flash_bwd_scatter" if with_comm else "flash_bwd", grid=(H, nq),
        out_shape=(jax.ShapeDtypeStruct((H, T, MLA_HD), F32), jax.ShapeDtypeStruct((H, T, MLA_HD), BF),
                   jax.ShapeDtypeStruct((H, T, LANE), BF)) + ((_exchange_out(comm_src, False),) if with_comm else ()),
        in_specs=[pl.BlockSpec((1, tb, MLA_HD), lambda h, j: (h, j, 0)),
                  pl.BlockSpec((1, tb, LANE), lambda h, j: (h, j, 0)),
                  pl.BlockSpec((1, T, MLA_HD), lambda h, j: (h, 0, 0), pipeline_mode=once),
                  pl.BlockSpec((T, LANE), lambda h, j: (0, h), pipeline_mode=once),
                  pl.BlockSpec((1, 1, T), lambda h, j: (h, 0, 0)),
                  pl.BlockSpec((1, 1, T), lambda h, j: (h, 0, 0))] + ([hbm] if with_comm else []),
        out_specs=(pl.BlockSpec((1, T, MLA_HD), lambda h, j: (h, 0, 0), pipeline_mode=once),
                   pl.BlockSpec((1, tb, MLA_HD), lambda h, j: (h, j, 0)),
                   pl.BlockSpec((1, tb, LANE), lambda h, j: (h, j, 0))) + ((hbm,) if with_comm else ()),
        scratch_shapes=[pltpu.VMEM((tb, MLA_HD), F32), pltpu.VMEM((tb, LANE), F32)]
        + (EXCHANGE_SEMS if with_comm else []),
        compiler_params=_params(2),
    )(*((k, v, q, do, lse, delta, comm_src) if with_comm else (k, v, q, do, lse, delta)))


def _mla_proj_bwd(dq, dk, dv, x, modl, dres, w_in, q_norm, w_q, kv_norm, w_kv, tabs_neg, tm):
    T = x.shape[0]
    ct_a, s1_n, s2_n = tabs_neg

    def body(dq_ref, dk_ref, dv_ref, x_ref, modl_ref, dres_ref, win_ref, qn_ref, wq_ref, kvn_ref, wkv_ref,
             ct_ref, s1_ref, s2_ref, dx_ref, dwin_ref, dwq_ref, dwkv_ref, sm_ref, dqn_ref, dkvn_ref):
        @pl.when(pl.program_id(0) == 0)
        def _():
            for r in (dwin_ref, dwq_ref, dwkv_ref, sm_ref, dqn_ref, dkvn_ref):
                r[...] = jnp.zeros(r.shape, F32)

        xv = x_ref[...]
        h = _modulate(xv, modl_ref, 1, 0).astype(BF)
        lat = _dot(h, win_ref[...])
        ql, kvl = lat[:, :MLA_QR], lat[:, MLA_QR:MLA_QR + MLA_KVR]
        qhat = ql * lax.rsqrt(jnp.mean(ql * ql, axis=1, keepdims=True) + RMS_EPS)
        kvhat = kvl * lax.rsqrt(jnp.mean(kvl * kvl, axis=1, keepdims=True) + RMS_EPS)
        rq = lax.rsqrt(jnp.mean(ql * ql, axis=1, keepdims=True) + RMS_EPS)
        rkv = lax.rsqrt(jnp.mean(kvl * kvl, axis=1, keepdims=True) + RMS_EPS)
        qn = (qhat * qn_ref[...]).astype(BF)
        kvn = (kvhat * kvn_ref[...]).astype(BF)
        ct, s1, s2 = ct_ref[...], s1_ref[...], s2_ref[...]

        dqn = jnp.zeros((tm, MLA_QR), F32)
        dkvn = jnp.zeros((tm, MLA_KVR), F32)
        dkr = jnp.zeros((tm, LANE), F32)
        for hd in range(MLA_H):
            cols = slice(hd * MLA_HD, (hd + 1) * MLA_HD)
            dqh = dq_ref[hd]
            dqr = _rope128(dqh[:, LANE:], ct, s1, s2, MLA_ROPE // 2).astype(BF)
            dqh = jnp.concatenate([dqh[:, :LANE].astype(BF), dqr], axis=1)
            dqn = dqn + _dot_nt(dqh, wq_ref[:, cols])
            dwq_ref[:, cols] += _dot_tn(qn, dqh)
            dkh = dk_ref[hd]
            dkr = dkr + dkh[:, LANE:].astype(F32)
            dkvh = jnp.concatenate([dkh[:, :LANE], dv_ref[hd]], axis=1)
            dkvn = dkvn + _dot_nt(dkvh, wkv_ref[:, cols])
            dwkv_ref[:, cols] += _dot_tn(kvn, dkvh)
        dkr = _rope128(dkr, ct, s1, s2, MLA_ROPE // 2)

        dqn_ref[...] += jnp.sum(dqn * qhat, axis=0, keepdims=True)
        dkvn_ref[...] += jnp.sum(dkvn * kvhat, axis=0, keepdims=True)
        dqh_ = dqn * qn_ref[...]
        dkvh_ = dkvn * kvn_ref[...]
        dql = rq * (dqh_ - qhat * jnp.mean(dqh_ * qhat, axis=1, keepdims=True))
        dkvl = rkv * (dkvh_ - kvhat * jnp.mean(dkvh_ * kvhat, axis=1, keepdims=True))
        dlat = jnp.concatenate([dql, dkvl, dkr], axis=1).astype(BF)
        dwin_ref[...] += _dot_tn(h, dlat)
        dh = _dot_nt(dlat, win_ref[...])
        sm_ref[0:1, :] += jnp.sum(dh * xv, axis=0, keepdims=True)
        sm_ref[1:2, :] += jnp.sum(dh, axis=0, keepdims=True)
        dx_ref[...] = dres_ref[...] + dh * (1.0 + modl_ref[1:2, :])

    row = lambda i: (i, 0)
    head = lambda i: (0, i, 0)
    nq = MLA_H * MLA_HD
    return pl.pallas_call(
        body, name="mla_proj_bwd", grid=(T // tm,),
        out_shape=(jax.ShapeDtypeStruct((T, D), F32), jax.ShapeDtypeStruct((D, MLA_LAT), F32),
                   jax.ShapeDtypeStruct((MLA_QR, nq), F32), jax.ShapeDtypeStruct((MLA_KVR, nq), F32),
                   jax.ShapeDtypeStruct((8, D), F32), jax.ShapeDtypeStruct((1, MLA_QR), F32),
                   jax.ShapeDtypeStruct((1, MLA_KVR), F32)),
        in_specs=[pl.BlockSpec((MLA_H, tm, MLA_HD), head), pl.BlockSpec((MLA_H, tm, MLA_HD), head),
                  pl.BlockSpec((MLA_H, tm, LANE), head), pl.BlockSpec((tm, D), row), _full((8, D)),
                  pl.BlockSpec((tm, D), row), _full((D, MLA_LAT)), _full((1, MLA_QR)), _full((MLA_QR, nq)),
                  _full((1, MLA_KVR)), _full((MLA_KVR, nq)),
                  pl.BlockSpec((tm, LANE), row), pl.BlockSpec((tm, LANE), row), pl.BlockSpec((tm, LANE), row)],
        out_specs=(pl.BlockSpec((tm, D), row), _full((D, MLA_LAT)), _full((MLA_QR, nq)), _full((MLA_KVR, nq)),
                   _full((8, D)), _full((1, MLA_QR)), _full((1, MLA_KVR))),
        compiler_params=_params(1),
    )(dq, dk, dv, x, modl, dres, w_in, q_norm, w_q, kv_norm, w_kv, ct_a, s1_n, s2_n)


def _swa_attn_bwd(qkv, sinks, do, tb):
    T = qkv.shape[0]
    kw = SWA_HKV * LANE
    nstep = T // tb

    def body(q_ref, kvc_ref, kvp_ref, sink_ref, do_ref, dq_ref, dkvc_ref, dkvp_ref, dsink_ref, dk_s, dv_s, bias_s):
        i = pl.program_id(0)
        _swa_fill_bias(bias_s)

        @pl.when(i == 0)
        def _():
            dsink_ref[...] = jnp.zeros(dsink_ref.shape, F32)

        dk_s[...] = jnp.zeros(dk_s.shape, F32)
        dv_s[...] = jnp.zeros(dv_s.shape, F32)
        lane = lax.broadcasted_iota(jnp.int32, (1, LANE), 1)
        for g in range(SWA_HKV):
            gl = slice(g * LANE, (g + 1) * LANE)
            gv = slice(kw + g * LANE, kw + (g + 1) * LANE)
            kall = jnp.concatenate([kvp_ref[:, gl], kvc_ref[:, gl]], axis=0)
            vall = jnp.concatenate([kvp_ref[:, gv], kvc_ref[:, gv]], axis=0)
            for c in range(tb // (SWA_NB * SWA_W)):
                qc, k3, v3, eb, num, linv, es = _swa_chain(q_ref, kall, vall, sink_ref, bias_s, g, c)
                doc = _swa_rows(do_ref, g, c)
                delta = jnp.sum(doc.astype(F32) * (num * linv), axis=1, keepdims=True)
                delta_rep = jnp.broadcast_to(delta, (SWA_R, LANE))
                dp = _dot_nt(doc, v3)
                linv_rep = jnp.broadcast_to(linv, (SWA_R, LANE))
                pch = [eb[:, ch * LANE:(ch + 1) * LANE].astype(F32) * linv_rep for ch in range(SWA_NK // LANE)]
                ds = jnp.concatenate([pch[ch] * (dp[:, ch * LANE:(ch + 1) * LANE] - delta_rep)
                                      for ch in range(SWA_NK // LANE)], axis=1).astype(BF)
                pb = jnp.concatenate(pch, axis=1).astype(BF)
                dqc = (_dot(ds, k3) * SWA_SCALE).astype(BF)
                keys = slice(c * SWA_NB * SWA_W, c * SWA_NB * SWA_W + SWA_NK)
                dk_s[keys, gl] += _dot_tn(ds, qc) * SWA_SCALE
                dv_s[keys, gl] += _dot_tn(pb, doc)
                dsk = es * linv * delta
                for hh in range(4):
                    hq = 4 * g + hh
                    tot = jnp.zeros((1, 1), F32)
                    for bl in range(SWA_NB):
                        piece = (bl * 4 + hh) * SWA_W
                        rows = slice((c * SWA_NB + bl) * SWA_W, (c * SWA_NB + bl + 1) * SWA_W)
                        dq_ref[rows, hq * LANE:(hq + 1) * LANE] = dqc[piece:piece + SWA_W]
                        tot = tot + jnp.sum(dsk[piece:piece + SWA_W], axis=0, keepdims=True)
                    dsink_ref[0:1, :] -= jnp.where(lane == hq, tot, 0.0)
        dkvp_ref[0, :, 0:kw] = dk_s[0:SWA_W, :]
        dkvp_ref[0, :, kw:2 * kw] = dv_s[0:SWA_W, :]
        dkvc_ref[:, 0:kw] = dk_s[SWA_W:, :]
        dkvc_ref[:, kw:2 * kw] = dv_s[SWA_W:, :]

    q_spec, kvc_spec, kvp_spec = _swa_specs(T, tb)
    return pl.pallas_call(
        body, name="swa_attn_bwd", grid=(nstep,),
        out_shape=(jax.ShapeDtypeStruct((T, SWA_O), BF), jax.ShapeDtypeStruct((T, 2 * kw), F32),
                   jax.ShapeDtypeStruct((nstep, SWA_W, 2 * kw), F32), jax.ShapeDtypeStruct((8, LANE), F32)),
        in_specs=[q_spec, kvc_spec, kvp_spec, pl.BlockSpec(memory_space=pltpu.SMEM),
                  pl.BlockSpec((tb, SWA_O), lambda i: (i, 0))],
        out_specs=(pl.BlockSpec((tb, SWA_O), lambda i: (i, 0)), pl.BlockSpec((tb, 2 * kw), lambda i: (i, 0)),
                   pl.BlockSpec((1, SWA_W, 2 * kw), lambda i: (i, 0, 0)), _full((8, LANE))),
        scratch_shapes=[pltpu.VMEM((tb + SWA_W, kw), F32), pltpu.VMEM((tb + SWA_W, kw), F32),
                        pltpu.VMEM((2, SWA_R, SWA_NK), F32)],
        compiler_params=_params(1),
    )(qkv, qkv, qkv, sinks, do)


def _swa_proj_bwd(dq, dkvc, dkvp, x, modl, dres, w, tabs_neg, tm):
    T = x.shape[0]
    nstep = T // tm
    kw = SWA_HKV * LANE
    ct_b, s1_n, s2_n = tabs_neg

    def body(dq_ref, dkvc_ref, dkvp_ref, x_ref, modl_ref, dres_ref, w_ref, ct_ref, s1_ref, s2_ref,
             dx_ref, dz_ref, sm_ref, db_ref):
        i = pl.program_id(0)

        @pl.when(i == 0)
        def _():
            sm_ref[...] = jnp.zeros(sm_ref.shape, F32)
            db_ref[...] = jnp.zeros(db_ref.shape, F32)

        ct, s1, s2 = ct_ref[...], s1_ref[...], s2_ref[...]
        has_next = i + 1 < nstep
        for grp in range(SWA_QKV // LANE):
            cols = slice(grp * LANE, (grp + 1) * LANE)
            if grp < SWA_HQ:
                z = dq_ref[:, cols].astype(F32)
            else:
                kc = slice((grp - SWA_HQ) * LANE, (grp - SWA_HQ + 1) * LANE)
                cur = dkvc_ref[:, kc]
                tail = cur[tm - SWA_W:] + jnp.where(has_next, dkvp_ref[0, :, kc], 0.0)
                z = jnp.concatenate([cur[:tm - SWA_W], tail], axis=0)
            if grp < SWA_HQ + SWA_HKV:
                z = _rope128(z, ct, s1, s2, 8)
            db_ref[0:1, cols] += jnp.sum(z, axis=0, keepdims=True)
            dz_ref[:, cols] = z.astype(BF)
        dh = _dot_nt(dz_ref[...], w_ref[...])
        sm_ref[0:1, :] += jnp.sum(dh * x_ref[...], axis=0, keepdims=True)
        sm_ref[1:2, :] += jnp.sum(dh, axis=0, keepdims=True)
        dx_ref[...] = dres_ref[...] + dh * (1.0 + modl_ref[1:2, :])

    row = lambda i: (i, 0)
    return pl.pallas_call(
        body, name="swa_proj_bwd", grid=(nstep,),
        out_shape=(jax.ShapeDtypeStruct((T, D), F32), jax.ShapeDtypeStruct((T, SWA_QKV), BF),
                   jax.ShapeDtypeStruct((8, D), F32), jax.ShapeDtypeStruct((8, SWA_QKV), F32)),
        in_specs=[pl.BlockSpec((tm, SWA_O), row), pl.BlockSpec((tm, 2 * kw), row),
                  pl.BlockSpec((1, SWA_W, 2 * kw), lambda i: (jnp.minimum(i + 1, nstep - 1), 0, 0)),
                  pl.BlockSpec((tm, D), row), _full((8, D)), pl.BlockSpec((tm, D), row), _full((D, SWA_QKV)),
                  pl.BlockSpec((tm, LANE), row), pl.BlockSpec((tm, LANE), row), pl.BlockSpec((tm, LANE), row)],
        out_specs=(pl.BlockSpec((tm, D), row), pl.BlockSpec((tm, SWA_QKV), row), _full((8, D)),
                   _full((8, SWA_QKV))),
        compiler_params=_params(1),
    )(dq, dkvc, dkvp, x, modl, dres, w, ct_b, s1_n, s2_n)


def _adamw(gparts, w, m, v, name):
    P, R, C = gparts.shape
    tr = R
    for cand in (512, 256, 128):
        if R % cand == 0 and R > cand:
            tr = cand
            break
    c1 = 1.0 / (1.0 - ADAM_B1 ** ADAM_STEP)
    c2 = 1.0 / (1.0 - ADAM_B2 ** ADAM_STEP)

    def body(gp_ref, w_ref, m_ref, v_ref, g_ref, d_ref, nm_ref, nv_ref):
        g = gp_ref[0].astype(F32)
        for p in range(1, P):
            g = g + gp_ref[p].astype(F32)
        nm = ADAM_B1 * m_ref[...] + (1.0 - ADAM_B1) * g
        nv = ADAM_B2 * v_ref[...] + (1.0 - ADAM_B2) * (g * g)
        g_ref[...] = g
        nm_ref[...] = nm
        nv_ref[...] = nv
        d_ref[...] = -ADAM_LR * ((nm * c1) / (jnp.sqrt(nv * c2) + ADAM_EPS) + ADAM_WD * w_ref[...])

    blk = pl.BlockSpec((tr, C), lambda i: (i, 0))
    return pl.pallas_call(
        body, name=name, grid=(R // tr,),
        out_shape=(jax.ShapeDtypeStruct((R, C), F32),) * 4,
        in_specs=[pl.BlockSpec((P, tr, C), lambda i: (0, i, 0)), blk, blk, blk],
        out_specs=(blk,) * 4,
        compiler_params=_params(1),
    )(gparts, w, m, v)


PACK_W = 1024

BIG = {
    "ffn_w_gate": ((DEPTH, D, F // NDEV), 2),
    "ffn_w_up": ((DEPTH, D, F // NDEV), 2),
    "ffn_w_down": ((DEPTH, F // NDEV, D), 1),
    "mla_w_in": ((2, D // NDEV, 704), 1),
    "mla_w_q_b": ((2, MLA_QR, 1536 // NDEV), 2),
    "mla_w_kv_b": ((2, MLA_KVR, 2048 // NDEV), 2),
    "mla_w_o": ((2, D // NDEV, D), 1),
    "swa_w_qkv": ((2, D, 1536 // NDEV), 2),
    "swa_w_o": ((2, D // NDEV, D), 1),
}


EARLY = [("mla_w_in", 0, 1), ("mla_w_q_b", 0, 1), ("mla_w_kv_b", 0, 1)]
LATE = [("ffn_w_gate", 0, 4), ("ffn_w_up", 0, 4), ("ffn_w_down", 0, 4), ("mla_w_in", 1, 2), ("mla_w_q_b", 1, 2),
        ("mla_w_kv_b", 1, 2), ("mla_w_o", 0, 2), ("swa_w_qkv", 0, 2), ("swa_w_o", 0, 2)]


def _pack_rows(n):
    return -(-n // (16 * PACK_W)) * 16


def _entry_shape(name, lo, hi):
    return (hi - lo,) + BIG[name][0][1:]


def _pack_local(inp, entries):
    parts = []
    for name, lo, hi in entries:
        n = math.prod(_entry_shape(name, lo, hi))
        flat = inp[name][lo:hi].astype(BF).reshape(-1)
        parts.append(jnp.pad(flat, (0, _pack_rows(n) * PACK_W - n)).reshape(-1, PACK_W))
    return jnp.concatenate(parts, axis=0)


def _pack_full(full, entries):
    parts = []
    for name, lo, hi in entries:
        a, axis = full[(name, lo)], BIG[name][1]
        split = a.shape[:axis] + (NDEV, a.shape[axis] // NDEV) + a.shape[axis + 1:]
        a = jnp.moveaxis(a.reshape(split), axis, 0).astype(BF).reshape(NDEV, -1)
        n = math.prod(_entry_shape(name, lo, hi))
        parts.append(jnp.pad(a, ((0, 0), (0, _pack_rows(n) * PACK_W - n))).reshape(NDEV, -1, PACK_W))
    return jnp.concatenate(parts, axis=1)


def _unpack_blocks(packed, entries):
    out, r0 = {}, 0
    for name, lo, hi in entries:
        shape = _entry_shape(name, lo, hi)
        n = math.prod(shape)
        rows = _pack_rows(n)
        out[(name, lo)] = packed[:, r0:r0 + rows].reshape(NDEV, -1)[:, :n].reshape((NDEV,) + shape)
        r0 += rows
    return out


def _unpack_full(packed, entries):
    out = {}
    for (name, lo), blk in _unpack_blocks(packed, entries).items():
        axis = BIG[name][1]
        a = jnp.moveaxis(blk, 0, axis)
        out[(name, lo)] = a.reshape(a.shape[:axis] + (a.shape[axis] * a.shape[axis + 1],) + a.shape[axis + 2:])
    return out


def _pad_heads(a, axis, nheads, width, to):
    shp = a.shape[:axis] + (nheads, width) + a.shape[axis + 1:]
    a = a.reshape(shp)
    pad = [(0, 0)] * a.ndim
    pad[axis + 1] = (0, to - width)
    a = jnp.pad(a, pad)
    return a.reshape(a.shape[:axis] + (nheads * to,) + a.shape[axis + 2:])


def _unpad_heads(a, axis, nheads, width, to):
    shp = a.shape[:axis] + (nheads, to) + a.shape[axis + 1:]
    a = lax.slice_in_dim(a.reshape(shp), 0, width, axis=axis + 1)
    return a.reshape(a.shape[:axis] + (nheads * width,) + a.shape[axis + 2:])


def _swa_pad_cols(a):
    return _pad_heads(a, a.ndim - 1, SWA_HQ + 2 * SWA_HKV, 64, LANE)


def _rope_tables(positions, half):
    rot = 2 * half
    inv = ROPE_THETA ** (-jnp.arange(0, rot, 2, dtype=F32) / rot)
    ang = positions.astype(F32)[:, None] * inv
    cos, sin = jnp.cos(ang), jnp.sin(ang)
    T = positions.shape[0]
    ones = jnp.ones((T, LANE - rot), F32)
    zeros = jnp.zeros((T, LANE - rot), F32)
    zh = jnp.zeros((T, half), F32)
    ct = jnp.concatenate([cos, cos, ones], axis=1)
    s1 = jnp.concatenate([-sin, zh, zeros], axis=1)
    s2 = jnp.concatenate([zh, sin, zeros], axis=1)
    return (ct, s1, s2), (ct, -s1, -s2)


def _small_pack(vecs, rows):
    flat = jnp.concatenate([v.astype(F32).reshape(-1) for v in vecs])
    return jnp.pad(flat, (0, rows * LANE - flat.shape[0])).reshape(rows, LANE)


def _small_unpack(buf, shapes):
    flat = buf.reshape(NDEV, -1)
    out, o = [], 0
    for shp in shapes:
        n = math.prod(shp)
        out.append(flat[:, o:o + n].reshape((NDEV,) + shp))
        o += n
    return out


def _step(inp):
    x = inp["x"][0]
    tgt = inp["loss_target"][0]
    T = x.shape[0]
    tm = min(512, T)
    tmf = min(256, T)
    tb = min(512, T)
    tbf = min(1024, T)
    tbq = min(2048, T)
    tnf = F // 2
    me = 4 * lax.axis_index("x") + 2 * lax.axis_index("y") + lax.axis_index("c")

    small_in = _small_pack([inp["c"], inp["swa_b_qkv"], inp["swa_b_o"]], 16)
    c_all, bqkv_blk, bo_blk = _small_unpack(_exchange(small_in, True, "gather_small"),
                                            [(D,), (2, 1536 // NDEV), (2, D // NDEV)])
    swa_b_qkv = jnp.moveaxis(bqkv_blk, 0, 1).reshape(2, 1536)
    swa_b_o = jnp.moveaxis(bo_blk, 0, 1).reshape(2, D)

    w_early = _unpack_full(_exchange(_pack_local(inp, EARLY), True, "gather_early"), EARLY)

    ncol = 6 * D // NDEV
    ada_b_loc = lax.dynamic_slice_in_dim(inp["ada_b"], me * ncol, ncol, axis=1)[:, None, :]
    mod_all = _mod_all(c_all, inp["ada_w"], ada_b_loc)
    mod_src = jnp.moveaxis(mod_all, 1, 0).reshape(NDEV, DEPTH * ncol // LANE, LANE)
    mod_got = _exchange(mod_src, False, "scatter_mod").reshape(NDEV, DEPTH, ncol)
    mod = jnp.moveaxis(mod_got, 0, 1).reshape(DEPTH, 6, D)
    modl = jnp.pad(mod, ((0, 0), (0, 2), (0, 0)))

    def mla_proj_weights(w, lo):
        return (jnp.pad(w[("mla_w_in", lo)][0], ((0, 0), (0, MLA_LAT - 704))),
                _pad_heads(w[("mla_w_q_b", lo)][0], 1, MLA_H, 192, MLA_HD), w[("mla_w_kv_b", lo)][0])

    w_in, w_q, w_kv = [[t] for t in mla_proj_weights(w_early, 0)]
    b_qkv = _swa_pad_cols(swa_b_qkv)
    zero_bias = jnp.zeros((1, D), F32)

    pos = inp["positions"][0]
    tabs_a, tabs_a_neg = _rope_tables(pos, MLA_ROPE // 2)
    tabs_b, tabs_b_neg = _rope_tables(pos, 8)

    saved = []
    xs = x
    for i in range(DEPTH):
        j = i // 2
        st = {"x0": xs}
        if i % 2 == 0:
            q, k, v = _mla_proj_fwd(xs, modl[i], w_in[j], inp["mla_q_norm"][j][None], w_q[j],
                                    inp["mla_kv_norm"][j][None], w_kv[j], tabs_a, tm)
            if i == 0:
                o, lse, got = _flash_fwd(q, k, v, tbq, _pack_local(inp, LATE))
                w_late = _unpack_full(got, LATE)
                for lst, t in zip((w_in, w_q, w_kv), mla_proj_weights(w_late, 1)):
                    lst.append(t)
                w_o_mla = w_late[("mla_w_o", 0)]
                w_qkv = _swa_pad_cols(w_late[("swa_w_qkv", 0)])
                w_o_swa = _pad_heads(w_late[("swa_w_o", 0)], 1, SWA_HQ, 64, LANE)
                w_gate, w_up, w_down = (w_late[(n, 0)] for n in ("ffn_w_gate", "ffn_w_up", "ffn_w_down"))
            else:
                o, lse = _flash_fwd(q, k, v, tbq)
            st.update(q=q, k=k, v=v, o=o, lse=lse)
            w_o, b_o = w_o_mla[j], zero_bias
        else:
            qkv = _swa_proj_fwd(xs, modl[i], w_qkv[j], b_qkv[j][None], tabs_b, tm)
            o = _swa_attn_fwd(qkv, inp["swa_sinks"][j], tb)
            st.update(qkv=qkv, o=o)
            w_o, b_o = w_o_swa[j], swa_b_o[j][None]
        y, u, xs = _outproj_ln_fwd(o, w_o, b_o, xs, modl[i], 2, inp["ln_mix_g"][i][None],
                                   inp["ln_mix_b"][i][None], tm, f"mix_out_fwd_{i % 2}")
        st.update(y_m=y, u_m=u, x1=xs, w_o=w_o)
        g, up, a = _ffn_up_fwd(xs, modl[i], w_gate[i], w_up[i], tmf, tnf)
        y, u, xs = _outproj_ln_fwd(a, w_down[i], zero_bias, xs, modl[i], 5,
                                   inp["ln_ffn_g"][i][None], inp["ln_ffn_b"][i][None], tmf, "ffn_out_fwd")
        st.update(g=g, up=up, a=a, y_f=y, u_f=u)
        saved.append(st)

    dx, loss_rows = _loss_grad(xs, tgt, tm)
    loss = lax.psum(jnp.sum(loss_rows[0]), ("x", "y", "c"))

    gfull = {n: [None] * (DEPTH if n.startswith("ffn") else 2) for n in BIG}
    dmod = [None] * DEPTH
    g_ln = {n: [None] * DEPTH for n in ("ln_mix_g", "ln_mix_b", "ln_ffn_g", "ln_ffn_b")}
    g_qn, g_kvn, g_sink, g_bqkv, g_bo = [None] * 2, [None] * 2, [None] * 2, [None] * 2, [None] * 2
    for i in reversed(range(DEPTH)):
        j = i // 2
        st = saved[i]
        dres, dy, da, sm = _outproj_ln_bwd(dx, st["u_f"], st["y_f"], w_down[i], modl[i], 5,
                                           inp["ln_ffn_g"][i][None], tmf, "ffn_out_bwd")
        g_ln["ln_ffn_g"][i], g_ln["ln_ffn_b"][i], dg_f = sm[0], sm[1], sm[2]
        gfull["ffn_w_down"][i] = _wgrad(st["a"], dy, tm, F // 2, D, "wgrad_down")
        dgp, dup, dx, sm = _ffn_mid_bwd(da, st["g"], st["up"], st["x1"], modl[i], dres,
                                        w_gate[i], w_up[i], tmf, tnf)
        dsc_f, dsh_f = sm[0], sm[1]
        gfull["ffn_w_gate"][i] = _wgrad(st["x1"], dgp, tm, D, tnf, "wgrad_gate", modl[i], (4, 3))
        gfull["ffn_w_up"][i] = _wgrad(st["x1"], dup, tm, D, tnf, "wgrad_up", modl[i], (4, 3))

        dres, dy, do, sm = _outproj_ln_bwd(dx, st["u_m"], st["y_m"], st["w_o"], modl[i], 2,
                                           inp["ln_mix_g"][i][None], tm, f"mix_out_bwd_{i % 2}")
        g_ln["ln_mix_g"][i], g_ln["ln_mix_b"][i], dg_m = sm[0], sm[1], sm[2]
        if i % 2 == 0:
            gfull["mla_w_o"][j] = _wgrad(st["o"], dy, tm, D, D, "wgrad_mla_o")
            delta = _flash_delta(st["o"], do, tb)
            if i == 0:
                late = {(n, lo): jnp.stack(gfull[n][lo:hi]) for n, lo, hi in LATE}
                dq, dk, dv, got = _flash_bwd(st["q"], st["k"], st["v"], do, st["lse"], delta, tbf,
                                             _pack_full(late, LATE))
                gparts = _unpack_blocks(got, LATE)
            else:
                dq, dk, dv = _flash_bwd(st["q"], st["k"], st["v"], do, st["lse"], delta, tbf)
            dx, dwin, dwq, dwkv, sm, dqn, dkvn = _mla_proj_bwd(
                dq, dk, dv, st["x0"], modl[i], dres, w_in[j], inp["mla_q_norm"][j][None], w_q[j],
                inp["mla_kv_norm"][j][None], w_kv[j], tabs_a_neg, tm)
            gfull["mla_w_in"][j] = dwin[:, :704]
            gfull["mla_w_q_b"][j] = _unpad_heads(dwq, 1, MLA_H, 192, MLA_HD)
            gfull["mla_w_kv_b"][j] = dwkv
            g_qn[j], g_kvn[j] = dqn[0], dkvn[0]
        else:
            g_bo[j] = sm[3]
            dwo = _wgrad(st["o"], dy, tm, SWA_O // 2, D, "wgrad_swa_o")
            gfull["swa_w_o"][j] = _unpad_heads(dwo, 0, SWA_HQ, 64, LANE)
            dq, dkvc, dkvp, dsink = _swa_attn_bwd(st["qkv"], inp["swa_sinks"][j], do, tb)
            g_sink[j] = dsink[0, :SWA_HQ]
            dx, dz, sm, db = _swa_proj_bwd(dq, dkvc, dkvp, st["x0"], modl[i], dres, w_qkv[j], tabs_b_neg, tm)
            dwqkv = _wgrad(st["x0"], dz, tm, D, SWA_QKV // 2, "wgrad_swa_qkv", modl[i], (1, 0))
            gfull["swa_w_qkv"][j] = _unpad_heads(dwqkv, 1, SWA_HQ + 2 * SWA_HKV, 64, LANE)
            g_bqkv[j] = _unpad_heads(db[0], 0, SWA_HQ + 2 * SWA_HKV, 64, LANE)
        dmod[i] = jnp.stack([sm[1], sm[0], dg_m, dsh_f, dsc_f, dg_f])
    grad_x = dx[None]

    small_shapes = [(DEPTH, 6 * D), (DEPTH, D), (DEPTH, D), (DEPTH, D), (DEPTH, D), (2, MLA_QR), (2, MLA_KVR),
                    (2, SWA_HQ), (2, 1536), (2, D)]
    small_vals = [jnp.stack(dmod).reshape(DEPTH, 6 * D), jnp.stack(g_ln["ln_mix_g"]), jnp.stack(g_ln["ln_mix_b"]),
                  jnp.stack(g_ln["ln_ffn_g"]), jnp.stack(g_ln["ln_ffn_b"]), jnp.stack(g_qn), jnp.stack(g_kvn),
                  jnp.stack(g_sink), jnp.stack(g_bqkv), jnp.stack(g_bo)]
    nsmall = sum(math.prod(s) for s in small_shapes)
    small_rows = -(-nsmall // (8 * LANE)) * 8
    (dmod_all, p_lmg, p_lmb, p_lfg, p_lfb, p_qn, p_kvn, p_sink, p_bqkv, p_bo) = _small_unpack(
        _exchange(_small_pack(small_vals, small_rows), True, "gather_small_grads"), small_shapes)

    early = {(n, lo): jnp.stack(gfull[n][lo:hi]) for n, lo, hi in EARLY}
    gparts.update(_unpack_blocks(_exchange(_pack_full(early, EARLY), False, "scatter_early_grads"), EARLY))
    for n, _, _ in EARLY:
        gparts[(n, 0)] = jnp.concatenate([gparts[(n, 0)], gparts.pop((n, 1))], axis=1)

    res = {}

    def update(name, parts):
        w = inp[name]
        shp = w.shape
        r2 = (math.prod(shp[:-1]), shp[-1])
        outs = _adamw(parts.reshape((parts.shape[0],) + r2), w.reshape(r2), inp["m_" + name].reshape(r2),
                      inp["v_" + name].reshape(r2), "adamw_" + name)
        res[name] = tuple(o.reshape(shp) for o in outs)

    dmod_loc = lax.dynamic_slice_in_dim(dmod_all, me * ncol, ncol, axis=2)
    g_ada_w = _ada_w_grad(c_all.T, jnp.moveaxis(dmod_loc, 0, 1))
    update("ada_w", g_ada_w[None])
    update("ada_b", dmod_all)
    update("ln_mix_g", p_lmg)
    update("ln_mix_b", p_lmb)
    update("ln_ffn_g", p_lfg)
    update("ln_ffn_b", p_lfb)
    for name in BIG:
        update(name, gparts[(name, 0)])
    update("mla_q_norm", p_qn)
    update("mla_kv_norm", p_kvn)
    update("swa_sinks", p_sink)
    nb = 1536 // NDEV
    update("swa_b_qkv", lax.dynamic_slice_in_dim(p_bqkv, me * nb, nb, axis=2))
    update("swa_b_o", lax.dynamic_slice_in_dim(p_bo, me * (D // NDEV), D // NDEV, axis=2))
    return loss, grad_x, res


WEIGHTS = ["ada_w", "ada_b", "ln_mix_g", "ln_mix_b", "ln_ffn_g", "ln_ffn_b", "ffn_w_gate", "ffn_w_up",
           "ffn_w_down", "mla_w_in", "mla_q_norm", "mla_w_q_b", "mla_kv_norm", "mla_w_kv_b", "mla_w_o",
           "swa_w_qkv", "swa_b_qkv", "swa_sinks", "swa_w_o", "swa_b_o"]
INPUTS = (["x", "c", "positions"] + WEIGHTS + ["loss_target"] + ["m_" + n for n in WEIGHTS]
          + ["v_" + n for n in WEIGHTS])


def kernel(x, c, positions, ada_w, ada_b, ln_mix_g, ln_mix_b, ln_ffn_g, ln_ffn_b, ffn_w_gate, ffn_w_up, ffn_w_down, mla_w_in, mla_q_norm, mla_w_q_b, mla_kv_norm, mla_w_kv_b, mla_w_o, swa_w_qkv, swa_b_qkv, swa_sinks, swa_w_o, swa_b_o, loss_target, m_ada_w, m_ada_b, m_ln_mix_g, m_ln_mix_b, m_ln_ffn_g, m_ln_ffn_b, m_ffn_w_gate, m_ffn_w_up, m_ffn_w_down, m_mla_w_in, m_mla_q_norm, m_mla_w_q_b, m_mla_kv_norm, m_mla_w_kv_b, m_mla_w_o, m_swa_w_qkv, m_swa_b_qkv, m_swa_sinks, m_swa_w_o, m_swa_b_o, v_ada_w, v_ada_b, v_ln_mix_g, v_ln_mix_b, v_ln_ffn_g, v_ln_ffn_b, v_ffn_w_gate, v_ffn_w_up, v_ffn_w_down, v_mla_w_in, v_mla_q_norm, v_mla_w_q_b, v_mla_kv_norm, v_mla_w_kv_b, v_mla_w_o, v_swa_w_qkv, v_swa_b_qkv, v_swa_sinks, v_swa_w_o, v_swa_b_o):
    args = (x, c, positions, ada_w, ada_b, ln_mix_g, ln_mix_b, ln_ffn_g, ln_ffn_b, ffn_w_gate, ffn_w_up, ffn_w_down, mla_w_in, mla_q_norm, mla_w_q_b, mla_kv_norm, mla_w_kv_b, mla_w_o, swa_w_qkv, swa_b_qkv, swa_sinks, swa_w_o, swa_b_o, loss_target, m_ada_w, m_ada_b, m_ln_mix_g, m_ln_mix_b, m_ln_ffn_g, m_ln_ffn_b, m_ffn_w_gate, m_ffn_w_up, m_ffn_w_down, m_mla_w_in, m_mla_q_norm, m_mla_w_q_b, m_mla_kv_norm, m_mla_w_kv_b, m_mla_w_o, m_swa_w_qkv, m_swa_b_qkv, m_swa_sinks, m_swa_w_o, m_swa_b_o, v_ada_w, v_ada_b, v_ln_mix_g, v_ln_mix_b, v_ln_ffn_g, v_ln_ffn_b, v_ffn_w_gate, v_ffn_w_up, v_ffn_w_down, v_mla_w_in, v_mla_q_norm, v_mla_w_q_b, v_mla_kv_norm, v_mla_w_kv_b, v_mla_w_o, v_swa_w_qkv, v_swa_b_qkv, v_swa_sinks, v_swa_w_o, v_swa_b_o)
    assert len(args) == len(INPUTS)
    loss, grad_x, res = _step(dict(zip(INPUTS, args)))
    return (loss, grad_x, *[res[n][0] for n in WEIGHTS], *[res[n][1] for n in WEIGHTS],
            *[res[n][2] for n in WEIGHTS], *[res[n][3] for n in WEIGHTS])
```

```python
import functools
import math

import jax
import jax.numpy as jnp
from jax import lax
from jax.experimental import pallas as pl
from jax.experimental.pallas import tpu as pltpu

F32 = jnp.float32
BF = jnp.bfloat16

NDEV = 8
D = 1024
DEPTH = 4
F = 2816
ALPHA = (2 * DEPTH) ** 0.25
LN_EPS = 1e-5
RMS_EPS = 1e-6
ROPE_THETA = 500000.0

MLA_H = 8
MLA_QR = 384
MLA_KVR = 256
MLA_ROPE = 64
MLA_LAT = 768
MLA_HD = 256
MLA_SCALE = (128 + 64) ** -0.5

SWA_HQ = 16
SWA_HKV = 4
SWA_W = 128
SWA_SCALE = 64 ** -0.5
SWA_QKV = (SWA_HQ + 2 * SWA_HKV) * 128
SWA_O = SWA_HQ * 128

LANE = 128
VMEM_LIMIT = 56 * 2 ** 20

ADAM_LR, ADAM_B1, ADAM_B2, ADAM_EPS, ADAM_WD, ADAM_STEP = 0.001, 0.9, 0.999, 1e-8, 0.01, 10


def _params(n_axes):
    return pltpu.CompilerParams(dimension_semantics=("arbitrary",) * n_axes, vmem_limit_bytes=VMEM_LIMIT)


def _dot(a, b):
    return jnp.dot(a, b, preferred_element_type=F32)


def _dot_nt(a, b):
    return lax.dot_general(a, b, (((1,), (1,)), ((), ())), preferred_element_type=F32)


def _dot_tn(a, b):
    return lax.dot_general(a, b, (((0,), (0,)), ((), ())), preferred_element_type=F32)


def _full(shape):
    return pl.BlockSpec(shape, lambda *_: (0,) * len(shape))


def _sigmoid(x):
    return 1.0 / (1.0 + jnp.exp(-x))


def _rope128(x, ct, s1, s2, half):
    return x * ct + pltpu.roll(x, LANE - half, 1) * s1 + pltpu.roll(x, half, 1) * s2


def _eye(n):
    return lax.broadcasted_iota(jnp.int32, (n, n), 0) == lax.broadcasted_iota(jnp.int32, (n, n), 1)


def _col_to_row(col):
    n = col.shape[0]
    return jnp.sum(jnp.where(_eye(n), col, 0.0), axis=0, keepdims=True)


def _row_to_col(row):
    n = row.shape[1]
    return jnp.sum(jnp.where(_eye(n), row, 0.0), axis=1, keepdims=True)


def _modulate(x, modl_ref, sc_row, sh_row):
    return x * (1.0 + modl_ref[sc_row:sc_row + 1, :]) + modl_ref[sh_row:sh_row + 1, :]


EXCHANGE_SEMS = [pltpu.SemaphoreType.DMA((NDEV - 1,)), pltpu.SemaphoreType.DMA((NDEV - 1,)), pltpu.SemaphoreType.DMA]


def _exchange_copies(src_ref, out_ref, send_sems, recv_sems, local_sem, gather):
    x, y, c = lax.axis_index("x"), lax.axis_index("y"), lax.axis_index("c")
    me = 4 * x + 2 * y + c

    def piece(dev):
        return src_ref if gather else src_ref.at[dev]

    mine = pltpu.make_async_copy(piece(me), out_ref.at[me], local_sem)
    sends, recvs = [], []
    for k in range(1, NDEV):
        px = 1 - x if k & 4 else x
        py = 1 - y if k & 2 else y
        pc = 1 - c if k & 1 else c
        peer = 4 * px + 2 * py + pc
        common = dict(send_sem=send_sems.at[k - 1], recv_sem=recv_sems.at[k - 1],
                      device_id=(px, py, pc), device_id_type=pl.DeviceIdType.MESH)
        sends.append(pltpu.make_async_remote_copy(src_ref=piece(peer), dst_ref=out_ref.at[me], **common))
        recvs.append(pltpu.make_async_remote_copy(src_ref=piece(peer), dst_ref=out_ref.at[peer], **common))
    return mine, sends, recvs


def _exchange_start(*refs, gather):
    mine, sends, _ = _exchange_copies(*refs, gather)
    mine.start()
    for s in sends:
        s.start()


def _exchange_wait(*refs, gather):
    mine, sends, recvs = _exchange_copies(*refs, gather)
    for r in recvs:
        r.wait_recv()
    for s in sends:
        s.wait_send()
    mine.wait()


def _exchange_out(src, gather):
    blk = tuple(src.shape) if gather else tuple(src.shape[1:])
    return jax.ShapeDtypeStruct((NDEV,) + blk, src.dtype)


def _exchange(src, gather, name):
    def body(*refs):
        _exchange_start(*refs, gather=gather)
        _exchange_wait(*refs, gather=gather)

    return pl.pallas_call(
        body, name=name,
        out_shape=_exchange_out(src, gather),
        in_specs=[pl.BlockSpec(memory_space=pltpu.HBM)],
        out_specs=pl.BlockSpec(memory_space=pltpu.HBM),
        scratch_shapes=EXCHANGE_SEMS,
    )(src)


def _mod_all(c_all, ada_w, ada_b_loc):
    ncol = ada_w.shape[2]

    def body(c_ref, w_ref, b_ref, o_ref):
        cv = c_ref[...]
        cond = cv * _sigmoid(cv)
        o_ref[0] = _dot(cond.astype(BF), w_ref[0].astype(BF)) + b_ref[0]

    return pl.pallas_call(
        body, name="mod_all", grid=(DEPTH,),
        out_shape=jax.ShapeDtypeStruct((DEPTH, NDEV, ncol), F32),
        in_specs=[_full((NDEV, D)), pl.BlockSpec((1, D, ncol), lambda i: (i, 0, 0)),
                  pl.BlockSpec((1, 1, ncol), lambda i: (i, 0, 0))],
        out_specs=pl.BlockSpec((1, NDEV, ncol), lambda i: (i, 0, 0)),
        compiler_params=_params(1),
    )(c_all, ada_w, ada_b_loc)


def _ada_w_grad(c_all_t, dmod_loc):
    ncol = dmod_loc.shape[2]

    def body(ct_ref, dm_ref, o_ref):
        cv = ct_ref[...]
        cond = cv * _sigmoid(cv)
        acc = cond[:, 0:1] * dm_ref[0, 0:1, :]
        for b in range(1, NDEV):
            acc = acc + cond[:, b:b + 1] * dm_ref[0, b:b + 1, :]
        o_ref[0] = acc

    return pl.pallas_call(
        body, name="ada_w_grad", grid=(DEPTH,),
        out_shape=jax.ShapeDtypeStruct((DEPTH, D, ncol), F32),
        in_specs=[_full((D, NDEV)), pl.BlockSpec((1, NDEV, ncol), lambda i: (i, 0, 0))],
        out_specs=pl.BlockSpec((1, D, ncol), lambda i: (i, 0, 0)),
        compiler_params=_params(1),
    )(c_all_t, dmod_loc)


def _mla_proj_fwd(x, modl, w_in, q_norm, w_q, kv_norm, w_kv, tabs, tm):
    T = x.shape[0]
    ct_a, s1_a, s2_a = tabs

    def body(x_ref, modl_ref, win_ref, qn_ref, wq_ref, kvn_ref, wkv_ref, ct_ref, s1_ref, s2_ref,
             q_ref, k_ref, v_ref):
        h = _modulate(x_ref[...], modl_ref, 1, 0).astype(BF)
        lat = _dot(h, win_ref[...])
        ql, kvl, kr = lat[:, :MLA_QR], lat[:, MLA_QR:MLA_QR + MLA_KVR], lat[:, MLA_QR + MLA_KVR:]
        qn = (ql * lax.rsqrt(jnp.mean(ql * ql, axis=1, keepdims=True) + RMS_EPS) * qn_ref[...]).astype(BF)
        kvn = (kvl * lax.rsqrt(jnp.mean(kvl * kvl, axis=1, keepdims=True) + RMS_EPS) * kvn_ref[...]).astype(BF)
        ct, s1, s2 = ct_ref[...], s1_ref[...], s2_ref[...]
        kr = _rope128(kr, ct, s1, s2, MLA_ROPE // 2).astype(BF)
        for hd in range(MLA_H):
            cols = slice(hd * MLA_HD, (hd + 1) * MLA_HD)
            qh = _dot(qn, wq_ref[:, cols])
            q_ref[hd, :, 0:LANE] = qh[:, :LANE].astype(BF)
            q_ref[hd, :, LANE:MLA_HD] = _rope128(qh[:, LANE:], ct, s1, s2, MLA_ROPE // 2).astype(BF)
            kvh = _dot(kvn, wkv_ref[:, cols])
            k_ref[hd, :, 0:LANE] = kvh[:, :LANE].astype(BF)
            k_ref[hd, :, LANE:MLA_HD] = kr
            v_ref[hd, :, 0:LANE] = kvh[:, LANE:].astype(BF)
            v_ref[hd, :, LANE:2 * LANE] = jnp.ones((tm, LANE), BF)

    row = lambda i: (i, 0)
    head = lambda i: (0, i, 0)
    return pl.pallas_call(
        body, name="mla_proj_fwd", grid=(T // tm,),
        out_shape=(jax.ShapeDtypeStruct((MLA_H, T, MLA_HD), BF), jax.ShapeDtypeStruct((MLA_H, T, MLA_HD), BF),
                   jax.ShapeDtypeStruct((MLA_H, T, 2 * LANE), BF)),
        in_specs=[pl.BlockSpec((tm, D), row), _full((8, D)), _full((D, MLA_LAT)), _full((1, MLA_QR)),
                  _full((MLA_QR, MLA_H * MLA_HD)), _full((1, MLA_KVR)), _full((MLA_KVR, MLA_H * MLA_HD)),
                  pl.BlockSpec((tm, LANE), row), pl.BlockSpec((tm, LANE), row), pl.BlockSpec((tm, LANE), row)],
        out_specs=(pl.BlockSpec((MLA_H, tm, MLA_HD), head), pl.BlockSpec((MLA_H, tm, MLA_HD), head),
                   pl.BlockSpec((MLA_H, tm, 2 * LANE), head)),
        compiler_params=_params(1),
    )(x, modl, w_in, q_norm, w_q, kv_norm, w_kv, ct_a, s1_a, s2_a)


def _causal_mask(row0, nrows, ncols, transposed):
    row = lax.broadcasted_iota(jnp.int32, (nrows, ncols), 0) + row0
    col = lax.broadcasted_iota(jnp.int32, (nrows, ncols), 1)
    return (row <= col) if transposed else (col <= row)


def _flash_fwd(q, k, v, tb, comm_src=None):
    H, T, _ = q.shape
    rh = min(256, tb)
    c_exp = MLA_SCALE * math.log2(math.e)

    def body(*refs):
        if comm_src is None:
            q_ref, k_ref, v_ref, o_ref, lse_ref, m_s, acc_s = refs
        else:
            q_ref, k_ref, v_ref, src_ref, o_ref, lse_ref, got_ref, m_s, acc_s, *sems = refs
            comm = (src_ref, got_ref, *sems)
            first = (pl.program_id(0) == 0) & (pl.program_id(1) == 0)
            last = (pl.program_id(0) == H - 1) & (pl.program_id(1) == T // tb - 1)
            pl.when(first)(functools.partial(_exchange_start, *comm, gather=True))
        i = pl.program_id(1)
        m_s[...] = jnp.full(m_s.shape, -jnp.inf, F32)
        acc_s[...] = jnp.zeros(acc_s.shape, F32)

        def block(j, masked):
            base = pl.multiple_of(j * tb, tb)
            for hf in range(tb // rh):
                r = slice(hf * rh, (hf + 1) * rh)
                nk = (hf + 1) * rh if masked else tb
                kb = k_ref[0, pl.ds(base, nk), :]
                vb = v_ref[0, pl.ds(base, nk), :]
                s = _dot_nt(q_ref[0, r, :], kb)
                if masked:
                    s = jnp.where(_causal_mask(hf * rh, rh, nk, False), s, -jnp.inf)
                sc = [s[:, c * LANE:(c + 1) * LANE] for c in range(nk // LANE)]
                mx = sc[0]
                for x in sc[1:]:
                    mx = jnp.maximum(mx, x)
                m_prev = m_s[r, :]
                m_new = jnp.maximum(m_prev, jnp.max(mx, axis=1, keepdims=True))
                p = jnp.concatenate([jnp.exp2((x - m_new) * c_exp) for x in sc], axis=1)
                corr = jnp.exp2((m_prev - m_new) * c_exp)
                acc_s[r, :] = jnp.concatenate([corr, corr], axis=1) * acc_s[r, :] + _dot(p.astype(BF), vb)
                m_s[r, :] = m_new

        def step(j, carry):
            block(j, False)
            return carry

        lax.fori_loop(0, i, step, 0)
        block(i, True)
        l = acc_s[:, LANE:]
        o_ref[...] = (acc_s[:, :LANE] / l).astype(BF)
        lse = (m_s[...] * MLA_SCALE + jnp.log(l))[:, 0:1]
        for c0 in range(0, tb, rh):
            lse_ref[0, :, c0:c0 + rh] = _col_to_row(lse[c0:c0 + rh])
        if comm_src is not None:
            pl.when(last)(functools.partial(_exchange_wait, *comm, gather=True))

    hbm = pl.BlockSpec(memory_space=pltpu.HBM)
    with_comm = comm_src is not None
    return pl.pallas_call(
        body, name="flash_fwd_gather" if with_comm else "flash_fwd", grid=(H, T // tb),
        out_shape=(jax.ShapeDtypeStruct((T, H * LANE), BF), jax.ShapeDtypeStruct((H, 1, T), F32))
        + ((_exchange_out(comm_src, True),) if with_comm else ()),
        in_specs=[pl.BlockSpec((1, tb, MLA_HD), lambda h, i: (h, i, 0)),
                  pl.BlockSpec((1, T, MLA_HD), lambda h, i: (h, 0, 0)),
                  pl.BlockSpec((1, T, 2 * LANE), lambda h, i: (h, 0, 0))] + ([hbm] if with_comm else []),
        out_specs=(pl.BlockSpec((tb, LANE), lambda h, i: (i, h)),
                   pl.BlockSpec((1, 1, tb), lambda h, i: (h, 0, i))) + ((hbm,) if with_comm else ()),
        scratch_shapes=[pltpu.VMEM((tb, LANE), F32), pltpu.VMEM((tb, 2 * LANE), F32)]
        + (EXCHANGE_SEMS if with_comm else []),
        compiler_params=_params(2),
    )(*((q, k, v, comm_src) if with_comm else (q, k, v)))


def _outproj_ln_fwd(a, w, bias, x, modl, g_row, ln_g, ln_b, tm, name):
    T, K = a.shape

    def body(a_ref, w_ref, b_ref, x_ref, modl_ref, g_ref, bb_ref, y_ref, u_ref, xn_ref):
        y = _dot(a_ref[...], w_ref[...]) + b_ref[...]
        u = ALPHA * x_ref[...] + modl_ref[g_row:g_row + 1, :] * y
        mu = jnp.mean(u, axis=1, keepdims=True)
        uc = u - mu
        var = jnp.mean(uc * uc, axis=1, keepdims=True)
        y_ref[...] = y.astype(BF)
        u_ref[...] = u
        xn_ref[...] = uc * lax.rsqrt(var + LN_EPS) * g_ref[...] + bb_ref[...]

    row = lambda i: (i, 0)
    return pl.pallas_call(
        body, name=name, grid=(T // tm,),
        out_shape=(jax.ShapeDtypeStruct((T, D), BF), jax.ShapeDtypeStruct((T, D), F32),
                   jax.ShapeDtypeStruct((T, D), F32)),
        in_specs=[pl.BlockSpec((tm, K), row), _full((K, D)), _full((1, D)), pl.BlockSpec((tm, D), row),
                  _full((8, D)), _full((1, D)), _full((1, D))],
        out_specs=(pl.BlockSpec((tm, D), row),) * 3,
        compiler_params=_params(1),
    )(a, w, bias, x, modl, ln_g, ln_b)


def _ffn_up_fwd(x, modl, wg, wu, tm, tn):
    T = x.shape[0]

    def body(x_ref, modl_ref, wg_ref, wu_ref, g_ref, u_ref, a_ref):
        h = _modulate(x_ref[...], modl_ref, 4, 3).astype(BF)
        g = _dot(h, wg_ref[...])
        u = _dot(h, wu_ref[...])
        g_ref[...] = g.astype(BF)
        u_ref[...] = u.astype(BF)
        a_ref[...] = (g * _sigmoid(g) * u).astype(BF)

    tile = pl.BlockSpec((tm, tn), lambda n, i: (i, n))
    wcol = pl.BlockSpec((D, tn), lambda n, i: (0, n))
    return pl.pallas_call(
        body, name="ffn_up_fwd", grid=(F // tn, T // tm),
        out_shape=(jax.ShapeDtypeStruct((T, F), BF),) * 3,
        in_specs=[pl.BlockSpec((tm, D), lambda n, i: (i, 0)), _full((8, D)), wcol, wcol],
        out_specs=(tile, tile, tile),
        compiler_params=_params(2),
    )(x, modl, wg, wu)


def _swa_proj_fwd(x, modl, w, b, tabs, tm):
    T = x.shape[0]
    ct_b, s1_b, s2_b = tabs
    n_rope = SWA_HQ + SWA_HKV

    def body(x_ref, modl_ref, w_ref, b_ref, ct_ref, s1_ref, s2_ref, o_ref):
        h = _modulate(x_ref[...], modl_ref, 1, 0).astype(BF)
        ct, s1, s2 = ct_ref[...], s1_ref[...], s2_ref[...]
        for grp in range(SWA_QKV // LANE):
            cols = slice(grp * LANE, (grp + 1) * LANE)
            z = _dot(h, w_ref[:, cols]) + b_ref[:, cols]
            if grp < n_rope:
                z = _rope128(z, ct, s1, s2, 8)
            o_ref[:, cols] = z.astype(BF)

    row = lambda i: (i, 0)
    return pl.pallas_call(
        body, name="swa_proj_fwd", grid=(T // tm,),
        out_shape=jax.ShapeDtypeStruct((T, SWA_QKV), BF),
        in_specs=[pl.BlockSpec((tm, D), row), _full((8, D)), _full((D, SWA_QKV)), _full((1, SWA_QKV)),
                  pl.BlockSpec((tm, LANE), row), pl.BlockSpec((tm, LANE), row), pl.BlockSpec((tm, LANE), row)],
        out_specs=pl.BlockSpec((tm, SWA_QKV), row),
        compiler_params=_params(1),
    )(x, modl, w, b, ct_b, s1_b, s2_b)


SWA_NB = 1
SWA_R = SWA_NB * 4 * SWA_W
SWA_NK = (SWA_NB + 1) * SWA_W


def _swa_bias(first):
    row = lax.broadcasted_iota(jnp.int32, (SWA_R, SWA_NK), 0)
    col = lax.broadcasted_iota(jnp.int32, (SWA_R, SWA_NK), 1)
    bl = row // (4 * SWA_W)
    r = row % SWA_W
    cp = col - bl * SWA_W
    band = (cp > r) & (cp <= r + SWA_W)
    if first:
        band = band & ((col >= SWA_W) | (bl > 0))
    return jnp.where(band, 0.0, -jnp.inf).astype(F32)


def _swa_fill_bias(bias_s):
    @pl.when(pl.program_id(0) == 0)
    def _():
        bias_s[0] = _swa_bias(False)
        bias_s[1] = _swa_bias(True)


def _swa_specs(T, tb):
    nsub = tb // SWA_W
    q_spec = pl.BlockSpec((tb, SWA_O), lambda i: (i, 0))
    kvc_spec = pl.BlockSpec((tb, 2 * SWA_HKV * LANE), lambda i: (i, 2))
    kvp_spec = pl.BlockSpec((SWA_W, 2 * SWA_HKV * LANE), lambda i: (jnp.maximum(i * nsub - 1, 0), 2))
    return q_spec, kvc_spec, kvp_spec


def _swa_rows(ref, g, c):
    return jnp.concatenate(
        [ref[(c * SWA_NB + bl) * SWA_W:(c * SWA_NB + bl + 1) * SWA_W, (4 * g + hh) * LANE:(4 * g + hh + 1) * LANE]
         for bl in range(SWA_NB) for hh in range(4)], axis=0)


def _swa_chain(q_ref, kall, vall, sink_ref, bias_s, g, c):
    i = pl.program_id(0)
    lane = lax.broadcasted_iota(jnp.int32, (1, LANE), 1)
    qc = _swa_rows(q_ref, g, c)
    keys = slice(c * SWA_NB * SWA_W, c * SWA_NB * SWA_W + SWA_NK)
    k3 = kall[keys]
    v3 = jnp.where(lane < 64, vall[keys], jnp.ones((), BF))
    bias = bias_s[jnp.where(i == 0, 1, 0)] if c == 0 else bias_s[0]
    s = _dot_nt(qc, k3)
    c_exp = SWA_SCALE * math.log2(math.e)
    sb = [s[:, ch * LANE:(ch + 1) * LANE] * c_exp + bias[:, ch * LANE:(ch + 1) * LANE] for ch in range(SWA_NK // LANE)]
    mx = sb[0]
    for x in sb[1:]:
        mx = jnp.maximum(mx, x)
    sink2 = jnp.concatenate([jnp.full((SWA_W, 1), sink_ref[4 * g + hh] * math.log2(math.e), F32)
                             for _ in range(SWA_NB) for hh in range(4)], axis=0)
    m = jnp.maximum(jnp.max(mx, axis=1, keepdims=True), sink2)
    m_rep = jnp.broadcast_to(m, (SWA_R, LANE))
    eb = jnp.concatenate([jnp.exp2(x - m_rep) for x in sb], axis=1).astype(BF)
    es = jnp.exp2(sink2 - m)
    acc = _dot(eb, v3)
    linv = 1.0 / (acc[:, 64:65] + es)
    num = jnp.where(lane < 64, acc, 0.0)
    return qc, k3, v3, eb, num, linv, es


def _swa_attn_fwd(qkv, sinks, tb):
    T = qkv.shape[0]
    kw = SWA_HKV * LANE

    def body(q_ref, kvc_ref, kvp_ref, sink_ref, o_ref, bias_s):
        _swa_fill_bias(bias_s)
        for g in range(SWA_HKV):
            gl = slice(g * LANE, (g + 1) * LANE)
            gv = slice(kw + g * LANE, kw + (g + 1) * LANE)
            kall = jnp.concatenate([kvp_ref[:, gl], kvc_ref[:, gl]], axis=0)
            vall = jnp.concatenate([kvp_ref[:, gv], kvc_ref[:, gv]], axis=0)
            for c in range(tb // (SWA_NB * SWA_W)):
                _, _, _, _, num, linv, _ = _swa_chain(q_ref, kall, vall, sink_ref, bias_s, g, c)
                o = (num * linv).astype(BF)
                for bl in range(SWA_NB):
                    for hh in range(4):
                        piece = (bl * 4 + hh) * SWA_W
                        rows = slice((c * SWA_NB + bl) * SWA_W, (c * SWA_NB + bl + 1) * SWA_W)
                        o_ref[rows, (4 * g + hh) * LANE:(4 * g + hh + 1) * LANE] = o[piece:piece + SWA_W]

    q_spec, kvc_spec, kvp_spec = _swa_specs(T, tb)
    return pl.pallas_call(
        body, name="swa_attn_fwd", grid=(T // tb,),
        out_shape=jax.ShapeDtypeStruct((T, SWA_O), BF),
        in_specs=[q_spec, kvc_spec, kvp_spec, pl.BlockSpec(memory_space=pltpu.SMEM)],
        out_specs=pl.BlockSpec((tb, SWA_O), lambda i: (i, 0)),
        scratch_shapes=[pltpu.VMEM((2, SWA_R, SWA_NK), F32)],
        compiler_params=_params(1),
    )(qkv, qkv, qkv, sinks)


def _loss_grad(x, tgt, tm):
    T = x.shape[0]

    def body(x_ref, t_ref, dx_ref, l_ref):
        @pl.when(pl.program_id(0) == 0)
        def _():
            l_ref[...] = jnp.zeros(l_ref.shape, F32)
        diff = x_ref[...] - t_ref[...]
        dx_ref[...] = diff * (1.0 / D)
        l_ref[0:1, :] += jnp.sum(diff * diff, axis=0, keepdims=True) * (0.5 / D)

    row = lambda i: (i, 0)
    return pl.pallas_call(
        body, name="loss_grad", grid=(T // tm,),
        out_shape=(jax.ShapeDtypeStruct((T, D), F32), jax.ShapeDtypeStruct((8, D), F32)),
        in_specs=[pl.BlockSpec((tm, D), row), pl.BlockSpec((tm, D), row)],
        out_specs=(pl.BlockSpec((tm, D), row), _full((8, D))),
        compiler_params=_params(1),
    )(x, tgt)


def _outproj_ln_bwd(dxn, u, y, w, modl, g_row, ln_g, tm, name):
    T = dxn.shape[0]
    K = w.shape[0]

    def body(dxn_ref, u_ref, y_ref, w_ref, modl_ref, g_ref, dres_ref, dy_ref, da_ref, sm_ref):
        @pl.when(pl.program_id(0) == 0)
        def _():
            sm_ref[...] = jnp.zeros(sm_ref.shape, F32)
        uu = u_ref[...]
        mu = jnp.mean(uu, axis=1, keepdims=True)
        uc = uu - mu
        rstd = lax.rsqrt(jnp.mean(uc * uc, axis=1, keepdims=True) + LN_EPS)
        xhat = uc * rstd
        dxo = dxn_ref[...]
        dyh = dxo * g_ref[...]
        du = rstd * (dyh - jnp.mean(dyh, axis=1, keepdims=True)
                     - xhat * jnp.mean(dyh * xhat, axis=1, keepdims=True))
        dy = modl_ref[g_row:g_row + 1, :] * du
        dyb = dy.astype(BF)
        dres_ref[...] = ALPHA * du
        dy_ref[...] = dyb
        da_ref[...] = _dot_nt(dyb, w_ref[...]).astype(BF)
        sm_ref[0:1, :] += jnp.sum(dxo * xhat, axis=0, keepdims=True)
        sm_ref[1:2, :] += jnp.sum(dxo, axis=0, keepdims=True)
        sm_ref[2:3, :] += jnp.sum(du * y_ref[...].astype(F32), axis=0, keepdims=True)
        sm_ref[3:4, :] += jnp.sum(dy, axis=0, keepdims=True)

    row = lambda i: (i, 0)
    return pl.pallas_call(
        body, name=name, grid=(T // tm,),
        out_shape=(jax.ShapeDtypeStruct((T, D), F32), jax.ShapeDtypeStruct((T, D), BF),
                   jax.ShapeDtypeStruct((T, K), BF), jax.ShapeDtypeStruct((8, D), F32)),
        in_specs=[pl.BlockSpec((tm, D), row), pl.BlockSpec((tm, D), row), pl.BlockSpec((tm, D), row),
                  _full((K, D)), _full((8, D)), _full((1, D))],
        out_specs=(pl.BlockSpec((tm, D), row), pl.BlockSpec((tm, D), row), pl.BlockSpec((tm, K), row),
                   _full((8, D))),
        compiler_params=_params(1),
    )(dxn, u, y, w, modl, ln_g)


def _ffn_mid_bwd(da, g, u, x, modl, dres, wg, wu, tm, tn):
    T = x.shape[0]
    nn = F // tn

    def body(da_ref, g_ref, u_ref, x_ref, modl_ref, dres_ref, wg_ref, wu_ref, dg_ref, du_ref, dx_ref, sm_ref):
        i, n = pl.program_id(0), pl.program_id(1)

        @pl.when((i == 0) & (n == 0))
        def _():
            sm_ref[...] = jnp.zeros(sm_ref.shape, F32)

        gg = g_ref[...].astype(F32)
        sg = _sigmoid(gg)
        dav = da_ref[...].astype(F32)
        dgp = (dav * u_ref[...].astype(F32) * sg * (1.0 + gg * (1.0 - sg))).astype(BF)
        dup = (dav * gg * sg).astype(BF)
        dg_ref[...] = dgp
        du_ref[...] = dup
        dh = _dot_nt(dgp, wg_ref[...]) + _dot_nt(dup, wu_ref[...])

        @pl.when(n == 0)
        def _():
            dx_ref[...] = dh

        @pl.when(n > 0)
        def _():
            dx_ref[...] += dh

        @pl.when(n == nn - 1)
        def _():
            dht = dx_ref[...]
            sm_ref[0:1, :] += jnp.sum(dht * x_ref[...], axis=0, keepdims=True)
            sm_ref[1:2, :] += jnp.sum(dht, axis=0, keepdims=True)
            dx_ref[...] = dres_ref[...] + dht * (1.0 + modl_ref[4:5, :])

    tile = pl.BlockSpec((tm, tn), lambda i, n: (i, n))
    rowd = pl.BlockSpec((tm, D), lambda i, n: (i, 0))
    rowd_once = pl.BlockSpec((tm, D), lambda i, n: (i, 0), pipeline_mode=pl.Buffered(1))
    wcol = pl.BlockSpec((D, tn), lambda i, n: (0, n))
    return pl.pallas_call(
        body, name="ffn_mid_bwd", grid=(T // tm, nn),
        out_shape=(jax.ShapeDtypeStruct((T, F), BF), jax.ShapeDtypeStruct((T, F), BF),
                   jax.ShapeDtypeStruct((T, D), F32), jax.ShapeDtypeStruct((8, D), F32)),
        in_specs=[tile, tile, tile, rowd_once, _full((8, D)), rowd_once, wcol, wcol],
        out_specs=(tile, tile, rowd, _full((8, D))),
        compiler_params=_params(2),
    )(da, g, u, x, modl, dres, wg, wu)


def _wgrad(a, b, tm, tk, tn, name, modl=None, rows=None):
    T, K = a.shape
    N = b.shape[1]

    def body(*refs):
        if modl is None:
            a_ref, b_ref, o_ref = refs
            av = a_ref[...]
        else:
            a_ref, modl_ref, b_ref, o_ref = refs
            av = _modulate(a_ref[...], modl_ref, rows[0], rows[1]).astype(BF)

        @pl.when(pl.program_id(2) == 0)
        def _():
            o_ref[...] = jnp.zeros(o_ref.shape, F32)
        o_ref[...] += _dot_tn(av, b_ref[...])

    in_specs = [pl.BlockSpec((tm, tk), lambda k, n, t: (t, k))]
    args = [a]
    if modl is not None:
        in_specs.append(_full((8, D)))
        args.append(modl)
    in_specs.append(pl.BlockSpec((tm, tn), lambda k, n, t: (t, n)))
    args.append(b)
    return pl.pallas_call(
        body, name=name, grid=(K // tk, N // tn, T // tm),
        out_shape=jax.ShapeDtypeStruct((K, N), F32),
        in_specs=in_specs,
        out_specs=pl.BlockSpec((tk, tn), lambda k, n, t: (k, n)),
        compiler_params=_params(3),
    )(*args)


def _flash_delta(o, do, tb):
    T = o.shape[0]
    H = o.shape[1] // LANE

    def body(o_ref, do_ref, delta_ref):
        delta = jnp.sum(o_ref[...].astype(F32) * do_ref[...].astype(F32), axis=1, keepdims=True)
        delta_ref[0] = _col_to_row(delta)

    blk = pl.BlockSpec((tb, LANE), lambda h, i: (i, h))
    return pl.pallas_call(
        body, name="flash_delta", grid=(H, T // tb),
        out_shape=jax.ShapeDtypeStruct((H, 1, T), F32),
        in_specs=[blk, blk],
        out_specs=pl.BlockSpec((1, 1, tb), lambda h, i: (h, 0, i)),
        compiler_params=_params(2),
    )(o, do)


def _flash_bwd(q, k, v, do, lse, delta, tb, comm_src=None):
    H, T, _ = q.shape
    nq = T // tb
    rh = min(256, tb)
    c_exp = MLA_SCALE * math.log2(math.e)

    def body(*refs):
        if comm_src is None:
            k_ref, v_ref, q_ref, do_ref, lse_ref, delta_ref, dq_ref, dk_ref, dv_ref, dk_s, dv_s = refs
        else:
            (k_ref, v_ref, q_ref, do_ref, lse_ref, delta_ref, src_ref, dq_ref, dk_ref, dv_ref, got_ref,
             dk_s, dv_s, *sems) = refs
            comm = (src_ref, got_ref, *sems)
            first = (pl.program_id(0) == 0) & (pl.program_id(1) == 0)
            last = (pl.program_id(0) == H - 1) & (pl.program_id(1) == nq - 1)
            pl.when(first)(functools.partial(_exchange_start, *comm, gather=False))
        j = pl.program_id(1)

        @pl.when(j == 0)
        def _():
            def zero(cix, carry):
                dq_ref[0, pl.ds(pl.multiple_of(cix * tb, tb), tb), :] = jnp.zeros((tb, MLA_HD), F32)
                return carry
            lax.fori_loop(0, nq, zero, 0)

        dk_s[...] = jnp.zeros(dk_s.shape, F32)
        dv_s[...] = jnp.zeros(dv_s.shape, F32)

        def block(i, masked):
            dq_part = None
            for hf in range(tb // rh):
                r = slice(hf * rh, (hf + 1) * rh)
                off = hf * rh if masked else 0
                qrows = pl.ds(pl.multiple_of(i * tb + off, rh), tb - off)
                qb = q_ref[0, qrows, :]
                dob = do_ref[qrows, :]
                l2 = lse_ref[0, :, qrows] * math.log2(math.e)
                dl = delta_ref[0, :, qrows]
                kc = k_ref[0, r, :]
                st = _dot_nt(kc, qb)
                if masked:
                    st = jnp.where(_causal_mask(0, rh, tb - off, True), st, -jnp.inf)
                pt = jnp.exp2(st * c_exp - l2)
                dpt = _dot_nt(v_ref[0, r, :], dob)
                dst = (pt * (dpt - dl)).astype(BF)
                dv_s[r, :] += _dot(pt.astype(BF), dob)
                dk_s[r, :] += _dot(dst, qb)
                part = _dot_tn(dst, kc)
                if masked:
                    dq_ref[0, qrows, :] += part * MLA_SCALE
                else:
                    dq_part = part if dq_part is None else dq_part + part
            if not masked:
                dq_ref[0, pl.ds(pl.multiple_of(i * tb, tb), tb), :] += dq_part * MLA_SCALE

        def step(i, carry):
            block(i, False)
            return carry

        block(j, True)
        lax.fori_loop(j + 1, nq, step, 0)
        dk_ref[0] = (dk_s[...] * MLA_SCALE).astype(BF)
        dv_ref[0] = dv_s[...].astype(BF)
        if comm_src is not None:
            pl.when(last)(functools.partial(_exchange_wait, *comm, gather=False))

    once = pl.Buffered(1)
    hbm = pl.BlockSpec(memory_space=pltpu.HBM)
    with_comm = comm_src is not None
    return pl.pallas_call(
        body, name="flash_bwd_scatter" if with_comm else "flash_bwd", grid=(H, nq),
        out_shape=(jax.ShapeDtypeStruct((H, T, MLA_HD), F32), jax.ShapeDtypeStruct((H, T, MLA_HD), BF),
                   jax.ShapeDtypeStruct((H, T, LANE), BF)) + ((_exchange_out(comm_src, False),) if with_comm else ()),
        in_specs=[pl.BlockSpec((1, tb, MLA_HD), lambda h, j: (h, j, 0)),
                  pl.BlockSpec((1, tb, LANE), lambda h, j: (h, j, 0)),
                  pl.BlockSpec((1, T, MLA_HD), lambda h, j: (h, 0, 0), pipeline_mode=once),
                  pl.BlockSpec((T, LANE), lambda h, j: (0, h), pipeline_mode=once),
                  pl.BlockSpec((1, 1, T), lambda h, j: (h, 0, 0)),
                  pl.BlockSpec((1, 1, T), lambda h, j: (h, 0, 0))] + ([hbm] if with_comm else []),
        out_specs=(pl.BlockSpec((1, T, MLA_HD), lambda h, j: (h, 0, 0), pipeline_mode=once),
                   pl.BlockSpec((1, tb, MLA_HD), lambda h, j: (h, j, 0)),
                   pl.BlockSpec((1, tb, LANE), lambda h, j: (h, j, 0))) + ((hbm,) if with_comm else ()),
        scratch_shapes=[pltpu.VMEM((tb, MLA_HD), F32), pltpu.VMEM((tb, LANE), F32)]
        + (EXCHANGE_SEMS if with_comm else []),
        compiler_params=_params(2),
    )(*((k, v, q, do, lse, delta, comm_src) if with_comm else (k, v, q, do, lse, delta)))


def _mla_proj_bwd(dq, dk, dv, x, modl, dres, w_in, q_norm, w_q, kv_norm, w_kv, tabs_neg, tm):
    T = x.shape[0]
    ct_a, s1_n, s2_n = tabs_neg

    def body(dq_ref, dk_ref, dv_ref, x_ref, modl_ref, dres_ref, win_ref, qn_ref, wq_ref, kvn_ref, wkv_ref,
             ct_ref, s1_ref, s2_ref, dx_ref, dwin_ref, dwq_ref, dwkv_ref, sm_ref, dqn_ref, dkvn_ref):
        @pl.when(pl.program_id(0) == 0)
        def _():
            for r in (dwin_ref, dwq_ref, dwkv_ref, sm_ref, dqn_ref, dkvn_ref):
                r[...] = jnp.zeros(r.shape, F32)

        xv = x_ref[...]
        h = _modulate(xv, modl_ref, 1, 0).astype(BF)
        lat = _dot(h, win_ref[...])
        ql, kvl = lat[:, :MLA_QR], lat[:, MLA_QR:MLA_QR + MLA_KVR]
        qhat = ql * lax.rsqrt(jnp.mean(ql * ql, axis=1, keepdims=True) + RMS_EPS)
        kvhat = kvl * lax.rsqrt(jnp.mean(kvl * kvl, axis=1, keepdims=True) + RMS_EPS)
        rq = lax.rsqrt(jnp.mean(ql * ql, axis=1, keepdims=True) + RMS_EPS)
        rkv = lax.rsqrt(jnp.mean(kvl * kvl, axis=1, keepdims=True) + RMS_EPS)
        qn = (qhat * qn_ref[...]).astype(BF)
        kvn = (kvhat * kvn_ref[...]).astype(BF)
        ct, s1, s2 = ct_ref[...], s1_ref[...], s2_ref[...]

        dqn = jnp.zeros((tm, MLA_QR), F32)
        dkvn = jnp.zeros((tm, MLA_KVR), F32)
        dkr = jnp.zeros((tm, LANE), F32)
        for hd in range(MLA_H):
            cols = slice(hd * MLA_HD, (hd + 1) * MLA_HD)
            dqh = dq_ref[hd]
            dqr = _rope128(dqh[:, LANE:], ct, s1, s2, MLA_ROPE // 2).astype(BF)
            dqh = jnp.concatenate([dqh[:, :LANE].astype(BF), dqr], axis=1)
            dqn = dqn + _dot_nt(dqh, wq_ref[:, cols])
            dwq_ref[:, cols] += _dot_tn(qn, dqh)
            dkh = dk_ref[hd]
            dkr = dkr + dkh[:, LANE:].astype(F32)
            dkvh = jnp.concatenate([dkh[:, :LANE], dv_ref[hd]], axis=1)
            dkvn = dkvn + _dot_nt(dkvh, wkv_ref[:, cols])
            dwkv_ref[:, cols] += _dot_tn(kvn, dkvh)
        dkr = _rope128(dkr, ct, s1, s2, MLA_ROPE // 2)

        dqn_ref[...] += jnp.sum(dqn * qhat, axis=0, keepdims=True)
        dkvn_ref[...] += jnp.sum(dkvn * kvhat, axis=0, keepdims=True)
        dqh_ = dqn * qn_ref[...]
        dkvh_ = dkvn * kvn_ref[...]
        dql = rq * (dqh_ - qhat * jnp.mean(dqh_ * qhat, axis=1, keepdims=True))
        dkvl = rkv * (dkvh_ - kvhat * jnp.mean(dkvh_ * kvhat, axis=1, keepdims=True))
        dlat = jnp.concatenate([dql, dkvl, dkr], axis=1).astype(BF)
        dwin_ref[...] += _dot_tn(h, dlat)
        dh = _dot_nt(dlat, win_ref[...])
        sm_ref[0:1, :] += jnp.sum(dh * xv, axis=0, keepdims=True)
        sm_ref[1:2, :] += jnp.sum(dh, axis=0, keepdims=True)
        dx_ref[...] = dres_ref[...] + dh * (1.0 + modl_ref[1:2, :])

    row = lambda i: (i, 0)
    head = lambda i: (0, i, 0)
    nq = MLA_H * MLA_HD
    return pl.pallas_call(
        body, name="mla_proj_bwd", grid=(T // tm,),
        out_shape=(jax.ShapeDtypeStruct((T, D), F32), jax.ShapeDtypeStruct((D, MLA_LAT), F32),
                   jax.ShapeDtypeStruct((MLA_QR, nq), F32), jax.ShapeDtypeStruct((MLA_KVR, nq), F32),
                   jax.ShapeDtypeStruct((8, D), F32), jax.ShapeDtypeStruct((1, MLA_QR), F32),
                   jax.ShapeDtypeStruct((1, MLA_KVR), F32)),
        in_specs=[pl.BlockSpec((MLA_H, tm, MLA_HD), head), pl.BlockSpec((MLA_H, tm, MLA_HD), head),
                  pl.BlockSpec((MLA_H, tm, LANE), head), pl.BlockSpec((tm, D), row), _full((8, D)),
                  pl.BlockSpec((tm, D), row), _full((D, MLA_LAT)), _full((1, MLA_QR)), _full((MLA_QR, nq)),
                  _full((1, MLA_KVR)), _full((MLA_KVR, nq)),
                  pl.BlockSpec((tm, LANE), row), pl.BlockSpec((tm, LANE), row), pl.BlockSpec((tm, LANE), row)],
        out_specs=(pl.BlockSpec((tm, D), row), _full((D, MLA_LAT)), _full((MLA_QR, nq)), _full((MLA_KVR, nq)),
                   _full((8, D)), _full((1, MLA_QR)), _full((1, MLA_KVR))),
        compiler_params=_params(1),
    )(dq, dk, dv, x, modl, dres, w_in, q_norm, w_q, kv_norm, w_kv, ct_a, s1_n, s2_n)


def _swa_attn_bwd(qkv, sinks, do, tb):
    T = qkv.shape[0]
    kw = SWA_HKV * LANE
    nstep = T // tb

    def body(q_ref, kvc_ref, kvp_ref, sink_ref, do_ref, dq_ref, dkvc_ref, dkvp_ref, dsink_ref, dk_s, dv_s, bias_s):
        i = pl.program_id(0)
        _swa_fill_bias(bias_s)

        @pl.when(i == 0)
        def _():
            dsink_ref[...] = jnp.zeros(dsink_ref.shape, F32)

        dk_s[...] = jnp.zeros(dk_s.shape, F32)
        dv_s[...] = jnp.zeros(dv_s.shape, F32)
        lane = lax.broadcasted_iota(jnp.int32, (1, LANE), 1)
        for g in range(SWA_HKV):
            gl = slice(g * LANE, (g + 1) * LANE)
            gv = slice(kw + g * LANE, kw + (g + 1) * LANE)
            kall = jnp.concatenate([kvp_ref[:, gl], kvc_ref[:, gl]], axis=0)
            vall = jnp.concatenate([kvp_ref[:, gv], kvc_ref[:, gv]], axis=0)
            for c in range(tb // (SWA_NB * SWA_W)):
                qc, k3, v3, eb, num, linv, es = _swa_chain(q_ref, kall, vall, sink_ref, bias_s, g, c)
                doc = _swa_rows(do_ref, g, c)
                delta = jnp.sum(doc.astype(F32) * (num * linv), axis=1, keepdims=True)
                delta_rep = jnp.broadcast_to(delta, (SWA_R, LANE))
                dp = _dot_nt(doc, v3)
                linv_rep = jnp.broadcast_to(linv, (SWA_R, LANE))
                pch = [eb[:, ch * LANE:(ch + 1) * LANE].astype(F32) * linv_rep for ch in range(SWA_NK // LANE)]
                ds = jnp.concatenate([pch[ch] * (dp[:, ch * LANE:(ch + 1) * LANE] - delta_rep)
                                      for ch in range(SWA_NK // LANE)], axis=1).astype(BF)
                pb = jnp.concatenate(pch, axis=1).astype(BF)
                dqc = (_dot(ds, k3) * SWA_SCALE).astype(BF)
                keys = slice(c * SWA_NB * SWA_W, c * SWA_NB * SWA_W + SWA_NK)
                dk_s[keys, gl] += _dot_tn(ds, qc) * SWA_SCALE
                dv_s[keys, gl] += _dot_tn(pb, doc)
                dsk = es * linv * delta
                for hh in range(4):
                    hq = 4 * g + hh
                    tot = jnp.zeros((1, 1), F32)
                    for bl in range(SWA_NB):
                        piece = (bl * 4 + hh) * SWA_W
                        rows = slice((c * SWA_NB + bl) * SWA_W, (c * SWA_NB + bl + 1) * SWA_W)
                        dq_ref[rows, hq * LANE:(hq + 1) * LANE] = dqc[piece:piece + SWA_W]
                        tot = tot + jnp.sum(dsk[piece:piece + SWA_W], axis=0, keepdims=True)
                    dsink_ref[0:1, :] -= jnp.where(lane == hq, tot, 0.0)
        dkvp_ref[0, :, 0:kw] = dk_s[0:SWA_W, :]
        dkvp_ref[0, :, kw:2 * kw] = dv_s[0:SWA_W, :]
        dkvc_ref[:, 0:kw] = dk_s[SWA_W:, :]
        dkvc_ref[:, kw:2 * kw] = dv_s[SWA_W:, :]

    q_spec, kvc_spec, kvp_spec = _swa_specs(T, tb)
    return pl.pallas_call(
        body, name="swa_attn_bwd", grid=(nstep,),
        out_shape=(jax.ShapeDtypeStruct((T, SWA_O), BF), jax.ShapeDtypeStruct((T, 2 * kw), F32),
                   jax.ShapeDtypeStruct((nstep, SWA_W, 2 * kw), F32), jax.ShapeDtypeStruct((8, LANE), F32)),
        in_specs=[q_spec, kvc_spec, kvp_spec, pl.BlockSpec(memory_space=pltpu.SMEM),
                  pl.BlockSpec((tb, SWA_O), lambda i: (i, 0))],
        out_specs=(pl.BlockSpec((tb, SWA_O), lambda i: (i, 0)), pl.BlockSpec((tb, 2 * kw), lambda i: (i, 0)),
                   pl.BlockSpec((1, SWA_W, 2 * kw), lambda i: (i, 0, 0)), _full((8, LANE))),
        scratch_shapes=[pltpu.VMEM((tb + SWA_W, kw), F32), pltpu.VMEM((tb + SWA_W, kw), F32),
                        pltpu.VMEM((2, SWA_R, SWA_NK), F32)],
        compiler_params=_params(1),
    )(qkv, qkv, qkv, sinks, do)


def _swa_proj_bwd(dq, dkvc, dkvp, x, modl, dres, w, tabs_neg, tm):
    T = x.shape[0]
    nstep = T // tm
    kw = SWA_HKV * LANE
    ct_b, s1_n, s2_n = tabs_neg

    def body(dq_ref, dkvc_ref, dkvp_ref, x_ref, modl_ref, dres_ref, w_ref, ct_ref, s1_ref, s2_ref,
             dx_ref, dz_ref, sm_ref, db_ref):
        i = pl.program_id(0)

        @pl.when(i == 0)
        def _():
            sm_ref[...] = jnp.zeros(sm_ref.shape, F32)
            db_ref[...] = jnp.zeros(db_ref.shape, F32)

        ct, s1, s2 = ct_ref[...], s1_ref[...], s2_ref[...]
        has_next = i + 1 < nstep
        for grp in range(SWA_QKV // LANE):
            cols = slice(grp * LANE, (grp + 1) * LANE)
            if grp < SWA_HQ:
                z = dq_ref[:, cols].astype(F32)
            else:
                kc = slice((grp - SWA_HQ) * LANE, (grp - SWA_HQ + 1) * LANE)
                cur = dkvc_ref[:, kc]
                tail = cur[tm - SWA_W:] + jnp.where(has_next, dkvp_ref[0, :, kc], 0.0)
                z = jnp.concatenate([cur[:tm - SWA_W], tail], axis=0)
            if grp < SWA_HQ + SWA_HKV:
                z = _rope128(z, ct, s1, s2, 8)
            db_ref[0:1, cols] += jnp.sum(z, axis=0, keepdims=True)
            dz_ref[:, cols] = z.astype(BF)
        dh = _dot_nt(dz_ref[...], w_ref[...])
        sm_ref[0:1, :] += jnp.sum(dh * x_ref[...], axis=0, keepdims=True)
        sm_ref[1:2, :] += jnp.sum(dh, axis=0, keepdims=True)
        dx_ref[...] = dres_ref[...] + dh * (1.0 + modl_ref[1:2, :])

    row = lambda i: (i, 0)
    return pl.pallas_call(
        body, name="swa_proj_bwd", grid=(nstep,),
        out_shape=(jax.ShapeDtypeStruct((T, D), F32), jax.ShapeDtypeStruct((T, SWA_QKV), BF),
                   jax.ShapeDtypeStruct((8, D), F32), jax.ShapeDtypeStruct((8, SWA_QKV), F32)),
        in_specs=[pl.BlockSpec((tm, SWA_O), row), pl.BlockSpec((tm, 2 * kw), row),
                  pl.BlockSpec((1, SWA_W, 2 * kw), lambda i: (jnp.minimum(i + 1, nstep - 1), 0, 0)),
                  pl.BlockSpec((tm, D), row), _full((8, D)), pl.BlockSpec((tm, D), row), _full((D, SWA_QKV)),
                  pl.BlockSpec((tm, LANE), row), pl.BlockSpec((tm, LANE), row), pl.BlockSpec((tm, LANE), row)],
        out_specs=(pl.BlockSpec((tm, D), row), pl.BlockSpec((tm, SWA_QKV), row), _full((8, D)),
                   _full((8, SWA_QKV))),
        compiler_params=_params(1),
    )(dq, dkvc, dkvp, x, modl, dres, w, ct_b, s1_n, s2_n)


def _adamw(gparts, w, m, v, name):
    P, R, C = gparts.shape
    tr = R
    for cand in (512, 256, 128):
        if R % cand == 0 and R > cand:
            tr = cand
            break
    c1 = 1.0 / (1.0 - ADAM_B1 ** ADAM_STEP)
    c2 = 1.0 / (1.0 - ADAM_B2 ** ADAM_STEP)

    def body(gp_ref, w_ref, m_ref, v_ref, g_ref, d_ref, nm_ref, nv_ref):
        g = gp_ref[0].astype(F32)
        for p in range(1, P):
            g = g + gp_ref[p].astype(F32)
        nm = ADAM_B1 * m_ref[...] + (1.0 - ADAM_B1) * g
        nv = ADAM_B2 * v_ref[...] + (1.0 - ADAM_B2) * (g * g)
        g_ref[...] = g
        nm_ref[...] = nm
        nv_ref[...] = nv
        d_ref[...] = -ADAM_LR * ((nm * c1) / (jnp.sqrt(nv * c2) + ADAM_EPS) + ADAM_WD * w_ref[...])

    blk = pl.BlockSpec((tr, C), lambda i: (i, 0))
    return pl.pallas_call(
        body, name=name, grid=(R // tr,),
        out_shape=(jax.ShapeDtypeStruct((R, C), F32),) * 4,
        in_specs=[pl.BlockSpec((P, tr, C), lambda i: (0, i, 0)), blk, blk, blk],
        out_specs=(blk,) * 4,
        compiler_params=_params(1),
    )(gparts, w, m, v)


PACK_W = 1024

BIG = {
    "ffn_w_gate": ((DEPTH, D, F // NDEV), 2),
    "ffn_w_up": ((DEPTH, D, F // NDEV), 2),
    "ffn_w_down": ((DEPTH, F // NDEV, D), 1),
    "mla_w_in": ((2, D // NDEV, 704), 1),
    "mla_w_q_b": ((2, MLA_QR, 1536 // NDEV), 2),
    "mla_w_kv_b": ((2, MLA_KVR, 2048 // NDEV), 2),
    "mla_w_o": ((2, D // NDEV, D), 1),
    "swa_w_qkv": ((2, D, 1536 // NDEV), 2),
    "swa_w_o": ((2, D // NDEV, D), 1),
}


EARLY = [("mla_w_in", 0, 1), ("mla_w_q_b", 0, 1), ("mla_w_kv_b", 0, 1)]
LATE = [("ffn_w_gate", 0, 4), ("ffn_w_up", 0, 4), ("ffn_w_down", 0, 4), ("mla_w_in", 1, 2), ("mla_w_q_b", 1, 2),
        ("mla_w_kv_b", 1, 2), ("mla_w_o", 0, 2), ("swa_w_qkv", 0, 2), ("swa_w_o", 0, 2)]


def _pack_rows(n):
    return -(-n // (16 * PACK_W)) * 16


def _entry_shape(name, lo, hi):
    return (hi - lo,) + BIG[name][0][1:]


def _pack_local(inp, entries):
    parts = []
    for name, lo, hi in entries:
        n = math.prod(_entry_shape(name, lo, hi))
        flat = inp[name][lo:hi].astype(BF).reshape(-1)
        parts.append(jnp.pad(flat, (0, _pack_rows(n) * PACK_W - n)).reshape(-1, PACK_W))
    return jnp.concatenate(parts, axis=0)


def _pack_full(full, entries):
    parts = []
    for name, lo, hi in entries:
        a, axis = full[(name, lo)], BIG[name][1]
        split = a.shape[:axis] + (NDEV, a.shape[axis] // NDEV) + a.shape[axis + 1:]
        a = jnp.moveaxis(a.reshape(split), axis, 0).astype(BF).reshape(NDEV, -1)
        n = math.prod(_entry_shape(name, lo, hi))
        parts.append(jnp.pad(a, ((0, 0), (0, _pack_rows(n) * PACK_W - n))).reshape(NDEV, -1, PACK_W))
    return jnp.concatenate(parts, axis=1)


def _unpack_blocks(packed, entries):
    out, r0 = {}, 0
    for name, lo, hi in entries:
        shape = _entry_shape(name, lo, hi)
        n = math.prod(shape)
        rows = _pack_rows(n)
        out[(name, lo)] = packed[:, r0:r0 + rows].reshape(NDEV, -1)[:, :n].reshape((NDEV,) + shape)
        r0 += rows
    return out


def _unpack_full(packed, entries):
    out = {}
    for (name, lo), blk in _unpack_blocks(packed, entries).items():
        axis = BIG[name][1]
        a = jnp.moveaxis(blk, 0, axis)
        out[(name, lo)] = a.reshape(a.shape[:axis] + (a.shape[axis] * a.shape[axis + 1],) + a.shape[axis + 2:])
    return out


def _pad_heads(a, axis, nheads, width, to):
    shp = a.shape[:axis] + (nheads, width) + a.shape[axis + 1:]
    a = a.reshape(shp)
    pad = [(0, 0)] * a.ndim
    pad[axis + 1] = (0, to - width)
    a = jnp.pad(a, pad)
    return a.reshape(a.shape[:axis] + (nheads * to,) + a.shape[axis + 2:])


def _unpad_heads(a, axis, nheads, width, to):
    shp = a.shape[:axis] + (nheads, to) + a.shape[axis + 1:]
    a = lax.slice_in_dim(a.reshape(shp), 0, width, axis=axis + 1)
    return a.reshape(a.shape[:axis] + (nheads * width,) + a.shape[axis + 2:])


def _swa_pad_cols(a):
    return _pad_heads(a, a.ndim - 1, SWA_HQ + 2 * SWA_HKV, 64, LANE)


def _rope_tables(positions, half):
    rot = 2 * half
    inv = ROPE_THETA ** (-jnp.arange(0, rot, 2, dtype=F32) / rot)
    ang = positions.astype(F32)[:, None] * inv
    cos, sin = jnp.cos(ang), jnp.sin(ang)
    T = positions.shape[0]
    ones = jnp.ones((T, LANE - rot), F32)
    zeros = jnp.zeros((T, LANE - rot), F32)
    zh = jnp.zeros((T, half), F32)
    ct = jnp.concatenate([cos, cos, ones], axis=1)
    s1 = jnp.concatenate([-sin, zh, zeros], axis=1)
    s2 = jnp.concatenate([zh, sin, zeros], axis=1)
    return (ct, s1, s2), (ct, -s1, -s2)


def _small_pack(vecs, rows):
    flat = jnp.concatenate([v.astype(F32).reshape(-1) for v in vecs])
    return jnp.pad(flat, (0, rows * LANE - flat.shape[0])).reshape(rows, LANE)


def _small_unpack(buf, shapes):
    flat = buf.reshape(NDEV, -1)
    out, o = [], 0
    for shp in shapes:
        n = math.prod(shp)
        out.append(flat[:, o:o + n].reshape((NDEV,) + shp))
        o += n
    return out


def _step(inp):
    x = inp["x"][0]
    tgt = inp["loss_target"][0]
    T = x.shape[0]
    tm = min(512, T)
    tmf = min(256, T)
    tb = min(512, T)
    tbf = min(1024, T)
    tbq = min(2048, T)
    tnf = F // 2
    me = 4 * lax.axis_index("x") + 2 * lax.axis_index("y") + lax.axis_index("c")

    small_in = _small_pack([inp["c"], inp["swa_b_qkv"], inp["swa_b_o"]], 16)
    c_all, bqkv_blk, bo_blk = _small_unpack(_exchange(small_in, True, "gather_small"),
                                            [(D,), (2, 1536 // NDEV), (2, D // NDEV)])
    swa_b_qkv = jnp.moveaxis(bqkv_blk, 0, 1).reshape(2, 1536)
    swa_b_o = jnp.moveaxis(bo_blk, 0, 1).reshape(2, D)

    w_early = _unpack_full(_exchange(_pack_local(inp, EARLY), True, "gather_early"), EARLY)

    ncol = 6 * D // NDEV
    ada_b_loc = lax.dynamic_slice_in_dim(inp["ada_b"], me * ncol, ncol, axis=1)[:, None, :]
    mod_all = _mod_all(c_all, inp["ada_w"], ada_b_loc)
    mod_src = jnp.moveaxis(mod_all, 1, 0).reshape(NDEV, DEPTH * ncol // LANE, LANE)
    mod_got = _exchange(mod_src, False, "scatter_mod").reshape(NDEV, DEPTH, ncol)
    mod = jnp.moveaxis(mod_got, 0, 1).reshape(DEPTH, 6, D)
    modl = jnp.pad(mod, ((0, 0), (0, 2), (0, 0)))

    def mla_proj_weights(w, lo):
        return (jnp.pad(w[("mla_w_in", lo)][0], ((0, 0), (0, MLA_LAT - 704))),
                _pad_heads(w[("mla_w_q_b", lo)][0], 1, MLA_H, 192, MLA_HD), w[("mla_w_kv_b", lo)][0])

    w_in, w_q, w_kv = [[t] for t in mla_proj_weights(w_early, 0)]
    b_qkv = _swa_pad_cols(swa_b_qkv)
    zero_bias = jnp.zeros((1, D), F32)

    pos = inp["positions"][0]
    tabs_a, tabs_a_neg = _rope_tables(pos, MLA_ROPE // 2)
    tabs_b, tabs_b_neg = _rope_tables(pos, 8)

    saved = []
    xs = x
    for i in range(DEPTH):
        j = i // 2
        st = {"x0": xs}
        if i % 2 == 0:
            q, k, v = _mla_proj_fwd(xs, modl[i], w_in[j], inp["mla_q_norm"][j][None], w_q[j],
                                    inp["mla_kv_norm"][j][None], w_kv[j], tabs_a, tm)
            if i == 0:
                o, lse, got = _flash_fwd(q, k, v, tbq, _pack_local(inp, LATE))
                w_late = _unpack_full(got, LATE)
                for lst, t in zip((w_in, w_q, w_kv), mla_proj_weights(w_late, 1)):
                    lst.append(t)
                w_o_mla = w_late[("mla_w_o", 0)]
                w_qkv = _swa_pad_cols(w_late[("swa_w_qkv", 0)])
                w_o_swa = _pad_heads(w_late[("swa_w_o", 0)], 1, SWA_HQ, 64, LANE)
                w_gate, w_up, w_down = (w_late[(n, 0)] for n in ("ffn_w_gate", "ffn_w_up", "ffn_w_down"))
            else:
                o, lse = _flash_fwd(q, k, v, tbq)
            st.update(q=q, k=k, v=v, o=o, lse=lse)
            w_o, b_o = w_o_mla[j], zero_bias
        else:
            qkv = _swa_proj_fwd(xs, modl[i], w_qkv[j], b_qkv[j][None], tabs_b, tm)
            o = _swa_attn_fwd(qkv, inp["swa_sinks"][j], tb)
            st.update(qkv=qkv, o=o)
            w_o, b_o = w_o_swa[j], swa_b_o[j][None]
        y, u, xs = _outproj_ln_fwd(o, w_o, b_o, xs, modl[i], 2, inp["ln_mix_g"][i][None],
                                   inp["ln_mix_b"][i][None], tm, f"mix_out_fwd_{i % 2}")
        st.update(y_m=y, u_m=u, x1=xs, w_o=w_o)
        g, up, a = _ffn_up_fwd(xs, modl[i], w_gate[i], w_up[i], tmf, tnf)
        y, u, xs = _outproj_ln_fwd(a, w_down[i], zero_bias, xs, modl[i], 5,
                                   inp["ln_ffn_g"][i][None], inp["ln_ffn_b"][i][None], tmf, "ffn_out_fwd")
        st.update(g=g, up=up, a=a, y_f=y, u_f=u)
        saved.append(st)

    dx, loss_rows = _loss_grad(xs, tgt, tm)
    loss = lax.psum(jnp.sum(loss_rows[0]), ("x", "y", "c"))

    gfull = {n: [None] * (DEPTH if n.startswith("ffn") else 2) for n in BIG}
    dmod = [None] * DEPTH
    g_ln = {n: [None] * DEPTH for n in ("ln_mix_g", "ln_mix_b", "ln_ffn_g", "ln_ffn_b")}
    g_qn, g_kvn, g_sink, g_bqkv, g_bo = [None] * 2, [None] * 2, [None] * 2, [None] * 2, [None] * 2
    for i in reversed(range(DEPTH)):
        j = i // 2
        st = saved[i]
        dres, dy, da, sm = _outproj_ln_bwd(dx, st["u_f"], st["y_f"], w_down[i], modl[i], 5,
                                           inp["ln_ffn_g"][i][None], tmf, "ffn_out_bwd")
        g_ln["ln_ffn_g"][i], g_ln["ln_ffn_b"][i], dg_f = sm[0], sm[1], sm[2]
        gfull["ffn_w_down"][i] = _wgrad(st["a"], dy, tm, F // 2, D, "wgrad_down")
        dgp, dup, dx, sm = _ffn_mid_bwd(da, st["g"], st["up"], st["x1"], modl[i], dres,
                                        w_gate[i], w_up[i], tm, tnf)
        dsc_f, dsh_f = sm[0], sm[1]
        gfull["ffn_w_gate"][i] = _wgrad(st["x1"], dgp, tm, D, tnf, "wgrad_gate", modl[i], (4, 3))
        gfull["ffn_w_up"][i] = _wgrad(st["x1"], dup, tm, D, tnf, "wgrad_up", modl[i], (4, 3))

        dres, dy, do, sm = _outproj_ln_bwd(dx, st["u_m"], st["y_m"], st["w_o"], modl[i], 2,
                                           inp["ln_mix_g"][i][None], tm, f"mix_out_bwd_{i % 2}")
        g_ln["ln_mix_g"][i], g_ln["ln_mix_b"][i], dg_m = sm[0], sm[1], sm[2]
        if i % 2 == 0:
            gfull["mla_w_o"][j] = _wgrad(st["o"], dy, tm, D, D, "wgrad_mla_o")
            delta = _flash_delta(st["o"], do, tb)
            if i == 0:
                late = {(n, lo): jnp.stack(gfull[n][lo:hi]) for n, lo, hi in LATE}
                dq, dk, dv, got = _flash_bwd(st["q"], st["k"], st["v"], do, st["lse"], delta, tbf,
                                             _pack_full(late, LATE))
                gparts = _unpack_blocks(got, LATE)
            else:
                dq, dk, dv = _flash_bwd(st["q"], st["k"], st["v"], do, st["lse"], delta, tbf)
            dx, dwin, dwq, dwkv, sm, dqn, dkvn = _mla_proj_bwd(
                dq, dk, dv, st["x0"], modl[i], dres, w_in[j], inp["mla_q_norm"][j][None], w_q[j],
                inp["mla_kv_norm"][j][None], w_kv[j], tabs_a_neg, tm)
            gfull["mla_w_in"][j] = dwin[:, :704]
            gfull["mla_w_q_b"][j] = _unpad_heads(dwq, 1, MLA_H, 192, MLA_HD)
            gfull["mla_w_kv_b"][j] = dwkv
            g_qn[j], g_kvn[j] = dqn[0], dkvn[0]
        else:
            g_bo[j] = sm[3]
            dwo = _wgrad(st["o"], dy, tm, SWA_O // 2, D, "wgrad_swa_o")
            gfull["swa_w_o"][j] = _unpad_heads(dwo, 0, SWA_HQ, 64, LANE)
            dq, dkvc, dkvp, dsink = _swa_attn_bwd(st["qkv"], inp["swa_sinks"][j], do, tb)
            g_sink[j] = dsink[0, :SWA_HQ]
            dx, dz, sm, db = _swa_proj_bwd(dq, dkvc, dkvp, st["x0"], modl[i], dres, w_qkv[j], tabs_b_neg, tm)
            dwqkv = _wgrad(st["x0"], dz, tm, D, SWA_QKV // 2, "wgrad_swa_qkv", modl[i], (1, 0))
            gfull["swa_w_qkv"][j] = _unpad_heads(dwqkv, 1, SWA_HQ + 2 * SWA_HKV, 64, LANE)
            g_bqkv[j] = _unpad_heads(db[0], 0, SWA_HQ + 2 * SWA_HKV, 64, LANE)
        dmod[i] = jnp.stack([sm[1], sm[0], dg_m, dsh_f, dsc_f, dg_f])
    grad_x = dx[None]

    small_shapes = [(DEPTH, 6 * D), (DEPTH, D), (DEPTH, D), (DEPTH, D), (DEPTH, D), (2, MLA_QR), (2, MLA_KVR),
                    (2, SWA_HQ), (2, 1536), (2, D)]
    small_vals = [jnp.stack(dmod).reshape(DEPTH, 6 * D), jnp.stack(g_ln["ln_mix_g"]), jnp.stack(g_ln["ln_mix_b"]),
                  jnp.stack(g_ln["ln_ffn_g"]), jnp.stack(g_ln["ln_ffn_b"]), jnp.stack(g_qn), jnp.stack(g_kvn),
                  jnp.stack(g_sink), jnp.stack(g_bqkv), jnp.stack(g_bo)]
    nsmall = sum(math.prod(s) for s in small_shapes)
    small_rows = -(-nsmall // (8 * LANE)) * 8
    (dmod_all, p_lmg, p_lmb, p_lfg, p_lfb, p_qn, p_kvn, p_sink, p_bqkv, p_bo) = _small_unpack(
        _exchange(_small_pack(small_vals, small_rows), True, "gather_small_grads"), small_shapes)

    early = {(n, lo): jnp.stack(gfull[n][lo:hi]) for n, lo, hi in EARLY}
    gparts.update(_unpack_blocks(_exchange(_pack_full(early, EARLY), False, "scatter_early_grads"), EARLY))
    for n, _, _ in EARLY:
        gparts[(n, 0)] = jnp.concatenate([gparts[(n, 0)], gparts.pop((n, 1))], axis=1)

    res = {}

    def update(name, parts):
        w = inp[name]
        shp = w.shape
        r2 = (math.prod(shp[:-1]), shp[-1])
        outs = _adamw(parts.reshape((parts.shape[0],) + r2), w.reshape(r2), inp["m_" + name].reshape(r2),
                      inp["v_" + name].reshape(r2), "adamw_" + name)
        res[name] = tuple(o.reshape(shp) for o in outs)

    dmod_loc = lax.dynamic_slice_in_dim(dmod_all, me * ncol, ncol, axis=2)
    g_ada_w = _ada_w_grad(c_all.T, jnp.moveaxis(dmod_loc, 0, 1))
    update("ada_w", g_ada_w[None])
    update("ada_b", dmod_all)
    update("ln_mix_g", p_lmg)
    update("ln_mix_b", p_lmb)
    update("ln_ffn_g", p_lfg)
    update("ln_ffn_b", p_lfb)
    for name in BIG:
        update(name, gparts[(name, 0)])
    update("mla_q_norm", p_qn)
    update("mla_kv_norm", p_kvn)
    update("swa_sinks", p_sink)
    nb = 1536 // NDEV
    update("swa_b_qkv", lax.dynamic_slice_in_dim(p_bqkv, me * nb, nb, axis=2))
    update("swa_b_o", lax.dynamic_slice_in_dim(p_bo, me * (D // NDEV), D // NDEV, axis=2))
    return loss, grad_x, res


WEIGHTS = ["ada_w", "ada_b", "ln_mix_g", "ln_mix_b", "ln_ffn_g", "ln_ffn_b", "ffn_w_gate", "ffn_w_up",
           "ffn_w_down", "mla_w_in", "mla_q_norm", "mla_w_q_b", "mla_kv_norm", "mla_w_kv_b", "mla_w_o",
           "swa_w_qkv", "swa_b_qkv", "swa_sinks", "swa_w_o", "swa_b_o"]
INPUTS = (["x", "c", "positions"] + WEIGHTS + ["loss_target"] + ["m_" + n for n in WEIGHTS]
          + ["v_" + n for n in WEIGHTS])


def kernel(x, c, positions, ada_w, ada_b, ln_mix_g, ln_mix_b, ln_ffn_g, ln_ffn_b, ffn_w_gate, ffn_w_up, ffn_w_down, mla_w_in, mla_q_norm, mla_w_q_b, mla_kv_norm, mla_w_kv_b, mla_w_o, swa_w_qkv, swa_b_qkv, swa_sinks, swa_w_o, swa_b_o, loss_target, m_ada_w, m_ada_b, m_ln_mix_g, m_ln_mix_b, m_ln_ffn_g, m_ln_ffn_b, m_ffn_w_gate, m_ffn_w_up, m_ffn_w_down, m_mla_w_in, m_mla_q_norm, m_mla_w_q_b, m_mla_kv_norm, m_mla_w_kv_b, m_mla_w_o, m_swa_w_qkv, m_swa_b_qkv, m_swa_sinks, m_swa_w_o, m_swa_b_o, v_ada_w, v_ada_b, v_ln_mix_g, v_ln_mix_b, v_ln_ffn_g, v_ln_ffn_b, v_ffn_w_gate, v_ffn_w_up, v_ffn_w_down, v_mla_w_in, v_mla_q_norm, v_mla_w_q_b, v_mla_kv_norm, v_mla_w_kv_b, v_mla_w_o, v_swa_w_qkv, v_swa_b_qkv, v_swa_sinks, v_swa_w_o, v_swa_b_o):
    args = (x, c, positions, ada_w, ada_b, ln_mix_g, ln_mix_b, ln_ffn_g, ln_ffn_b, ffn_w_gate, ffn_w_up, ffn_w_down, mla_w_in, mla_q_norm, mla_w_q_b, mla_kv_norm, mla_w_kv_b, mla_w_o, swa_w_qkv, swa_b_qkv, swa_sinks, swa_w_o, swa_b_o, loss_target, m_ada_w, m_ada_b, m_ln_mix_g, m_ln_mix_b, m_ln_ffn_g, m_ln_ffn_b, m_ffn_w_gate, m_ffn_w_up, m_ffn_w_down, m_mla_w_in, m_mla_q_norm, m_mla_w_q_b, m_mla_kv_norm, m_mla_w_kv_b, m_mla_w_o, m_swa_w_qkv, m_swa_b_qkv, m_swa_sinks, m_swa_w_o, m_swa_b_o, v_ada_w, v_ada_b, v_ln_mix_g, v_ln_mix_b, v_ln_ffn_g, v_ln_ffn_b, v_ffn_w_gate, v_ffn_w_up, v_ffn_w_down, v_mla_w_in, v_mla_q_norm, v_mla_w_q_b, v_mla_kv_norm, v_mla_w_kv_b, v_mla_w_o, v_swa_w_qkv, v_swa_b_qkv, v_swa_sinks, v_swa_w_o, v_swa_b_o)
    assert len(args) == len(INPUTS)
    loss, grad_x, res = _step(dict(zip(INPUTS, args)))
    return (loss, grad_x, *[res[n][0] for n in WEIGHTS], *[res[n][1] for n in WEIGHTS],
            *[res[n][2] for n in WEIGHTS], *[res[n][3] for n in WEIGHTS])
```

```python
import functools
import math

import jax
import jax.numpy as jnp
from jax import lax
from jax.experimental import pallas as pl
from jax.experimental.pallas import tpu as pltpu

F32 = jnp.float32
BF = jnp.bfloat16

NDEV = 8
D = 1024
DEPTH = 4
F = 2816
ALPHA = (2 * DEPTH) ** 0.25
LN_EPS = 1e-5
RMS_EPS = 1e-6
ROPE_THETA = 500000.0

MLA_H = 8
MLA_QR = 384
MLA_KVR = 256
MLA_ROPE = 64
MLA_LAT = 768
MLA_HD = 256
MLA_SCALE = (128 + 64) ** -0.5

SWA_HQ = 16
SWA_HKV = 4
SWA_W = 128
SWA_SCALE = 64 ** -0.5
SWA_QKV = (SWA_HQ + 2 * SWA_HKV) * 128
SWA_O = SWA_HQ * 128

LANE = 128
VMEM_LIMIT = 56 * 2 ** 20

ADAM_LR, ADAM_B1, ADAM_B2, ADAM_EPS, ADAM_WD, ADAM_STEP = 0.001, 0.9, 0.999, 1e-8, 0.01, 10


def _params(n_axes):
    return pltpu.CompilerParams(dimension_semantics=("arbitrary",) * n_axes, vmem_limit_bytes=VMEM_LIMIT)


def _dot(a, b):
    return jnp.dot(a, b, preferred_element_type=F32)


def _dot_nt(a, b):
    return lax.dot_general(a, b, (((1,), (1,)), ((), ())), preferred_element_type=F32)


def _dot_tn(a, b):
    return lax.dot_general(a, b, (((0,), (0,)), ((), ())), preferred_element_type=F32)


def _full(shape):
    return pl.BlockSpec(shape, lambda *_: (0,) * len(shape))


def _resident(shape):
    return pl.BlockSpec(shape, lambda *_: (0,) * len(shape), pipeline_mode=pl.Buffered(1))


def _sigmoid(x):
    return 1.0 / (1.0 + jnp.exp(-x))


def _rope128(x, ct, s1, s2, half):
    return x * ct + pltpu.roll(x, LANE - half, 1) * s1 + pltpu.roll(x, half, 1) * s2


def _eye(n):
    return lax.broadcasted_iota(jnp.int32, (n, n), 0) == lax.broadcasted_iota(jnp.int32, (n, n), 1)


def _col_to_row(col):
    n = col.shape[0]
    return jnp.sum(jnp.where(_eye(n), col, 0.0), axis=0, keepdims=True)


def _row_to_col(row):
    n = row.shape[1]
    return jnp.sum(jnp.where(_eye(n), row, 0.0), axis=1, keepdims=True)


def _modulate(x, modl_ref, sc_row, sh_row):
    return x * (1.0 + modl_ref[sc_row:sc_row + 1, :]) + modl_ref[sh_row:sh_row + 1, :]


EXCHANGE_SEMS = [pltpu.SemaphoreType.DMA((NDEV - 1,)), pltpu.SemaphoreType.DMA((NDEV - 1,)), pltpu.SemaphoreType.DMA]


def _exchange_copies(src_ref, out_ref, send_sems, recv_sems, local_sem, gather):
    x, y, c = lax.axis_index("x"), lax.axis_index("y"), lax.axis_index("c")
    me = 4 * x + 2 * y + c

    def piece(dev):
        return src_ref if gather else src_ref.at[dev]

    mine = pltpu.make_async_copy(piece(me), out_ref.at[me], local_sem)
    sends, recvs = [], []
    for k in range(1, NDEV):
        px = 1 - x if k & 4 else x
        py = 1 - y if k & 2 else y
        pc = 1 - c if k & 1 else c
        peer = 4 * px + 2 * py + pc
        common = dict(send_sem=send_sems.at[k - 1], recv_sem=recv_sems.at[k - 1],
                      device_id=(px, py, pc), device_id_type=pl.DeviceIdType.MESH)
        sends.append(pltpu.make_async_remote_copy(src_ref=piece(peer), dst_ref=out_ref.at[me], **common))
        recvs.append(pltpu.make_async_remote_copy(src_ref=piece(peer), dst_ref=out_ref.at[peer], **common))
    return mine, sends, recvs


def _exchange_start(*refs, gather):
    mine, sends, _ = _exchange_copies(*refs, gather)
    mine.start()
    for s in sends:
        s.start()


def _exchange_wait(*refs, gather):
    mine, sends, recvs = _exchange_copies(*refs, gather)
    for r in recvs:
        r.wait_recv()
    for s in sends:
        s.wait_send()
    mine.wait()


def _exchange_out(src, gather):
    blk = tuple(src.shape) if gather else tuple(src.shape[1:])
    return jax.ShapeDtypeStruct((NDEV,) + blk, src.dtype)


def _exchange(src, gather, name):
    def body(*refs):
        _exchange_start(*refs, gather=gather)
        _exchange_wait(*refs, gather=gather)

    return pl.pallas_call(
        body, name=name,
        out_shape=_exchange_out(src, gather),
        in_specs=[pl.BlockSpec(memory_space=pltpu.HBM)],
        out_specs=pl.BlockSpec(memory_space=pltpu.HBM),
        scratch_shapes=EXCHANGE_SEMS,
    )(src)


def _mod_all(c_all, ada_w, ada_b_loc):
    ncol = ada_w.shape[2]

    def body(c_ref, w_ref, b_ref, o_ref):
        cv = c_ref[...]
        cond = cv * _sigmoid(cv)
        o_ref[0] = _dot(cond.astype(BF), w_ref[0].astype(BF)) + b_ref[0]

    return pl.pallas_call(
        body, name="mod_all", grid=(DEPTH,),
        out_shape=jax.ShapeDtypeStruct((DEPTH, NDEV, ncol), F32),
        in_specs=[_full((NDEV, D)), pl.BlockSpec((1, D, ncol), lambda i: (i, 0, 0)),
                  pl.BlockSpec((1, 1, ncol), lambda i: (i, 0, 0))],
        out_specs=pl.BlockSpec((1, NDEV, ncol), lambda i: (i, 0, 0)),
        compiler_params=_params(1),
    )(c_all, ada_w, ada_b_loc)


def _ada_w_grad(c_all_t, dmod_loc):
    ncol = dmod_loc.shape[2]

    def body(ct_ref, dm_ref, o_ref):
        cv = ct_ref[...]
        cond = cv * _sigmoid(cv)
        acc = cond[:, 0:1] * dm_ref[0, 0:1, :]
        for b in range(1, NDEV):
            acc = acc + cond[:, b:b + 1] * dm_ref[0, b:b + 1, :]
        o_ref[0] = acc

    return pl.pallas_call(
        body, name="ada_w_grad", grid=(DEPTH,),
        out_shape=jax.ShapeDtypeStruct((DEPTH, D, ncol), F32),
        in_specs=[_full((D, NDEV)), pl.BlockSpec((1, NDEV, ncol), lambda i: (i, 0, 0))],
        out_specs=pl.BlockSpec((1, D, ncol), lambda i: (i, 0, 0)),
        compiler_params=_params(1),
    )(c_all_t, dmod_loc)


def _mla_proj_fwd(x, modl, w_in, q_norm, w_q, kv_norm, w_kv, tabs, tm):
    T = x.shape[0]
    ct_a, s1_a, s2_a = tabs

    def body(x_ref, modl_ref, win_ref, qn_ref, wq_ref, kvn_ref, wkv_ref, ct_ref, s1_ref, s2_ref,
             q_ref, k_ref, v_ref):
        h = _modulate(x_ref[...], modl_ref, 1, 0).astype(BF)
        lat = _dot(h, win_ref[...])
        ql, kvl, kr = lat[:, :MLA_QR], lat[:, MLA_QR:MLA_QR + MLA_KVR], lat[:, MLA_QR + MLA_KVR:]
        qn = (ql * lax.rsqrt(jnp.mean(ql * ql, axis=1, keepdims=True) + RMS_EPS) * qn_ref[...]).astype(BF)
        kvn = (kvl * lax.rsqrt(jnp.mean(kvl * kvl, axis=1, keepdims=True) + RMS_EPS) * kvn_ref[...]).astype(BF)
        ct, s1, s2 = ct_ref[...], s1_ref[...], s2_ref[...]
        kr = _rope128(kr, ct, s1, s2, MLA_ROPE // 2).astype(BF)
        for hd in range(MLA_H):
            cols = slice(hd * MLA_HD, (hd + 1) * MLA_HD)
            qh = _dot(qn, wq_ref[:, cols])
            q_ref[hd, :, 0:LANE] = qh[:, :LANE].astype(BF)
            q_ref[hd, :, LANE:MLA_HD] = _rope128(qh[:, LANE:], ct, s1, s2, MLA_ROPE // 2).astype(BF)
            kvh = _dot(kvn, wkv_ref[:, cols])
            k_ref[hd, :, 0:LANE] = kvh[:, :LANE].astype(BF)
            k_ref[hd, :, LANE:MLA_HD] = kr
            v_ref[hd, :, 0:LANE] = kvh[:, LANE:].astype(BF)
            v_ref[hd, :, LANE:2 * LANE] = jnp.ones((tm, LANE), BF)

    row = lambda i: (i, 0)
    head = lambda i: (0, i, 0)
    return pl.pallas_call(
        body, name="mla_proj_fwd", grid=(T // tm,),
        out_shape=(jax.ShapeDtypeStruct((MLA_H, T, MLA_HD), BF), jax.ShapeDtypeStruct((MLA_H, T, MLA_HD), BF),
                   jax.ShapeDtypeStruct((MLA_H, T, 2 * LANE), BF)),
        in_specs=[pl.BlockSpec((tm, D), row), _full((8, D)), _full((D, MLA_LAT)), _full((1, MLA_QR)),
                  _full((MLA_QR, MLA_H * MLA_HD)), _full((1, MLA_KVR)), _full((MLA_KVR, MLA_H * MLA_HD)),
                  pl.BlockSpec((tm, LANE), row), pl.BlockSpec((tm, LANE), row), pl.BlockSpec((tm, LANE), row)],
        out_specs=(pl.BlockSpec((MLA_H, tm, MLA_HD), head), pl.BlockSpec((MLA_H, tm, MLA_HD), head),
                   pl.BlockSpec((MLA_H, tm, 2 * LANE), head)),
        compiler_params=_params(1),
    )(x, modl, w_in, q_norm, w_q, kv_norm, w_kv, ct_a, s1_a, s2_a)


def _causal_mask(row0, nrows, ncols, transposed):
    row = lax.broadcasted_iota(jnp.int32, (nrows, ncols), 0) + row0
    col = lax.broadcasted_iota(jnp.int32, (nrows, ncols), 1)
    return (row <= col) if transposed else (col <= row)


def _flash_fwd(q, k, v, tb, comm_src=None):
    H, T, _ = q.shape
    rh = min(256, tb)
    c_exp = MLA_SCALE * math.log2(math.e)

    def body(*refs):
        if comm_src is None:
            q_ref, k_ref, v_ref, o_ref, lse_ref, m_s, acc_s = refs
        else:
            q_ref, k_ref, v_ref, src_ref, o_ref, lse_ref, got_ref, m_s, acc_s, *sems = refs
            comm = (src_ref, got_ref, *sems)
            first = (pl.program_id(0) == 0) & (pl.program_id(1) == 0)
            last = (pl.program_id(0) == H - 1) & (pl.program_id(1) == T // tb - 1)
            pl.when(first)(functools.partial(_exchange_start, *comm, gather=True))
        i = pl.program_id(1)
        m_s[...] = jnp.full(m_s.shape, -jnp.inf, F32)
        acc_s[...] = jnp.zeros(acc_s.shape, F32)

        def block(j, masked):
            base = pl.multiple_of(j * tb, tb)
            for hf in range(tb // rh):
                r = slice(hf * rh, (hf + 1) * rh)
                nk = (hf + 1) * rh if masked else tb
                kb = k_ref[0, pl.ds(base, nk), :]
                vb = v_ref[0, pl.ds(base, nk), :]
                s = _dot_nt(q_ref[0, r, :], kb)
                if masked:
                    s = jnp.where(_causal_mask(hf * rh, rh, nk, False), s, -jnp.inf)
                sc = [s[:, c * LANE:(c + 1) * LANE] for c in range(nk // LANE)]
                mx = sc[0]
                for x in sc[1:]:
                    mx = jnp.maximum(mx, x)
                m_prev = m_s[r, :]
                m_new = jnp.maximum(m_prev, jnp.max(mx, axis=1, keepdims=True))
                p = jnp.concatenate([jnp.exp2((x - m_new) * c_exp) for x in sc], axis=1)
                corr = jnp.exp2((m_prev - m_new) * c_exp)
                acc_s[r, :] = jnp.concatenate([corr, corr], axis=1) * acc_s[r, :] + _dot(p.astype(BF), vb)
                m_s[r, :] = m_new

        def step(j, carry):
            block(j, False)
            return carry

        lax.fori_loop(0, i, step, 0)
        block(i, True)
        l = acc_s[:, LANE:]
        o_ref[...] = (acc_s[:, :LANE] / l).astype(BF)
        lse = (m_s[...] * MLA_SCALE + jnp.log(l))[:, 0:1]
        for c0 in range(0, tb, rh):
            lse_ref[0, :, c0:c0 + rh] = _col_to_row(lse[c0:c0 + rh])
        if comm_src is not None:
            pl.when(last)(functools.partial(_exchange_wait, *comm, gather=True))

    hbm = pl.BlockSpec(memory_space=pltpu.HBM)
    with_comm = comm_src is not None
    return pl.pallas_call(
        body, name="flash_fwd_gather" if with_comm else "flash_fwd", grid=(H, T // tb),
        out_shape=(jax.ShapeDtypeStruct((T, H * LANE), BF), jax.ShapeDtypeStruct((H, 1, T), F32))
        + ((_exchange_out(comm_src, True),) if with_comm else ()),
        in_specs=[pl.BlockSpec((1, tb, MLA_HD), lambda h, i: (h, i, 0)),
                  pl.BlockSpec((1, T, MLA_HD), lambda h, i: (h, 0, 0)),
                  pl.BlockSpec((1, T, 2 * LANE), lambda h, i: (h, 0, 0))] + ([hbm] if with_comm else []),
        out_specs=(pl.BlockSpec((tb, LANE), lambda h, i: (i, h)),
                   pl.BlockSpec((1, 1, tb), lambda h, i: (h, 0, i))) + ((hbm,) if with_comm else ()),
        scratch_shapes=[pltpu.VMEM((tb, LANE), F32), pltpu.VMEM((tb, 2 * LANE), F32)]
        + (EXCHANGE_SEMS if with_comm else []),
        compiler_params=_params(2),
    )(*((q, k, v, comm_src) if with_comm else (q, k, v)))


def _outproj_ln_fwd(a, w, bias, x, modl, g_row, ln_g, ln_b, tm, name):
    T, K = a.shape

    def body(a_ref, w_ref, b_ref, x_ref, modl_ref, g_ref, bb_ref, y_ref, u_ref, xn_ref):
        y = _dot(a_ref[...], w_ref[...]) + b_ref[...]
        u = ALPHA * x_ref[...] + modl_ref[g_row:g_row + 1, :] * y
        mu = jnp.mean(u, axis=1, keepdims=True)
        uc = u - mu
        var = jnp.mean(uc * uc, axis=1, keepdims=True)
        y_ref[...] = y.astype(BF)
        u_ref[...] = u
        xn_ref[...] = uc * lax.rsqrt(var + LN_EPS) * g_ref[...] + bb_ref[...]

    row = lambda i: (i, 0)
    return pl.pallas_call(
        body, name=name, grid=(T // tm,),
        out_shape=(jax.ShapeDtypeStruct((T, D), BF), jax.ShapeDtypeStruct((T, D), F32),
                   jax.ShapeDtypeStruct((T, D), F32)),
        in_specs=[pl.BlockSpec((tm, K), row), _resident((K, D)), _full((1, D)), pl.BlockSpec((tm, D), row),
                  _full((8, D)), _full((1, D)), _full((1, D))],
        out_specs=(pl.BlockSpec((tm, D), row),) * 3,
        compiler_params=_params(1),
    )(a, w, bias, x, modl, ln_g, ln_b)


def _ffn_up_fwd(x, modl, wg, wu, tm, tn):
    T = x.shape[0]

    def body(x_ref, modl_ref, wg_ref, wu_ref, g_ref, u_ref, a_ref):
        h = _modulate(x_ref[...], modl_ref, 4, 3).astype(BF)
        g = _dot(h, wg_ref[...])
        u = _dot(h, wu_ref[...])
        g_ref[...] = g.astype(BF)
        u_ref[...] = u.astype(BF)
        a_ref[...] = (g * _sigmoid(g) * u).astype(BF)

    tile = pl.BlockSpec((tm, tn), lambda n, i: (i, n))
    wcol = pl.BlockSpec((D, tn), lambda n, i: (0, n))
    return pl.pallas_call(
        body, name="ffn_up_fwd", grid=(F // tn, T // tm),
        out_shape=(jax.ShapeDtypeStruct((T, F), BF),) * 3,
        in_specs=[pl.BlockSpec((tm, D), lambda n, i: (i, 0)), _full((8, D)), wcol, wcol],
        out_specs=(tile, tile, tile),
        compiler_params=_params(2),
    )(x, modl, wg, wu)


def _swa_proj_fwd(x, modl, w, b, tabs, tm):
    T = x.shape[0]
    ct_b, s1_b, s2_b = tabs
    n_rope = SWA_HQ + SWA_HKV

    def body(x_ref, modl_ref, w_ref, b_ref, ct_ref, s1_ref, s2_ref, o_ref):
        h = _modulate(x_ref[...], modl_ref, 1, 0).astype(BF)
        ct, s1, s2 = ct_ref[...], s1_ref[...], s2_ref[...]
        for grp in range(SWA_QKV // LANE):
            cols = slice(grp * LANE, (grp + 1) * LANE)
            z = _dot(h, w_ref[:, cols]) + b_ref[:, cols]
            if grp < n_rope:
                z = _rope128(z, ct, s1, s2, 8)
            o_ref[:, cols] = z.astype(BF)

    row = lambda i: (i, 0)
    return pl.pallas_call(
        body, name="swa_proj_fwd", grid=(T // tm,),
        out_shape=jax.ShapeDtypeStruct((T, SWA_QKV), BF),
        in_specs=[pl.BlockSpec((tm, D), row), _full((8, D)), _full((D, SWA_QKV)), _full((1, SWA_QKV)),
                  pl.BlockSpec((tm, LANE), row), pl.BlockSpec((tm, LANE), row), pl.BlockSpec((tm, LANE), row)],
        out_specs=pl.BlockSpec((tm, SWA_QKV), row),
        compiler_params=_params(1),
    )(x, modl, w, b, ct_b, s1_b, s2_b)


SWA_NB = 1
SWA_R = SWA_NB * 4 * SWA_W
SWA_NK = (SWA_NB + 1) * SWA_W


def _swa_bias(first):
    row = lax.broadcasted_iota(jnp.int32, (SWA_R, SWA_NK), 0)
    col = lax.broadcasted_iota(jnp.int32, (SWA_R, SWA_NK), 1)
    bl = row // (4 * SWA_W)
    r = row % SWA_W
    cp = col - bl * SWA_W
    band = (cp > r) & (cp <= r + SWA_W)
    if first:
        band = band & ((col >= SWA_W) | (bl > 0))
    return jnp.where(band, 0.0, -jnp.inf).astype(F32)


def _swa_fill_bias(bias_s):
    @pl.when(pl.program_id(0) == 0)
    def _():
        bias_s[0] = _swa_bias(False)
        bias_s[1] = _swa_bias(True)


def _swa_specs(T, tb):
    nsub = tb // SWA_W
    q_spec = pl.BlockSpec((tb, SWA_O), lambda i: (i, 0))
    kvc_spec = pl.BlockSpec((tb, 2 * SWA_HKV * LANE), lambda i: (i, 2))
    kvp_spec = pl.BlockSpec((SWA_W, 2 * SWA_HKV * LANE), lambda i: (jnp.maximum(i * nsub - 1, 0), 2))
    return q_spec, kvc_spec, kvp_spec


def _swa_rows(ref, g, c):
    return jnp.concatenate(
        [ref[(c * SWA_NB + bl) * SWA_W:(c * SWA_NB + bl + 1) * SWA_W, (4 * g + hh) * LANE:(4 * g + hh + 1) * LANE]
         for bl in range(SWA_NB) for hh in range(4)], axis=0)


def _swa_chain(q_ref, kall, vall, sink_ref, bias_s, g, c):
    i = pl.program_id(0)
    lane = lax.broadcasted_iota(jnp.int32, (1, LANE), 1)
    qc = _swa_rows(q_ref, g, c)
    keys = slice(c * SWA_NB * SWA_W, c * SWA_NB * SWA_W + SWA_NK)
    k3 = kall[keys]
    v3 = jnp.where(lane < 64, vall[keys], jnp.ones((), BF))
    bias = bias_s[jnp.where(i == 0, 1, 0)] if c == 0 else bias_s[0]
    s = _dot_nt(qc, k3)
    c_exp = SWA_SCALE * math.log2(math.e)
    sb = [s[:, ch * LANE:(ch + 1) * LANE] * c_exp + bias[:, ch * LANE:(ch + 1) * LANE] for ch in range(SWA_NK // LANE)]
    mx = sb[0]
    for x in sb[1:]:
        mx = jnp.maximum(mx, x)
    sink2 = jnp.concatenate([jnp.full((SWA_W, 1), sink_ref[4 * g + hh] * math.log2(math.e), F32)
                             for _ in range(SWA_NB) for hh in range(4)], axis=0)
    m = jnp.maximum(jnp.max(mx, axis=1, keepdims=True), sink2)
    m_rep = jnp.broadcast_to(m, (SWA_R, LANE))
    eb = jnp.concatenate([jnp.exp2(x - m_rep) for x in sb], axis=1).astype(BF)
    es = jnp.exp2(sink2 - m)
    acc = _dot(eb, v3)
    linv = 1.0 / (acc[:, 64:65] + es)
    num = jnp.where(lane < 64, acc, 0.0)
    return qc, k3, v3, eb, num, linv, es


def _swa_attn_fwd(qkv, sinks, tb):
    T = qkv.shape[0]
    kw = SWA_HKV * LANE

    def body(q_ref, kvc_ref, kvp_ref, sink_ref, o_ref, bias_s):
        _swa_fill_bias(bias_s)
        for g in range(SWA_HKV):
            gl = slice(g * LANE, (g + 1) * LANE)
            gv = slice(kw + g * LANE, kw + (g + 1) * LANE)
            kall = jnp.concatenate([kvp_ref[:, gl], kvc_ref[:, gl]], axis=0)
            vall = jnp.concatenate([kvp_ref[:, gv], kvc_ref[:, gv]], axis=0)
            for c in range(tb // (SWA_NB * SWA_W)):
                _, _, _, _, num, linv, _ = _swa_chain(q_ref, kall, vall, sink_ref, bias_s, g, c)
                o = (num * linv).astype(BF)
                for bl in range(SWA_NB):
                    for hh in range(4):
                        piece = (bl * 4 + hh) * SWA_W
                        rows = slice((c * SWA_NB + bl) * SWA_W, (c * SWA_NB + bl + 1) * SWA_W)
                        o_ref[rows, (4 * g + hh) * LANE:(4 * g + hh + 1) * LANE] = o[piece:piece + SWA_W]

    q_spec, kvc_spec, kvp_spec = _swa_specs(T, tb)
    return pl.pallas_call(
        body, name="swa_attn_fwd", grid=(T // tb,),
        out_shape=jax.ShapeDtypeStruct((T, SWA_O), BF),
        in_specs=[q_spec, kvc_spec, kvp_spec, pl.BlockSpec(memory_space=pltpu.SMEM)],
        out_specs=pl.BlockSpec((tb, SWA_O), lambda i: (i, 0)),
        scratch_shapes=[pltpu.VMEM((2, SWA_R, SWA_NK), F32)],
        compiler_params=_params(1),
    )(qkv, qkv, qkv, sinks)


def _loss_grad(x, tgt, tm):
    T = x.shape[0]

    def body(x_ref, t_ref, dx_ref, l_ref):
        @pl.when(pl.program_id(0) == 0)
        def _():
            l_ref[...] = jnp.zeros(l_ref.shape, F32)
        diff = x_ref[...] - t_ref[...]
        dx_ref[...] = diff * (1.0 / D)
        l_ref[0:1, :] += jnp.sum(diff * diff, axis=0, keepdims=True) * (0.5 / D)

    row = lambda i: (i, 0)
    return pl.pallas_call(
        body, name="loss_grad", grid=(T // tm,),
        out_shape=(jax.ShapeDtypeStruct((T, D), F32), jax.ShapeDtypeStruct((8, D), F32)),
        in_specs=[pl.BlockSpec((tm, D), row), pl.BlockSpec((tm, D), row)],
        out_specs=(pl.BlockSpec((tm, D), row), _full((8, D))),
        compiler_params=_params(1),
    )(x, tgt)


def _outproj_ln_bwd(dxn, u, y, w, modl, g_row, ln_g, tm, name):
    T = dxn.shape[0]
    K = w.shape[0]

    def body(dxn_ref, u_ref, y_ref, w_ref, modl_ref, g_ref, dres_ref, dy_ref, da_ref, sm_ref):
        @pl.when(pl.program_id(0) == 0)
        def _():
            sm_ref[...] = jnp.zeros(sm_ref.shape, F32)
        uu = u_ref[...]
        mu = jnp.mean(uu, axis=1, keepdims=True)
        uc = uu - mu
        rstd = lax.rsqrt(jnp.mean(uc * uc, axis=1, keepdims=True) + LN_EPS)
        xhat = uc * rstd
        dxo = dxn_ref[...]
        dyh = dxo * g_ref[...]
        du = rstd * (dyh - jnp.mean(dyh, axis=1, keepdims=True)
                     - xhat * jnp.mean(dyh * xhat, axis=1, keepdims=True))
        dy = modl_ref[g_row:g_row + 1, :] * du
        dyb = dy.astype(BF)
        dres_ref[...] = ALPHA * du
        dy_ref[...] = dyb
        da_ref[...] = _dot_nt(dyb, w_ref[...]).astype(BF)
        sm_ref[0:1, :] += jnp.sum(dxo * xhat, axis=0, keepdims=True)
        sm_ref[1:2, :] += jnp.sum(dxo, axis=0, keepdims=True)
        sm_ref[2:3, :] += jnp.sum(du * y_ref[...].astype(F32), axis=0, keepdims=True)
        sm_ref[3:4, :] += jnp.sum(dy, axis=0, keepdims=True)

    row = lambda i: (i, 0)
    return pl.pallas_call(
        body, name=name, grid=(T // tm,),
        out_shape=(jax.ShapeDtypeStruct((T, D), F32), jax.ShapeDtypeStruct((T, D), BF),
                   jax.ShapeDtypeStruct((T, K), BF), jax.ShapeDtypeStruct((8, D), F32)),
        in_specs=[pl.BlockSpec((tm, D), row), pl.BlockSpec((tm, D), row), pl.BlockSpec((tm, D), row),
                  _resident((K, D)), _full((8, D)), _full((1, D))],
        out_specs=(pl.BlockSpec((tm, D), row), pl.BlockSpec((tm, D), row), pl.BlockSpec((tm, K), row),
                   _full((8, D))),
        compiler_params=_params(1),
    )(dxn, u, y, w, modl, ln_g)


def _ffn_mid_bwd(da, g, u, x, modl, dres, wg, wu, tm, tn):
    T = x.shape[0]
    nn = F // tn

    def body(da_ref, g_ref, u_ref, x_ref, modl_ref, dres_ref, wg_ref, wu_ref, dg_ref, du_ref, dx_ref, sm_ref):
        i, n = pl.program_id(0), pl.program_id(1)

        @pl.when((i == 0) & (n == 0))
        def _():
            sm_ref[...] = jnp.zeros(sm_ref.shape, F32)

        gg = g_ref[...].astype(F32)
        sg = _sigmoid(gg)
        dav = da_ref[...].astype(F32)
        dgp = (dav * u_ref[...].astype(F32) * sg * (1.0 + gg * (1.0 - sg))).astype(BF)
        dup = (dav * gg * sg).astype(BF)
        dg_ref[...] = dgp
        du_ref[...] = dup
        dh = _dot_nt(dgp, wg_ref[...]) + _dot_nt(dup, wu_ref[...])

        @pl.when(n == 0)
        def _():
            dx_ref[...] = dh

        @pl.when(n > 0)
        def _():
            dx_ref[...] += dh

        @pl.when(n == nn - 1)
        def _():
            dht = dx_ref[...]
            sm_ref[0:1, :] += jnp.sum(dht * x_ref[...], axis=0, keepdims=True)
            sm_ref[1:2, :] += jnp.sum(dht, axis=0, keepdims=True)
            dx_ref[...] = dres_ref[...] + dht * (1.0 + modl_ref[4:5, :])

    tile = pl.BlockSpec((tm, tn), lambda i, n: (i, n))
    rowd = pl.BlockSpec((tm, D), lambda i, n: (i, 0))
    rowd_once = pl.BlockSpec((tm, D), lambda i, n: (i, 0), pipeline_mode=pl.Buffered(1))
    wcol = pl.BlockSpec((D, tn), lambda i, n: (0, n))
    return pl.pallas_call(
        body, name="ffn_mid_bwd", grid=(T // tm, nn),
        out_shape=(jax.ShapeDtypeStruct((T, F), BF), jax.ShapeDtypeStruct((T, F), BF),
                   jax.ShapeDtypeStruct((T, D), F32), jax.ShapeDtypeStruct((8, D), F32)),
        in_specs=[tile, tile, tile, rowd_once, _full((8, D)), rowd_once, wcol, wcol],
        out_specs=(tile, tile, rowd, _full((8, D))),
        compiler_params=_params(2),
    )(da, g, u, x, modl, dres, wg, wu)


def _wgrad(a, b, tm, tk, tn, name, modl=None, rows=None):
    T, K = a.shape
    N = b.shape[1]

    def body(*refs):
        if modl is None:
            a_ref, b_ref, o_ref = refs
            av = a_ref[...]
        else:
            a_ref, modl_ref, b_ref, o_ref = refs
            av = _modulate(a_ref[...], modl_ref, rows[0], rows[1]).astype(BF)

        @pl.when(pl.program_id(2) == 0)
        def _():
            o_ref[...] = jnp.zeros(o_ref.shape, F32)
        o_ref[...] += _dot_tn(av, b_ref[...])

    in_specs = [pl.BlockSpec((tm, tk), lambda k, n, t: (t, k))]
    args = [a]
    if modl is not None:
        in_specs.append(_full((8, D)))
        args.append(modl)
    in_specs.append(pl.BlockSpec((tm, tn), lambda k, n, t: (t, n)))
    args.append(b)
    return pl.pallas_call(
        body, name=name, grid=(K // tk, N // tn, T // tm),
        out_shape=jax.ShapeDtypeStruct((K, N), F32),
        in_specs=in_specs,
        out_specs=pl.BlockSpec((tk, tn), lambda k, n, t: (k, n)),
        compiler_params=_params(3),
    )(*args)


def _flash_delta(o, do, tb):
    T = o.shape[0]
    H = o.shape[1] // LANE

    def body(o_ref, do_ref, delta_ref):
        delta = jnp.sum(o_ref[...].astype(F32) * do_ref[...].astype(F32), axis=1, keepdims=True)
        delta_ref[0] = _col_to_row(delta)

    blk = pl.BlockSpec((tb, LANE), lambda h, i: (i, h))
    return pl.pallas_call(
        body, name="flash_delta", grid=(H, T // tb),
        out_shape=jax.ShapeDtypeStruct((H, 1, T), F32),
        in_specs=[blk, blk],
        out_specs=pl.BlockSpec((1, 1, tb), lambda h, i: (h, 0, i)),
        compiler_params=_params(2),
    )(o, do)


def _flash_bwd(q, k, v, do, lse, delta, tb, comm_src=None):
    H, T, _ = q.shape
    nq = T // tb
    rh = min(256, tb)
    c_exp = MLA_SCALE * math.log2(math.e)

    def body(*refs):
        if comm_src is None:
            k_ref, v_ref, q_ref, do_ref, lse_ref, delta_ref, dq_ref, dk_ref, dv_ref, dk_s, dv_s = refs
        else:
            (k_ref, v_ref, q_ref, do_ref, lse_ref, delta_ref, src_ref, dq_ref, dk_ref, dv_ref, got_ref,
             dk_s, dv_s, *sems) = refs
            comm = (src_ref, got_ref, *sems)
            first = (pl.program_id(0) == 0) & (pl.program_id(1) == 0)
            last = (pl.program_id(0) == H - 1) & (pl.program_id(1) == nq - 1)
            pl.when(first)(functools.partial(_exchange_start, *comm, gather=False))
        j = pl.program_id(1)

        @pl.when(j == 0)
        def _():
            def zero(cix, carry):
                dq_ref[0, pl.ds(pl.multiple_of(cix * tb, tb), tb), :] = jnp.zeros((tb, MLA_HD), F32)
                return carry
            lax.fori_loop(0, nq, zero, 0)

        dk_s[...] = jnp.zeros(dk_s.shape, F32)
        dv_s[...] = jnp.zeros(dv_s.shape, F32)

        def block(i, masked):
            dq_part = None
            for hf in range(tb // rh):
                r = slice(hf * rh, (hf + 1) * rh)
                off = hf * rh if masked else 0
                qrows = pl.ds(pl.multiple_of(i * tb + off, rh), tb - off)
                qb = q_ref[0, qrows, :]
                dob = do_ref[qrows, :]
                l2 = lse_ref[0, :, qrows] * math.log2(math.e)
                dl = delta_ref[0, :, qrows]
                kc = k_ref[0, r, :]
                st = _dot_nt(kc, qb)
                if masked:
                    st = jnp.where(_causal_mask(0, rh, tb - off, True), st, -jnp.inf)
                pt = jnp.exp2(st * c_exp - l2)
                dpt = _dot_nt(v_ref[0, r, :], dob)
                dst = (pt * (dpt - dl)).astype(BF)
                dv_s[r, :] += _dot(pt.astype(BF), dob)
                dk_s[r, :] += _dot(dst, qb)
                part = _dot_tn(dst, kc)
                if masked:
                    dq_ref[0, qrows, :] += part * MLA_SCALE
                else:
                    dq_part = part if dq_part is None else dq_part + part
            if not masked:
                dq_ref[0, pl.ds(pl.multiple_of(i * tb, tb), tb), :] += dq_part * MLA_SCALE

        def step(i, carry):
            block(i, False)
            return carry

        block(j, True)
        lax.fori_loop(j + 1, nq, step, 0)
        dk_ref[0] = (dk_s[...] * MLA_SCALE).astype(BF)
        dv_ref[0] = dv_s[...].astype(BF)
        if comm_src is not None:
            pl.when(last)(functools.partial(_exchange_wait, *comm, gather=False))

    once = pl.Buffered(1)
    hbm = pl.BlockSpec(memory_space=pltpu.HBM)
    with_comm = comm_src is not None
    return pl.pallas_call(
        body, name="flash_bwd_scatter" if with_comm else "flash_bwd", grid=(H, nq),
        out_shape=(jax.ShapeDtypeStruct((H, T, MLA_HD), F32), jax.ShapeDtypeStruct((H, T, MLA_HD), BF),
                   jax.ShapeDtypeStruct((H, T, LANE), BF)) + ((_exchange_out(comm_src, False),) if with_comm else ()),
        in_specs=[pl.BlockSpec((1, tb, MLA_HD), lambda h, j: (h, j, 0)),
                  pl.BlockSpec((1, tb, LANE), lambda h, j: (h, j, 0)),
                  pl.BlockSpec((1, T, MLA_HD), lambda h, j: (h, 0, 0), pipeline_mode=once),
                  pl.BlockSpec((T, LANE), lambda h, j: (0, h), pipeline_mode=once),
                  pl.BlockSpec((1, 1, T), lambda h, j: (h, 0, 0)),
                  pl.BlockSpec((1, 1, T), lambda h, j: (h, 0, 0))] + ([hbm] if with_comm else []),
        out_specs=(pl.BlockSpec((1, T, MLA_HD), lambda h, j: (h, 0, 0), pipeline_mode=once),
                   pl.BlockSpec((1, tb, MLA_HD), lambda h, j: (h, j, 0)),
                   pl.BlockSpec((1, tb, LANE), lambda h, j: (h, j, 0))) + ((hbm,) if with_comm else ()),
        scratch_shapes=[pltpu.VMEM((tb, MLA_HD), F32), pltpu.VMEM((tb, LANE), F32)]
        + (EXCHANGE_SEMS if with_comm else []),
        compiler_params=_params(2),
    )(*((k, v, q, do, lse, delta, comm_src) if with_comm else (k, v, q, do, lse, delta)))


def _mla_proj_bwd(dq, dk, dv, x, modl, dres, w_in, q_norm, w_q, kv_norm, w_kv, tabs_neg, tm):
    T = x.shape[0]
    ct_a, s1_n, s2_n = tabs_neg

    def body(dq_ref, dk_ref, dv_ref, x_ref, modl_ref, dres_ref, win_ref, qn_ref, wq_ref, kvn_ref, wkv_ref,
             ct_ref, s1_ref, s2_ref, dx_ref, dwin_ref, dwq_ref, dwkv_ref, sm_ref, dqn_ref, dkvn_ref):
        @pl.when(pl.program_id(0) == 0)
        def _():
            for r in (dwin_ref, dwq_ref, dwkv_ref, sm_ref, dqn_ref, dkvn_ref):
                r[...] = jnp.zeros(r.shape, F32)

        xv = x_ref[...]
        h = _modulate(xv, modl_ref, 1, 0).astype(BF)
        lat = _dot(h, win_ref[...])
        ql, kvl = lat[:, :MLA_QR], lat[:, MLA_QR:MLA_QR + MLA_KVR]
        qhat = ql * lax.rsqrt(jnp.mean(ql * ql, axis=1, keepdims=True) + RMS_EPS)
        kvhat = kvl * lax.rsqrt(jnp.mean(kvl * kvl, axis=1, keepdims=True) + RMS_EPS)
        rq = lax.rsqrt(jnp.mean(ql * ql, axis=1, keepdims=True) + RMS_EPS)
        rkv = lax.rsqrt(jnp.mean(kvl * kvl, axis=1, keepdims=True) + RMS_EPS)
        qn = (qhat * qn_ref[...]).astype(BF)
        kvn = (kvhat * kvn_ref[...]).astype(BF)
        ct, s1, s2 = ct_ref[...], s1_ref[...], s2_ref[...]

        dqn = jnp.zeros((tm, MLA_QR), F32)
        dkvn = jnp.zeros((tm, MLA_KVR), F32)
        dkr = jnp.zeros((tm, LANE), F32)
        for hd in range(MLA_H):
            cols = slice(hd * MLA_HD, (hd + 1) * MLA_HD)
            dqh = dq_ref[hd]
            dqr = _rope128(dqh[:, LANE:], ct, s1, s2, MLA_ROPE // 2).astype(BF)
            dqh = jnp.concatenate([dqh[:, :LANE].astype(BF), dqr], axis=1)
            dqn = dqn + _dot_nt(dqh, wq_ref[:, cols])
            dwq_ref[:, cols] += _dot_tn(qn, dqh)
            dkh = dk_ref[hd]
            dkr = dkr + dkh[:, LANE:].astype(F32)
            dkvh = jnp.concatenate([dkh[:, :LANE], dv_ref[hd]], axis=1)
            dkvn = dkvn + _dot_nt(dkvh, wkv_ref[:, cols])
            dwkv_ref[:, cols] += _dot_tn(kvn, dkvh)
        dkr = _rope128(dkr, ct, s1, s2, MLA_ROPE // 2)

        dqn_ref[...] += jnp.sum(dqn * qhat, axis=0, keepdims=True)
        dkvn_ref[...] += jnp.sum(dkvn * kvhat, axis=0, keepdims=True)
        dqh_ = dqn * qn_ref[...]
        dkvh_ = dkvn * kvn_ref[...]
        dql = rq * (dqh_ - qhat * jnp.mean(dqh_ * qhat, axis=1, keepdims=True))
        dkvl = rkv * (dkvh_ - kvhat * jnp.mean(dkvh_ * kvhat, axis=1, keepdims=True))
        dlat = jnp.concatenate([dql, dkvl, dkr], axis=1).astype(BF)
        dwin_ref[...] += _dot_tn(h, dlat)
        dh = _dot_nt(dlat, win_ref[...])
        sm_ref[0:1, :] += jnp.sum(dh * xv, axis=0, keepdims=True)
        sm_ref[1:2, :] += jnp.sum(dh, axis=0, keepdims=True)
        dx_ref[...] = dres_ref[...] + dh * (1.0 + modl_ref[1:2, :])

    row = lambda i: (i, 0)
    head = lambda i: (0, i, 0)
    nq = MLA_H * MLA_HD
    return pl.pallas_call(
        body, name="mla_proj_bwd", grid=(T // tm,),
        out_shape=(jax.ShapeDtypeStruct((T, D), F32), jax.ShapeDtypeStruct((D, MLA_LAT), F32),
                   jax.ShapeDtypeStruct((MLA_QR, nq), F32), jax.ShapeDtypeStruct((MLA_KVR, nq), F32),
                   jax.ShapeDtypeStruct((8, D), F32), jax.ShapeDtypeStruct((1, MLA_QR), F32),
                   jax.ShapeDtypeStruct((1, MLA_KVR), F32)),
        in_specs=[pl.BlockSpec((MLA_H, tm, MLA_HD), head), pl.BlockSpec((MLA_H, tm, MLA_HD), head),
                  pl.BlockSpec((MLA_H, tm, LANE), head), pl.BlockSpec((tm, D), row), _full((8, D)),
                  pl.BlockSpec((tm, D), row), _full((D, MLA_LAT)), _full((1, MLA_QR)), _full((MLA_QR, nq)),
                  _full((1, MLA_KVR)), _full((MLA_KVR, nq)),
                  pl.BlockSpec((tm, LANE), row), pl.BlockSpec((tm, LANE), row), pl.BlockSpec((tm, LANE), row)],
        out_specs=(pl.BlockSpec((tm, D), row), _full((D, MLA_LAT)), _full((MLA_QR, nq)), _full((MLA_KVR, nq)),
                   _full((8, D)), _full((1, MLA_QR)), _full((1, MLA_KVR))),
        compiler_params=_params(1),
    )(dq, dk, dv, x, modl, dres, w_in, q_norm, w_q, kv_norm, w_kv, ct_a, s1_n, s2_n)


def _swa_attn_bwd(qkv, sinks, do, tb):
    T = qkv.shape[0]
    kw = SWA_HKV * LANE
    nstep = T // tb

    def body(q_ref, kvc_ref, kvp_ref, sink_ref, do_ref, dq_ref, dkvc_ref, dkvp_ref, dsink_ref, dk_s, dv_s, bias_s):
        i = pl.program_id(0)
        _swa_fill_bias(bias_s)

        @pl.when(i == 0)
        def _():
            dsink_ref[...] = jnp.zeros(dsink_ref.shape, F32)

        dk_s[...] = jnp.zeros(dk_s.shape, F32)
        dv_s[...] = jnp.zeros(dv_s.shape, F32)
        lane = lax.broadcasted_iota(jnp.int32, (1, LANE), 1)
        for g in range(SWA_HKV):
            gl = slice(g * LANE, (g + 1) * LANE)
            gv = slice(kw + g * LANE, kw + (g + 1) * LANE)
            kall = jnp.concatenate([kvp_ref[:, gl], kvc_ref[:, gl]], axis=0)
            vall = jnp.concatenate([kvp_ref[:, gv], kvc_ref[:, gv]], axis=0)
            for c in range(tb // (SWA_NB * SWA_W)):
                qc, k3, v3, eb, num, linv, es = _swa_chain(q_ref, kall, vall, sink_ref, bias_s, g, c)
                doc = _swa_rows(do_ref, g, c)
                delta = jnp.sum(doc.astype(F32) * (num * linv), axis=1, keepdims=True)
                delta_rep = jnp.broadcast_to(delta, (SWA_R, LANE))
                dp = _dot_nt(doc, v3)
                linv_rep = jnp.broadcast_to(linv, (SWA_R, LANE))
                pch = [eb[:, ch * LANE:(ch + 1) * LANE].astype(F32) * linv_rep for ch in range(SWA_NK // LANE)]
                ds = jnp.concatenate([pch[ch] * (dp[:, ch * LANE:(ch + 1) * LANE] - delta_rep)
                                      for ch in range(SWA_NK // LANE)], axis=1).astype(BF)
                pb = jnp.concatenate(pch, axis=1).astype(BF)
                dqc = (_dot(ds, k3) * SWA_SCALE).astype(BF)
                keys = slice(c * SWA_NB * SWA_W, c * SWA_NB * SWA_W + SWA_NK)
                dk_s[keys, gl] += _dot_tn(ds, qc) * SWA_SCALE
                dv_s[keys, gl] += _dot_tn(pb, doc)
                dsk = es * linv * delta
                for hh in range(4):
                    hq = 4 * g + hh
                    tot = jnp.zeros((1, 1), F32)
                    for bl in range(SWA_NB):
                        piece = (bl * 4 + hh) * SWA_W
                        rows = slice((c * SWA_NB + bl) * SWA_W, (c * SWA_NB + bl + 1) * SWA_W)
                        dq_ref[rows, hq * LANE:(hq + 1) * LANE] = dqc[piece:piece + SWA_W]
                        tot = tot + jnp.sum(dsk[piece:piece + SWA_W], axis=0, keepdims=True)
                    dsink_ref[0:1, :] -= jnp.where(lane == hq, tot, 0.0)
        dkvp_ref[0, :, 0:kw] = dk_s[0:SWA_W, :]
        dkvp_ref[0, :, kw:2 * kw] = dv_s[0:SWA_W, :]
        dkvc_ref[:, 0:kw] = dk_s[SWA_W:, :]
        dkvc_ref[:, kw:2 * kw] = dv_s[SWA_W:, :]

    q_spec, kvc_spec, kvp_spec = _swa_specs(T, tb)
    return pl.pallas_call(
        body, name="swa_attn_bwd", grid=(nstep,),
        out_shape=(jax.ShapeDtypeStruct((T, SWA_O), BF), jax.ShapeDtypeStruct((T, 2 * kw), F32),
                   jax.ShapeDtypeStruct((nstep, SWA_W, 2 * kw), F32), jax.ShapeDtypeStruct((8, LANE), F32)),
        in_specs=[q_spec, kvc_spec, kvp_spec, pl.BlockSpec(memory_space=pltpu.SMEM),
                  pl.BlockSpec((tb, SWA_O), lambda i: (i, 0))],
        out_specs=(pl.BlockSpec((tb, SWA_O), lambda i: (i, 0)), pl.BlockSpec((tb, 2 * kw), lambda i: (i, 0)),
                   pl.BlockSpec((1, SWA_W, 2 * kw), lambda i: (i, 0, 0)), _full((8, LANE))),
        scratch_shapes=[pltpu.VMEM((tb + SWA_W, kw), F32), pltpu.VMEM((tb + SWA_W, kw), F32),
                        pltpu.VMEM((2, SWA_R, SWA_NK), F32)],
        compiler_params=_params(1),
    )(qkv, qkv, qkv, sinks, do)


def _swa_proj_bwd(dq, dkvc, dkvp, x, modl, dres, w, tabs_neg, tm):
    T = x.shape[0]
    nstep = T // tm
    kw = SWA_HKV * LANE
    ct_b, s1_n, s2_n = tabs_neg

    def body(dq_ref, dkvc_ref, dkvp_ref, x_ref, modl_ref, dres_ref, w_ref, ct_ref, s1_ref, s2_ref,
             dx_ref, dz_ref, sm_ref, db_ref):
        i = pl.program_id(0)

        @pl.when(i == 0)
        def _():
            sm_ref[...] = jnp.zeros(sm_ref.shape, F32)
            db_ref[...] = jnp.zeros(db_ref.shape, F32)

        ct, s1, s2 = ct_ref[...], s1_ref[...], s2_ref[...]
        has_next = i + 1 < nstep
        for grp in range(SWA_QKV // LANE):
            cols = slice(grp * LANE, (grp + 1) * LANE)
            if grp < SWA_HQ:
                z = dq_ref[:, cols].astype(F32)
            else:
                kc = slice((grp - SWA_HQ) * LANE, (grp - SWA_HQ + 1) * LANE)
                cur = dkvc_ref[:, kc]
                tail = cur[tm - SWA_W:] + jnp.where(has_next, dkvp_ref[0, :, kc], 0.0)
                z = jnp.concatenate([cur[:tm - SWA_W], tail], axis=0)
            if grp < SWA_HQ + SWA_HKV:
                z = _rope128(z, ct, s1, s2, 8)
            db_ref[0:1, cols] += jnp.sum(z, axis=0, keepdims=True)
            dz_ref[:, cols] = z.astype(BF)
        dh = _dot_nt(dz_ref[...], w_ref[...])
        sm_ref[0:1, :] += jnp.sum(dh * x_ref[...], axis=0, keepdims=True)
        sm_ref[1:2, :] += jnp.sum(dh, axis=0, keepdims=True)
        dx_ref[...] = dres_ref[...] + dh * (1.0 + modl_ref[1:2, :])

    row = lambda i: (i, 0)
    return pl.pallas_call(
        body, name="swa_proj_bwd", grid=(nstep,),
        out_shape=(jax.ShapeDtypeStruct((T, D), F32), jax.ShapeDtypeStruct((T, SWA_QKV), BF),
                   jax.ShapeDtypeStruct((8, D), F32), jax.ShapeDtypeStruct((8, SWA_QKV), F32)),
        in_specs=[pl.BlockSpec((tm, SWA_O), row), pl.BlockSpec((tm, 2 * kw), row),
                  pl.BlockSpec((1, SWA_W, 2 * kw), lambda i: (jnp.minimum(i + 1, nstep - 1), 0, 0)),
                  pl.BlockSpec((tm, D), row), _full((8, D)), pl.BlockSpec((tm, D), row), _full((D, SWA_QKV)),
                  pl.BlockSpec((tm, LANE), row), pl.BlockSpec((tm, LANE), row), pl.BlockSpec((tm, LANE), row)],
        out_specs=(pl.BlockSpec((tm, D), row), pl.BlockSpec((tm, SWA_QKV), row), _full((8, D)),
                   _full((8, SWA_QKV))),
        compiler_params=_params(1),
    )(dq, dkvc, dkvp, x, modl, dres, w, ct_b, s1_n, s2_n)


def _adamw(gparts, w, m, v, name):
    P, R, C = gparts.shape
    tr = R
    for cand in (512, 256, 128):
        if R % cand == 0 and R > cand:
            tr = cand
            break
    c1 = 1.0 / (1.0 - ADAM_B1 ** ADAM_STEP)
    c2 = 1.0 / (1.0 - ADAM_B2 ** ADAM_STEP)

    def body(gp_ref, w_ref, m_ref, v_ref, g_ref, d_ref, nm_ref, nv_ref):
        g = gp_ref[0].astype(F32)
        for p in range(1, P):
            g = g + gp_ref[p].astype(F32)
        nm = ADAM_B1 * m_ref[...] + (1.0 - ADAM_B1) * g
        nv = ADAM_B2 * v_ref[...] + (1.0 - ADAM_B2) * (g * g)
        g_ref[...] = g
        nm_ref[...] = nm
        nv_ref[...] = nv
        d_ref[...] = -ADAM_LR * ((nm * c1) / (jnp.sqrt(nv * c2) + ADAM_EPS) + ADAM_WD * w_ref[...])

    blk = pl.BlockSpec((tr, C), lambda i: (i, 0))
    return pl.pallas_call(
        body, name=name, grid=(R // tr,),
        out_shape=(jax.ShapeDtypeStruct((R, C), F32),) * 4,
        in_specs=[pl.BlockSpec((P, tr, C), lambda i: (0, i, 0)), blk, blk, blk],
        out_specs=(blk,) * 4,
        compiler_params=_params(1),
    )(gparts, w, m, v)


PACK_W = 1024

BIG = {
    "ffn_w_gate": ((DEPTH, D, F // NDEV), 2),
    "ffn_w_up": ((DEPTH, D, F // NDEV), 2),
    "ffn_w_down": ((DEPTH, F // NDEV, D), 1),
    "mla_w_in": ((2, D // NDEV, 704), 1),
    "mla_w_q_b": ((2, MLA_QR, 1536 // NDEV), 2),
    "mla_w_kv_b": ((2, MLA_KVR, 2048 // NDEV), 2),
    "mla_w_o": ((2, D // NDEV, D), 1),
    "swa_w_qkv": ((2, D, 1536 // NDEV), 2),
    "swa_w_o": ((2, D // NDEV, D), 1),
}


EARLY = [("mla_w_in", 0, 1), ("mla_w_q_b", 0, 1), ("mla_w_kv_b", 0, 1)]
LATE = [("ffn_w_gate", 0, 4), ("ffn_w_up", 0, 4), ("ffn_w_down", 0, 4), ("mla_w_in", 1, 2), ("mla_w_q_b", 1, 2),
        ("mla_w_kv_b", 1, 2), ("mla_w_o", 0, 2), ("swa_w_qkv", 0, 2), ("swa_w_o", 0, 2)]


def _pack_rows(n):
    return -(-n // (16 * PACK_W)) * 16


def _entry_shape(name, lo, hi):
    return (hi - lo,) + BIG[name][0][1:]


def _pack_local(inp, entries):
    parts = []
    for name, lo, hi in entries:
        n = math.prod(_entry_shape(name, lo, hi))
        flat = inp[name][lo:hi].astype(BF).reshape(-1)
        parts.append(jnp.pad(flat, (0, _pack_rows(n) * PACK_W - n)).reshape(-1, PACK_W))
    return jnp.concatenate(parts, axis=0)


def _pack_full(full, entries):
    parts = []
    for name, lo, hi in entries:
        a, axis = full[(name, lo)], BIG[name][1]
        split = a.shape[:axis] + (NDEV, a.shape[axis] // NDEV) + a.shape[axis + 1:]
        a = jnp.moveaxis(a.reshape(split), axis, 0).astype(BF).reshape(NDEV, -1)
        n = math.prod(_entry_shape(name, lo, hi))
        parts.append(jnp.pad(a, ((0, 0), (0, _pack_rows(n) * PACK_W - n))).reshape(NDEV, -1, PACK_W))
    return jnp.concatenate(parts, axis=1)


def _unpack_blocks(packed, entries):
    out, r0 = {}, 0
    for name, lo, hi in entries:
        shape = _entry_shape(name, lo, hi)
        n = math.prod(shape)
        rows = _pack_rows(n)
        out[(name, lo)] = packed[:, r0:r0 + rows].reshape(NDEV, -1)[:, :n].reshape((NDEV,) + shape)
        r0 += rows
    return out


def _unpack_full(packed, entries):
    out = {}
    for (name, lo), blk in _unpack_blocks(packed, entries).items():
        axis = BIG[name][1]
        a = jnp.moveaxis(blk, 0, axis)
        out[(name, lo)] = a.reshape(a.shape[:axis] + (a.shape[axis] * a.shape[axis + 1],) + a.shape[axis + 2:])
    return out


def _pad_heads(a, axis, nheads, width, to):
    shp = a.shape[:axis] + (nheads, width) + a.shape[axis + 1:]
    a = a.reshape(shp)
    pad = [(0, 0)] * a.ndim
    pad[axis + 1] = (0, to - width)
    a = jnp.pad(a, pad)
    return a.reshape(a.shape[:axis] + (nheads * to,) + a.shape[axis + 2:])


def _unpad_heads(a, axis, nheads, width, to):
    shp = a.shape[:axis] + (nheads, to) + a.shape[axis + 1:]
    a = lax.slice_in_dim(a.reshape(shp), 0, width, axis=axis + 1)
    return a.reshape(a.shape[:axis] + (nheads * width,) + a.shape[axis + 2:])


def _swa_pad_cols(a):
    return _pad_heads(a, a.ndim - 1, SWA_HQ + 2 * SWA_HKV, 64, LANE)


def _rope_tables(positions, half):
    rot = 2 * half
    inv = ROPE_THETA ** (-jnp.arange(0, rot, 2, dtype=F32) / rot)
    ang = positions.astype(F32)[:, None] * inv
    cos, sin = jnp.cos(ang), jnp.sin(ang)
    T = positions.shape[0]
    ones = jnp.ones((T, LANE - rot), F32)
    zeros = jnp.zeros((T, LANE - rot), F32)
    zh = jnp.zeros((T, half), F32)
    ct = jnp.concatenate([cos, cos, ones], axis=1)
    s1 = jnp.concatenate([-sin, zh, zeros], axis=1)
    s2 = jnp.concatenate([zh, sin, zeros], axis=1)
    return (ct, s1, s2), (ct, -s1, -s2)


def _small_pack(vecs, rows):
    flat = jnp.concatenate([v.astype(F32).reshape(-1) for v in vecs])
    return jnp.pad(flat, (0, rows * LANE - flat.shape[0])).reshape(rows, LANE)


def _small_unpack(buf, shapes):
    flat = buf.reshape(NDEV, -1)
    out, o = [], 0
    for shp in shapes:
        n = math.prod(shp)
        out.append(flat[:, o:o + n].reshape((NDEV,) + shp))
        o += n
    return out


def _step(inp):
    x = inp["x"][0]
    tgt = inp["loss_target"][0]
    T = x.shape[0]
    tm = min(512, T)
    tmf = min(512, T)
    tw = min(1024, T)
    tb = min(512, T)
    tbf = min(1024, T)
    tbq = min(2048, T)
    tnf = F // 2
    me = 4 * lax.axis_index("x") + 2 * lax.axis_index("y") + lax.axis_index("c")

    small_in = _small_pack([inp["c"], inp["swa_b_qkv"], inp["swa_b_o"]], 16)
    c_all, bqkv_blk, bo_blk = _small_unpack(_exchange(small_in, True, "gather_small"),
                                            [(D,), (2, 1536 // NDEV), (2, D // NDEV)])
    swa_b_qkv = jnp.moveaxis(bqkv_blk, 0, 1).reshape(2, 1536)
    swa_b_o = jnp.moveaxis(bo_blk, 0, 1).reshape(2, D)

    w_early = _unpack_full(_exchange(_pack_local(inp, EARLY), True, "gather_early"), EARLY)

    ncol = 6 * D // NDEV
    ada_b_loc = lax.dynamic_slice_in_dim(inp["ada_b"], me * ncol, ncol, axis=1)[:, None, :]
    mod_all = _mod_all(c_all, inp["ada_w"], ada_b_loc)
    mod_src = jnp.moveaxis(mod_all, 1, 0).reshape(NDEV, DEPTH * ncol // LANE, LANE)
    mod_got = _exchange(mod_src, False, "scatter_mod").reshape(NDEV, DEPTH, ncol)
    mod = jnp.moveaxis(mod_got, 0, 1).reshape(DEPTH, 6, D)
    modl = jnp.pad(mod, ((0, 0), (0, 2), (0, 0)))

    def mla_proj_weights(w, lo):
        return (jnp.pad(w[("mla_w_in", lo)][0], ((0, 0), (0, MLA_LAT - 704))),
                _pad_heads(w[("mla_w_q_b", lo)][0], 1, MLA_H, 192, MLA_HD), w[("mla_w_kv_b", lo)][0])

    w_in, w_q, w_kv = [[t] for t in mla_proj_weights(w_early, 0)]
    b_qkv = _swa_pad_cols(swa_b_qkv)
    zero_bias = jnp.zeros((1, D), F32)

    pos = inp["positions"][0]
    tabs_a, tabs_a_neg = _rope_tables(pos, MLA_ROPE // 2)
    tabs_b, tabs_b_neg = _rope_tables(pos, 8)

    saved = []
    xs = x
    for i in range(DEPTH):
        j = i // 2
        st = {"x0": xs}
        if i % 2 == 0:
            q, k, v = _mla_proj_fwd(xs, modl[i], w_in[j], inp["mla_q_norm"][j][None], w_q[j],
                                    inp["mla_kv_norm"][j][None], w_kv[j], tabs_a, tm)
            if i == 0:
                o, lse, got = _flash_fwd(q, k, v, tbq, _pack_local(inp, LATE))
                w_late = _unpack_full(got, LATE)
                for lst, t in zip((w_in, w_q, w_kv), mla_proj_weights(w_late, 1)):
                    lst.append(t)
                w_o_mla = w_late[("mla_w_o", 0)]
                w_qkv = _swa_pad_cols(w_late[("swa_w_qkv", 0)])
                w_o_swa = _pad_heads(w_late[("swa_w_o", 0)], 1, SWA_HQ, 64, LANE)
                w_gate, w_up, w_down = (w_late[(n, 0)] for n in ("ffn_w_gate", "ffn_w_up", "ffn_w_down"))
            else:
                o, lse = _flash_fwd(q, k, v, tbq)
            st.update(q=q, k=k, v=v, o=o, lse=lse)
            w_o, b_o = w_o_mla[j], zero_bias
        else:
            qkv = _swa_proj_fwd(xs, modl[i], w_qkv[j], b_qkv[j][None], tabs_b, tm)
            o = _swa_attn_fwd(qkv, inp["swa_sinks"][j], tb)
            st.update(qkv=qkv, o=o)
            w_o, b_o = w_o_swa[j], swa_b_o[j][None]
        y, u, xs = _outproj_ln_fwd(o, w_o, b_o, xs, modl[i], 2, inp["ln_mix_g"][i][None],
                                   inp["ln_mix_b"][i][None], tm, f"mix_out_fwd_{i % 2}")
        st.update(y_m=y, u_m=u, x1=xs, w_o=w_o)
        g, up, a = _ffn_up_fwd(xs, modl[i], w_gate[i], w_up[i], tmf, tnf)
        y, u, xs = _outproj_ln_fwd(a, w_down[i], zero_bias, xs, modl[i], 5,
                                   inp["ln_ffn_g"][i][None], inp["ln_ffn_b"][i][None], tmf, "ffn_out_fwd")
        st.update(g=g, up=up, a=a, y_f=y, u_f=u)
        saved.append(st)

    dx, loss_rows = _loss_grad(xs, tgt, tm)
    loss = lax.psum(jnp.sum(loss_rows[0]), ("x", "y", "c"))

    gfull = {n: [None] * (DEPTH if n.startswith("ffn") else 2) for n in BIG}
    dmod = [None] * DEPTH
    g_ln = {n: [None] * DEPTH for n in ("ln_mix_g", "ln_mix_b", "ln_ffn_g", "ln_ffn_b")}
    g_qn, g_kvn, g_sink, g_bqkv, g_bo = [None] * 2, [None] * 2, [None] * 2, [None] * 2, [None] * 2
    for i in reversed(range(DEPTH)):
        j = i // 2
        st = saved[i]
        dres, dy, da, sm = _outproj_ln_bwd(dx, st["u_f"], st["y_f"], w_down[i], modl[i], 5,
                                           inp["ln_ffn_g"][i][None], tmf, "ffn_out_bwd")
        g_ln["ln_ffn_g"][i], g_ln["ln_ffn_b"][i], dg_f = sm[0], sm[1], sm[2]
        gfull["ffn_w_down"][i] = _wgrad(st["a"], dy, tw, F // 2, D, "wgrad_down")
        dgp, dup, dx, sm = _ffn_mid_bwd(da, st["g"], st["up"], st["x1"], modl[i], dres,
                                        w_gate[i], w_up[i], tm, tnf)
        dsc_f, dsh_f = sm[0], sm[1]
        gfull["ffn_w_gate"][i] = _wgrad(st["x1"], dgp, tw, D, tnf, "wgrad_gate", modl[i], (4, 3))
        gfull["ffn_w_up"][i] = _wgrad(st["x1"], dup, tw, D, tnf, "wgrad_up", modl[i], (4, 3))

        dres, dy, do, sm = _outproj_ln_bwd(dx, st["u_m"], st["y_m"], st["w_o"], modl[i], 2,
                                           inp["ln_mix_g"][i][None], tm, f"mix_out_bwd_{i % 2}")
        g_ln["ln_mix_g"][i], g_ln["ln_mix_b"][i], dg_m = sm[0], sm[1], sm[2]
        if i % 2 == 0:
            gfull["mla_w_o"][j] = _wgrad(st["o"], dy, tw, D, D, "wgrad_mla_o")
            delta = _flash_delta(st["o"], do, tb)
            if i == 0:
                late = {(n, lo): jnp.stack(gfull[n][lo:hi]) for n, lo, hi in LATE}
                dq, dk, dv, got = _flash_bwd(st["q"], st["k"], st["v"], do, st["lse"], delta, tbf,
                                             _pack_full(late, LATE))
                gparts = _unpack_blocks(got, LATE)
            else:
                dq, dk, dv = _flash_bwd(st["q"], st["k"], st["v"], do, st["lse"], delta, tbf)
            dx, dwin, dwq, dwkv, sm, dqn, dkvn = _mla_proj_bwd(
                dq, dk, dv, st["x0"], modl[i], dres, w_in[j], inp["mla_q_norm"][j][None], w_q[j],
                inp["mla_kv_norm"][j][None], w_kv[j], tabs_a_neg, tm)
            gfull["mla_w_in"][j] = dwin[:, :704]
            gfull["mla_w_q_b"][j] = _unpad_heads(dwq, 1, MLA_H, 192, MLA_HD)
            gfull["mla_w_kv_b"][j] = dwkv
            g_qn[j], g_kvn[j] = dqn[0], dkvn[0]
        else:
            g_bo[j] = sm[3]
            dwo = _wgrad(st["o"], dy, tw, SWA_O // 2, D, "wgrad_swa_o")
            gfull["swa_w_o"][j] = _unpad_heads(dwo, 0, SWA_HQ, 64, LANE)
            dq, dkvc, dkvp, dsink = _swa_attn_bwd(st["qkv"], inp["swa_sinks"][j], do, tb)
            g_sink[j] = dsink[0, :SWA_HQ]
            dx, dz, sm, db = _swa_proj_bwd(dq, dkvc, dkvp, st["x0"], modl[i], dres, w_qkv[j], tabs_b_neg, tm)
            dwqkv = _wgrad(st["x0"], dz, tw, D, SWA_QKV // 2, "wgrad_swa_qkv", modl[i], (1, 0))
            gfull["swa_w_qkv"][j] = _unpad_heads(dwqkv, 1, SWA_HQ + 2 * SWA_HKV, 64, LANE)
            g_bqkv[j] = _unpad_heads(db[0], 0, SWA_HQ + 2 * SWA_HKV, 64, LANE)
        dmod[i] = jnp.stack([sm[1], sm[0], dg_m, dsh_f, dsc_f, dg_f])
    grad_x = dx[None]

    small_shapes = [(DEPTH, 6 * D), (DEPTH, D), (DEPTH, D), (DEPTH, D), (DEPTH, D), (2, MLA_QR), (2, MLA_KVR),
                    (2, SWA_HQ), (2, 1536), (2, D)]
    small_vals = [jnp.stack(dmod).reshape(DEPTH, 6 * D), jnp.stack(g_ln["ln_mix_g"]), jnp.stack(g_ln["ln_mix_b"]),
                  jnp.stack(g_ln["ln_ffn_g"]), jnp.stack(g_ln["ln_ffn_b"]), jnp.stack(g_qn), jnp.stack(g_kvn),
                  jnp.stack(g_sink), jnp.stack(g_bqkv), jnp.stack(g_bo)]
    nsmall = sum(math.prod(s) for s in small_shapes)
    small_rows = -(-nsmall // (8 * LANE)) * 8
    (dmod_all, p_lmg, p_lmb, p_lfg, p_lfb, p_qn, p_kvn, p_sink, p_bqkv, p_bo) = _small_unpack(
        _exchange(_small_pack(small_vals, small_rows), True, "gather_small_grads"), small_shapes)

    early = {(n, lo): jnp.stack(gfull[n][lo:hi]) for n, lo, hi in EARLY}
    gparts.update(_unpack_blocks(_exchange(_pack_full(early, EARLY), False, "scatter_early_grads"), EARLY))
    for n, _, _ in EARLY:
        gparts[(n, 0)] = jnp.concatenate([gparts[(n, 0)], gparts.pop((n, 1))], axis=1)

    res = {}

    def update(name, parts):
        w = inp[name]
        shp = w.shape
        r2 = (math.prod(shp[:-1]), shp[-1])
        outs = _adamw(parts.reshape((parts.shape[0],) + r2), w.reshape(r2), inp["m_" + name].reshape(r2),
                      inp["v_" + name].reshape(r2), "adamw_" + name)
        res[name] = tuple(o.reshape(shp) for o in outs)

    dmod_loc = lax.dynamic_slice_in_dim(dmod_all, me * ncol, ncol, axis=2)
    g_ada_w = _ada_w_grad(c_all.T, jnp.moveaxis(dmod_loc, 0, 1))
    update("ada_w", g_ada_w[None])
    update("ada_b", dmod_all)
    update("ln_mix_g", p_lmg)
    update("ln_mix_b", p_lmb)
    update("ln_ffn_g", p_lfg)
    update("ln_ffn_b", p_lfb)
    for name in BIG:
        update(name, gparts[(name, 0)])
    update("mla_q_norm", p_qn)
    update("mla_kv_norm", p_kvn)
    update("swa_sinks", p_sink)
    nb = 1536 // NDEV
    update("swa_b_qkv", lax.dynamic_slice_in_dim(p_bqkv, me * nb, nb, axis=2))
    update("swa_b_o", lax.dynamic_slice_in_dim(p_bo, me * (D // NDEV), D // NDEV, axis=2))
    return loss, grad_x, res


WEIGHTS = ["ada_w", "ada_b", "ln_mix_g", "ln_mix_b", "ln_ffn_g", "ln_ffn_b", "ffn_w_gate", "ffn_w_up",
           "ffn_w_down", "mla_w_in", "mla_q_norm", "mla_w_q_b", "mla_kv_norm", "mla_w_kv_b", "mla_w_o",
           "swa_w_qkv", "swa_b_qkv", "swa_sinks", "swa_w_o", "swa_b_o"]
INPUTS = (["x", "c", "positions"] + WEIGHTS + ["loss_target"] + ["m_" + n for n in WEIGHTS]
          + ["v_" + n for n in WEIGHTS])


def kernel(x, c, positions, ada_w, ada_b, ln_mix_g, ln_mix_b, ln_ffn_g, ln_ffn_b, ffn_w_gate, ffn_w_up, ffn_w_down, mla_w_in, mla_q_norm, mla_w_q_b, mla_kv_norm, mla_w_kv_b, mla_w_o, swa_w_qkv, swa_b_qkv, swa_sinks, swa_w_o, swa_b_o, loss_target, m_ada_w, m_ada_b, m_ln_mix_g, m_ln_mix_b, m_ln_ffn_g, m_ln_ffn_b, m_ffn_w_gate, m_ffn_w_up, m_ffn_w_down, m_mla_w_in, m_mla_q_norm, m_mla_w_q_b, m_mla_kv_norm, m_mla_w_kv_b, m_mla_w_o, m_swa_w_qkv, m_swa_b_qkv, m_swa_sinks, m_swa_w_o, m_swa_b_o, v_ada_w, v_ada_b, v_ln_mix_g, v_ln_mix_b, v_ln_ffn_g, v_ln_ffn_b, v_ffn_w_gate, v_ffn_w_up, v_ffn_w_down, v_mla_w_in, v_mla_q_norm, v_mla_w_q_b, v_mla_kv_norm, v_mla_w_kv_b, v_mla_w_o, v_swa_w_qkv, v_swa_b_qkv, v_swa_sinks, v_swa_w_o, v_swa_b_o):
    args = (x, c, positions, ada_w, ada_b, ln_mix_g, ln_mix_b, ln_ffn_g, ln_ffn_b, ffn_w_gate, ffn_w_up, ffn_w_down, mla_w_in, mla_q_norm, mla_w_q_b, mla_kv_norm, mla_w_kv_b, mla_w_o, swa_w_qkv, swa_b_qkv, swa_sinks, swa_w_o, swa_b_o, loss_target, m_ada_w, m_ada_b, m_ln_mix_g, m_ln_mix_b, m_ln_ffn_g, m_ln_ffn_b, m_ffn_w_gate, m_ffn_w_up, m_ffn_w_down, m_mla_w_in, m_mla_q_norm, m_mla_w_q_b, m_mla_kv_norm, m_mla_w_kv_b, m_mla_w_o, m_swa_w_qkv, m_swa_b_qkv, m_swa_sinks, m_swa_w_o, m_swa_b_o, v_ada_w, v_ada_b, v_ln_mix_g, v_ln_mix_b, v_ln_ffn_g, v_ln_ffn_b, v_ffn_w_gate, v_ffn_w_up, v_ffn_w_down, v_mla_w_in, v_mla_q_norm, v_mla_w_q_b, v_mla_kv_norm, v_mla_w_kv_b, v_mla_w_o, v_swa_w_qkv, v_swa_b_qkv, v_swa_sinks, v_swa_w_o, v_swa_b_o)
    assert len(args) == len(INPUTS)
    loss, grad_x, res = _step(dict(zip(INPUTS, args)))
    return (loss, grad_x, *[res[n][0] for n in WEIGHTS], *[res[n][1] for n in WEIGHTS],
            *[res[n][2] for n in WEIGHTS], *[res[n][3] for n in WEIGHTS])
```

```python
import functools
import math

import jax
import jax.numpy as jnp
from jax import lax
from jax.experimental import pallas as pl
from jax.experimental.pallas import tpu as pltpu

F32 = jnp.float32
BF = jnp.bfloat16

NDEV = 8
D = 1024
DEPTH = 4
F = 2816
ALPHA = (2 * DEPTH) ** 0.25
LN_EPS = 1e-5
RMS_EPS = 1e-6
ROPE_THETA = 500000.0

MLA_H = 8
MLA_QR = 384
MLA_KVR = 256
MLA_ROPE = 64
MLA_LAT = 768
MLA_HD = 256
MLA_SCALE = (128 + 64) ** -0.5

SWA_HQ = 16
SWA_HKV = 4
SWA_W = 128
SWA_SCALE = 64 ** -0.5
SWA_QKV = (SWA_HQ + 2 * SWA_HKV) * 128
SWA_O = SWA_HQ * 128

LANE = 128
VMEM_LIMIT = 56 * 2 ** 20

ADAM_LR, ADAM_B1, ADAM_B2, ADAM_EPS, ADAM_WD, ADAM_STEP = 0.001, 0.9, 0.999, 1e-8, 0.01, 10


def _params(n_axes):
    return pltpu.CompilerParams(dimension_semantics=("arbitrary",) * n_axes, vmem_limit_bytes=VMEM_LIMIT)


def _dot(a, b):
    return jnp.dot(a, b, preferred_element_type=F32)


def _dot_nt(a, b):
    return lax.dot_general(a, b, (((1,), (1,)), ((), ())), preferred_element_type=F32)


def _dot_tn(a, b):
    return lax.dot_general(a, b, (((0,), (0,)), ((), ())), preferred_element_type=F32)


def _full(shape):
    return pl.BlockSpec(shape, lambda *_: (0,) * len(shape))


def _resident(shape):
    return pl.BlockSpec(shape, lambda *_: (0,) * len(shape), pipeline_mode=pl.Buffered(1))


def _sigmoid(x):
    return 1.0 / (1.0 + jnp.exp(-x))


def _rope128(x, ct, s1, s2, half):
    return x * ct + pltpu.roll(x, LANE - half, 1) * s1 + pltpu.roll(x, half, 1) * s2


def _eye(n):
    return lax.broadcasted_iota(jnp.int32, (n, n), 0) == lax.broadcasted_iota(jnp.int32, (n, n), 1)


def _col_to_row(col):
    n = col.shape[0]
    return jnp.sum(jnp.where(_eye(n), col, 0.0), axis=0, keepdims=True)


def _row_to_col(row):
    n = row.shape[1]
    return jnp.sum(jnp.where(_eye(n), row, 0.0), axis=1, keepdims=True)


def _modulate(x, modl_ref, sc_row, sh_row):
    return x * (1.0 + modl_ref[sc_row:sc_row + 1, :]) + modl_ref[sh_row:sh_row + 1, :]


EXCHANGE_SEMS = [pltpu.SemaphoreType.DMA((NDEV - 1,)), pltpu.SemaphoreType.DMA((NDEV - 1,)), pltpu.SemaphoreType.DMA]


def _exchange_copies(src_ref, out_ref, send_sems, recv_sems, local_sem, gather):
    x, y, c = lax.axis_index("x"), lax.axis_index("y"), lax.axis_index("c")
    me = 4 * x + 2 * y + c

    def piece(dev):
        return src_ref if gather else src_ref.at[dev]

    mine = pltpu.make_async_copy(piece(me), out_ref.at[me], local_sem)
    sends, recvs = [], []
    for k in range(1, NDEV):
        px = 1 - x if k & 4 else x
        py = 1 - y if k & 2 else y
        pc = 1 - c if k & 1 else c
        peer = 4 * px + 2 * py + pc
        common = dict(send_sem=send_sems.at[k - 1], recv_sem=recv_sems.at[k - 1],
                      device_id=(px, py, pc), device_id_type=pl.DeviceIdType.MESH)
        sends.append(pltpu.make_async_remote_copy(src_ref=piece(peer), dst_ref=out_ref.at[me], **common))
        recvs.append(pltpu.make_async_remote_copy(src_ref=piece(peer), dst_ref=out_ref.at[peer], **common))
    return mine, sends, recvs


def _exchange_start(*refs, gather):
    mine, sends, _ = _exchange_copies(*refs, gather)
    mine.start()
    for s in sends:
        s.start()


def _exchange_wait(*refs, gather):
    mine, sends, recvs = _exchange_copies(*refs, gather)
    for r in recvs:
        r.wait_recv()
    for s in sends:
        s.wait_send()
    mine.wait()


def _exchange_out(src, gather):
    blk = tuple(src.shape) if gather else tuple(src.shape[1:])
    return jax.ShapeDtypeStruct((NDEV,) + blk, src.dtype)


def _exchange(src, gather, name):
    def body(*refs):
        _exchange_start(*refs, gather=gather)
        _exchange_wait(*refs, gather=gather)

    return pl.pallas_call(
        body, name=name,
        out_shape=_exchange_out(src, gather),
        in_specs=[pl.BlockSpec(memory_space=pltpu.HBM)],
        out_specs=pl.BlockSpec(memory_space=pltpu.HBM),
        scratch_shapes=EXCHANGE_SEMS,
    )(src)


def _mod_all(c_all, ada_w, ada_b_loc):
    ncol = ada_w.shape[2]

    def body(c_ref, w_ref, b_ref, o_ref):
        cv = c_ref[...]
        cond = cv * _sigmoid(cv)
        o_ref[0] = _dot(cond.astype(BF), w_ref[0].astype(BF)) + b_ref[0]

    return pl.pallas_call(
        body, name="mod_all", grid=(DEPTH,),
        out_shape=jax.ShapeDtypeStruct((DEPTH, NDEV, ncol), F32),
        in_specs=[_full((NDEV, D)), pl.BlockSpec((1, D, ncol), lambda i: (i, 0, 0)),
                  pl.BlockSpec((1, 1, ncol), lambda i: (i, 0, 0))],
        out_specs=pl.BlockSpec((1, NDEV, ncol), lambda i: (i, 0, 0)),
        compiler_params=_params(1),
    )(c_all, ada_w, ada_b_loc)


def _ada_w_grad(c_all_t, dmod_loc):
    ncol = dmod_loc.shape[2]

    def body(ct_ref, dm_ref, o_ref):
        cv = ct_ref[...]
        cond = cv * _sigmoid(cv)
        acc = cond[:, 0:1] * dm_ref[0, 0:1, :]
        for b in range(1, NDEV):
            acc = acc + cond[:, b:b + 1] * dm_ref[0, b:b + 1, :]
        o_ref[0] = acc

    return pl.pallas_call(
        body, name="ada_w_grad", grid=(DEPTH,),
        out_shape=jax.ShapeDtypeStruct((DEPTH, D, ncol), F32),
        in_specs=[_full((D, NDEV)), pl.BlockSpec((1, NDEV, ncol), lambda i: (i, 0, 0))],
        out_specs=pl.BlockSpec((1, D, ncol), lambda i: (i, 0, 0)),
        compiler_params=_params(1),
    )(c_all_t, dmod_loc)


def _mla_proj_fwd(x, modl, w_in, q_norm, w_q, kv_norm, w_kv, tabs, tm):
    T = x.shape[0]
    ct_a, s1_a, s2_a = tabs

    def body(x_ref, modl_ref, win_ref, qn_ref, wq_ref, kvn_ref, wkv_ref, ct_ref, s1_ref, s2_ref,
             q_ref, k_ref, v_ref):
        h = _modulate(x_ref[...], modl_ref, 1, 0).astype(BF)
        lat = _dot(h, win_ref[...])
        ql, kvl, kr = lat[:, :MLA_QR], lat[:, MLA_QR:MLA_QR + MLA_KVR], lat[:, MLA_QR + MLA_KVR:]
        qn = (ql * lax.rsqrt(jnp.mean(ql * ql, axis=1, keepdims=True) + RMS_EPS) * qn_ref[...]).astype(BF)
        kvn = (kvl * lax.rsqrt(jnp.mean(kvl * kvl, axis=1, keepdims=True) + RMS_EPS) * kvn_ref[...]).astype(BF)
        ct, s1, s2 = ct_ref[...], s1_ref[...], s2_ref[...]
        kr = _rope128(kr, ct, s1, s2, MLA_ROPE // 2).astype(BF)
        for hd in range(MLA_H):
            cols = slice(hd * MLA_HD, (hd + 1) * MLA_HD)
            qh = _dot(qn, wq_ref[:, cols])
            q_ref[hd, :, 0:LANE] = qh[:, :LANE].astype(BF)
            q_ref[hd, :, LANE:MLA_HD] = _rope128(qh[:, LANE:], ct, s1, s2, MLA_ROPE // 2).astype(BF)
            kvh = _dot(kvn, wkv_ref[:, cols])
            k_ref[hd, :, 0:LANE] = kvh[:, :LANE].astype(BF)
            k_ref[hd, :, LANE:MLA_HD] = kr
            v_ref[hd, :, 0:LANE] = kvh[:, LANE:].astype(BF)
            v_ref[hd, :, LANE:2 * LANE] = jnp.ones((tm, LANE), BF)

    row = lambda i: (i, 0)
    head = lambda i: (0, i, 0)
    return pl.pallas_call(
        body, name="mla_proj_fwd", grid=(T // tm,),
        out_shape=(jax.ShapeDtypeStruct((MLA_H, T, MLA_HD), BF), jax.ShapeDtypeStruct((MLA_H, T, MLA_HD), BF),
                   jax.ShapeDtypeStruct((MLA_H, T, 2 * LANE), BF)),
        in_specs=[pl.BlockSpec((tm, D), row), _full((8, D)), _full((D, MLA_LAT)), _full((1, MLA_QR)),
                  _full((MLA_QR, MLA_H * MLA_HD)), _full((1, MLA_KVR)), _full((MLA_KVR, MLA_H * MLA_HD)),
                  pl.BlockSpec((tm, LANE), row), pl.BlockSpec((tm, LANE), row), pl.BlockSpec((tm, LANE), row)],
        out_specs=(pl.BlockSpec((MLA_H, tm, MLA_HD), head), pl.BlockSpec((MLA_H, tm, MLA_HD), head),
                   pl.BlockSpec((MLA_H, tm, 2 * LANE), head)),
        compiler_params=_params(1),
    )(x, modl, w_in, q_norm, w_q, kv_norm, w_kv, ct_a, s1_a, s2_a)


def _causal_mask(row0, nrows, ncols, transposed):
    row = lax.broadcasted_iota(jnp.int32, (nrows, ncols), 0) + row0
    col = lax.broadcasted_iota(jnp.int32, (nrows, ncols), 1)
    return (row <= col) if transposed else (col <= row)


def _flash_fwd(q, k, v, tb, comm_src=None):
    H, T, _ = q.shape
    rh = min(256, tb)
    c_exp = MLA_SCALE * math.log2(math.e)

    def body(*refs):
        if comm_src is None:
            q_ref, k_ref, v_ref, o_ref, lse_ref, m_s, acc_s = refs
        else:
            q_ref, k_ref, v_ref, src_ref, o_ref, lse_ref, got_ref, m_s, acc_s, *sems = refs
            comm = (src_ref, got_ref, *sems)
            first = (pl.program_id(0) == 0) & (pl.program_id(1) == 0)
            last = (pl.program_id(0) == H - 1) & (pl.program_id(1) == T // tb - 1)
            pl.when(first)(functools.partial(_exchange_start, *comm, gather=True))
        i = pl.program_id(1)
        m_s[...] = jnp.full(m_s.shape, -jnp.inf, F32)
        acc_s[...] = jnp.zeros(acc_s.shape, F32)

        def block(j, masked):
            base = pl.multiple_of(j * tb, tb)
            for hf in range(tb // rh):
                r = slice(hf * rh, (hf + 1) * rh)
                nk = (hf + 1) * rh if masked else tb
                kb = k_ref[0, pl.ds(base, nk), :]
                vb = v_ref[0, pl.ds(base, nk), :]
                s = _dot_nt(q_ref[0, r, :], kb)
                if masked:
                    s = jnp.where(_causal_mask(hf * rh, rh, nk, False), s, -jnp.inf)
                sc = [s[:, c * LANE:(c + 1) * LANE] for c in range(nk // LANE)]
                mx = sc[0]
                for x in sc[1:]:
                    mx = jnp.maximum(mx, x)
                m_prev = m_s[r, :]
                m_new = jnp.maximum(m_prev, jnp.max(mx, axis=1, keepdims=True))
                p = jnp.concatenate([jnp.exp2((x - m_new) * c_exp) for x in sc], axis=1)
                corr = jnp.exp2((m_prev - m_new) * c_exp)
                acc_s[r, :] = jnp.concatenate([corr, corr], axis=1) * acc_s[r, :] + _dot(p.astype(BF), vb)
                m_s[r, :] = m_new

        def step(j, carry):
            block(j, False)
            return carry

        lax.fori_loop(0, i, step, 0)
        block(i, True)
        l = acc_s[:, LANE:]
        o_ref[...] = (acc_s[:, :LANE] / l).astype(BF)
        lse = (m_s[...] * MLA_SCALE + jnp.log(l))[:, 0:1]
        for c0 in range(0, tb, rh):
            lse_ref[0, :, c0:c0 + rh] = _col_to_row(lse[c0:c0 + rh])
        if comm_src is not None:
            pl.when(last)(functools.partial(_exchange_wait, *comm, gather=True))

    hbm = pl.BlockSpec(memory_space=pltpu.HBM)
    with_comm = comm_src is not None
    return pl.pallas_call(
        body, name="flash_fwd_gather" if with_comm else "flash_fwd", grid=(H, T // tb),
        out_shape=(jax.ShapeDtypeStruct((T, H * LANE), BF), jax.ShapeDtypeStruct((H, 1, T), F32))
        + ((_exchange_out(comm_src, True),) if with_comm else ()),
        in_specs=[pl.BlockSpec((1, tb, MLA_HD), lambda h, i: (h, i, 0)),
                  pl.BlockSpec((1, T, MLA_HD), lambda h, i: (h, 0, 0)),
                  pl.BlockSpec((1, T, 2 * LANE), lambda h, i: (h, 0, 0))] + ([hbm] if with_comm else []),
        out_specs=(pl.BlockSpec((tb, LANE), lambda h, i: (i, h)),
                   pl.BlockSpec((1, 1, tb), lambda h, i: (h, 0, i))) + ((hbm,) if with_comm else ()),
        scratch_shapes=[pltpu.VMEM((tb, LANE), F32), pltpu.VMEM((tb, 2 * LANE), F32)]
        + (EXCHANGE_SEMS if with_comm else []),
        compiler_params=_params(2),
    )(*((q, k, v, comm_src) if with_comm else (q, k, v)))


def _outproj_ln_fwd(a, w, bias, x, modl, g_row, ln_g, ln_b, tm, name):
    T, K = a.shape

    def body(a_ref, w_ref, b_ref, x_ref, modl_ref, g_ref, bb_ref, y_ref, u_ref, xn_ref):
        y = _dot(a_ref[...], w_ref[...]) + b_ref[...]
        u = ALPHA * x_ref[...] + modl_ref[g_row:g_row + 1, :] * y
        mu = jnp.mean(u, axis=1, keepdims=True)
        uc = u - mu
        var = jnp.mean(uc * uc, axis=1, keepdims=True)
        y_ref[...] = y.astype(BF)
        u_ref[...] = u
        xn_ref[...] = uc * lax.rsqrt(var + LN_EPS) * g_ref[...] + bb_ref[...]

    row = lambda i: (i, 0)
    return pl.pallas_call(
        body, name=name, grid=(T // tm,),
        out_shape=(jax.ShapeDtypeStruct((T, D), BF), jax.ShapeDtypeStruct((T, D), F32),
                   jax.ShapeDtypeStruct((T, D), F32)),
        in_specs=[pl.BlockSpec((tm, K), row), _resident((K, D)), _full((1, D)), pl.BlockSpec((tm, D), row),
                  _full((8, D)), _full((1, D)), _full((1, D))],
        out_specs=(pl.BlockSpec((tm, D), row),) * 3,
        compiler_params=_params(1),
    )(a, w, bias, x, modl, ln_g, ln_b)


def _ffn_up_fwd(x, modl, wg, wu, tm, tn):
    T = x.shape[0]

    def body(x_ref, modl_ref, wg_ref, wu_ref, g_ref, u_ref, a_ref):
        h = _modulate(x_ref[...], modl_ref, 4, 3).astype(BF)
        g = _dot(h, wg_ref[...])
        u = _dot(h, wu_ref[...])
        g_ref[...] = g.astype(BF)
        u_ref[...] = u.astype(BF)
        a_ref[...] = (g * _sigmoid(g) * u).astype(BF)

    tile = pl.BlockSpec((tm, tn), lambda n, i: (i, n))
    wcol = pl.BlockSpec((D, tn), lambda n, i: (0, n))
    return pl.pallas_call(
        body, name="ffn_up_fwd", grid=(F // tn, T // tm),
        out_shape=(jax.ShapeDtypeStruct((T, F), BF),) * 3,
        in_specs=[pl.BlockSpec((tm, D), lambda n, i: (i, 0)), _full((8, D)), wcol, wcol],
        out_specs=(tile, tile, tile),
        compiler_params=_params(2),
    )(x, modl, wg, wu)


def _swa_proj_fwd(x, modl, w, b, tabs, tm):
    T = x.shape[0]
    ct_b, s1_b, s2_b = tabs
    n_rope = SWA_HQ + SWA_HKV

    def body(x_ref, modl_ref, w_ref, b_ref, ct_ref, s1_ref, s2_ref, o_ref):
        h = _modulate(x_ref[...], modl_ref, 1, 0).astype(BF)
        ct, s1, s2 = ct_ref[...], s1_ref[...], s2_ref[...]
        for pair in range(SWA_QKV // (2 * LANE)):
            cols2 = slice(2 * pair * LANE, (2 * pair + 2) * LANE)
            z2 = _dot(h, w_ref[:, cols2]) + b_ref[:, cols2]
            for half in range(2):
                grp = 2 * pair + half
                z = z2[:, half * LANE:(half + 1) * LANE]
                if grp < n_rope:
                    z = _rope128(z, ct, s1, s2, 8)
                o_ref[:, grp * LANE:(grp + 1) * LANE] = z.astype(BF)

    row = lambda i: (i, 0)
    return pl.pallas_call(
        body, name="swa_proj_fwd", grid=(T // tm,),
        out_shape=jax.ShapeDtypeStruct((T, SWA_QKV), BF),
        in_specs=[pl.BlockSpec((tm, D), row), _full((8, D)), _full((D, SWA_QKV)), _full((1, SWA_QKV)),
                  pl.BlockSpec((tm, LANE), row), pl.BlockSpec((tm, LANE), row), pl.BlockSpec((tm, LANE), row)],
        out_specs=pl.BlockSpec((tm, SWA_QKV), row),
        compiler_params=_params(1),
    )(x, modl, w, b, ct_b, s1_b, s2_b)


SWA_NB = 1
SWA_R = SWA_NB * 4 * SWA_W
SWA_NK = (SWA_NB + 1) * SWA_W


def _swa_bias(first):
    row = lax.broadcasted_iota(jnp.int32, (SWA_R, SWA_NK), 0)
    col = lax.broadcasted_iota(jnp.int32, (SWA_R, SWA_NK), 1)
    bl = row // (4 * SWA_W)
    r = row % SWA_W
    cp = col - bl * SWA_W
    band = (cp > r) & (cp <= r + SWA_W)
    if first:
        band = band & ((col >= SWA_W) | (bl > 0))
    return jnp.where(band, 0.0, -jnp.inf).astype(F32)


def _swa_fill_bias(bias_s):
    @pl.when(pl.program_id(0) == 0)
    def _():
        bias_s[0] = _swa_bias(False)
        bias_s[1] = _swa_bias(True)


def _swa_specs(T, tb):
    nsub = tb // SWA_W
    q_spec = pl.BlockSpec((tb, SWA_O), lambda i: (i, 0))
    kvc_spec = pl.BlockSpec((tb, 2 * SWA_HKV * LANE), lambda i: (i, 2))
    kvp_spec = pl.BlockSpec((SWA_W, 2 * SWA_HKV * LANE), lambda i: (jnp.maximum(i * nsub - 1, 0), 2))
    return q_spec, kvc_spec, kvp_spec


def _swa_rows(ref, g, c):
    return jnp.concatenate(
        [ref[(c * SWA_NB + bl) * SWA_W:(c * SWA_NB + bl + 1) * SWA_W, (4 * g + hh) * LANE:(4 * g + hh + 1) * LANE]
         for bl in range(SWA_NB) for hh in range(4)], axis=0)


def _swa_chain(q_ref, kall, vall, sink_ref, bias_s, g, c):
    i = pl.program_id(0)
    lane = lax.broadcasted_iota(jnp.int32, (1, LANE), 1)
    qc = _swa_rows(q_ref, g, c)
    keys = slice(c * SWA_NB * SWA_W, c * SWA_NB * SWA_W + SWA_NK)
    k3 = kall[keys]
    v3 = jnp.where(lane < 64, vall[keys], jnp.ones((), BF))
    bias = bias_s[jnp.where(i == 0, 1, 0)] if c == 0 else bias_s[0]
    s = _dot_nt(qc, k3)
    c_exp = SWA_SCALE * math.log2(math.e)
    sb = [s[:, ch * LANE:(ch + 1) * LANE] * c_exp + bias[:, ch * LANE:(ch + 1) * LANE] for ch in range(SWA_NK // LANE)]
    mx = sb[0]
    for x in sb[1:]:
        mx = jnp.maximum(mx, x)
    sink2 = jnp.concatenate([jnp.full((SWA_W, 1), sink_ref[4 * g + hh] * math.log2(math.e), F32)
                             for _ in range(SWA_NB) for hh in range(4)], axis=0)
    m = jnp.maximum(jnp.max(mx, axis=1, keepdims=True), sink2)
    m_rep = jnp.broadcast_to(m, (SWA_R, LANE))
    eb = jnp.concatenate([jnp.exp2(x - m_rep) for x in sb], axis=1).astype(BF)
    es = jnp.exp2(sink2 - m)
    acc = _dot(eb, v3)
    linv = 1.0 / (acc[:, 64:65] + es)
    num = jnp.where(lane < 64, acc, 0.0)
    return qc, k3, v3, eb, num, linv, es


def _swa_attn_fwd(qkv, sinks, tb):
    T = qkv.shape[0]
    kw = SWA_HKV * LANE

    def body(q_ref, kvc_ref, kvp_ref, sink_ref, o_ref, bias_s):
        _swa_fill_bias(bias_s)
        for g in range(SWA_HKV):
            gl = slice(g * LANE, (g + 1) * LANE)
            gv = slice(kw + g * LANE, kw + (g + 1) * LANE)
            kall = jnp.concatenate([kvp_ref[:, gl], kvc_ref[:, gl]], axis=0)
            vall = jnp.concatenate([kvp_ref[:, gv], kvc_ref[:, gv]], axis=0)
            for c in range(tb // (SWA_NB * SWA_W)):
                _, _, _, _, num, linv, _ = _swa_chain(q_ref, kall, vall, sink_ref, bias_s, g, c)
                o = (num * linv).astype(BF)
                for bl in range(SWA_NB):
                    for hh in range(4):
                        piece = (bl * 4 + hh) * SWA_W
                        rows = slice((c * SWA_NB + bl) * SWA_W, (c * SWA_NB + bl + 1) * SWA_W)
                        o_ref[rows, (4 * g + hh) * LANE:(4 * g + hh + 1) * LANE] = o[piece:piece + SWA_W]

    q_spec, kvc_spec, kvp_spec = _swa_specs(T, tb)
    return pl.pallas_call(
        body, name="swa_attn_fwd", grid=(T // tb,),
        out_shape=jax.ShapeDtypeStruct((T, SWA_O), BF),
        in_specs=[q_spec, kvc_spec, kvp_spec, pl.BlockSpec(memory_space=pltpu.SMEM)],
        out_specs=pl.BlockSpec((tb, SWA_O), lambda i: (i, 0)),
        scratch_shapes=[pltpu.VMEM((2, SWA_R, SWA_NK), F32)],
        compiler_params=_params(1),
    )(qkv, qkv, qkv, sinks)


def _loss_grad(x, tgt, tm):
    T = x.shape[0]

    def body(x_ref, t_ref, dx_ref, l_ref):
        @pl.when(pl.program_id(0) == 0)
        def _():
            l_ref[...] = jnp.zeros(l_ref.shape, F32)
        diff = x_ref[...] - t_ref[...]
        dx_ref[...] = diff * (1.0 / D)
        l_ref[0:1, :] += jnp.sum(diff * diff, axis=0, keepdims=True) * (0.5 / D)

    row = lambda i: (i, 0)
    return pl.pallas_call(
        body, name="loss_grad", grid=(T // tm,),
        out_shape=(jax.ShapeDtypeStruct((T, D), F32), jax.ShapeDtypeStruct((8, D), F32)),
        in_specs=[pl.BlockSpec((tm, D), row), pl.BlockSpec((tm, D), row)],
        out_specs=(pl.BlockSpec((tm, D), row), _full((8, D))),
        compiler_params=_params(1),
    )(x, tgt)


def _outproj_ln_bwd(dxn, u, y, w, modl, g_row, ln_g, tm, name):
    T = dxn.shape[0]
    K = w.shape[0]

    def body(dxn_ref, u_ref, y_ref, w_ref, modl_ref, g_ref, dres_ref, dy_ref, da_ref, sm_ref):
        @pl.when(pl.program_id(0) == 0)
        def _():
            sm_ref[...] = jnp.zeros(sm_ref.shape, F32)
        uu = u_ref[...]
        mu = jnp.mean(uu, axis=1, keepdims=True)
        uc = uu - mu
        rstd = lax.rsqrt(jnp.mean(uc * uc, axis=1, keepdims=True) + LN_EPS)
        xhat = uc * rstd
        dxo = dxn_ref[...]
        dyh = dxo * g_ref[...]
        du = rstd * (dyh - jnp.mean(dyh, axis=1, keepdims=True)
                     - xhat * jnp.mean(dyh * xhat, axis=1, keepdims=True))
        dy = modl_ref[g_row:g_row + 1, :] * du
        dyb = dy.astype(BF)
        dres_ref[...] = ALPHA * du
        dy_ref[...] = dyb
        da_ref[...] = _dot_nt(dyb, w_ref[...]).astype(BF)
        sm_ref[0:1, :] += jnp.sum(dxo * xhat, axis=0, keepdims=True)
        sm_ref[1:2, :] += jnp.sum(dxo, axis=0, keepdims=True)
        sm_ref[2:3, :] += jnp.sum(du * y_ref[...].astype(F32), axis=0, keepdims=True)
        sm_ref[3:4, :] += jnp.sum(dy, axis=0, keepdims=True)

    row = lambda i: (i, 0)
    return pl.pallas_call(
        body, name=name, grid=(T // tm,),
        out_shape=(jax.ShapeDtypeStruct((T, D), F32), jax.ShapeDtypeStruct((T, D), BF),
                   jax.ShapeDtypeStruct((T, K), BF), jax.ShapeDtypeStruct((8, D), F32)),
        in_specs=[pl.BlockSpec((tm, D), row), pl.BlockSpec((tm, D), row), pl.BlockSpec((tm, D), row),
                  _resident((K, D)), _full((8, D)), _full((1, D))],
        out_specs=(pl.BlockSpec((tm, D), row), pl.BlockSpec((tm, D), row), pl.BlockSpec((tm, K), row),
                   _full((8, D))),
        compiler_params=_params(1),
    )(dxn, u, y, w, modl, ln_g)


def _ffn_mid_bwd(da, g, u, x, modl, dres, wg, wu, tm, tn):
    T = x.shape[0]
    nn = F // tn

    def body(da_ref, g_ref, u_ref, x_ref, modl_ref, dres_ref, wg_ref, wu_ref, dg_ref, du_ref, dx_ref, sm_ref):
        i, n = pl.program_id(0), pl.program_id(1)

        @pl.when((i == 0) & (n == 0))
        def _():
            sm_ref[...] = jnp.zeros(sm_ref.shape, F32)

        gg = g_ref[...].astype(F32)
        sg = _sigmoid(gg)
        dav = da_ref[...].astype(F32)
        dgp = (dav * u_ref[...].astype(F32) * sg * (1.0 + gg * (1.0 - sg))).astype(BF)
        dup = (dav * gg * sg).astype(BF)
        dg_ref[...] = dgp
        du_ref[...] = dup
        dh = _dot_nt(dgp, wg_ref[...]) + _dot_nt(dup, wu_ref[...])

        @pl.when(n == 0)
        def _():
            dx_ref[...] = dh

        @pl.when(n > 0)
        def _():
            dx_ref[...] += dh

        @pl.when(n == nn - 1)
        def _():
            dht = dx_ref[...]
            sm_ref[0:1, :] += jnp.sum(dht * x_ref[...], axis=0, keepdims=True)
            sm_ref[1:2, :] += jnp.sum(dht, axis=0, keepdims=True)
            dx_ref[...] = dres_ref[...] + dht * (1.0 + modl_ref[4:5, :])

    tile = pl.BlockSpec((tm, tn), lambda i, n: (i, n))
    rowd = pl.BlockSpec((tm, D), lambda i, n: (i, 0))
    rowd_once = pl.BlockSpec((tm, D), lambda i, n: (i, 0), pipeline_mode=pl.Buffered(1))
    wcol = pl.BlockSpec((D, tn), lambda i, n: (0, n))
    return pl.pallas_call(
        body, name="ffn_mid_bwd", grid=(T // tm, nn),
        out_shape=(jax.ShapeDtypeStruct((T, F), BF), jax.ShapeDtypeStruct((T, F), BF),
                   jax.ShapeDtypeStruct((T, D), F32), jax.ShapeDtypeStruct((8, D), F32)),
        in_specs=[tile, tile, tile, rowd_once, _full((8, D)), rowd_once, wcol, wcol],
        out_specs=(tile, tile, rowd, _full((8, D))),
        compiler_params=_params(2),
    )(da, g, u, x, modl, dres, wg, wu)


def _wgrad(a, b, tm, tk, tn, name, modl=None, rows=None):
    T, K = a.shape
    N = b.shape[1]

    def body(*refs):
        if modl is None:
            a_ref, b_ref, o_ref = refs
            av = a_ref[...]
        else:
            a_ref, modl_ref, b_ref, o_ref = refs
            av = _modulate(a_ref[...], modl_ref, rows[0], rows[1]).astype(BF)

        @pl.when(pl.program_id(2) == 0)
        def _():
            o_ref[...] = jnp.zeros(o_ref.shape, F32)
        o_ref[...] += _dot_tn(av, b_ref[...])

    in_specs = [pl.BlockSpec((tm, tk), lambda k, n, t: (t, k))]
    args = [a]
    if modl is not None:
        in_specs.append(_full((8, D)))
        args.append(modl)
    in_specs.append(pl.BlockSpec((tm, tn), lambda k, n, t: (t, n)))
    args.append(b)
    return pl.pallas_call(
        body, name=name, grid=(K // tk, N // tn, T // tm),
        out_shape=jax.ShapeDtypeStruct((K, N), F32),
        in_specs=in_specs,
        out_specs=pl.BlockSpec((tk, tn), lambda k, n, t: (k, n)),
        compiler_params=_params(3),
    )(*args)


def _flash_delta(o, do, tb):
    T = o.shape[0]
    H = o.shape[1] // LANE

    def body(o_ref, do_ref, delta_ref):
        delta = jnp.sum(o_ref[...].astype(F32) * do_ref[...].astype(F32), axis=1, keepdims=True)
        delta_ref[0] = _col_to_row(delta)

    blk = pl.BlockSpec((tb, LANE), lambda h, i: (i, h))
    return pl.pallas_call(
        body, name="flash_delta", grid=(H, T // tb),
        out_shape=jax.ShapeDtypeStruct((H, 1, T), F32),
        in_specs=[blk, blk],
        out_specs=pl.BlockSpec((1, 1, tb), lambda h, i: (h, 0, i)),
        compiler_params=_params(2),
    )(o, do)


def _flash_bwd(q, k, v, do, lse, delta, tb, comm_src=None):
    H, T, _ = q.shape
    nq = T // tb
    rh = min(256, tb)
    c_exp = MLA_SCALE * math.log2(math.e)

    def body(*refs):
        if comm_src is None:
            k_ref, v_ref, q_ref, do_ref, lse_ref, delta_ref, dq_ref, dk_ref, dv_ref, dk_s, dv_s = refs
        else:
            (k_ref, v_ref, q_ref, do_ref, lse_ref, delta_ref, src_ref, dq_ref, dk_ref, dv_ref, got_ref,
             dk_s, dv_s, *sems) = refs
            comm = (src_ref, got_ref, *sems)
            first = (pl.program_id(0) == 0) & (pl.program_id(1) == 0)
            last = (pl.program_id(0) == H - 1) & (pl.program_id(1) == nq - 1)
            pl.when(first)(functools.partial(_exchange_start, *comm, gather=False))
        j = pl.program_id(1)

        @pl.when(j == 0)
        def _():
            def zero(cix, carry):
                dq_ref[0, pl.ds(pl.multiple_of(cix * tb, tb), tb), :] = jnp.zeros((tb, MLA_HD), F32)
                return carry
            lax.fori_loop(0, nq, zero, 0)

        dk_s[...] = jnp.zeros(dk_s.shape, F32)
        dv_s[...] = jnp.zeros(dv_s.shape, F32)

        def block(i, masked):
            dq_part = None
            for hf in range(tb // rh):
                r = slice(hf * rh, (hf + 1) * rh)
                off = hf * rh if masked else 0
                qrows = pl.ds(pl.multiple_of(i * tb + off, rh), tb - off)
                qb = q_ref[0, qrows, :]
                dob = do_ref[qrows, :]
                l2 = lse_ref[0, :, qrows] * math.log2(math.e)
                dl = delta_ref[0, :, qrows]
                kc = k_ref[0, r, :]
                st = _dot_nt(kc, qb)
                if masked:
                    st = jnp.where(_causal_mask(0, rh, tb - off, True), st, -jnp.inf)
                pt = jnp.exp2(st * c_exp - l2)
                dpt = _dot_nt(v_ref[0, r, :], dob)
                dst = (pt * (dpt - dl)).astype(BF)
                dv_s[r, :] += _dot(pt.astype(BF), dob)
                dk_s[r, :] += _dot(dst, qb)
                part = _dot_tn(dst, kc)
                if masked:
                    dq_ref[0, qrows, :] += part * MLA_SCALE
                else:
                    dq_part = part if dq_part is None else dq_part + part
            if not masked:
                dq_ref[0, pl.ds(pl.multiple_of(i * tb, tb), tb), :] += dq_part * MLA_SCALE

        def step(i, carry):
            block(i, False)
            return carry

        block(j, True)
        lax.fori_loop(j + 1, nq, step, 0)
        dk_ref[0] = (dk_s[...] * MLA_SCALE).astype(BF)
        dv_ref[0] = dv_s[...].astype(BF)
        if comm_src is not None:
            pl.when(last)(functools.partial(_exchange_wait, *comm, gather=False))

    once = pl.Buffered(1)
    hbm = pl.BlockSpec(memory_space=pltpu.HBM)
    with_comm = comm_src is not None
    return pl.pallas_call(
        body, name="flash_bwd_scatter" if with_comm else "flash_bwd", grid=(H, nq),
        out_shape=(jax.ShapeDtypeStruct((H, T, MLA_HD), F32), jax.ShapeDtypeStruct((H, T, MLA_HD), BF),
                   jax.ShapeDtypeStruct((H, T, LANE), BF)) + ((_exchange_out(comm_src, False),) if with_comm else ()),
        in_specs=[pl.BlockSpec((1, tb, MLA_HD), lambda h, j: (h, j, 0)),
                  pl.BlockSpec((1, tb, LANE), lambda h, j: (h, j, 0)),
                  pl.BlockSpec((1, T, MLA_HD), lambda h, j: (h, 0, 0), pipeline_mode=once),
                  pl.BlockSpec((T, LANE), lambda h, j: (0, h), pipeline_mode=once),
                  pl.BlockSpec((1, 1, T), lambda h, j: (h, 0, 0)),
                  pl.BlockSpec((1, 1, T), lambda h, j: (h, 0, 0))] + ([hbm] if with_comm else []),
        out_specs=(pl.BlockSpec((1, T, MLA_HD), lambda h, j: (h, 0, 0), pipeline_mode=once),
                   pl.BlockSpec((1, tb, MLA_HD), lambda h, j: (h, j, 0)),
                   pl.BlockSpec((1, tb, LANE), lambda h, j: (h, j, 0))) + ((hbm,) if with_comm else ()),
        scratch_shapes=[pltpu.VMEM((tb, MLA_HD), F32), pltpu.VMEM((tb, LANE), F32)]
        + (EXCHANGE_SEMS if with_comm else []),
        compiler_params=_params(2),
    )(*((k, v, q, do, lse, delta, comm_src) if with_comm else (k, v, q, do, lse, delta)))


def _mla_proj_bwd(dq, dk, dv, x, modl, dres, w_in, q_norm, w_q, kv_norm, w_kv, tabs_neg, tm):
    T = x.shape[0]
    ct_a, s1_n, s2_n = tabs_neg

    def body(dq_ref, dk_ref, dv_ref, x_ref, modl_ref, dres_ref, win_ref, qn_ref, wq_ref, kvn_ref, wkv_ref,
             ct_ref, s1_ref, s2_ref, dx_ref, dwin_ref, dwq_ref, dwkv_ref, sm_ref, dqn_ref, dkvn_ref):
        @pl.when(pl.program_id(0) == 0)
        def _():
            for r in (dwin_ref, dwq_ref, dwkv_ref, sm_ref, dqn_ref, dkvn_ref):
                r[...] = jnp.zeros(r.shape, F32)

        xv = x_ref[...]
        h = _modulate(xv, modl_ref, 1, 0).astype(BF)
        lat = _dot(h, win_ref[...])
        ql, kvl = lat[:, :MLA_QR], lat[:, MLA_QR:MLA_QR + MLA_KVR]
        qhat = ql * lax.rsqrt(jnp.mean(ql * ql, axis=1, keepdims=True) + RMS_EPS)
        kvhat = kvl * lax.rsqrt(jnp.mean(kvl * kvl, axis=1, keepdims=True) + RMS_EPS)
        rq = lax.rsqrt(jnp.mean(ql * ql, axis=1, keepdims=True) + RMS_EPS)
        rkv = lax.rsqrt(jnp.mean(kvl * kvl, axis=1, keepdims=True) + RMS_EPS)
        qn = (qhat * qn_ref[...]).astype(BF)
        kvn = (kvhat * kvn_ref[...]).astype(BF)
        ct, s1, s2 = ct_ref[...], s1_ref[...], s2_ref[...]

        dqn = jnp.zeros((tm, MLA_QR), F32)
        dkvn = jnp.zeros((tm, MLA_KVR), F32)
        dkr = jnp.zeros((tm, LANE), F32)
        for hd in range(MLA_H):
            cols = slice(hd * MLA_HD, (hd + 1) * MLA_HD)
            dqh = dq_ref[hd]
            dqr = _rope128(dqh[:, LANE:], ct, s1, s2, MLA_ROPE // 2).astype(BF)
            dqh = jnp.concatenate([dqh[:, :LANE].astype(BF), dqr], axis=1)
            dqn = dqn + _dot_nt(dqh, wq_ref[:, cols])
            dwq_ref[:, cols] += _dot_tn(qn, dqh)
            dkh = dk_ref[hd]
            dkr = dkr + dkh[:, LANE:].astype(F32)
            dkvh = jnp.concatenate([dkh[:, :LANE], dv_ref[hd]], axis=1)
            dkvn = dkvn + _dot_nt(dkvh, wkv_ref[:, cols])
            dwkv_ref[:, cols] += _dot_tn(kvn, dkvh)
        dkr = _rope128(dkr, ct, s1, s2, MLA_ROPE // 2)

        dqn_ref[...] += jnp.sum(dqn * qhat, axis=0, keepdims=True)
        dkvn_ref[...] += jnp.sum(dkvn * kvhat, axis=0, keepdims=True)
        dqh_ = dqn * qn_ref[...]
        dkvh_ = dkvn * kvn_ref[...]
        dql = rq * (dqh_ - qhat * jnp.mean(dqh_ * qhat, axis=1, keepdims=True))
        dkvl = rkv * (dkvh_ - kvhat * jnp.mean(dkvh_ * kvhat, axis=1, keepdims=True))
        dlat = jnp.concatenate([dql, dkvl, dkr], axis=1).astype(BF)
        dwin_ref[...] += _dot_tn(h, dlat)
        dh = _dot_nt(dlat, win_ref[...])
        sm_ref[0:1, :] += jnp.sum(dh * xv, axis=0, keepdims=True)
        sm_ref[1:2, :] += jnp.sum(dh, axis=0, keepdims=True)
        dx_ref[...] = dres_ref[...] + dh * (1.0 + modl_ref[1:2, :])

    row = lambda i: (i, 0)
    head = lambda i: (0, i, 0)
    nq = MLA_H * MLA_HD
    return pl.pallas_call(
        body, name="mla_proj_bwd", grid=(T // tm,),
        out_shape=(jax.ShapeDtypeStruct((T, D), F32), jax.ShapeDtypeStruct((D, MLA_LAT), F32),
                   jax.ShapeDtypeStruct((MLA_QR, nq), F32), jax.ShapeDtypeStruct((MLA_KVR, nq), F32),
                   jax.ShapeDtypeStruct((8, D), F32), jax.ShapeDtypeStruct((1, MLA_QR), F32),
                   jax.ShapeDtypeStruct((1, MLA_KVR), F32)),
        in_specs=[pl.BlockSpec((MLA_H, tm, MLA_HD), head), pl.BlockSpec((MLA_H, tm, MLA_HD), head),
                  pl.BlockSpec((MLA_H, tm, LANE), head), pl.BlockSpec((tm, D), row), _full((8, D)),
                  pl.BlockSpec((tm, D), row), _full((D, MLA_LAT)), _full((1, MLA_QR)), _full((MLA_QR, nq)),
                  _full((1, MLA_KVR)), _full((MLA_KVR, nq)),
                  pl.BlockSpec((tm, LANE), row), pl.BlockSpec((tm, LANE), row), pl.BlockSpec((tm, LANE), row)],
        out_specs=(pl.BlockSpec((tm, D), row), _full((D, MLA_LAT)), _full((MLA_QR, nq)), _full((MLA_KVR, nq)),
                   _full((8, D)), _full((1, MLA_QR)), _full((1, MLA_KVR))),
        compiler_params=_params(1),
    )(dq, dk, dv, x, modl, dres, w_in, q_norm, w_q, kv_norm, w_kv, ct_a, s1_n, s2_n)


def _swa_attn_bwd(qkv, sinks, do, tb):
    T = qkv.shape[0]
    kw = SWA_HKV * LANE
    nstep = T // tb

    def body(q_ref, kvc_ref, kvp_ref, sink_ref, do_ref, dq_ref, dkvc_ref, dkvp_ref, dsink_ref, dk_s, dv_s, bias_s):
        i = pl.program_id(0)
        _swa_fill_bias(bias_s)

        @pl.when(i == 0)
        def _():
            dsink_ref[...] = jnp.zeros(dsink_ref.shape, F32)

        dk_s[...] = jnp.zeros(dk_s.shape, F32)
        dv_s[...] = jnp.zeros(dv_s.shape, F32)
        lane = lax.broadcasted_iota(jnp.int32, (1, LANE), 1)
        for g in range(SWA_HKV):
            gl = slice(g * LANE, (g + 1) * LANE)
            gv = slice(kw + g * LANE, kw + (g + 1) * LANE)
            kall = jnp.concatenate([kvp_ref[:, gl], kvc_ref[:, gl]], axis=0)
            vall = jnp.concatenate([kvp_ref[:, gv], kvc_ref[:, gv]], axis=0)
            for c in range(tb // (SWA_NB * SWA_W)):
                qc, k3, v3, eb, num, linv, es = _swa_chain(q_ref, kall, vall, sink_ref, bias_s, g, c)
                doc = _swa_rows(do_ref, g, c)
                delta = jnp.sum(doc.astype(F32) * (num * linv), axis=1, keepdims=True)
                delta_rep = jnp.broadcast_to(delta, (SWA_R, LANE))
                dp = _dot_nt(doc, v3)
                linv_rep = jnp.broadcast_to(linv, (SWA_R, LANE))
                pch = [eb[:, ch * LANE:(ch + 1) * LANE].astype(F32) * linv_rep for ch in range(SWA_NK // LANE)]
                ds = jnp.concatenate([pch[ch] * (dp[:, ch * LANE:(ch + 1) * LANE] - delta_rep)
                                      for ch in range(SWA_NK // LANE)], axis=1).astype(BF)
                pb = jnp.concatenate(pch, axis=1).astype(BF)
                dqc = (_dot(ds, k3) * SWA_SCALE).astype(BF)
                keys = slice(c * SWA_NB * SWA_W, c * SWA_NB * SWA_W + SWA_NK)
                dk_s[keys, gl] += _dot_tn(ds, qc) * SWA_SCALE
                dv_s[keys, gl] += _dot_tn(pb, doc)
                dsk = es * linv * delta
                for hh in range(4):
                    hq = 4 * g + hh
                    tot = jnp.zeros((1, 1), F32)
                    for bl in range(SWA_NB):
                        piece = (bl * 4 + hh) * SWA_W
                        rows = slice((c * SWA_NB + bl) * SWA_W, (c * SWA_NB + bl + 1) * SWA_W)
                        dq_ref[rows, hq * LANE:(hq + 1) * LANE] = dqc[piece:piece + SWA_W]
                        tot = tot + jnp.sum(dsk[piece:piece + SWA_W], axis=0, keepdims=True)
                    dsink_ref[0:1, :] -= jnp.where(lane == hq, tot, 0.0)
        dkvp_ref[0, :, 0:kw] = dk_s[0:SWA_W, :]
        dkvp_ref[0, :, kw:2 * kw] = dv_s[0:SWA_W, :]
        dkvc_ref[:, 0:kw] = dk_s[SWA_W:, :]
        dkvc_ref[:, kw:2 * kw] = dv_s[SWA_W:, :]

    q_spec, kvc_spec, kvp_spec = _swa_specs(T, tb)
    return pl.pallas_call(
        body, name="swa_attn_bwd", grid=(nstep,),
        out_shape=(jax.ShapeDtypeStruct((T, SWA_O), BF), jax.ShapeDtypeStruct((T, 2 * kw), F32),
                   jax.ShapeDtypeStruct((nstep, SWA_W, 2 * kw), F32), jax.ShapeDtypeStruct((8, LANE), F32)),
        in_specs=[q_spec, kvc_spec, kvp_spec, pl.BlockSpec(memory_space=pltpu.SMEM),
                  pl.BlockSpec((tb, SWA_O), lambda i: (i, 0))],
        out_specs=(pl.BlockSpec((tb, SWA_O), lambda i: (i, 0)), pl.BlockSpec((tb, 2 * kw), lambda i: (i, 0)),
                   pl.BlockSpec((1, SWA_W, 2 * kw), lambda i: (i, 0, 0)), _full((8, LANE))),
        scratch_shapes=[pltpu.VMEM((tb + SWA_W, kw), F32), pltpu.VMEM((tb + SWA_W, kw), F32),
                        pltpu.VMEM((2, SWA_R, SWA_NK), F32)],
        compiler_params=_params(1),
    )(qkv, qkv, qkv, sinks, do)


def _swa_proj_bwd(dq, dkvc, dkvp, x, modl, dres, w, tabs_neg, tm):
    T = x.shape[0]
    nstep = T // tm
    kw = SWA_HKV * LANE
    ct_b, s1_n, s2_n = tabs_neg

    def body(dq_ref, dkvc_ref, dkvp_ref, x_ref, modl_ref, dres_ref, w_ref, ct_ref, s1_ref, s2_ref,
             dx_ref, dz_ref, sm_ref, db_ref):
        i = pl.program_id(0)

        @pl.when(i == 0)
        def _():
            sm_ref[...] = jnp.zeros(sm_ref.shape, F32)
            db_ref[...] = jnp.zeros(db_ref.shape, F32)

        ct, s1, s2 = ct_ref[...], s1_ref[...], s2_ref[...]
        has_next = i + 1 < nstep
        for grp in range(SWA_QKV // LANE):
            cols = slice(grp * LANE, (grp + 1) * LANE)
            if grp < SWA_HQ:
                z = dq_ref[:, cols].astype(F32)
            else:
                kc = slice((grp - SWA_HQ) * LANE, (grp - SWA_HQ + 1) * LANE)
                cur = dkvc_ref[:, kc]
                tail = cur[tm - SWA_W:] + jnp.where(has_next, dkvp_ref[0, :, kc], 0.0)
                z = jnp.concatenate([cur[:tm - SWA_W], tail], axis=0)
            if grp < SWA_HQ + SWA_HKV:
                z = _rope128(z, ct, s1, s2, 8)
            db_ref[0:1, cols] += jnp.sum(z, axis=0, keepdims=True)
            dz_ref[:, cols] = z.astype(BF)
        dh = _dot_nt(dz_ref[...], w_ref[...])
        sm_ref[0:1, :] += jnp.sum(dh * x_ref[...], axis=0, keepdims=True)
        sm_ref[1:2, :] += jnp.sum(dh, axis=0, keepdims=True)
        dx_ref[...] = dres_ref[...] + dh * (1.0 + modl_ref[1:2, :])

    row = lambda i: (i, 0)
    return pl.pallas_call(
        body, name="swa_proj_bwd", grid=(nstep,),
        out_shape=(jax.ShapeDtypeStruct((T, D), F32), jax.ShapeDtypeStruct((T, SWA_QKV), BF),
                   jax.ShapeDtypeStruct((8, D), F32), jax.ShapeDtypeStruct((8, SWA_QKV), F32)),
        in_specs=[pl.BlockSpec((tm, SWA_O), row), pl.BlockSpec((tm, 2 * kw), row),
                  pl.BlockSpec((1, SWA_W, 2 * kw), lambda i: (jnp.minimum(i + 1, nstep - 1), 0, 0)),
                  pl.BlockSpec((tm, D), row), _full((8, D)), pl.BlockSpec((tm, D), row), _full((D, SWA_QKV)),
                  pl.BlockSpec((tm, LANE), row), pl.BlockSpec((tm, LANE), row), pl.BlockSpec((tm, LANE), row)],
        out_specs=(pl.BlockSpec((tm, D), row), pl.BlockSpec((tm, SWA_QKV), row), _full((8, D)),
                   _full((8, SWA_QKV))),
        compiler_params=_params(1),
    )(dq, dkvc, dkvp, x, modl, dres, w, ct_b, s1_n, s2_n)


def _adamw(gparts, w, m, v, name):
    P, R, C = gparts.shape
    tr = R
    for cand in (512, 256, 128):
        if R % cand == 0 and R > cand:
            tr = cand
            break
    c1 = 1.0 / (1.0 - ADAM_B1 ** ADAM_STEP)
    c2 = 1.0 / (1.0 - ADAM_B2 ** ADAM_STEP)

    def body(gp_ref, w_ref, m_ref, v_ref, g_ref, d_ref, nm_ref, nv_ref):
        g = gp_ref[0].astype(F32)
        for p in range(1, P):
            g = g + gp_ref[p].astype(F32)
        nm = ADAM_B1 * m_ref[...] + (1.0 - ADAM_B1) * g
        nv = ADAM_B2 * v_ref[...] + (1.0 - ADAM_B2) * (g * g)
        g_ref[...] = g
        nm_ref[...] = nm
        nv_ref[...] = nv
        d_ref[...] = -ADAM_LR * ((nm * c1) / (jnp.sqrt(nv * c2) + ADAM_EPS) + ADAM_WD * w_ref[...])

    blk = pl.BlockSpec((tr, C), lambda i: (i, 0))
    return pl.pallas_call(
        body, name=name, grid=(R // tr,),
        out_shape=(jax.ShapeDtypeStruct((R, C), F32),) * 4,
        in_specs=[pl.BlockSpec((P, tr, C), lambda i: (0, i, 0)), blk, blk, blk],
        out_specs=(blk,) * 4,
        compiler_params=_params(1),
    )(gparts, w, m, v)


PACK_W = 1024

BIG = {
    "ffn_w_gate": ((DEPTH, D, F // NDEV), 2),
    "ffn_w_up": ((DEPTH, D, F // NDEV), 2),
    "ffn_w_down": ((DEPTH, F // NDEV, D), 1),
    "mla_w_in": ((2, D // NDEV, 704), 1),
    "mla_w_q_b": ((2, MLA_QR, 1536 // NDEV), 2),
    "mla_w_kv_b": ((2, MLA_KVR, 2048 // NDEV), 2),
    "mla_w_o": ((2, D // NDEV, D), 1),
    "swa_w_qkv": ((2, D, 1536 // NDEV), 2),
    "swa_w_o": ((2, D // NDEV, D), 1),
}


EARLY = [("mla_w_in", 0, 1), ("mla_w_q_b", 0, 1), ("mla_w_kv_b", 0, 1)]
LATE = [("ffn_w_gate", 0, 4), ("ffn_w_up", 0, 4), ("ffn_w_down", 0, 4), ("mla_w_in", 1, 2), ("mla_w_q_b", 1, 2),
        ("mla_w_kv_b", 1, 2), ("mla_w_o", 0, 2), ("swa_w_qkv", 0, 2), ("swa_w_o", 0, 2)]


def _pack_rows(n):
    return -(-n // (16 * PACK_W)) * 16


def _entry_shape(name, lo, hi):
    return (hi - lo,) + BIG[name][0][1:]


def _pack_local(inp, entries):
    parts = []
    for name, lo, hi in entries:
        n = math.prod(_entry_shape(name, lo, hi))
        flat = inp[name][lo:hi].astype(BF).reshape(-1)
        parts.append(jnp.pad(flat, (0, _pack_rows(n) * PACK_W - n)).reshape(-1, PACK_W))
    return jnp.concatenate(parts, axis=0)


def _pack_full(full, entries):
    parts = []
    for name, lo, hi in entries:
        a, axis = full[(name, lo)], BIG[name][1]
        split = a.shape[:axis] + (NDEV, a.shape[axis] // NDEV) + a.shape[axis + 1:]
        a = jnp.moveaxis(a.reshape(split), axis, 0).astype(BF).reshape(NDEV, -1)
        n = math.prod(_entry_shape(name, lo, hi))
        parts.append(jnp.pad(a, ((0, 0), (0, _pack_rows(n) * PACK_W - n))).reshape(NDEV, -1, PACK_W))
    return jnp.concatenate(parts, axis=1)


def _unpack_blocks(packed, entries):
    out, r0 = {}, 0
    for name, lo, hi in entries:
        shape = _entry_shape(name, lo, hi)
        n = math.prod(shape)
        rows = _pack_rows(n)
        out[(name, lo)] = packed[:, r0:r0 + rows].reshape(NDEV, -1)[:, :n].reshape((NDEV,) + shape)
        r0 += rows
    return out


def _unpack_full(packed, entries):
    out = {}
    for (name, lo), blk in _unpack_blocks(packed, entries).items():
        axis = BIG[name][1]
        a = jnp.moveaxis(blk, 0, axis)
        out[(name, lo)] = a.reshape(a.shape[:axis] + (a.shape[axis] * a.shape[axis + 1],) + a.shape[axis + 2:])
    return out


def _pad_heads(a, axis, nheads, width, to):
    shp = a.shape[:axis] + (nheads, width) + a.shape[axis + 1:]
    a = a.reshape(shp)
    pad = [(0, 0)] * a.ndim
    pad[axis + 1] = (0, to - width)
    a = jnp.pad(a, pad)
    return a.reshape(a.shape[:axis] + (nheads * to,) + a.shape[axis + 2:])


def _unpad_heads(a, axis, nheads, width, to):
    shp = a.shape[:axis] + (nheads, to) + a.shape[axis + 1:]
    a = lax.slice_in_dim(a.reshape(shp), 0, width, axis=axis + 1)
    return a.reshape(a.shape[:axis] + (nheads * width,) + a.shape[axis + 2:])


def _swa_pad_cols(a):
    return _pad_heads(a, a.ndim - 1, SWA_HQ + 2 * SWA_HKV, 64, LANE)


def _rope_tables(positions, half):
    rot = 2 * half
    inv = ROPE_THETA ** (-jnp.arange(0, rot, 2, dtype=F32) / rot)
    ang = positions.astype(F32)[:, None] * inv
    cos, sin = jnp.cos(ang), jnp.sin(ang)
    T = positions.shape[0]
    ones = jnp.ones((T, LANE - rot), F32)
    zeros = jnp.zeros((T, LANE - rot), F32)
    zh = jnp.zeros((T, half), F32)
    ct = jnp.concatenate([cos, cos, ones], axis=1)
    s1 = jnp.concatenate([-sin, zh, zeros], axis=1)
    s2 = jnp.concatenate([zh, sin, zeros], axis=1)
    return (ct, s1, s2), (ct, -s1, -s2)


def _small_pack(vecs, rows):
    flat = jnp.concatenate([v.astype(F32).reshape(-1) for v in vecs])
    return jnp.pad(flat, (0, rows * LANE - flat.shape[0])).reshape(rows, LANE)


def _small_unpack(buf, shapes):
    flat = buf.reshape(NDEV, -1)
    out, o = [], 0
    for shp in shapes:
        n = math.prod(shp)
        out.append(flat[:, o:o + n].reshape((NDEV,) + shp))
        o += n
    return out


def _step(inp):
    x = inp["x"][0]
    tgt = inp["loss_target"][0]
    T = x.shape[0]
    tm = min(512, T)
    tmf = min(512, T)
    tw = min(1024, T)
    tb = min(512, T)
    tbf = min(1024, T)
    tbq = min(2048, T)
    tnf = F // 2
    me = 4 * lax.axis_index("x") + 2 * lax.axis_index("y") + lax.axis_index("c")

    small_in = _small_pack([inp["c"], inp["swa_b_qkv"], inp["swa_b_o"]], 16)
    c_all, bqkv_blk, bo_blk = _small_unpack(_exchange(small_in, True, "gather_small"),
                                            [(D,), (2, 1536 // NDEV), (2, D // NDEV)])
    swa_b_qkv = jnp.moveaxis(bqkv_blk, 0, 1).reshape(2, 1536)
    swa_b_o = jnp.moveaxis(bo_blk, 0, 1).reshape(2, D)

    w_early = _unpack_full(_exchange(_pack_local(inp, EARLY), True, "gather_early"), EARLY)

    ncol = 6 * D // NDEV
    ada_b_loc = lax.dynamic_slice_in_dim(inp["ada_b"], me * ncol, ncol, axis=1)[:, None, :]
    mod_all = _mod_all(c_all, inp["ada_w"], ada_b_loc)
    mod_src = jnp.moveaxis(mod_all, 1, 0).reshape(NDEV, DEPTH * ncol // LANE, LANE)
    mod_got = _exchange(mod_src, False, "scatter_mod").reshape(NDEV, DEPTH, ncol)
    mod = jnp.moveaxis(mod_got, 0, 1).reshape(DEPTH, 6, D)
    modl = jnp.pad(mod, ((0, 0), (0, 2), (0, 0)))

    def mla_proj_weights(w, lo):
        return (jnp.pad(w[("mla_w_in", lo)][0], ((0, 0), (0, MLA_LAT - 704))),
                _pad_heads(w[("mla_w_q_b", lo)][0], 1, MLA_H, 192, MLA_HD), w[("mla_w_kv_b", lo)][0])

    w_in, w_q, w_kv = [[t] for t in mla_proj_weights(w_early, 0)]
    b_qkv = _swa_pad_cols(swa_b_qkv)
    zero_bias = jnp.zeros((1, D), F32)

    pos = inp["positions"][0]
    tabs_a, tabs_a_neg = _rope_tables(pos, MLA_ROPE // 2)
    tabs_b, tabs_b_neg = _rope_tables(pos, 8)

    saved = []
    xs = x
    for i in range(DEPTH):
        j = i // 2
        st = {"x0": xs}
        if i % 2 == 0:
            q, k, v = _mla_proj_fwd(xs, modl[i], w_in[j], inp["mla_q_norm"][j][None], w_q[j],
                                    inp["mla_kv_norm"][j][None], w_kv[j], tabs_a, tm)
            if i == 0:
                o, lse, got = _flash_fwd(q, k, v, tbq, _pack_local(inp, LATE))
                w_late = _unpack_full(got, LATE)
                for lst, t in zip((w_in, w_q, w_kv), mla_proj_weights(w_late, 1)):
                    lst.append(t)
                w_o_mla = w_late[("mla_w_o", 0)]
                w_qkv = _swa_pad_cols(w_late[("swa_w_qkv", 0)])
                w_o_swa = _pad_heads(w_late[("swa_w_o", 0)], 1, SWA_HQ, 64, LANE)
                w_gate, w_up, w_down = (w_late[(n, 0)] for n in ("ffn_w_gate", "ffn_w_up", "ffn_w_down"))
            else:
                o, lse = _flash_fwd(q, k, v, tbq)
            st.update(q=q, k=k, v=v, o=o, lse=lse)
            w_o, b_o = w_o_mla[j], zero_bias
        else:
            qkv = _swa_proj_fwd(xs, modl[i], w_qkv[j], b_qkv[j][None], tabs_b, tm)
            o = _swa_attn_fwd(qkv, inp["swa_sinks"][j], tb)
            st.update(qkv=qkv, o=o)
            w_o, b_o = w_o_swa[j], swa_b_o[j][None]
        y, u, xs = _outproj_ln_fwd(o, w_o, b_o, xs, modl[i], 2, inp["ln_mix_g"][i][None],
                                   inp["ln_mix_b"][i][None], tm, f"mix_out_fwd_{i % 2}")
        st.update(y_m=y, u_m=u, x1=xs, w_o=w_o)
        g, up, a = _ffn_up_fwd(xs, modl[i], w_gate[i], w_up[i], tmf, tnf)
        y, u, xs = _outproj_ln_fwd(a, w_down[i], zero_bias, xs, modl[i], 5,
                                   inp["ln_ffn_g"][i][None], inp["ln_ffn_b"][i][None], tmf, "ffn_out_fwd")
        st.update(g=g, up=up, a=a, y_f=y, u_f=u)
        saved.append(st)

    dx, loss_rows = _loss_grad(xs, tgt, tm)
    loss = lax.psum(jnp.sum(loss_rows[0]), ("x", "y", "c"))

    gfull = {n: [None] * (DEPTH if n.startswith("ffn") else 2) for n in BIG}
    dmod = [None] * DEPTH
    g_ln = {n: [None] * DEPTH for n in ("ln_mix_g", "ln_mix_b", "ln_ffn_g", "ln_ffn_b")}
    g_qn, g_kvn, g_sink, g_bqkv, g_bo = [None] * 2, [None] * 2, [None] * 2, [None] * 2, [None] * 2
    for i in reversed(range(DEPTH)):
        j = i // 2
        st = saved[i]
        dres, dy, da, sm = _outproj_ln_bwd(dx, st["u_f"], st["y_f"], w_down[i], modl[i], 5,
                                           inp["ln_ffn_g"][i][None], tmf, "ffn_out_bwd")
        g_ln["ln_ffn_g"][i], g_ln["ln_ffn_b"][i], dg_f = sm[0], sm[1], sm[2]
        gfull["ffn_w_down"][i] = _wgrad(st["a"], dy, tw, F // 2, D, "wgrad_down")
        dgp, dup, dx, sm = _ffn_mid_bwd(da, st["g"], st["up"], st["x1"], modl[i], dres,
                                        w_gate[i], w_up[i], tm, tnf)
        dsc_f, dsh_f = sm[0], sm[1]
        gfull["ffn_w_gate"][i] = _wgrad(st["x1"], dgp, tw, D, tnf, "wgrad_gate", modl[i], (4, 3))
        gfull["ffn_w_up"][i] = _wgrad(st["x1"], dup, tw, D, tnf, "wgrad_up", modl[i], (4, 3))

        dres, dy, do, sm = _outproj_ln_bwd(dx, st["u_m"], st["y_m"], st["w_o"], modl[i], 2,
                                           inp["ln_mix_g"][i][None], tm, f"mix_out_bwd_{i % 2}")
        g_ln["ln_mix_g"][i], g_ln["ln_mix_b"][i], dg_m = sm[0], sm[1], sm[2]
        if i % 2 == 0:
            gfull["mla_w_o"][j] = _wgrad(st["o"], dy, tw, D, D, "wgrad_mla_o")
            delta = _flash_delta(st["o"], do, tb)
            if i == 0:
                late = {(n, lo): jnp.stack(gfull[n][lo:hi]) for n, lo, hi in LATE}
                dq, dk, dv, got = _flash_bwd(st["q"], st["k"], st["v"], do, st["lse"], delta, tbf,
                                             _pack_full(late, LATE))
                gparts = _unpack_blocks(got, LATE)
            else:
                dq, dk, dv = _flash_bwd(st["q"], st["k"], st["v"], do, st["lse"], delta, tbf)
            dx, dwin, dwq, dwkv, sm, dqn, dkvn = _mla_proj_bwd(
                dq, dk, dv, st["x0"], modl[i], dres, w_in[j], inp["mla_q_norm"][j][None], w_q[j],
                inp["mla_kv_norm"][j][None], w_kv[j], tabs_a_neg, tm)
            gfull["mla_w_in"][j] = dwin[:, :704]
            gfull["mla_w_q_b"][j] = _unpad_heads(dwq, 1, MLA_H, 192, MLA_HD)
            gfull["mla_w_kv_b"][j] = dwkv
            g_qn[j], g_kvn[j] = dqn[0], dkvn[0]
        else:
            g_bo[j] = sm[3]
            dwo = _wgrad(st["o"], dy, tw, SWA_O // 2, D, "wgrad_swa_o")
            gfull["swa_w_o"][j] = _unpad_heads(dwo, 0, SWA_HQ, 64, LANE)
            dq, dkvc, dkvp, dsink = _swa_attn_bwd(st["qkv"], inp["swa_sinks"][j], do, tb)
            g_sink[j] = dsink[0, :SWA_HQ]
            dx, dz, sm, db = _swa_proj_bwd(dq, dkvc, dkvp, st["x0"], modl[i], dres, w_qkv[j], tabs_b_neg, tm)
            dwqkv = _wgrad(st["x0"], dz, tw, D, SWA_QKV // 2, "wgrad_swa_qkv", modl[i], (1, 0))
            gfull["swa_w_qkv"][j] = _unpad_heads(dwqkv, 1, SWA_HQ + 2 * SWA_HKV, 64, LANE)
            g_bqkv[j] = _unpad_heads(db[0], 0, SWA_HQ + 2 * SWA_HKV, 64, LANE)
        dmod[i] = jnp.stack([sm[1], sm[0], dg_m, dsh_f, dsc_f, dg_f])
    grad_x = dx[None]

    small_shapes = [(DEPTH, 6 * D), (DEPTH, D), (DEPTH, D), (DEPTH, D), (DEPTH, D), (2, MLA_QR), (2, MLA_KVR),
                    (2, SWA_HQ), (2, 1536), (2, D)]
    small_vals = [jnp.stack(dmod).reshape(DEPTH, 6 * D), jnp.stack(g_ln["ln_mix_g"]), jnp.stack(g_ln["ln_mix_b"]),
                  jnp.stack(g_ln["ln_ffn_g"]), jnp.stack(g_ln["ln_ffn_b"]), jnp.stack(g_qn), jnp.stack(g_kvn),
                  jnp.stack(g_sink), jnp.stack(g_bqkv), jnp.stack(g_bo)]
    nsmall = sum(math.prod(s) for s in small_shapes)
    small_rows = -(-nsmall // (8 * LANE)) * 8
    (dmod_all, p_lmg, p_lmb, p_lfg, p_lfb, p_qn, p_kvn, p_sink, p_bqkv, p_bo) = _small_unpack(
        _exchange(_small_pack(small_vals, small_rows), True, "gather_small_grads"), small_shapes)

    early = {(n, lo): jnp.stack(gfull[n][lo:hi]) for n, lo, hi in EARLY}
    gparts.update(_unpack_blocks(_exchange(_pack_full(early, EARLY), False, "scatter_early_grads"), EARLY))
    for n, _, _ in EARLY:
        gparts[(n, 0)] = jnp.concatenate([gparts[(n, 0)], gparts.pop((n, 1))], axis=1)

    res = {}

    def update(name, parts):
        w = inp[name]
        shp = w.shape
        r2 = (math.prod(shp[:-1]), shp[-1])
        outs = _adamw(parts.reshape((parts.shape[0],) + r2), w.reshape(r2), inp["m_" + name].reshape(r2),
                      inp["v_" + name].reshape(r2), "adamw_" + name)
        res[name] = tuple(o.reshape(shp) for o in outs)

    dmod_loc = lax.dynamic_slice_in_dim(dmod_all, me * ncol, ncol, axis=2)
    g_ada_w = _ada_w_grad(c_all.T, jnp.moveaxis(dmod_loc, 0, 1))
    update("ada_w", g_ada_w[None])
    update("ada_b", dmod_all)
    update("ln_mix_g", p_lmg)
    update("ln_mix_b", p_lmb)
    update("ln_ffn_g", p_lfg)
    update("ln_ffn_b", p_lfb)
    for name in BIG:
        update(name, gparts[(name, 0)])
    update("mla_q_norm", p_qn)
    update("mla_kv_norm", p_kvn)
    update("swa_sinks", p_sink)
    nb = 1536 // NDEV
    update("swa_b_qkv", lax.dynamic_slice_in_dim(p_bqkv, me * nb, nb, axis=2))
    update("swa_b_o", lax.dynamic_slice_in_dim(p_bo, me * (D // NDEV), D // NDEV, axis=2))
    return loss, grad_x, res


WEIGHTS = ["ada_w", "ada_b", "ln_mix_g", "ln_mix_b", "ln_ffn_g", "ln_ffn_b", "ffn_w_gate", "ffn_w_up",
           "ffn_w_down", "mla_w_in", "mla_q_norm", "mla_w_q_b", "mla_kv_norm", "mla_w_kv_b", "mla_w_o",
           "swa_w_qkv", "swa_b_qkv", "swa_sinks", "swa_w_o", "swa_b_o"]
INPUTS = (["x", "c", "positions"] + WEIGHTS + ["loss_target"] + ["m_" + n for n in WEIGHTS]
          + ["v_" + n for n in WEIGHTS])


def kernel(x, c, positions, ada_w, ada_b, ln_mix_g, ln_mix_b, ln_ffn_g, ln_ffn_b, ffn_w_gate, ffn_w_up, ffn_w_down, mla_w_in, mla_q_norm, mla_w_q_b, mla_kv_norm, mla_w_kv_b, mla_w_o, swa_w_qkv, swa_b_qkv, swa_sinks, swa_w_o, swa_b_o, loss_target, m_ada_w, m_ada_b, m_ln_mix_g, m_ln_mix_b, m_ln_ffn_g, m_ln_ffn_b, m_ffn_w_gate, m_ffn_w_up, m_ffn_w_down, m_mla_w_in, m_mla_q_norm, m_mla_w_q_b, m_mla_kv_norm, m_mla_w_kv_b, m_mla_w_o, m_swa_w_qkv, m_swa_b_qkv, m_swa_sinks, m_swa_w_o, m_swa_b_o, v_ada_w, v_ada_b, v_ln_mix_g, v_ln_mix_b, v_ln_ffn_g, v_ln_ffn_b, v_ffn_w_gate, v_ffn_w_up, v_ffn_w_down, v_mla_w_in, v_mla_q_norm, v_mla_w_q_b, v_mla_kv_norm, v_mla_w_kv_b, v_mla_w_o, v_swa_w_qkv, v_swa_b_qkv, v_swa_sinks, v_swa_w_o, v_swa_b_o):
    args = (x, c, positions, ada_w, ada_b, ln_mix_g, ln_mix_b, ln_ffn_g, ln_ffn_b, ffn_w_gate, ffn_w_up, ffn_w_down, mla_w_in, mla_q_norm, mla_w_q_b, mla_kv_norm, mla_w_kv_b, mla_w_o, swa_w_qkv, swa_b_qkv, swa_sinks, swa_w_o, swa_b_o, loss_target, m_ada_w, m_ada_b, m_ln_mix_g, m_ln_mix_b, m_ln_ffn_g, m_ln_ffn_b, m_ffn_w_gate, m_ffn_w_up, m_ffn_w_down, m_mla_w_in, m_mla_q_norm, m_mla_w_q_b, m_mla_kv_norm, m_mla_w_kv_b, m_mla_w_o, m_swa_w_qkv, m_swa_b_qkv, m_swa_sinks, m_swa_w_o, m_swa_b_o, v_ada_w, v_ada_b, v_ln_mix_g, v_ln_mix_b, v_ln_ffn_g, v_ln_ffn_b, v_ffn_w_gate, v_ffn_w_up, v_ffn_w_down, v_mla_w_in, v_mla_q_norm, v_mla_w_q_b, v_mla_kv_norm, v_mla_w_kv_b, v_mla_w_o, v_swa_w_qkv, v_swa_b_qkv, v_swa_sinks, v_swa_w_o, v_swa_b_o)
    assert len(args) == len(INPUTS)
    loss, grad_x, res = _step(dict(zip(INPUTS, args)))
    return (loss, grad_x, *[res[n][0] for n in WEIGHTS], *[res[n][1] for n in WEIGHTS],
            *[res[n][2] for n in WEIGHTS], *[res[n][3] for n in WEIGHTS])
```

```python
import functools
import math

import jax
import jax.numpy as jnp
from jax import lax
from jax.experimental import pallas as pl
from jax.experimental.pallas import tpu as pltpu

F32 = jnp.float32
BF = jnp.bfloat16

NDEV = 8
D = 1024
DEPTH = 4
F = 2816
ALPHA = (2 * DEPTH) ** 0.25
LN_EPS = 1e-5
RMS_EPS = 1e-6
ROPE_THETA = 500000.0

MLA_H = 8
MLA_QR = 384
MLA_KVR = 256
MLA_ROPE = 64
MLA_LAT = 768
MLA_HD = 256
MLA_SCALE = (128 + 64) ** -0.5

SWA_HQ = 16
SWA_HKV = 4
SWA_W = 128
SWA_SCALE = 64 ** -0.5
SWA_QKV = (SWA_HQ + 2 * SWA_HKV) * 128
SWA_O = SWA_HQ * 128

LANE = 128
VMEM_LIMIT = 56 * 2 ** 20

ADAM_LR, ADAM_B1, ADAM_B2, ADAM_EPS, ADAM_WD, ADAM_STEP = 0.001, 0.9, 0.999, 1e-8, 0.01, 10


def _params(n_axes):
    return pltpu.CompilerParams(dimension_semantics=("arbitrary",) * n_axes, vmem_limit_bytes=VMEM_LIMIT)


def _dot(a, b):
    return jnp.dot(a, b, preferred_element_type=F32)


def _dot_nt(a, b):
    return lax.dot_general(a, b, (((1,), (1,)), ((), ())), preferred_element_type=F32)


def _dot_tn(a, b):
    return lax.dot_general(a, b, (((0,), (0,)), ((), ())), preferred_element_type=F32)


def _full(shape):
    return pl.BlockSpec(shape, lambda *_: (0,) * len(shape))


def _resident(shape):
    return pl.BlockSpec(shape, lambda *_: (0,) * len(shape), pipeline_mode=pl.Buffered(1))


def _sigmoid(x):
    return 1.0 / (1.0 + jnp.exp(-x))


def _rope128(x, ct, s1, s2, half):
    return x * ct + pltpu.roll(x, LANE - half, 1) * s1 + pltpu.roll(x, half, 1) * s2


def _eye(n):
    return lax.broadcasted_iota(jnp.int32, (n, n), 0) == lax.broadcasted_iota(jnp.int32, (n, n), 1)


def _col_to_row(col):
    n = col.shape[0]
    return jnp.sum(jnp.where(_eye(n), col, 0.0), axis=0, keepdims=True)


def _row_to_col(row):
    n = row.shape[1]
    return jnp.sum(jnp.where(_eye(n), row, 0.0), axis=1, keepdims=True)


def _modulate(x, modl_ref, sc_row, sh_row):
    return x * (1.0 + modl_ref[sc_row:sc_row + 1, :]) + modl_ref[sh_row:sh_row + 1, :]


EXCHANGE_SEMS = [pltpu.SemaphoreType.DMA((NDEV - 1,)), pltpu.SemaphoreType.DMA((NDEV - 1,)), pltpu.SemaphoreType.DMA]


def _exchange_copies(src_ref, out_ref, send_sems, recv_sems, local_sem, gather):
    x, y, c = lax.axis_index("x"), lax.axis_index("y"), lax.axis_index("c")
    me = 4 * x + 2 * y + c

    def piece(dev):
        return src_ref if gather else src_ref.at[dev]

    mine = pltpu.make_async_copy(piece(me), out_ref.at[me], local_sem)
    sends, recvs = [], []
    for k in range(1, NDEV):
        px = 1 - x if k & 4 else x
        py = 1 - y if k & 2 else y
        pc = 1 - c if k & 1 else c
        peer = 4 * px + 2 * py + pc
        common = dict(send_sem=send_sems.at[k - 1], recv_sem=recv_sems.at[k - 1],
                      device_id=(px, py, pc), device_id_type=pl.DeviceIdType.MESH)
        sends.append(pltpu.make_async_remote_copy(src_ref=piece(peer), dst_ref=out_ref.at[me], **common))
        recvs.append(pltpu.make_async_remote_copy(src_ref=piece(peer), dst_ref=out_ref.at[peer], **common))
    return mine, sends, recvs


def _exchange_start(*refs, gather):
    mine, sends, _ = _exchange_copies(*refs, gather)
    mine.start()
    for s in sends:
        s.start()


def _exchange_wait(*refs, gather):
    mine, sends, recvs = _exchange_copies(*refs, gather)
    for r in recvs:
        r.wait_recv()
    for s in sends:
        s.wait_send()
    mine.wait()


def _exchange_out(src, gather):
    blk = tuple(src.shape) if gather else tuple(src.shape[1:])
    return jax.ShapeDtypeStruct((NDEV,) + blk, src.dtype)


def _exchange(src, gather, name):
    def body(*refs):
        _exchange_start(*refs, gather=gather)
        _exchange_wait(*refs, gather=gather)

    return pl.pallas_call(
        body, name=name,
        out_shape=_exchange_out(src, gather),
        in_specs=[pl.BlockSpec(memory_space=pltpu.HBM)],
        out_specs=pl.BlockSpec(memory_space=pltpu.HBM),
        scratch_shapes=EXCHANGE_SEMS,
    )(src)


def _mod_all(c_all, ada_w, ada_b_loc):
    ncol = ada_w.shape[2]

    def body(c_ref, w_ref, b_ref, o_ref):
        cv = c_ref[...]
        cond = cv * _sigmoid(cv)
        o_ref[0] = _dot(cond.astype(BF), w_ref[0].astype(BF)) + b_ref[0]

    return pl.pallas_call(
        body, name="mod_all", grid=(DEPTH,),
        out_shape=jax.ShapeDtypeStruct((DEPTH, NDEV, ncol), F32),
        in_specs=[_full((NDEV, D)), pl.BlockSpec((1, D, ncol), lambda i: (i, 0, 0)),
                  pl.BlockSpec((1, 1, ncol), lambda i: (i, 0, 0))],
        out_specs=pl.BlockSpec((1, NDEV, ncol), lambda i: (i, 0, 0)),
        compiler_params=_params(1),
    )(c_all, ada_w, ada_b_loc)


def _ada_w_grad(c_all_t, dmod_loc):
    ncol = dmod_loc.shape[2]

    def body(ct_ref, dm_ref, o_ref):
        cv = ct_ref[...]
        cond = cv * _sigmoid(cv)
        acc = cond[:, 0:1] * dm_ref[0, 0:1, :]
        for b in range(1, NDEV):
            acc = acc + cond[:, b:b + 1] * dm_ref[0, b:b + 1, :]
        o_ref[0] = acc

    return pl.pallas_call(
        body, name="ada_w_grad", grid=(DEPTH,),
        out_shape=jax.ShapeDtypeStruct((DEPTH, D, ncol), F32),
        in_specs=[_full((D, NDEV)), pl.BlockSpec((1, NDEV, ncol), lambda i: (i, 0, 0))],
        out_specs=pl.BlockSpec((1, D, ncol), lambda i: (i, 0, 0)),
        compiler_params=_params(1),
    )(c_all_t, dmod_loc)


def _mla_proj_fwd(x, modl, w_in, q_norm, w_q, kv_norm, w_kv, tabs, tm):
    T = x.shape[0]
    ct_a, s1_a, s2_a = tabs

    def body(x_ref, modl_ref, win_ref, qn_ref, wq_ref, kvn_ref, wkv_ref, ct_ref, s1_ref, s2_ref,
             q_ref, k_ref, v_ref):
        h = _modulate(x_ref[...], modl_ref, 1, 0).astype(BF)
        lat = _dot(h, win_ref[...])
        ql, kvl, kr = lat[:, :MLA_QR], lat[:, MLA_QR:MLA_QR + MLA_KVR], lat[:, MLA_QR + MLA_KVR:]
        qn = (ql * lax.rsqrt(jnp.mean(ql * ql, axis=1, keepdims=True) + RMS_EPS) * qn_ref[...]).astype(BF)
        kvn = (kvl * lax.rsqrt(jnp.mean(kvl * kvl, axis=1, keepdims=True) + RMS_EPS) * kvn_ref[...]).astype(BF)
        ct, s1, s2 = ct_ref[...], s1_ref[...], s2_ref[...]
        kr = _rope128(kr, ct, s1, s2, MLA_ROPE // 2).astype(BF)
        for hd in range(MLA_H):
            cols = slice(hd * MLA_HD, (hd + 1) * MLA_HD)
            qh = _dot(qn, wq_ref[:, cols])
            q_ref[hd, :, 0:LANE] = qh[:, :LANE].astype(BF)
            q_ref[hd, :, LANE:MLA_HD] = _rope128(qh[:, LANE:], ct, s1, s2, MLA_ROPE // 2).astype(BF)
            kvh = _dot(kvn, wkv_ref[:, cols])
            k_ref[hd, :, 0:LANE] = kvh[:, :LANE].astype(BF)
            k_ref[hd, :, LANE:MLA_HD] = kr
            v_ref[hd, :, 0:LANE] = kvh[:, LANE:].astype(BF)
            v_ref[hd, :, LANE:2 * LANE] = jnp.ones((tm, LANE), BF)

    row = lambda i: (i, 0)
    head = lambda i: (0, i, 0)
    return pl.pallas_call(
        body, name="mla_proj_fwd", grid=(T // tm,),
        out_shape=(jax.ShapeDtypeStruct((MLA_H, T, MLA_HD), BF), jax.ShapeDtypeStruct((MLA_H, T, MLA_HD), BF),
                   jax.ShapeDtypeStruct((MLA_H, T, 2 * LANE), BF)),
        in_specs=[pl.BlockSpec((tm, D), row), _full((8, D)), _full((D, MLA_LAT)), _full((1, MLA_QR)),
                  _full((MLA_QR, MLA_H * MLA_HD)), _full((1, MLA_KVR)), _full((MLA_KVR, MLA_H * MLA_HD)),
                  pl.BlockSpec((tm, LANE), row), pl.BlockSpec((tm, LANE), row), pl.BlockSpec((tm, LANE), row)],
        out_specs=(pl.BlockSpec((MLA_H, tm, MLA_HD), head), pl.BlockSpec((MLA_H, tm, MLA_HD), head),
                   pl.BlockSpec((MLA_H, tm, 2 * LANE), head)),
        compiler_params=_params(1),
    )(x, modl, w_in, q_norm, w_q, kv_norm, w_kv, ct_a, s1_a, s2_a)


def _causal_mask(row0, nrows, ncols, transposed):
    row = lax.broadcasted_iota(jnp.int32, (nrows, ncols), 0) + row0
    col = lax.broadcasted_iota(jnp.int32, (nrows, ncols), 1)
    return (row <= col) if transposed else (col <= row)


def _flash_fwd(q, k, v, tb, comm_src=None):
    H, T, _ = q.shape
    rh = min(256, tb)
    c_exp = MLA_SCALE * math.log2(math.e)

    def body(*refs):
        if comm_src is None:
            q_ref, k_ref, v_ref, o_ref, lse_ref, m_s, acc_s = refs
        else:
            q_ref, k_ref, v_ref, src_ref, o_ref, lse_ref, got_ref, m_s, acc_s, *sems = refs
            comm = (src_ref, got_ref, *sems)
            first = (pl.program_id(0) == 0) & (pl.program_id(1) == 0)
            last = (pl.program_id(0) == H - 1) & (pl.program_id(1) == T // tb - 1)
            pl.when(first)(functools.partial(_exchange_start, *comm, gather=True))
        i = pl.program_id(1)
        m_s[...] = jnp.full(m_s.shape, -jnp.inf, F32)
        acc_s[...] = jnp.zeros(acc_s.shape, F32)

        def block(j, masked):
            base = pl.multiple_of(j * tb, tb)
            for hf in range(tb // rh):
                r = slice(hf * rh, (hf + 1) * rh)
                nk = (hf + 1) * rh if masked else tb
                kb = k_ref[0, pl.ds(base, nk), :]
                vb = v_ref[0, pl.ds(base, nk), :]
                s = _dot_nt(q_ref[0, r, :], kb)
                if masked:
                    s = jnp.where(_causal_mask(hf * rh, rh, nk, False), s, -jnp.inf)
                sc = [s[:, c * LANE:(c + 1) * LANE] for c in range(nk // LANE)]
                mx = sc[0]
                for x in sc[1:]:
                    mx = jnp.maximum(mx, x)
                m_prev = m_s[r, :]
                m_new = jnp.maximum(m_prev, jnp.max(mx, axis=1, keepdims=True))
                p = jnp.concatenate([jnp.exp2((x - m_new) * c_exp) for x in sc], axis=1)
                corr = jnp.exp2((m_prev - m_new) * c_exp)
                acc_s[r, :] = jnp.concatenate([corr, corr], axis=1) * acc_s[r, :] + _dot(p.astype(BF), vb)
                m_s[r, :] = m_new

        def step(j, carry):
            block(j, False)
            return carry

        lax.fori_loop(0, i, step, 0)
        block(i, True)
        l = acc_s[:, LANE:]
        o_ref[...] = (acc_s[:, :LANE] / l).astype(BF)
        lse = (m_s[...] * MLA_SCALE + jnp.log(l))[:, 0:1]
        for c0 in range(0, tb, rh):
            lse_ref[0, :, c0:c0 + rh] = _col_to_row(lse[c0:c0 + rh])
        if comm_src is not None:
            pl.when(last)(functools.partial(_exchange_wait, *comm, gather=True))

    hbm = pl.BlockSpec(memory_space=pltpu.HBM)
    with_comm = comm_src is not None
    return pl.pallas_call(
        body, name="flash_fwd_gather" if with_comm else "flash_fwd", grid=(H, T // tb),
        out_shape=(jax.ShapeDtypeStruct((T, H * LANE), BF), jax.ShapeDtypeStruct((H, 1, T), F32))
        + ((_exchange_out(comm_src, True),) if with_comm else ()),
        in_specs=[pl.BlockSpec((1, tb, MLA_HD), lambda h, i: (h, i, 0)),
                  pl.BlockSpec((1, T, MLA_HD), lambda h, i: (h, 0, 0)),
                  pl.BlockSpec((1, T, 2 * LANE), lambda h, i: (h, 0, 0))] + ([hbm] if with_comm else []),
        out_specs=(pl.BlockSpec((tb, LANE), lambda h, i: (i, h)),
                   pl.BlockSpec((1, 1, tb), lambda h, i: (h, 0, i))) + ((hbm,) if with_comm else ()),
        scratch_shapes=[pltpu.VMEM((tb, LANE), F32), pltpu.VMEM((tb, 2 * LANE), F32)]
        + (EXCHANGE_SEMS if with_comm else []),
        compiler_params=_params(2),
    )(*((q, k, v, comm_src) if with_comm else (q, k, v)))


def _outproj_ln_fwd(a, w, bias, x, modl, g_row, ln_g, ln_b, tm, name):
    T, K = a.shape

    def body(a_ref, w_ref, b_ref, x_ref, modl_ref, g_ref, bb_ref, y_ref, u_ref, xn_ref):
        y = _dot(a_ref[...], w_ref[...]) + b_ref[...]
        u = ALPHA * x_ref[...] + modl_ref[g_row:g_row + 1, :] * y
        mu = jnp.mean(u, axis=1, keepdims=True)
        uc = u - mu
        var = jnp.mean(uc * uc, axis=1, keepdims=True)
        y_ref[...] = y.astype(BF)
        u_ref[...] = u
        xn_ref[...] = uc * lax.rsqrt(var + LN_EPS) * g_ref[...] + bb_ref[...]

    row = lambda i: (i, 0)
    return pl.pallas_call(
        body, name=name, grid=(T // tm,),
        out_shape=(jax.ShapeDtypeStruct((T, D), BF), jax.ShapeDtypeStruct((T, D), F32),
                   jax.ShapeDtypeStruct((T, D), F32)),
        in_specs=[pl.BlockSpec((tm, K), row), _resident((K, D)), _full((1, D)), pl.BlockSpec((tm, D), row),
                  _full((8, D)), _full((1, D)), _full((1, D))],
        out_specs=(pl.BlockSpec((tm, D), row),) * 3,
        compiler_params=_params(1),
    )(a, w, bias, x, modl, ln_g, ln_b)


def _ffn_up_fwd(x, modl, wg, wu, tm, tn):
    T = x.shape[0]

    def body(x_ref, modl_ref, wg_ref, wu_ref, g_ref, u_ref, a_ref):
        h = _modulate(x_ref[...], modl_ref, 4, 3).astype(BF)
        g = _dot(h, wg_ref[...])
        u = _dot(h, wu_ref[...])
        g_ref[...] = g.astype(BF)
        u_ref[...] = u.astype(BF)
        a_ref[...] = (g * _sigmoid(g) * u).astype(BF)

    tile = pl.BlockSpec((tm, tn), lambda n, i: (i, n))
    wcol = pl.BlockSpec((D, tn), lambda n, i: (0, n))
    return pl.pallas_call(
        body, name="ffn_up_fwd", grid=(F // tn, T // tm),
        out_shape=(jax.ShapeDtypeStruct((T, F), BF),) * 3,
        in_specs=[pl.BlockSpec((tm, D), lambda n, i: (i, 0)), _full((8, D)), wcol, wcol],
        out_specs=(tile, tile, tile),
        compiler_params=_params(2),
    )(x, modl, wg, wu)


def _swa_proj_fwd(x, modl, w, b, tabs, tm):
    T = x.shape[0]
    ct_b, s1_b, s2_b = tabs
    n_rope = SWA_HQ + SWA_HKV

    def body(x_ref, modl_ref, w_ref, b_ref, ct_ref, s1_ref, s2_ref, o_ref):
        h = _modulate(x_ref[...], modl_ref, 1, 0).astype(BF)
        ct, s1, s2 = ct_ref[...], s1_ref[...], s2_ref[...]
        for pair in range(SWA_QKV // (2 * LANE)):
            cols2 = slice(2 * pair * LANE, (2 * pair + 2) * LANE)
            z2 = _dot(h, w_ref[:, cols2]) + b_ref[:, cols2]
            for half in range(2):
                grp = 2 * pair + half
                z = z2[:, half * LANE:(half + 1) * LANE]
                if grp < n_rope:
                    z = _rope128(z, ct, s1, s2, 8)
                o_ref[:, grp * LANE:(grp + 1) * LANE] = z.astype(BF)

    row = lambda i: (i, 0)
    return pl.pallas_call(
        body, name="swa_proj_fwd", grid=(T // tm,),
        out_shape=jax.ShapeDtypeStruct((T, SWA_QKV), BF),
        in_specs=[pl.BlockSpec((tm, D), row), _full((8, D)), _full((D, SWA_QKV)), _full((1, SWA_QKV)),
                  pl.BlockSpec((tm, LANE), row), pl.BlockSpec((tm, LANE), row), pl.BlockSpec((tm, LANE), row)],
        out_specs=pl.BlockSpec((tm, SWA_QKV), row),
        compiler_params=_params(1),
    )(x, modl, w, b, ct_b, s1_b, s2_b)


SWA_NB = 1
SWA_R = SWA_NB * 4 * SWA_W
SWA_NK = (SWA_NB + 1) * SWA_W


def _swa_bias(first):
    row = lax.broadcasted_iota(jnp.int32, (SWA_R, SWA_NK), 0)
    col = lax.broadcasted_iota(jnp.int32, (SWA_R, SWA_NK), 1)
    bl = row // (4 * SWA_W)
    r = row % SWA_W
    cp = col - bl * SWA_W
    band = (cp > r) & (cp <= r + SWA_W)
    if first:
        band = band & ((col >= SWA_W) | (bl > 0))
    return jnp.where(band, 0.0, -jnp.inf).astype(F32)


def _swa_fill_bias(bias_s):
    @pl.when(pl.program_id(0) == 0)
    def _():
        bias_s[0] = _swa_bias(False)
        bias_s[1] = _swa_bias(True)


def _swa_specs(T, tb):
    nsub = tb // SWA_W
    q_spec = pl.BlockSpec((tb, SWA_O), lambda i: (i, 0))
    kvc_spec = pl.BlockSpec((tb, 2 * SWA_HKV * LANE), lambda i: (i, 2))
    kvp_spec = pl.BlockSpec((SWA_W, 2 * SWA_HKV * LANE), lambda i: (jnp.maximum(i * nsub - 1, 0), 2))
    return q_spec, kvc_spec, kvp_spec


def _swa_rows(ref, g, c):
    return jnp.concatenate(
        [ref[(c * SWA_NB + bl) * SWA_W:(c * SWA_NB + bl + 1) * SWA_W, (4 * g + hh) * LANE:(4 * g + hh + 1) * LANE]
         for bl in range(SWA_NB) for hh in range(4)], axis=0)


def _swa_chain(q_ref, kall, vall, sink_ref, bias_s, g, c):
    i = pl.program_id(0)
    lane = lax.broadcasted_iota(jnp.int32, (1, LANE), 1)
    qc = _swa_rows(q_ref, g, c)
    keys = slice(c * SWA_NB * SWA_W, c * SWA_NB * SWA_W + SWA_NK)
    k3 = kall[keys]
    v3 = jnp.where(lane < 64, vall[keys], jnp.ones((), BF))
    bias = bias_s[jnp.where(i == 0, 1, 0)] if c == 0 else bias_s[0]
    s = _dot_nt(qc, k3)
    c_exp = SWA_SCALE * math.log2(math.e)
    sb = [s[:, ch * LANE:(ch + 1) * LANE] * c_exp + bias[:, ch * LANE:(ch + 1) * LANE] for ch in range(SWA_NK // LANE)]
    mx = sb[0]
    for x in sb[1:]:
        mx = jnp.maximum(mx, x)
    sink2 = jnp.concatenate([jnp.full((SWA_W, 1), sink_ref[4 * g + hh] * math.log2(math.e), F32)
                             for _ in range(SWA_NB) for hh in range(4)], axis=0)
    m = jnp.maximum(jnp.max(mx, axis=1, keepdims=True), sink2)
    m_rep = jnp.broadcast_to(m, (SWA_R, LANE))
    eb = jnp.concatenate([jnp.exp2(x - m_rep) for x in sb], axis=1).astype(BF)
    es = jnp.exp2(sink2 - m)
    acc = _dot(eb, v3)
    linv = 1.0 / (acc[:, 64:65] + es)
    num = jnp.where(lane < 64, acc, 0.0)
    return qc, k3, v3, eb, num, linv, es


def _swa_attn_fwd(qkv, sinks, tb):
    T = qkv.shape[0]
    kw = SWA_HKV * LANE

    def body(q_ref, kvc_ref, kvp_ref, sink_ref, o_ref, bias_s):
        _swa_fill_bias(bias_s)
        for g in range(SWA_HKV):
            gl = slice(g * LANE, (g + 1) * LANE)
            gv = slice(kw + g * LANE, kw + (g + 1) * LANE)
            kall = jnp.concatenate([kvp_ref[:, gl], kvc_ref[:, gl]], axis=0)
            vall = jnp.concatenate([kvp_ref[:, gv], kvc_ref[:, gv]], axis=0)
            for c in range(tb // (SWA_NB * SWA_W)):
                _, _, _, _, num, linv, _ = _swa_chain(q_ref, kall, vall, sink_ref, bias_s, g, c)
                o = (num * linv).astype(BF)
                for bl in range(SWA_NB):
                    for hh in range(4):
                        piece = (bl * 4 + hh) * SWA_W
                        rows = slice((c * SWA_NB + bl) * SWA_W, (c * SWA_NB + bl + 1) * SWA_W)
                        o_ref[rows, (4 * g + hh) * LANE:(4 * g + hh + 1) * LANE] = o[piece:piece + SWA_W]

    q_spec, kvc_spec, kvp_spec = _swa_specs(T, tb)
    return pl.pallas_call(
        body, name="swa_attn_fwd", grid=(T // tb,),
        out_shape=jax.ShapeDtypeStruct((T, SWA_O), BF),
        in_specs=[q_spec, kvc_spec, kvp_spec, pl.BlockSpec(memory_space=pltpu.SMEM)],
        out_specs=pl.BlockSpec((tb, SWA_O), lambda i: (i, 0)),
        scratch_shapes=[pltpu.VMEM((2, SWA_R, SWA_NK), F32)],
        compiler_params=_params(1),
    )(qkv, qkv, qkv, sinks)


def _loss_grad(x, tgt, tm):
    T = x.shape[0]

    def body(x_ref, t_ref, dx_ref, l_ref):
        @pl.when(pl.program_id(0) == 0)
        def _():
            l_ref[...] = jnp.zeros(l_ref.shape, F32)
        diff = x_ref[...] - t_ref[...]
        dx_ref[...] = diff * (1.0 / D)
        l_ref[0:1, :] += jnp.sum(diff * diff, axis=0, keepdims=True) * (0.5 / D)

    row = lambda i: (i, 0)
    return pl.pallas_call(
        body, name="loss_grad", grid=(T // tm,),
        out_shape=(jax.ShapeDtypeStruct((T, D), F32), jax.ShapeDtypeStruct((8, D), F32)),
        in_specs=[pl.BlockSpec((tm, D), row), pl.BlockSpec((tm, D), row)],
        out_specs=(pl.BlockSpec((tm, D), row), _full((8, D))),
        compiler_params=_params(1),
    )(x, tgt)


def _outproj_ln_bwd(dxn, u, y, w, modl, g_row, ln_g, tm, name):
    T = dxn.shape[0]
    K = w.shape[0]

    def body(dxn_ref, u_ref, y_ref, w_ref, modl_ref, g_ref, dres_ref, dy_ref, da_ref, sm_ref):
        @pl.when(pl.program_id(0) == 0)
        def _():
            sm_ref[...] = jnp.zeros(sm_ref.shape, F32)
        uu = u_ref[...]
        mu = jnp.mean(uu, axis=1, keepdims=True)
        uc = uu - mu
        rstd = lax.rsqrt(jnp.mean(uc * uc, axis=1, keepdims=True) + LN_EPS)
        xhat = uc * rstd
        dxo = dxn_ref[...]
        dyh = dxo * g_ref[...]
        du = rstd * (dyh - jnp.mean(dyh, axis=1, keepdims=True)
                     - xhat * jnp.mean(dyh * xhat, axis=1, keepdims=True))
        dy = modl_ref[g_row:g_row + 1, :] * du
        dyb = dy.astype(BF)
        dres_ref[...] = ALPHA * du
        dy_ref[...] = dyb
        da_ref[...] = _dot_nt(dyb, w_ref[...]).astype(BF)
        sm_ref[0:1, :] += jnp.sum(dxo * xhat, axis=0, keepdims=True)
        sm_ref[1:2, :] += jnp.sum(dxo, axis=0, keepdims=True)
        sm_ref[2:3, :] += jnp.sum(du * y_ref[...].astype(F32), axis=0, keepdims=True)
        sm_ref[3:4, :] += jnp.sum(dy, axis=0, keepdims=True)

    row = lambda i: (i, 0)
    return pl.pallas_call(
        body, name=name, grid=(T // tm,),
        out_shape=(jax.ShapeDtypeStruct((T, D), F32), jax.ShapeDtypeStruct((T, D), BF),
                   jax.ShapeDtypeStruct((T, K), BF), jax.ShapeDtypeStruct((8, D), F32)),
        in_specs=[pl.BlockSpec((tm, D), row), pl.BlockSpec((tm, D), row), pl.BlockSpec((tm, D), row),
                  _resident((K, D)), _full((8, D)), _full((1, D))],
        out_specs=(pl.BlockSpec((tm, D), row), pl.BlockSpec((tm, D), row), pl.BlockSpec((tm, K), row),
                   _full((8, D))),
        compiler_params=_params(1),
    )(dxn, u, y, w, modl, ln_g)


def _ffn_mid_bwd(da, g, u, x, modl, dres, wg, wu, tm, tn):
    T = x.shape[0]
    nn = F // tn

    def body(da_ref, g_ref, u_ref, x_ref, modl_ref, dres_ref, wg_ref, wu_ref, dg_ref, du_ref, dx_ref, sm_ref):
        i, n = pl.program_id(0), pl.program_id(1)

        @pl.when((i == 0) & (n == 0))
        def _():
            sm_ref[...] = jnp.zeros(sm_ref.shape, F32)

        gg = g_ref[...].astype(F32)
        sg = _sigmoid(gg)
        dav = da_ref[...].astype(F32)
        dgp = (dav * u_ref[...].astype(F32) * sg * (1.0 + gg * (1.0 - sg))).astype(BF)
        dup = (dav * gg * sg).astype(BF)
        dg_ref[...] = dgp
        du_ref[...] = dup
        dh = _dot_nt(dgp, wg_ref[...]) + _dot_nt(dup, wu_ref[...])

        @pl.when(n == 0)
        def _():
            dx_ref[...] = dh

        @pl.when(n > 0)
        def _():
            dx_ref[...] += dh

        @pl.when(n == nn - 1)
        def _():
            dht = dx_ref[...]
            sm_ref[0:1, :] += jnp.sum(dht * x_ref[...], axis=0, keepdims=True)
            sm_ref[1:2, :] += jnp.sum(dht, axis=0, keepdims=True)
            dx_ref[...] = dres_ref[...] + dht * (1.0 + modl_ref[4:5, :])

    tile = pl.BlockSpec((tm, tn), lambda i, n: (i, n))
    rowd = pl.BlockSpec((tm, D), lambda i, n: (i, 0))
    rowd_once = pl.BlockSpec((tm, D), lambda i, n: (i, 0), pipeline_mode=pl.Buffered(1))
    wcol = pl.BlockSpec((D, tn), lambda i, n: (0, n))
    return pl.pallas_call(
        body, name="ffn_mid_bwd", grid=(T // tm, nn),
        out_shape=(jax.ShapeDtypeStruct((T, F), BF), jax.ShapeDtypeStruct((T, F), BF),
                   jax.ShapeDtypeStruct((T, D), F32), jax.ShapeDtypeStruct((8, D), F32)),
        in_specs=[tile, tile, tile, rowd_once, _full((8, D)), rowd_once, wcol, wcol],
        out_specs=(tile, tile, rowd, _full((8, D))),
        compiler_params=_params(2),
    )(da, g, u, x, modl, dres, wg, wu)


def _wgrad(a, b, tm, tk, tn, name, modl=None, rows=None):
    T, K = a.shape
    N = b.shape[1]

    def body(*refs):
        if modl is None:
            a_ref, b_ref, o_ref = refs
            av = a_ref[...]
        else:
            a_ref, modl_ref, b_ref, o_ref = refs
            av = _modulate(a_ref[...], modl_ref, rows[0], rows[1]).astype(BF)

        @pl.when(pl.program_id(2) == 0)
        def _():
            o_ref[...] = jnp.zeros(o_ref.shape, F32)
        o_ref[...] += _dot_tn(av, b_ref[...])

    in_specs = [pl.BlockSpec((tm, tk), lambda k, n, t: (t, k))]
    args = [a]
    if modl is not None:
        in_specs.append(_full((8, D)))
        args.append(modl)
    in_specs.append(pl.BlockSpec((tm, tn), lambda k, n, t: (t, n)))
    args.append(b)
    return pl.pallas_call(
        body, name=name, grid=(K // tk, N // tn, T // tm),
        out_shape=jax.ShapeDtypeStruct((K, N), F32),
        in_specs=in_specs,
        out_specs=pl.BlockSpec((tk, tn), lambda k, n, t: (k, n)),
        compiler_params=_params(3),
    )(*args)


def _flash_delta(o, do, tb):
    T = o.shape[0]
    H = o.shape[1] // LANE

    def body(o_ref, do_ref, delta_ref):
        delta = jnp.sum(o_ref[...].astype(F32) * do_ref[...].astype(F32), axis=1, keepdims=True)
        delta_ref[0] = _col_to_row(delta)

    blk = pl.BlockSpec((tb, LANE), lambda h, i: (i, h))
    return pl.pallas_call(
        body, name="flash_delta", grid=(H, T // tb),
        out_shape=jax.ShapeDtypeStruct((H, 1, T), F32),
        in_specs=[blk, blk],
        out_specs=pl.BlockSpec((1, 1, tb), lambda h, i: (h, 0, i)),
        compiler_params=_params(2),
    )(o, do)


def _flash_bwd(q, k, v, do, lse, delta, tb, comm_src=None):
    H, T, _ = q.shape
    nq = T // tb
    rh = min(256, tb)
    c_exp = MLA_SCALE * math.log2(math.e)

    def body(*refs):
        if comm_src is None:
            k_ref, v_ref, q_ref, do_ref, lse_ref, delta_ref, dq_ref, dk_ref, dv_ref, dk_s, dv_s = refs
        else:
            (k_ref, v_ref, q_ref, do_ref, lse_ref, delta_ref, src_ref, dq_ref, dk_ref, dv_ref, got_ref,
             dk_s, dv_s, *sems) = refs
            comm = (src_ref, got_ref, *sems)
            first = (pl.program_id(0) == 0) & (pl.program_id(1) == 0)
            last = (pl.program_id(0) == H - 1) & (pl.program_id(1) == nq - 1)
            pl.when(first)(functools.partial(_exchange_start, *comm, gather=False))
        j = pl.program_id(1)

        @pl.when(j == 0)
        def _():
            def zero(cix, carry):
                dq_ref[0, pl.ds(pl.multiple_of(cix * tb, tb), tb), :] = jnp.zeros((tb, MLA_HD), F32)
                return carry
            lax.fori_loop(0, nq, zero, 0)

        dk_s[...] = jnp.zeros(dk_s.shape, F32)
        dv_s[...] = jnp.zeros(dv_s.shape, F32)

        def block(i, masked):
            dq_part = None
            for hf in range(tb // rh):
                r = slice(hf * rh, (hf + 1) * rh)
                off = hf * rh if masked else 0
                qrows = pl.ds(pl.multiple_of(i * tb + off, rh), tb - off)
                qb = q_ref[0, qrows, :]
                dob = do_ref[qrows, :]
                l2 = lse_ref[0, :, qrows] * math.log2(math.e)
                dl = delta_ref[0, :, qrows]
                kc = k_ref[0, r, :]
                st = _dot_nt(kc, qb)
                if masked:
                    st = jnp.where(_causal_mask(0, rh, tb - off, True), st, -jnp.inf)
                pt = jnp.exp2(st * c_exp - l2)
                dpt = _dot_nt(v_ref[0, r, :], dob)
                dst = (pt * (dpt - dl)).astype(BF)
                dv_s[r, :] += _dot(pt.astype(BF), dob)
                dk_s[r, :] += _dot(dst, qb)
                part = _dot_tn(dst, kc)
                if masked:
                    dq_ref[0, qrows, :] += part * MLA_SCALE
                else:
                    dq_part = part if dq_part is None else dq_part + part
            if not masked:
                dq_ref[0, pl.ds(pl.multiple_of(i * tb, tb), tb), :] += dq_part * MLA_SCALE

        def step(i, carry):
            block(i, False)
            return carry

        block(j, True)
        lax.fori_loop(j + 1, nq, step, 0)
        dk_ref[0] = (dk_s[...] * MLA_SCALE).astype(BF)
        dv_ref[0] = dv_s[...].astype(BF)
        if comm_src is not None:
            pl.when(last)(functools.partial(_exchange_wait, *comm, gather=False))

    once = pl.Buffered(1)
    hbm = pl.BlockSpec(memory_space=pltpu.HBM)
    with_comm = comm_src is not None
    return pl.pallas_call(
        body, name="flash_bwd_scatter" if with_comm else "flash_bwd", grid=(H, nq),
        out_shape=(jax.ShapeDtypeStruct((H, T, MLA_HD), F32), jax.ShapeDtypeStruct((H, T, MLA_HD), BF),
                   jax.ShapeDtypeStruct((H, T, LANE), BF)) + ((_exchange_out(comm_src, False),) if with_comm else ()),
        in_specs=[pl.BlockSpec((1, tb, MLA_HD), lambda h, j: (h, j, 0)),
                  pl.BlockSpec((1, tb, LANE), lambda h, j: (h, j, 0)),
                  pl.BlockSpec((1, T, MLA_HD), lambda h, j: (h, 0, 0), pipeline_mode=once),
                  pl.BlockSpec((T, LANE), lambda h, j: (0, h), pipeline_mode=once),
                  pl.BlockSpec((1, 1, T), lambda h, j: (h, 0, 0)),
                  pl.BlockSpec((1, 1, T), lambda h, j: (h, 0, 0))] + ([hbm] if with_comm else []),
        out_specs=(pl.BlockSpec((1, T, MLA_HD), lambda h, j: (h, 0, 0), pipeline_mode=once),
                   pl.BlockSpec((1, tb, MLA_HD), lambda h, j: (h, j, 0)),
                   pl.BlockSpec((1, tb, LANE), lambda h, j: (h, j, 0))) + ((hbm,) if with_comm else ()),
        scratch_shapes=[pltpu.VMEM((tb, MLA_HD), F32), pltpu.VMEM((tb, LANE), F32)]
        + (EXCHANGE_SEMS if with_comm else []),
        compiler_params=_params(2),
    )(*((k, v, q, do, lse, delta, comm_src) if with_comm else (k, v, q, do, lse, delta)))


def _mla_proj_bwd(dq, dk, dv, x, modl, dres, w_in, q_norm, w_q, kv_norm, w_kv, tabs_neg, tm):
    T = x.shape[0]
    ct_a, s1_n, s2_n = tabs_neg

    def body(dq_ref, dk_ref, dv_ref, x_ref, modl_ref, dres_ref, win_ref, qn_ref, wq_ref, kvn_ref, wkv_ref,
             ct_ref, s1_ref, s2_ref, dx_ref, dwin_ref, dwq_ref, dwkv_ref, sm_ref, dqn_ref, dkvn_ref):
        @pl.when(pl.program_id(0) == 0)
        def _():
            for r in (dwin_ref, dwq_ref, dwkv_ref, sm_ref, dqn_ref, dkvn_ref):
                r[...] = jnp.zeros(r.shape, F32)

        xv = x_ref[...]
        h = _modulate(xv, modl_ref, 1, 0).astype(BF)
        lat = _dot(h, win_ref[...])
        ql, kvl = lat[:, :MLA_QR], lat[:, MLA_QR:MLA_QR + MLA_KVR]
        qhat = ql * lax.rsqrt(jnp.mean(ql * ql, axis=1, keepdims=True) + RMS_EPS)
        kvhat = kvl * lax.rsqrt(jnp.mean(kvl * kvl, axis=1, keepdims=True) + RMS_EPS)
        rq = lax.rsqrt(jnp.mean(ql * ql, axis=1, keepdims=True) + RMS_EPS)
        rkv = lax.rsqrt(jnp.mean(kvl * kvl, axis=1, keepdims=True) + RMS_EPS)
        qn = (qhat * qn_ref[...]).astype(BF)
        kvn = (kvhat * kvn_ref[...]).astype(BF)
        ct, s1, s2 = ct_ref[...], s1_ref[...], s2_ref[...]

        dqn = jnp.zeros((tm, MLA_QR), F32)
        dkvn = jnp.zeros((tm, MLA_KVR), F32)
        dkr = jnp.zeros((tm, LANE), F32)
        for hd in range(MLA_H):
            cols = slice(hd * MLA_HD, (hd + 1) * MLA_HD)
            dqh = dq_ref[hd]
            dqr = _rope128(dqh[:, LANE:], ct, s1, s2, MLA_ROPE // 2).astype(BF)
            dqh = jnp.concatenate([dqh[:, :LANE].astype(BF), dqr], axis=1)
            dqn = dqn + _dot_nt(dqh, wq_ref[:, cols])
            dwq_ref[:, cols] += _dot_tn(qn, dqh)
            dkh = dk_ref[hd]
            dkr = dkr + dkh[:, LANE:].astype(F32)
            dkvh = jnp.concatenate([dkh[:, :LANE], dv_ref[hd]], axis=1)
            dkvn = dkvn + _dot_nt(dkvh, wkv_ref[:, cols])
            dwkv_ref[:, cols] += _dot_tn(kvn, dkvh)
        dkr = _rope128(dkr, ct, s1, s2, MLA_ROPE // 2)

        dqn_ref[...] += jnp.sum(dqn * qhat, axis=0, keepdims=True)
        dkvn_ref[...] += jnp.sum(dkvn * kvhat, axis=0, keepdims=True)
        dqh_ = dqn * qn_ref[...]
        dkvh_ = dkvn * kvn_ref[...]
        dql = rq * (dqh_ - qhat * jnp.mean(dqh_ * qhat, axis=1, keepdims=True))
        dkvl = rkv * (dkvh_ - kvhat * jnp.mean(dkvh_ * kvhat, axis=1, keepdims=True))
        dlat = jnp.concatenate([dql, dkvl, dkr], axis=1).astype(BF)
        dwin_ref[...] += _dot_tn(h, dlat)
        dh = _dot_nt(dlat, win_ref[...])
        sm_ref[0:1, :] += jnp.sum(dh * xv, axis=0, keepdims=True)
        sm_ref[1:2, :] += jnp.sum(dh, axis=0, keepdims=True)
        dx_ref[...] = dres_ref[...] + dh * (1.0 + modl_ref[1:2, :])

    row = lambda i: (i, 0)
    head = lambda i: (0, i, 0)
    nq = MLA_H * MLA_HD
    return pl.pallas_call(
        body, name="mla_proj_bwd", grid=(T // tm,),
        out_shape=(jax.ShapeDtypeStruct((T, D), F32), jax.ShapeDtypeStruct((D, MLA_LAT), F32),
                   jax.ShapeDtypeStruct((MLA_QR, nq), F32), jax.ShapeDtypeStruct((MLA_KVR, nq), F32),
                   jax.ShapeDtypeStruct((8, D), F32), jax.ShapeDtypeStruct((1, MLA_QR), F32),
                   jax.ShapeDtypeStruct((1, MLA_KVR), F32)),
        in_specs=[pl.BlockSpec((MLA_H, tm, MLA_HD), head), pl.BlockSpec((MLA_H, tm, MLA_HD), head),
                  pl.BlockSpec((MLA_H, tm, LANE), head), pl.BlockSpec((tm, D), row), _full((8, D)),
                  pl.BlockSpec((tm, D), row), _full((D, MLA_LAT)), _full((1, MLA_QR)), _full((MLA_QR, nq)),
                  _full((1, MLA_KVR)), _full((MLA_KVR, nq)),
                  pl.BlockSpec((tm, LANE), row), pl.BlockSpec((tm, LANE), row), pl.BlockSpec((tm, LANE), row)],
        out_specs=(pl.BlockSpec((tm, D), row), _full((D, MLA_LAT)), _full((MLA_QR, nq)), _full((MLA_KVR, nq)),
                   _full((8, D)), _full((1, MLA_QR)), _full((1, MLA_KVR))),
        compiler_params=_params(1),
    )(dq, dk, dv, x, modl, dres, w_in, q_norm, w_q, kv_norm, w_kv, ct_a, s1_n, s2_n)


def _swa_attn_bwd(qkv, sinks, do, tb):
    T = qkv.shape[0]
    kw = SWA_HKV * LANE
    nstep = T // tb

    def body(q_ref, kvc_ref, kvp_ref, sink_ref, do_ref, dq_ref, dkvc_ref, dkvp_ref, dsink_ref, dk_s, dv_s, bias_s):
        i = pl.program_id(0)
        _swa_fill_bias(bias_s)

        @pl.when(i == 0)
        def _():
            dsink_ref[...] = jnp.zeros(dsink_ref.shape, F32)

        dk_s[...] = jnp.zeros(dk_s.shape, F32)
        dv_s[...] = jnp.zeros(dv_s.shape, F32)
        lane = lax.broadcasted_iota(jnp.int32, (1, LANE), 1)
        for g in range(SWA_HKV):
            gl = slice(g * LANE, (g + 1) * LANE)
            gv = slice(kw + g * LANE, kw + (g + 1) * LANE)
            kall = jnp.concatenate([kvp_ref[:, gl], kvc_ref[:, gl]], axis=0)
            vall = jnp.concatenate([kvp_ref[:, gv], kvc_ref[:, gv]], axis=0)
            for c in range(tb // (SWA_NB * SWA_W)):
                qc, k3, v3, eb, num, linv, es = _swa_chain(q_ref, kall, vall, sink_ref, bias_s, g, c)
                doc = _swa_rows(do_ref, g, c)
                delta = jnp.sum(doc.astype(F32) * (num * linv), axis=1, keepdims=True)
                delta_rep = jnp.broadcast_to(delta, (SWA_R, LANE))
                dp = _dot_nt(doc, v3)
                linv_rep = jnp.broadcast_to(linv, (SWA_R, LANE))
                pch = [eb[:, ch * LANE:(ch + 1) * LANE].astype(F32) * linv_rep for ch in range(SWA_NK // LANE)]
                ds = jnp.concatenate([pch[ch] * (dp[:, ch * LANE:(ch + 1) * LANE] - delta_rep)
                                      for ch in range(SWA_NK // LANE)], axis=1).astype(BF)
                pb = jnp.concatenate(pch, axis=1).astype(BF)
                dqc = (_dot(ds, k3) * SWA_SCALE).astype(BF)
                keys = slice(c * SWA_NB * SWA_W, c * SWA_NB * SWA_W + SWA_NK)
                dk_s[keys, gl] += _dot_tn(ds, qc) * SWA_SCALE
                dv_s[keys, gl] += _dot_tn(pb, doc)
                dsk = es * linv * delta
                for hh in range(4):
                    hq = 4 * g + hh
                    tot = jnp.zeros((1, 1), F32)
                    for bl in range(SWA_NB):
                        piece = (bl * 4 + hh) * SWA_W
                        rows = slice((c * SWA_NB + bl) * SWA_W, (c * SWA_NB + bl + 1) * SWA_W)
                        dq_ref[rows, hq * LANE:(hq + 1) * LANE] = dqc[piece:piece + SWA_W]
                        tot = tot + jnp.sum(dsk[piece:piece + SWA_W], axis=0, keepdims=True)
                    dsink_ref[0:1, :] -= jnp.where(lane == hq, tot, 0.0)
        dkvp_ref[0, :, 0:kw] = dk_s[0:SWA_W, :]
        dkvp_ref[0, :, kw:2 * kw] = dv_s[0:SWA_W, :]
        dkvc_ref[:, 0:kw] = dk_s[SWA_W:, :]
        dkvc_ref[:, kw:2 * kw] = dv_s[SWA_W:, :]

    q_spec, kvc_spec, kvp_spec = _swa_specs(T, tb)
    return pl.pallas_call(
        body, name="swa_attn_bwd", grid=(nstep,),
        out_shape=(jax.ShapeDtypeStruct((T, SWA_O), BF), jax.ShapeDtypeStruct((T, 2 * kw), F32),
                   jax.ShapeDtypeStruct((nstep, SWA_W, 2 * kw), F32), jax.ShapeDtypeStruct((8, LANE), F32)),
        in_specs=[q_spec, kvc_spec, kvp_spec, pl.BlockSpec(memory_space=pltpu.SMEM),
                  pl.BlockSpec((tb, SWA_O), lambda i: (i, 0))],
        out_specs=(pl.BlockSpec((tb, SWA_O), lambda i: (i, 0)), pl.BlockSpec((tb, 2 * kw), lambda i: (i, 0)),
                   pl.BlockSpec((1, SWA_W, 2 * kw), lambda i: (i, 0, 0)), _full((8, LANE))),
        scratch_shapes=[pltpu.VMEM((tb + SWA_W, kw), F32), pltpu.VMEM((tb + SWA_W, kw), F32),
                        pltpu.VMEM((2, SWA_R, SWA_NK), F32)],
        compiler_params=_params(1),
    )(qkv, qkv, qkv, sinks, do)


def _swa_proj_bwd(dq, dkvc, dkvp, x, modl, dres, w, tabs_neg, tm):
    T = x.shape[0]
    nstep = T // tm
    kw = SWA_HKV * LANE
    ct_b, s1_n, s2_n = tabs_neg

    def body(dq_ref, dkvc_ref, dkvp_ref, x_ref, modl_ref, dres_ref, w_ref, ct_ref, s1_ref, s2_ref,
             dx_ref, dz_ref, sm_ref, db_ref):
        i = pl.program_id(0)

        @pl.when(i == 0)
        def _():
            sm_ref[...] = jnp.zeros(sm_ref.shape, F32)
            db_ref[...] = jnp.zeros(db_ref.shape, F32)

        ct, s1, s2 = ct_ref[...], s1_ref[...], s2_ref[...]
        has_next = i + 1 < nstep
        for grp in range(SWA_QKV // LANE):
            cols = slice(grp * LANE, (grp + 1) * LANE)
            if grp < SWA_HQ:
                z = dq_ref[:, cols].astype(F32)
            else:
                kc = slice((grp - SWA_HQ) * LANE, (grp - SWA_HQ + 1) * LANE)
                cur = dkvc_ref[:, kc]
                tail = cur[tm - SWA_W:] + jnp.where(has_next, dkvp_ref[0, :, kc], 0.0)
                z = jnp.concatenate([cur[:tm - SWA_W], tail], axis=0)
            if grp < SWA_HQ + SWA_HKV:
                z = _rope128(z, ct, s1, s2, 8)
            db_ref[0:1, cols] += jnp.sum(z, axis=0, keepdims=True)
            dz_ref[:, cols] = z.astype(BF)
        dh = _dot_nt(dz_ref[...], w_ref[...])
        sm_ref[0:1, :] += jnp.sum(dh * x_ref[...], axis=0, keepdims=True)
        sm_ref[1:2, :] += jnp.sum(dh, axis=0, keepdims=True)
        dx_ref[...] = dres_ref[...] + dh * (1.0 + modl_ref[1:2, :])

    row = lambda i: (i, 0)
    return pl.pallas_call(
        body, name="swa_proj_bwd", grid=(nstep,),
        out_shape=(jax.ShapeDtypeStruct((T, D), F32), jax.ShapeDtypeStruct((T, SWA_QKV), BF),
                   jax.ShapeDtypeStruct((8, D), F32), jax.ShapeDtypeStruct((8, SWA_QKV), F32)),
        in_specs=[pl.BlockSpec((tm, SWA_O), row), pl.BlockSpec((tm, 2 * kw), row),
                  pl.BlockSpec((1, SWA_W, 2 * kw), lambda i: (jnp.minimum(i + 1, nstep - 1), 0, 0)),
                  pl.BlockSpec((tm, D), row), _full((8, D)), pl.BlockSpec((tm, D), row), _full((D, SWA_QKV)),
                  pl.BlockSpec((tm, LANE), row), pl.BlockSpec((tm, LANE), row), pl.BlockSpec((tm, LANE), row)],
        out_specs=(pl.BlockSpec((tm, D), row), pl.BlockSpec((tm, SWA_QKV), row), _full((8, D)),
                   _full((8, SWA_QKV))),
        compiler_params=_params(1),
    )(dq, dkvc, dkvp, x, modl, dres, w, ct_b, s1_n, s2_n)


def _adamw(gparts, w, m, v, name):
    P, R, C = gparts.shape
    tr = R
    for cand in (512, 256, 128):
        if R % cand == 0 and R > cand:
            tr = cand
            break
    c1 = 1.0 / (1.0 - ADAM_B1 ** ADAM_STEP)
    c2 = 1.0 / (1.0 - ADAM_B2 ** ADAM_STEP)

    def body(gp_ref, w_ref, m_ref, v_ref, g_ref, d_ref, nm_ref, nv_ref):
        g = gp_ref[0].astype(F32)
        for p in range(1, P):
            g = g + gp_ref[p].astype(F32)
        nm = ADAM_B1 * m_ref[...] + (1.0 - ADAM_B1) * g
        nv = ADAM_B2 * v_ref[...] + (1.0 - ADAM_B2) * (g * g)
        g_ref[...] = g
        nm_ref[...] = nm
        nv_ref[...] = nv
        d_ref[...] = -ADAM_LR * ((nm * c1) / (jnp.sqrt(nv * c2) + ADAM_EPS) + ADAM_WD * w_ref[...])

    blk = pl.BlockSpec((tr, C), lambda i: (i, 0))
    return pl.pallas_call(
        body, name=name, grid=(R // tr,),
        out_shape=(jax.ShapeDtypeStruct((R, C), F32),) * 4,
        in_specs=[pl.BlockSpec((P, tr, C), lambda i: (0, i, 0)), blk, blk, blk],
        out_specs=(blk,) * 4,
        compiler_params=_params(1),
    )(gparts, w, m, v)


PACK_W = 1024

BIG = {
    "ffn_w_gate": ((DEPTH, D, F // NDEV), 2),
    "ffn_w_up": ((DEPTH, D, F // NDEV), 2),
    "ffn_w_down": ((DEPTH, F // NDEV, D), 1),
    "mla_w_in": ((2, D // NDEV, 704), 1),
    "mla_w_q_b": ((2, MLA_QR, 1536 // NDEV), 2),
    "mla_w_kv_b": ((2, MLA_KVR, 2048 // NDEV), 2),
    "mla_w_o": ((2, D // NDEV, D), 1),
    "swa_w_qkv": ((2, D, 1536 // NDEV), 2),
    "swa_w_o": ((2, D // NDEV, D), 1),
}


EARLY = [("mla_w_in", 0, 1), ("mla_w_q_b", 0, 1), ("mla_w_kv_b", 0, 1)]
LATE = [("ffn_w_gate", 0, 4), ("ffn_w_up", 0, 4), ("ffn_w_down", 0, 4), ("mla_w_in", 1, 2), ("mla_w_q_b", 1, 2),
        ("mla_w_kv_b", 1, 2), ("mla_w_o", 0, 2), ("swa_w_qkv", 0, 2), ("swa_w_o", 0, 2)]


def _pack_rows(n):
    return -(-n // (16 * PACK_W)) * 16


def _entry_shape(name, lo, hi):
    return (hi - lo,) + BIG[name][0][1:]


def _pack_local(inp, entries):
    parts = []
    for name, lo, hi in entries:
        n = math.prod(_entry_shape(name, lo, hi))
        flat = inp[name][lo:hi].astype(BF).reshape(-1)
        parts.append(jnp.pad(flat, (0, _pack_rows(n) * PACK_W - n)).reshape(-1, PACK_W))
    return jnp.concatenate(parts, axis=0)


def _pack_full(full, entries):
    parts = []
    for name, lo, hi in entries:
        a, axis = full[(name, lo)], BIG[name][1]
        split = a.shape[:axis] + (NDEV, a.shape[axis] // NDEV) + a.shape[axis + 1:]
        a = jnp.moveaxis(a.reshape(split), axis, 0).astype(BF).reshape(NDEV, -1)
        n = math.prod(_entry_shape(name, lo, hi))
        parts.append(jnp.pad(a, ((0, 0), (0, _pack_rows(n) * PACK_W - n))).reshape(NDEV, -1, PACK_W))
    return jnp.concatenate(parts, axis=1)


def _unpack_blocks(packed, entries):
    out, r0 = {}, 0
    for name, lo, hi in entries:
        shape = _entry_shape(name, lo, hi)
        n = math.prod(shape)
        rows = _pack_rows(n)
        out[(name, lo)] = packed[:, r0:r0 + rows].reshape(NDEV, -1)[:, :n].reshape((NDEV,) + shape)
        r0 += rows
    return out


def _unpack_full(packed, entries):
    out = {}
    for (name, lo), blk in _unpack_blocks(packed, entries).items():
        axis = BIG[name][1]
        a = jnp.moveaxis(blk, 0, axis)
        out[(name, lo)] = a.reshape(a.shape[:axis] + (a.shape[axis] * a.shape[axis + 1],) + a.shape[axis + 2:])
    return out


def _pad_heads(a, axis, nheads, width, to):
    shp = a.shape[:axis] + (nheads, width) + a.shape[axis + 1:]
    a = a.reshape(shp)
    pad = [(0, 0)] * a.ndim
    pad[axis + 1] = (0, to - width)
    a = jnp.pad(a, pad)
    return a.reshape(a.shape[:axis] + (nheads * to,) + a.shape[axis + 2:])


def _unpad_heads(a, axis, nheads, width, to):
    shp = a.shape[:axis] + (nheads, to) + a.shape[axis + 1:]
    a = lax.slice_in_dim(a.reshape(shp), 0, width, axis=axis + 1)
    return a.reshape(a.shape[:axis] + (nheads * width,) + a.shape[axis + 2:])


def _swa_pad_cols(a):
    return _pad_heads(a, a.ndim - 1, SWA_HQ + 2 * SWA_HKV, 64, LANE)


def _rope_tables(positions, half):
    rot = 2 * half
    inv = ROPE_THETA ** (-jnp.arange(0, rot, 2, dtype=F32) / rot)
    ang = positions.astype(F32)[:, None] * inv
    cos, sin = jnp.cos(ang), jnp.sin(ang)
    T = positions.shape[0]
    ones = jnp.ones((T, LANE - rot), F32)
    zeros = jnp.zeros((T, LANE - rot), F32)
    zh = jnp.zeros((T, half), F32)
    ct = jnp.concatenate([cos, cos, ones], axis=1)
    s1 = jnp.concatenate([-sin, zh, zeros], axis=1)
    s2 = jnp.concatenate([zh, sin, zeros], axis=1)
    return (ct, s1, s2), (ct, -s1, -s2)


def _small_pack(vecs, rows):
    flat = jnp.concatenate([v.astype(F32).reshape(-1) for v in vecs])
    return jnp.pad(flat, (0, rows * LANE - flat.shape[0])).reshape(rows, LANE)


def _small_unpack(buf, shapes):
    flat = buf.reshape(NDEV, -1)
    out, o = [], 0
    for shp in shapes:
        n = math.prod(shp)
        out.append(flat[:, o:o + n].reshape((NDEV,) + shp))
        o += n
    return out


def _step(inp):
    x = inp["x"][0]
    tgt = inp["loss_target"][0]
    T = x.shape[0]
    tm = min(512, T)
    tmf = min(512, T)
    tw = min(1024, T)
    tb = min(512, T)
    tbf = min(2048, T)
    tbq = min(2048, T)
    tnf = F // 2
    me = 4 * lax.axis_index("x") + 2 * lax.axis_index("y") + lax.axis_index("c")

    small_in = _small_pack([inp["c"], inp["swa_b_qkv"], inp["swa_b_o"]], 16)
    c_all, bqkv_blk, bo_blk = _small_unpack(_exchange(small_in, True, "gather_small"),
                                            [(D,), (2, 1536 // NDEV), (2, D // NDEV)])
    swa_b_qkv = jnp.moveaxis(bqkv_blk, 0, 1).reshape(2, 1536)
    swa_b_o = jnp.moveaxis(bo_blk, 0, 1).reshape(2, D)

    w_early = _unpack_full(_exchange(_pack_local(inp, EARLY), True, "gather_early"), EARLY)

    ncol = 6 * D // NDEV
    ada_b_loc = lax.dynamic_slice_in_dim(inp["ada_b"], me * ncol, ncol, axis=1)[:, None, :]
    mod_all = _mod_all(c_all, inp["ada_w"], ada_b_loc)
    mod_src = jnp.moveaxis(mod_all, 1, 0).reshape(NDEV, DEPTH * ncol // LANE, LANE)
    mod_got = _exchange(mod_src, False, "scatter_mod").reshape(NDEV, DEPTH, ncol)
    mod = jnp.moveaxis(mod_got, 0, 1).reshape(DEPTH, 6, D)
    modl = jnp.pad(mod, ((0, 0), (0, 2), (0, 0)))

    def mla_proj_weights(w, lo):
        return (jnp.pad(w[("mla_w_in", lo)][0], ((0, 0), (0, MLA_LAT - 704))),
                _pad_heads(w[("mla_w_q_b", lo)][0], 1, MLA_H, 192, MLA_HD), w[("mla_w_kv_b", lo)][0])

    w_in, w_q, w_kv = [[t] for t in mla_proj_weights(w_early, 0)]
    b_qkv = _swa_pad_cols(swa_b_qkv)
    zero_bias = jnp.zeros((1, D), F32)

    pos = inp["positions"][0]
    tabs_a, tabs_a_neg = _rope_tables(pos, MLA_ROPE // 2)
    tabs_b, tabs_b_neg = _rope_tables(pos, 8)

    saved = []
    xs = x
    for i in range(DEPTH):
        j = i // 2
        st = {"x0": xs}
        if i % 2 == 0:
            q, k, v = _mla_proj_fwd(xs, modl[i], w_in[j], inp["mla_q_norm"][j][None], w_q[j],
                                    inp["mla_kv_norm"][j][None], w_kv[j], tabs_a, tm)
            if i == 0:
                o, lse, got = _flash_fwd(q, k, v, tbq, _pack_local(inp, LATE))
                w_late = _unpack_full(got, LATE)
                for lst, t in zip((w_in, w_q, w_kv), mla_proj_weights(w_late, 1)):
                    lst.append(t)
                w_o_mla = w_late[("mla_w_o", 0)]
                w_qkv = _swa_pad_cols(w_late[("swa_w_qkv", 0)])
                w_o_swa = _pad_heads(w_late[("swa_w_o", 0)], 1, SWA_HQ, 64, LANE)
                w_gate, w_up, w_down = (w_late[(n, 0)] for n in ("ffn_w_gate", "ffn_w_up", "ffn_w_down"))
            else:
                o, lse = _flash_fwd(q, k, v, tbq)
            st.update(q=q, k=k, v=v, o=o, lse=lse)
            w_o, b_o = w_o_mla[j], zero_bias
        else:
            qkv = _swa_proj_fwd(xs, modl[i], w_qkv[j], b_qkv[j][None], tabs_b, tm)
            o = _swa_attn_fwd(qkv, inp["swa_sinks"][j], tb)
            st.update(qkv=qkv, o=o)
            w_o, b_o = w_o_swa[j], swa_b_o[j][None]
        y, u, xs = _outproj_ln_fwd(o, w_o, b_o, xs, modl[i], 2, inp["ln_mix_g"][i][None],
                                   inp["ln_mix_b"][i][None], tm, f"mix_out_fwd_{i % 2}")
        st.update(y_m=y, u_m=u, x1=xs, w_o=w_o)
        g, up, a = _ffn_up_fwd(xs, modl[i], w_gate[i], w_up[i], tmf, tnf)
        y, u, xs = _outproj_ln_fwd(a, w_down[i], zero_bias, xs, modl[i], 5,
                                   inp["ln_ffn_g"][i][None], inp["ln_ffn_b"][i][None], tmf, "ffn_out_fwd")
        st.update(g=g, up=up, a=a, y_f=y, u_f=u)
        saved.append(st)

    dx, loss_rows = _loss_grad(xs, tgt, tm)
    loss = lax.psum(jnp.sum(loss_rows[0]), ("x", "y", "c"))

    gfull = {n: [None] * (DEPTH if n.startswith("ffn") else 2) for n in BIG}
    dmod = [None] * DEPTH
    g_ln = {n: [None] * DEPTH for n in ("ln_mix_g", "ln_mix_b", "ln_ffn_g", "ln_ffn_b")}
    g_qn, g_kvn, g_sink, g_bqkv, g_bo = [None] * 2, [None] * 2, [None] * 2, [None] * 2, [None] * 2
    for i in reversed(range(DEPTH)):
        j = i // 2
        st = saved[i]
        dres, dy, da, sm = _outproj_ln_bwd(dx, st["u_f"], st["y_f"], w_down[i], modl[i], 5,
                                           inp["ln_ffn_g"][i][None], tmf, "ffn_out_bwd")
        g_ln["ln_ffn_g"][i], g_ln["ln_ffn_b"][i], dg_f = sm[0], sm[1], sm[2]
        gfull["ffn_w_down"][i] = _wgrad(st["a"], dy, tw, F // 2, D, "wgrad_down")
        dgp, dup, dx, sm = _ffn_mid_bwd(da, st["g"], st["up"], st["x1"], modl[i], dres,
                                        w_gate[i], w_up[i], tm, tnf)
        dsc_f, dsh_f = sm[0], sm[1]
        gfull["ffn_w_gate"][i] = _wgrad(st["x1"], dgp, tw, D, tnf, "wgrad_gate", modl[i], (4, 3))
        gfull["ffn_w_up"][i] = _wgrad(st["x1"], dup, tw, D, tnf, "wgrad_up", modl[i], (4, 3))

        dres, dy, do, sm = _outproj_ln_bwd(dx, st["u_m"], st["y_m"], st["w_o"], modl[i], 2,
                                           inp["ln_mix_g"][i][None], tm, f"mix_out_bwd_{i % 2}")
        g_ln["ln_mix_g"][i], g_ln["ln_mix_b"][i], dg_m = sm[0], sm[1], sm[2]
        if i % 2 == 0:
            gfull["mla_w_o"][j] = _wgrad(st["o"], dy, tw, D, D, "wgrad_mla_o")
            delta = _flash_delta(st["o"], do, tb)
            if i == 0:
                late = {(n, lo): jnp.stack(gfull[n][lo:hi]) for n, lo, hi in LATE}
                dq, dk, dv, got = _flash_bwd(st["q"], st["k"], st["v"], do, st["lse"], delta, tbf,
                                             _pack_full(late, LATE))
                gparts = _unpack_blocks(got, LATE)
            else:
                dq, dk, dv = _flash_bwd(st["q"], st["k"], st["v"], do, st["lse"], delta, tbf)
            dx, dwin, dwq, dwkv, sm, dqn, dkvn = _mla_proj_bwd(
                dq, dk, dv, st["x0"], modl[i], dres, w_in[j], inp["mla_q_norm"][j][None], w_q[j],
                inp["mla_kv_norm"][j][None], w_kv[j], tabs_a_neg, tm)
            gfull["mla_w_in"][j] = dwin[:, :704]
            gfull["mla_w_q_b"][j] = _unpad_heads(dwq, 1, MLA_H, 192, MLA_HD)
            gfull["mla_w_kv_b"][j] = dwkv
            g_qn[j], g_kvn[j] = dqn[0], dkvn[0]
        else:
            g_bo[j] = sm[3]
            dwo = _wgrad(st["o"], dy, tw, SWA_O // 2, D, "wgrad_swa_o")
            gfull["swa_w_o"][j] = _unpad_heads(dwo, 0, SWA_HQ, 64, LANE)
            dq, dkvc, dkvp, dsink = _swa_attn_bwd(st["qkv"], inp["swa_sinks"][j], do, tb)
            g_sink[j] = dsink[0, :SWA_HQ]
            dx, dz, sm, db = _swa_proj_bwd(dq, dkvc, dkvp, st["x0"], modl[i], dres, w_qkv[j], tabs_b_neg, tm)
            dwqkv = _wgrad(st["x0"], dz, tw, D, SWA_QKV // 2, "wgrad_swa_qkv", modl[i], (1, 0))
            gfull["swa_w_qkv"][j] = _unpad_heads(dwqkv, 1, SWA_HQ + 2 * SWA_HKV, 64, LANE)
            g_bqkv[j] = _unpad_heads(db[0], 0, SWA_HQ + 2 * SWA_HKV, 64, LANE)
        dmod[i] = jnp.stack([sm[1], sm[0], dg_m, dsh_f, dsc_f, dg_f])
    grad_x = dx[None]

    small_shapes = [(DEPTH, 6 * D), (DEPTH, D), (DEPTH, D), (DEPTH, D), (DEPTH, D), (2, MLA_QR), (2, MLA_KVR),
                    (2, SWA_HQ), (2, 1536), (2, D)]
    small_vals = [jnp.stack(dmod).reshape(DEPTH, 6 * D), jnp.stack(g_ln["ln_mix_g"]), jnp.stack(g_ln["ln_mix_b"]),
                  jnp.stack(g_ln["ln_ffn_g"]), jnp.stack(g_ln["ln_ffn_b"]), jnp.stack(g_qn), jnp.stack(g_kvn),
                  jnp.stack(g_sink), jnp.stack(g_bqkv), jnp.stack(g_bo)]
    nsmall = sum(math.prod(s) for s in small_shapes)
    small_rows = -(-nsmall // (8 * LANE)) * 8
    (dmod_all, p_lmg, p_lmb, p_lfg, p_lfb, p_qn, p_kvn, p_sink, p_bqkv, p_bo) = _small_unpack(
        _exchange(_small_pack(small_vals, small_rows), True, "gather_small_grads"), small_shapes)

    early = {(n, lo): jnp.stack(gfull[n][lo:hi]) for n, lo, hi in EARLY}
    gparts.update(_unpack_blocks(_exchange(_pack_full(early, EARLY), False, "scatter_early_grads"), EARLY))
    for n, _, _ in EARLY:
        gparts[(n, 0)] = jnp.concatenate([gparts[(n, 0)], gparts.pop((n, 1))], axis=1)

    res = {}

    def update(name, parts):
        w = inp[name]
        shp = w.shape
        r2 = (math.prod(shp[:-1]), shp[-1])
        outs = _adamw(parts.reshape((parts.shape[0],) + r2), w.reshape(r2), inp["m_" + name].reshape(r2),
                      inp["v_" + name].reshape(r2), "adamw_" + name)
        res[name] = tuple(o.reshape(shp) for o in outs)

    dmod_loc = lax.dynamic_slice_in_dim(dmod_all, me * ncol, ncol, axis=2)
    g_ada_w = _ada_w_grad(c_all.T, jnp.moveaxis(dmod_loc, 0, 1))
    update("ada_w", g_ada_w[None])
    update("ada_b", dmod_all)
    update("ln_mix_g", p_lmg)
    update("ln_mix_b", p_lmb)
    update("ln_ffn_g", p_lfg)
    update("ln_ffn_b", p_lfb)
    for name in BIG:
        update(name, gparts[(name, 0)])
    update("mla_q_norm", p_qn)
    update("mla_kv_norm", p_kvn)
    update("swa_sinks", p_sink)
    nb = 1536 // NDEV
    update("swa_b_qkv", lax.dynamic_slice_in_dim(p_bqkv, me * nb, nb, axis=2))
    update("swa_b_o", lax.dynamic_slice_in_dim(p_bo, me * (D // NDEV), D // NDEV, axis=2))
    return loss, grad_x, res


WEIGHTS = ["ada_w", "ada_b", "ln_mix_g", "ln_mix_b", "ln_ffn_g", "ln_ffn_b", "ffn_w_gate", "ffn_w_up",
           "ffn_w_down", "mla_w_in", "mla_q_norm", "mla_w_q_b", "mla_kv_norm", "mla_w_kv_b", "mla_w_o",
           "swa_w_qkv", "swa_b_qkv", "swa_sinks", "swa_w_o", "swa_b_o"]
INPUTS = (["x", "c", "positions"] + WEIGHTS + ["loss_target"] + ["m_" + n for n in WEIGHTS]
          + ["v_" + n for n in WEIGHTS])


def kernel(x, c, positions, ada_w, ada_b, ln_mix_g, ln_mix_b, ln_ffn_g, ln_ffn_b, ffn_w_gate, ffn_w_up, ffn_w_down, mla_w_in, mla_q_norm, mla_w_q_b, mla_kv_norm, mla_w_kv_b, mla_w_o, swa_w_qkv, swa_b_qkv, swa_sinks, swa_w_o, swa_b_o, loss_target, m_ada_w, m_ada_b, m_ln_mix_g, m_ln_mix_b, m_ln_ffn_g, m_ln_ffn_b, m_ffn_w_gate, m_ffn_w_up, m_ffn_w_down, m_mla_w_in, m_mla_q_norm, m_mla_w_q_b, m_mla_kv_norm, m_mla_w_kv_b, m_mla_w_o, m_swa_w_qkv, m_swa_b_qkv, m_swa_sinks, m_swa_w_o, m_swa_b_o, v_ada_w, v_ada_b, v_ln_mix_g, v_ln_mix_b, v_ln_ffn_g, v_ln_ffn_b, v_ffn_w_gate, v_ffn_w_up, v_ffn_w_down, v_mla_w_in, v_mla_q_norm, v_mla_w_q_b, v_mla_kv_norm, v_mla_w_kv_b, v_mla_w_o, v_swa_w_qkv, v_swa_b_qkv, v_swa_sinks, v_swa_w_o, v_swa_b_o):
    args = (x, c, positions, ada_w, ada_b, ln_mix_g, ln_mix_b, ln_ffn_g, ln_ffn_b, ffn_w_gate, ffn_w_up, ffn_w_down, mla_w_in, mla_q_norm, mla_w_q_b, mla_kv_norm, mla_w_kv_b, mla_w_o, swa_w_qkv, swa_b_qkv, swa_sinks, swa_w_o, swa_b_o, loss_target, m_ada_w, m_ada_b, m_ln_mix_g, m_ln_mix_b, m_ln_ffn_g, m_ln_ffn_b, m_ffn_w_gate, m_ffn_w_up, m_ffn_w_down, m_mla_w_in, m_mla_q_norm, m_mla_w_q_b, m_mla_kv_norm, m_mla_w_kv_b, m_mla_w_o, m_swa_w_qkv, m_swa_b_qkv, m_swa_sinks, m_swa_w_o, m_swa_b_o, v_ada_w, v_ada_b, v_ln_mix_g, v_ln_mix_b, v_ln_ffn_g, v_ln_ffn_b, v_ffn_w_gate, v_ffn_w_up, v_ffn_w_down, v_mla_w_in, v_mla_q_norm, v_mla_w_q_b, v_mla_kv_norm, v_mla_w_kv_b, v_mla_w_o, v_swa_w_qkv, v_swa_b_qkv, v_swa_sinks, v_swa_w_o, v_swa_b_o)
    assert len(args) == len(INPUTS)
    loss, grad_x, res = _step(dict(zip(INPUTS, args)))
    return (loss, grad_x, *[res[n][0] for n in WEIGHTS], *[res[n][1] for n in WEIGHTS],
            *[res[n][2] for n in WEIGHTS], *[res[n][3] for n in WEIGHTS])
```
